```python
import jax
import jax.numpy as jnp
from jax import lax
import numpy as np

D_MODEL = 1024
BATCH = 8
SEQ = 8192
DEPTH = 2

PLE_DIM = 256
POOL_WINDOWS = (2, 4, 8, 16)
N_POOL_GROUPS = len(POOL_WINDOWS)
POOL_GROUP = D_MODEL // 8
POOL_WIDTH = N_POOL_GROUPS * POOL_GROUP
N_HEADS = 8
Q_LORA = D_MODEL // 2
KV_LORA = D_MODEL // 4
QK_NOPE = 128
QK_ROPE = 64
QK_HEAD = QK_NOPE + QK_ROPE
V_HEAD = 128
ATTN_WIDTH = N_HEADS * V_HEAD
D_FF = -(-8 * D_MODEL // (3 * 256)) * 256
ROPE_THETA = 10000.0
EPS = 1e-6
Q_BLOCK = 128

_OFF_Q = POOL_WIDTH
_OFF_KV = _OFF_Q + Q_LORA
_OFF_KR = _OFF_KV + KV_LORA
_OFF_GA = _OFF_KR + QK_ROPE
_OFF_GB = _OFF_GA + D_MODEL
IN_WIDTH = _OFF_GB + D_MODEL

kernel_name = 'hybrid_pool_mla_gated_block'


def rms_norm(x, g):
    xf = x.astype(jnp.float32)
    y = xf * lax.rsqrt(jnp.mean(xf * xf, axis=-1, keepdims=True) + EPS)
    return (y * g.astype(jnp.float32)).astype(x.dtype)


def rope_tables(positions):
    inv_freq = 1.0 / (ROPE_THETA ** (jnp.arange(0, QK_ROPE, 2, dtype=jnp.float32) / QK_ROPE))
    ang = positions.astype(jnp.float32)[..., None] * inv_freq
    return jnp.cos(ang)[:, :, None, :], jnp.sin(ang)[:, :, None, :]


def apply_rope(x, cos, sin):
    xf = x.astype(jnp.float32)
    half = QK_ROPE // 2
    x1, x2 = xf[..., :half], xf[..., half:]
    return jnp.concatenate([x1 * cos - x2 * sin, x2 * cos + x1 * sin], axis=-1).astype(x.dtype)


def multiscale_pool(u, w_pool, pool_scale):
    B, S, _ = u.shape
    uf = u.astype(jnp.float32).reshape(B, S, N_POOL_GROUPS, POOL_GROUP)
    csum = jnp.concatenate([jnp.zeros((B, 1, N_POOL_GROUPS, POOL_GROUP), jnp.float32),
                            jnp.cumsum(uf, axis=1)], axis=1)
    t = jnp.arange(S)
    pooled = []
    for g, w in enumerate(POOL_WINDOWS):
        hi = csum[:, 1:, g]
        lo = jnp.concatenate([jnp.zeros((B, w - 1, POOL_GROUP), jnp.float32),
                              csum[:, :S - w + 1, g]], axis=1)
        cnt = jnp.minimum(t + 1, w).astype(jnp.float32)[None, :, None]
        pooled.append((hi - lo) / cnt - uf[:, :, g])
    pooled = jnp.stack(pooled, axis=2).astype(u.dtype)
    mixed = jnp.einsum('bsgc,gcd->bsgd', pooled, w_pool)
    return mixed.reshape(B, S, POOL_WIDTH) * pool_scale


def causal_block_attention(q, k, v):
    B, S, H, Dq = q.shape
    nb = S // Q_BLOCK
    scale = QK_HEAD ** -0.5
    q_blocks = q.reshape(B, nb, Q_BLOCK, H, Dq).transpose(1, 0, 2, 3, 4)
    key_idx = jnp.arange(S)
    neg = jnp.finfo(jnp.float32).min

    def one_block(args):
        qb, bi = args
        s = jnp.einsum('bqhd,bkhd->bhqk', qb, k, preferred_element_type=jnp.float32) * scale
        q_idx = bi * Q_BLOCK + jnp.arange(Q_BLOCK)
        mask = key_idx[None, :] <= q_idx[:, None]
        s = jnp.where(mask[None, None], s, neg)
        pr = jax.nn.softmax(s, axis=-1)
        return jnp.einsum('bhqk,bkhd->bqhd', pr.astype(v.dtype), v)

    out = lax.map(one_block, (q_blocks, jnp.arange(nb)))
    return out.transpose(1, 0, 2, 3, 4).reshape(B, S, H * v.shape[-1])


def mla(c_q, c_kv, k_rope, q_norm_g, kv_norm_g, w_uq, w_ukv, cos, sin):
    B, S, _ = c_q.shape
    q = jnp.einsum('bsr,rhd->bshd', rms_norm(c_q, q_norm_g), w_uq)
    q = jnp.concatenate([q[..., :QK_NOPE], apply_rope(q[..., QK_NOPE:], cos, sin)], axis=-1)
    kv = jnp.einsum('bsr,rhd->bshd', rms_norm(c_kv, kv_norm_g), w_ukv)
    k_nope, v = kv[..., :QK_NOPE], kv[..., QK_NOPE:]
    k_pe = apply_rope(k_rope[:, :, None, :], cos, sin)
    k = jnp.concatenate([k_nope, jnp.broadcast_to(k_pe, (B, S, N_HEADS, QK_ROPE))], axis=-1)
    return causal_block_attention(q, k, v)


def _fwd_setup_inputs(seed: int = 0) -> dict:
    key = jax.random.key(seed)
    ks = jax.random.split(key, 24)
    f32 = jnp.float32

    def dense(k, shape, fan_in):
        return jax.random.normal(k, shape, f32) * fan_in ** -0.5

    def gain(k, shape, s=0.05):
        return 1.0 + s * jax.random.normal(k, shape, f32)

    L = DEPTH
    return {
        'x': jax.random.normal(ks[0], (BATCH, SEQ, D_MODEL), f32),
        'p': jax.random.normal(ks[1], (L, BATCH, SEQ, PLE_DIM), f32),
        'positions': jnp.tile(jnp.arange(SEQ, dtype=jnp.int32)[None, :], (BATCH, 1)),
        'norm_mix': gain(ks[2], (L, D_MODEL)),
        'w_in': dense(ks[3], (L, D_MODEL, IN_WIDTH), D_MODEL),
        'w_pool': dense(ks[4], (L, N_POOL_GROUPS, POOL_GROUP, POOL_GROUP), POOL_GROUP),
        'pool_scale': gain(ks[5], (L, POOL_WIDTH), 0.1),
        'q_norm': gain(ks[6], (L, Q_LORA)),
        'kv_norm': gain(ks[7], (L, KV_LORA)),
        'w_uq': dense(ks[8], (L, Q_LORA, N_HEADS, QK_HEAD), Q_LORA),
        'w_ukv': dense(ks[9], (L, KV_LORA, N_HEADS, QK_NOPE + V_HEAD), KV_LORA),
        'w_a': dense(ks[10], (L, POOL_WIDTH, D_MODEL), POOL_WIDTH),
        'w_b': dense(ks[11], (L, ATTN_WIDTH, D_MODEL), ATTN_WIDTH),
        'w_o': dense(ks[12], (L, D_MODEL, D_MODEL), D_MODEL),
        'norm_ffn': gain(ks[13], (L, D_MODEL)),
        'w_gate': dense(ks[14], (L, D_MODEL, D_FF), D_MODEL),
        'w_up': dense(ks[15], (L, D_MODEL, D_FF), D_MODEL),
        'w_down': dense(ks[16], (L, D_FF, D_MODEL), D_FF),
        'norm_ple': gain(ks[17], (L, D_MODEL)),
        'w_ple_gate': dense(ks[18], (L, D_MODEL, D_MODEL), D_MODEL),
        'w_ple': dense(ks[19], (L, PLE_DIM, D_MODEL), PLE_DIM),
        'final_norm': gain(ks[20], (D_MODEL,)),
    }


def _fwd_reference(x, p, positions, norm_mix, w_in, w_pool, pool_scale, q_norm, kv_norm, w_uq, w_ukv,
              w_a, w_b, w_o, norm_ffn, w_gate, w_up, w_down, norm_ple, w_ple_gate, w_ple,
              final_norm):
    cos, sin = rope_tables(positions)
    for i in range(DEPTH):
        h = rms_norm(x, norm_mix[i])
        z = h @ w_in[i]
        u = z[..., :_OFF_Q]
        c_q = z[..., _OFF_Q:_OFF_KV]
        c_kv = z[..., _OFF_KV:_OFF_KR]
        k_rope = z[..., _OFF_KR:_OFF_GA]
        gate_a = z[..., _OFF_GA:_OFF_GB]
        gate_b = z[..., _OFF_GB:]
        y_a = multiscale_pool(u, w_pool[i], pool_scale[i]) @ w_a[i]
        y_b = mla(c_q, c_kv, k_rope, q_norm[i], kv_norm[i], w_uq[i], w_ukv[i], cos, sin) @ w_b[i]
        merged = jax.nn.sigmoid(gate_a) * y_a + jax.nn.sigmoid(gate_b) * y_b
        x = x + merged @ w_o[i]
        h = rms_norm(x, norm_ffn[i])
        x = x + (jax.nn.silu(h @ w_gate[i]) * (h @ w_up[i])) @ w_down[i]
        g = jax.nn.sigmoid(rms_norm(x, norm_ple[i]) @ w_ple_gate[i])
        x = x + g * (p[i] @ w_ple[i])
    return rms_norm(x, final_norm)


import jax as _jax
import jax.numpy as _jnp

TWIN_FORMAT = 'train_step'
FWD_PARAMS = ['x', 'p', 'positions', 'norm_mix', 'w_in', 'w_pool', 'pool_scale', 'q_norm', 'kv_norm', 'w_uq', 'w_ukv', 'w_a', 'w_b', 'w_o', 'norm_ffn', 'w_gate', 'w_up', 'w_down', 'norm_ple', 'w_ple_gate', 'w_ple', 'final_norm']
TWIN_WEIGHTS = ['norm_mix', 'w_in', 'w_pool', 'pool_scale', 'q_norm', 'kv_norm', 'w_uq', 'w_ukv', 'w_a', 'w_b', 'w_o', 'norm_ffn', 'w_gate', 'w_up', 'w_down', 'norm_ple', 'w_ple_gate', 'w_ple', 'final_norm']
TWIN_DIFF_INPUT = 'x'
TWIN_INPUTS = ['x', 'p', 'positions', 'norm_mix', 'w_in', 'w_pool', 'pool_scale', 'q_norm', 'kv_norm', 'w_uq', 'w_ukv', 'w_a', 'w_b', 'w_o', 'norm_ffn', 'w_gate', 'w_up', 'w_down', 'norm_ple', 'w_ple_gate', 'w_ple', 'final_norm', 'loss_target', 'm_norm_mix', 'm_w_in', 'm_w_pool', 'm_pool_scale', 'm_q_norm', 'm_kv_norm', 'm_w_uq', 'm_w_ukv', 'm_w_a', 'm_w_b', 'm_w_o', 'm_norm_ffn', 'm_w_gate', 'm_w_up', 'm_w_down', 'm_norm_ple', 'm_w_ple_gate', 'm_w_ple', 'm_final_norm', 'v_norm_mix', 'v_w_in', 'v_w_pool', 'v_pool_scale', 'v_q_norm', 'v_kv_norm', 'v_w_uq', 'v_w_ukv', 'v_w_a', 'v_w_b', 'v_w_o', 'v_norm_ffn', 'v_w_gate', 'v_w_up', 'v_w_down', 'v_norm_ple', 'v_w_ple_gate', 'v_w_ple', 'v_final_norm']
TWIN_OUTPUTS = ['loss', 'grad_x', 'grad_norm_mix', 'grad_w_in', 'grad_w_pool', 'grad_pool_scale', 'grad_q_norm', 'grad_kv_norm', 'grad_w_uq', 'grad_w_ukv', 'grad_w_a', 'grad_w_b', 'grad_w_o', 'grad_norm_ffn', 'grad_w_gate', 'grad_w_up', 'grad_w_down', 'grad_norm_ple', 'grad_w_ple_gate', 'grad_w_ple', 'grad_final_norm', 'delta_norm_mix', 'delta_w_in', 'delta_w_pool', 'delta_pool_scale', 'delta_q_norm', 'delta_kv_norm', 'delta_w_uq', 'delta_w_ukv', 'delta_w_a', 'delta_w_b', 'delta_w_o', 'delta_norm_ffn', 'delta_w_gate', 'delta_w_up', 'delta_w_down', 'delta_norm_ple', 'delta_w_ple_gate', 'delta_w_ple', 'delta_final_norm', 'new_m_norm_mix', 'new_m_w_in', 'new_m_w_pool', 'new_m_pool_scale', 'new_m_q_norm', 'new_m_kv_norm', 'new_m_w_uq', 'new_m_w_ukv', 'new_m_w_a', 'new_m_w_b', 'new_m_w_o', 'new_m_norm_ffn', 'new_m_w_gate', 'new_m_w_up', 'new_m_w_down', 'new_m_norm_ple', 'new_m_w_ple_gate', 'new_m_w_ple', 'new_m_final_norm', 'new_v_norm_mix', 'new_v_w_in', 'new_v_w_pool', 'new_v_pool_scale', 'new_v_q_norm', 'new_v_kv_norm', 'new_v_w_uq', 'new_v_w_ukv', 'new_v_w_a', 'new_v_w_b', 'new_v_w_o', 'new_v_norm_ffn', 'new_v_w_gate', 'new_v_w_up', 'new_v_w_down', 'new_v_norm_ple', 'new_v_w_ple_gate', 'new_v_w_ple', 'new_v_final_norm']
TWIN_LEAF_KINDS = {'loss': 'loss', 'grad_x': 'grad_x', 'grad_norm_mix': 'grad_w', 'grad_w_in': 'grad_w', 'grad_w_pool': 'grad_w', 'grad_pool_scale': 'grad_w', 'grad_q_norm': 'grad_w', 'grad_kv_norm': 'grad_w', 'grad_w_uq': 'grad_w', 'grad_w_ukv': 'grad_w', 'grad_w_a': 'grad_w', 'grad_w_b': 'grad_w', 'grad_w_o': 'grad_w', 'grad_norm_ffn': 'grad_w', 'grad_w_gate': 'grad_w', 'grad_w_up': 'grad_w', 'grad_w_down': 'grad_w', 'grad_norm_ple': 'grad_w', 'grad_w_ple_gate': 'grad_w', 'grad_w_ple': 'grad_w', 'grad_final_norm': 'grad_w', 'delta_norm_mix': 'delta_w', 'delta_w_in': 'delta_w', 'delta_w_pool': 'delta_w', 'delta_pool_scale': 'delta_w', 'delta_q_norm': 'delta_w', 'delta_kv_norm': 'delta_w', 'delta_w_uq': 'delta_w', 'delta_w_ukv': 'delta_w', 'delta_w_a': 'delta_w', 'delta_w_b': 'delta_w', 'delta_w_o': 'delta_w', 'delta_norm_ffn': 'delta_w', 'delta_w_gate': 'delta_w', 'delta_w_up': 'delta_w', 'delta_w_down': 'delta_w', 'delta_norm_ple': 'delta_w', 'delta_w_ple_gate': 'delta_w', 'delta_w_ple': 'delta_w', 'delta_final_norm': 'delta_w', 'new_m_norm_mix': 'new_m', 'new_m_w_in': 'new_m', 'new_m_w_pool': 'new_m', 'new_m_pool_scale': 'new_m', 'new_m_q_norm': 'new_m', 'new_m_kv_norm': 'new_m', 'new_m_w_uq': 'new_m', 'new_m_w_ukv': 'new_m', 'new_m_w_a': 'new_m', 'new_m_w_b': 'new_m', 'new_m_w_o': 'new_m', 'new_m_norm_ffn': 'new_m', 'new_m_w_gate': 'new_m', 'new_m_w_up': 'new_m', 'new_m_w_down': 'new_m', 'new_m_norm_ple': 'new_m', 'new_m_w_ple_gate': 'new_m', 'new_m_w_ple': 'new_m', 'new_m_final_norm': 'new_m', 'new_v_norm_mix': 'new_v', 'new_v_w_in': 'new_v', 'new_v_w_pool': 'new_v', 'new_v_pool_scale': 'new_v', 'new_v_q_norm': 'new_v', 'new_v_kv_norm': 'new_v', 'new_v_w_uq': 'new_v', 'new_v_w_ukv': 'new_v', 'new_v_w_a': 'new_v', 'new_v_w_b': 'new_v', 'new_v_w_o': 'new_v', 'new_v_norm_ffn': 'new_v', 'new_v_w_gate': 'new_v', 'new_v_w_up': 'new_v', 'new_v_w_down': 'new_v', 'new_v_norm_ple': 'new_v', 'new_v_w_ple_gate': 'new_v', 'new_v_w_ple': 'new_v', 'new_v_final_norm': 'new_v'}


def _forward(args):
    return _fwd_reference(*[args[k] for k in FWD_PARAMS])


def _output_shape():
    def fwd():
        inp = _fwd_setup_inputs(0)
        return _fwd_reference(*[inp[k] for k in FWD_PARAMS])
    out = _jax.eval_shape(fwd)
    return out.shape, out.dtype

N_MICROBATCH = 1
ADAM_LR = 0.001
ADAM_B1 = 0.9
ADAM_B2 = 0.999
ADAM_EPS = 1e-08
ADAM_WD = 0.01
ADAM_STEP = 10
PER_EXAMPLE_BATCH_AXIS = {'x': 0, 'p': 1, 'positions': 0, 'loss_target': 0}
SHARED_INPUTS = []
_WEIGHT_DTYPES = {'norm_mix': _jnp.float32, 'w_in': _jnp.float32, 'w_pool': _jnp.float32, 'pool_scale': _jnp.float32, 'q_norm': _jnp.float32, 'kv_norm': _jnp.float32, 'w_uq': _jnp.float32, 'w_ukv': _jnp.float32, 'w_a': _jnp.float32, 'w_b': _jnp.float32, 'w_o': _jnp.float32, 'norm_ffn': _jnp.float32, 'w_gate': _jnp.float32, 'w_up': _jnp.float32, 'w_down': _jnp.float32, 'norm_ple': _jnp.float32, 'w_ple_gate': _jnp.float32, 'w_ple': _jnp.float32, 'final_norm': _jnp.float32}
MOMENT_SCALE = {'norm_mix': 1.163670e-01, 'w_in': 6.389832e-02, 'w_pool': 1.480307e-01, 'pool_scale': 1.567496e-01, 'q_norm': 3.063436e-02, 'kv_norm': 6.521416e-02, 'w_uq': 1.709621e-02, 'w_ukv': 2.190215e-02, 'w_a': 1.054222e-01, 'w_b': 2.573739e-02, 'w_o': 1.048743e-01, 'norm_ffn': 1.563738e-01, 'w_gate': 6.504171e-02, 'w_up': 6.351169e-02, 'w_down': 1.058050e-01, 'norm_ple': 3.708624e-02, 'w_ple_gate': 3.660039e-02, 'w_ple': 9.563529e-02, 'final_norm': 6.420652e+01}


def _to_microbatches(a, axis):
    t = _jnp.moveaxis(a, axis, 0)
    t = t.reshape((N_MICROBATCH, t.shape[0] // N_MICROBATCH) + t.shape[1:])
    return _jnp.moveaxis(t, 1, axis + 1)


def setup_inputs(seed: int = 0) -> dict:
    inp = _fwd_setup_inputs(seed)
    key = _jax.random.fold_in(_jax.random.key(seed), 7919)
    shape, _ = _output_shape()
    out = dict(inp)
    out["loss_target"] = _jax.random.normal(_jax.random.fold_in(key, 0), shape, _jnp.float32)
    for i, name in enumerate(TWIN_WEIGHTS):
        w = inp[name].astype(_jnp.float32)
        if MOMENT_SCALE is None:
            s = _jnp.sqrt(_jnp.mean(_jnp.square(w)) + 1e-30)
        else:
            s = MOMENT_SCALE[name]
        km, kv = _jax.random.split(_jax.random.fold_in(key, i + 1))
        out[name] = w
        out["m_" + name] = s * _jax.random.normal(km, w.shape, _jnp.float32)
        out["v_" + name] = (s * s) * _jax.random.uniform(kv, w.shape, _jnp.float32, 0.5, 1.5)
    if N_MICROBATCH > 1:
        for name, axis in PER_EXAMPLE_BATCH_AXIS.items():
            out[name] = _to_microbatches(out[name], axis)
    return {'x': out['x'], 'p': out['p'], 'positions': out['positions'], 'norm_mix': out['norm_mix'], 'w_in': out['w_in'], 'w_pool': out['w_pool'], 'pool_scale': out['pool_scale'], 'q_norm': out['q_norm'], 'kv_norm': out['kv_norm'], 'w_uq': out['w_uq'], 'w_ukv': out['w_ukv'], 'w_a': out['w_a'], 'w_b': out['w_b'], 'w_o': out['w_o'], 'norm_ffn': out['norm_ffn'], 'w_gate': out['w_gate'], 'w_up': out['w_up'], 'w_down': out['w_down'], 'norm_ple': out['norm_ple'], 'w_ple_gate': out['w_ple_gate'], 'w_ple': out['w_ple'], 'final_norm': out['final_norm'], 'loss_target': out['loss_target'], 'm_norm_mix': out['m_norm_mix'], 'm_w_in': out['m_w_in'], 'm_w_pool': out['m_w_pool'], 'm_pool_scale': out['m_pool_scale'], 'm_q_norm': out['m_q_norm'], 'm_kv_norm': out['m_kv_norm'], 'm_w_uq': out['m_w_uq'], 'm_w_ukv': out['m_w_ukv'], 'm_w_a': out['m_w_a'], 'm_w_b': out['m_w_b'], 'm_w_o': out['m_w_o'], 'm_norm_ffn': out['m_norm_ffn'], 'm_w_gate': out['m_w_gate'], 'm_w_up': out['m_w_up'], 'm_w_down': out['m_w_down'], 'm_norm_ple': out['m_norm_ple'], 'm_w_ple_gate': out['m_w_ple_gate'], 'm_w_ple': out['m_w_ple'], 'm_final_norm': out['m_final_norm'], 'v_norm_mix': out['v_norm_mix'], 'v_w_in': out['v_w_in'], 'v_w_pool': out['v_w_pool'], 'v_pool_scale': out['v_pool_scale'], 'v_q_norm': out['v_q_norm'], 'v_kv_norm': out['v_kv_norm'], 'v_w_uq': out['v_w_uq'], 'v_w_ukv': out['v_w_ukv'], 'v_w_a': out['v_w_a'], 'v_w_b': out['v_w_b'], 'v_w_o': out['v_w_o'], 'v_norm_ffn': out['v_norm_ffn'], 'v_w_gate': out['v_w_gate'], 'v_w_up': out['v_w_up'], 'v_w_down': out['v_w_down'], 'v_norm_ple': out['v_norm_ple'], 'v_w_ple_gate': out['v_w_ple_gate'], 'v_w_ple': out['v_w_ple'], 'v_final_norm': out['v_final_norm']}


def _loss(weights, diff, rest, loss_target):
    with _jax.named_scope("forward"):
        args = {**rest, TWIN_DIFF_INPUT: diff, **{k: w.astype(_WEIGHT_DTYPES[k]) for k, w in weights.items()}}
        y = _forward(args)
    with _jax.named_scope("loss_head"):
        err = _jnp.square(y.astype(_jnp.float32) - loss_target)
        return 0.5 * _jnp.sum(_jnp.mean(err, axis=-1)) if err.ndim else 0.5 * err


def _adamw(w, g, m, v):
    m = ADAM_B1 * m + (1.0 - ADAM_B1) * g
    v = ADAM_B2 * v + (1.0 - ADAM_B2) * _jnp.square(g)
    m_hat = m / (1.0 - ADAM_B1 ** ADAM_STEP)
    v_hat = v / (1.0 - ADAM_B2 ** ADAM_STEP)
    delta = -ADAM_LR * (m_hat / (_jnp.sqrt(v_hat) + ADAM_EPS) + ADAM_WD * w)
    return delta, m, v


def reference(x, p, positions, norm_mix, w_in, w_pool, pool_scale, q_norm, kv_norm, w_uq, w_ukv, w_a, w_b, w_o, norm_ffn, w_gate, w_up, w_down, norm_ple, w_ple_gate, w_ple, final_norm, loss_target, m_norm_mix, m_w_in, m_w_pool, m_pool_scale, m_q_norm, m_kv_norm, m_w_uq, m_w_ukv, m_w_a, m_w_b, m_w_o, m_norm_ffn, m_w_gate, m_w_up, m_w_down, m_norm_ple, m_w_ple_gate, m_w_ple, m_final_norm, v_norm_mix, v_w_in, v_w_pool, v_pool_scale, v_q_norm, v_kv_norm, v_w_uq, v_w_ukv, v_w_a, v_w_b, v_w_o, v_norm_ffn, v_w_gate, v_w_up, v_w_down, v_norm_ple, v_w_ple_gate, v_w_ple, v_final_norm):
    given = dict(x=x, p=p, positions=positions, norm_mix=norm_mix, w_in=w_in, w_pool=w_pool, pool_scale=pool_scale, q_norm=q_norm, kv_norm=kv_norm, w_uq=w_uq, w_ukv=w_ukv, w_a=w_a, w_b=w_b, w_o=w_o, norm_ffn=norm_ffn, w_gate=w_gate, w_up=w_up, w_down=w_down, norm_ple=norm_ple, w_ple_gate=w_ple_gate, w_ple=w_ple, final_norm=final_norm, loss_target=loss_target, m_norm_mix=m_norm_mix, m_w_in=m_w_in, m_w_pool=m_w_pool, m_pool_scale=m_pool_scale, m_q_norm=m_q_norm, m_kv_norm=m_kv_norm, m_w_uq=m_w_uq, m_w_ukv=m_w_ukv, m_w_a=m_w_a, m_w_b=m_w_b, m_w_o=m_w_o, m_norm_ffn=m_norm_ffn, m_w_gate=m_w_gate, m_w_up=m_w_up, m_w_down=m_w_down, m_norm_ple=m_norm_ple, m_w_ple_gate=m_w_ple_gate, m_w_ple=m_w_ple, m_final_norm=m_final_norm, v_norm_mix=v_norm_mix, v_w_in=v_w_in, v_w_pool=v_w_pool, v_pool_scale=v_pool_scale, v_q_norm=v_q_norm, v_kv_norm=v_kv_norm, v_w_uq=v_w_uq, v_w_ukv=v_w_ukv, v_w_a=v_w_a, v_w_b=v_w_b, v_w_o=v_w_o, v_norm_ffn=v_norm_ffn, v_w_gate=v_w_gate, v_w_up=v_w_up, v_w_down=v_w_down, v_norm_ple=v_norm_ple, v_w_ple_gate=v_w_ple_gate, v_w_ple=v_w_ple, v_final_norm=v_final_norm)
    weights = {n: given[n] for n in TWIN_WEIGHTS}
    shared = {n: given[n] for n in SHARED_INPUTS}
    per_example = {n: given[n] for n in ['x', 'p', 'positions']}
    grad_fn = _jax.value_and_grad(_loss, argnums=(0, 1))

    def one_microbatch(ex, loss_target):
        ex = dict(ex)
        diff = ex.pop(TWIN_DIFF_INPUT)
        return grad_fn(weights, diff, {**shared, **ex}, loss_target)

    if N_MICROBATCH == 1:
        loss, (grad_w, grad_x) = one_microbatch(per_example, given["loss_target"])
    else:
        def body(carry, xs):
            loss_sum, grad_sum = carry
            l_k, (gw_k, gx_k) = one_microbatch(xs[0], xs[1])
            with _jax.named_scope("update"):
                return (loss_sum + l_k, _jax.tree.map(_jnp.add, grad_sum, gw_k)), gx_k

        init = (_jnp.zeros((), _jnp.float32), _jax.tree.map(_jnp.zeros_like, weights))
        (loss, grad_w), grad_x = _jax.lax.scan(body, init, (per_example, given["loss_target"]))
    with _jax.named_scope("update"):
        delta_w, new_m, new_v = {}, {}, {}
        for n in TWIN_WEIGHTS:
            delta_w[n], new_m[n], new_v[n] = _adamw(weights[n], grad_w[n], given["m_" + n], given["v_" + n])
    return (loss, grad_x, *[grad_w[n] for n in TWIN_WEIGHTS], *[delta_w[n] for n in TWIN_WEIGHTS],
            *[new_m[n] for n in TWIN_WEIGHTS], *[new_v[n] for n in TWIN_WEIGHTS])
```

```python
import functools

import numpy as np
import jax
import jax.numpy as jnp
from jax import lax
from jax.experimental import pallas as pl
from jax.experimental.pallas import tpu as pltpu

F32 = jnp.float32
BF16 = jnp.bfloat16

D_MODEL = 1024
DEPTH = 2
PLE_DIM = 256
POOL_WINDOWS = (2, 4, 8, 16)
POOL_GROUP = 128
POOL_WIDTH = 512
N_HEADS = 8
Q_LORA = 512
KV_LORA = 256
QK_NOPE = 128
QK_ROPE = 64
QK_HEAD = 192
V_HEAD = 128
D_FF = 2816
ROPE_THETA = 10000.0
EPS = 1e-6
ATTN_SCALE = QK_HEAD ** -0.5

ADAM_LR = 0.001
ADAM_B1 = 0.9
ADAM_B2 = 0.999
ADAM_EPS = 1e-08
ADAM_WD = 0.01
ADAM_STEP = 10

LANES = 128
HALO = 16
HEAD_PAD = 256
V7X_VMEM_BYTES = 64 * 1024 * 1024
VMEM_LIMIT = (V7X_VMEM_BYTES * 3) // 4
N_CHIPS = 4
N_DEV = 8
NEG_INF = -1e30

ZC_GA, ZC_GB, ZC_U, ZC_CQ, ZC_CKV, ZC_KR = 0, 1024, 2048, 2560, 3072, 3328
Z_WIDTH = 3456

WEIGHTS = ['norm_mix', 'w_in', 'w_pool', 'pool_scale', 'q_norm', 'kv_norm', 'w_uq', 'w_ukv', 'w_a', 'w_b', 'w_o',
           'norm_ffn', 'w_gate', 'w_up', 'w_down', 'norm_ple', 'w_ple_gate', 'w_ple', 'final_norm']
SHARDED = {'w_in': 2, 'w_uq': 1, 'w_ukv': 1, 'w_a': 2, 'w_b': 1, 'w_o': 1, 'w_gate': 2, 'w_up': 2, 'w_down': 1,
           'w_ple_gate': 1, 'w_ple': 2}
REPLICATED = [n for n in WEIGHTS if n not in SHARDED]


def _tile(n, target, mult=LANES):
    if n <= target:
        return n
    best = None
    for t in range(mult, target + 1, mult):
        if n % t == 0:
            best = t
    assert best is not None, (n, target)
    return best


def _cparams(*sem):
    return pltpu.CompilerParams(dimension_semantics=sem, vmem_limit_bytes=VMEM_LIMIT)


def _rope(t, cos_t, sin_t):
    return t * cos_t + pltpu.roll(t, 64, 1) * sin_t


def _rope_bwd(d, cos_t, sin_t):
    return d * cos_t + pltpu.roll(d * sin_t, 64, 1)


def _sigmoid(v):
    return 1.0 / (1.0 + jnp.exp(-v))


def _mm_nn(a, b, *, name, out_dtype, a_col=0, gain=None, emit_a=False, res=None, epi=None, epi_rows=(),
           tm=512, tn=512):
    M = a.shape[0]
    K, N = b.shape
    tm = min(tm, M)
    tn = _tile(N, tn)
    assert a_col % K == 0 and M % tm == 0
    a_blk = a_col // K
    stage = gain is not None or a.dtype != BF16 or emit_a
    n_rows = len(epi_rows)

    def body(*refs):
        refs = list(refs)
        a_ref, b_ref = refs[0], refs[1]
        pos = 2
        g_ref = res_ref = None
        if gain is not None:
            g_ref = refs[pos]; pos += 1
        if res is not None:
            res_ref = refs[pos]; pos += 1
        row_refs = refs[pos:pos + n_rows]; pos += n_rows
        o_ref = refs[pos]; pos += 1
        ah_ref = None
        if emit_a:
            ah_ref = refs[pos]; pos += 1
        abf = refs[pos] if stage else None

        if stage:
            @pl.when(pl.program_id(1) == 0)
            def _():
                av = a_ref[...].astype(F32)
                if gain is not None:
                    av = av * lax.rsqrt(jnp.mean(av * av, axis=-1, keepdims=True) + EPS) * g_ref[...]
                abf[...] = av.astype(BF16)
                if emit_a:
                    ah_ref[...] = abf[...]
            lhs = abf[...]
        else:
            lhs = a_ref[...]
        acc = jnp.dot(lhs, b_ref[...], preferred_element_type=F32)
        if res is not None:
            acc = acc + res_ref[...]
        if epi is not None:
            acc = epi(acc, *[r[...] for r in row_refs])
        o_ref[...] = acc.astype(o_ref.dtype)

    in_specs = [pl.BlockSpec((tm, K), lambda i, j: (i, a_blk)), pl.BlockSpec((K, tn), lambda i, j: (0, j))]
    args = [a, b]
    if gain is not None:
        in_specs.append(pl.BlockSpec((1, K), lambda i, j: (0, 0)))
        args.append(gain.reshape(1, K).astype(F32))
    if res is not None:
        in_specs.append(pl.BlockSpec((tm, tn), lambda i, j: (i, j)))
        args.append(res)
    for arr, w, blk in epi_rows:
        in_specs.append(pl.BlockSpec((tm, w), lambda i, j, blk=blk: (i, blk)))
        args.append(arr)
    out_shape = [jax.ShapeDtypeStruct((M, N), out_dtype)]
    out_specs = [pl.BlockSpec((tm, tn), lambda i, j: (i, j))]
    if emit_a:
        out_shape.append(jax.ShapeDtypeStruct((M, K), BF16))
        out_specs.append(pl.BlockSpec((tm, K), lambda i, j: (i, 0)))
    scratch = [pltpu.VMEM((tm, K), BF16)] if stage else []
    outs = pl.pallas_call(body, grid=(M // tm, N // tn), in_specs=in_specs, out_specs=out_specs,
                          out_shape=out_shape, scratch_shapes=scratch, name=name,
                          compiler_params=_cparams("parallel", "arbitrary"))(*args)
    return outs if emit_a else outs[0]


def _mm_nt(pairs, *, name, out_dtype, tm=512):
    M = pairs[0][0].shape[0]
    N = pairs[0][1].shape[0]
    tm = min(tm, M)
    n_p = len(pairs)

    def body(*refs):
        o_ref = refs[2 * n_p]
        acc = None
        for k in range(n_p):
            av = refs[2 * k][...].astype(BF16)
            part = lax.dot_general(av, refs[2 * k + 1][...], (((1,), (1,)), ((), ())), preferred_element_type=F32)
            acc = part if acc is None else acc + part
        o_ref[...] = acc.astype(o_ref.dtype)

    in_specs, args = [], []
    for a, b in pairs:
        assert a.shape[1] == b.shape[1] and b.shape[0] == N and a.shape[0] == M
        in_specs.append(pl.BlockSpec((tm, a.shape[1]), lambda i: (i, 0)))
        in_specs.append(pl.BlockSpec(b.shape, lambda i: (0, 0)))
        args += [a, b]
    return pl.pallas_call(body, grid=(M // tm,), in_specs=in_specs,
                          out_specs=pl.BlockSpec((tm, N), lambda i: (i, 0)),
                          out_shape=jax.ShapeDtypeStruct((M, N), out_dtype), name=name,
                          compiler_params=_cparams("parallel"))(*args)


def _mm_tn(a, b, *, name, a_col=0, a_w=None, tk=1024, tn=1152, tm=512):
    M = a.shape[0]
    a_w = a.shape[1] if a_w is None else a_w
    N = b.shape[1]
    tm = min(tm, M)
    tk = _tile(a_w, tk)
    tn = _tile(N, tn)
    assert a_col % tk == 0 and M % tm == 0
    a_blk0 = a_col // tk

    def body(a_ref, b_ref, o_ref):
        @pl.when(pl.program_id(2) == 0)
        def _():
            o_ref[...] = jnp.zeros_like(o_ref)
        o_ref[...] += lax.dot_general(a_ref[...].astype(BF16), b_ref[...].astype(BF16), (((0,), (0,)), ((), ())),
                                      preferred_element_type=F32)

    return pl.pallas_call(body, grid=(a_w // tk, N // tn, M // tm),
                          in_specs=[pl.BlockSpec((tm, tk), lambda k, j, m: (m, k + a_blk0)),
                                    pl.BlockSpec((tm, tn), lambda k, j, m: (m, j))],
                          out_specs=pl.BlockSpec((tk, tn), lambda k, j, m: (k, j)),
                          out_shape=jax.ShapeDtypeStruct((a_w, N), F32), name=name,
                          compiler_params=_cparams("parallel", "parallel", "arbitrary"))(a, b)


def _row_call(fn, rows, tr, ins, outs, accs=(), *, name):
    n_in, n_out, n_acc = len(ins), len(outs), len(accs)

    def body(*refs):
        i = pl.program_id(0)
        vals = fn(i, *[r[...] for r in refs[:n_in]])
        if not isinstance(vals, (tuple, list)):
            vals = (vals,)
        for r, v in zip(refs[n_in:n_in + n_out], vals[:n_out]):
            r[...] = v.astype(r.dtype)
        if n_acc:
            acc_refs = refs[n_in + n_out:]

            @pl.when(i == 0)
            def _():
                for r in acc_refs:
                    r[...] = jnp.zeros_like(r)
            for r, v in zip(acc_refs, vals[n_out:]):
                r[...] += v

    out_shape = [jax.ShapeDtypeStruct((rows, w), dt) for w, dt in outs]
    out_specs = [pl.BlockSpec((tr, w), lambda i: (i, 0)) for w, dt in outs]
    for s in accs:
        out_shape.append(jax.ShapeDtypeStruct(s, F32))
        out_specs.append(pl.BlockSpec(s, lambda i, n=len(s): (0,) * n))
    res = pl.pallas_call(body, grid=(rows // tr,), in_specs=[pl.BlockSpec(bs, im) for _, bs, im in ins],
                         out_specs=out_specs, out_shape=out_shape, name=name,
                         compiler_params=_cparams("arbitrary"))(*[a for a, _, _ in ins])
    return res


def _wide_rows(tr):
    return max(tr // 4, 8)


def _rspec(arr, tr, w=None, blk=0):
    w = arr.shape[1] if w is None else w
    return (arr, (tr, w), lambda i, blk=blk: (i, blk))


def _bspec(arr):
    return (arr, arr.shape, lambda i, n=arr.ndim: (0,) * n)


def _rms_bwd_rows(x, x_col, gain, dh, dres, *, name, tr, emit_bf16=True):
    rows = x.shape[0]
    W = gain.shape[-1]
    g2 = gain.reshape(1, W).astype(F32)

    def fn(i, xv, gv, dhv, *rest):
        xv = xv.astype(F32)
        dhv = dhv.astype(F32)
        rstd = lax.rsqrt(jnp.mean(xv * xv, axis=-1, keepdims=True) + EPS)
        xhat = xv * rstd
        dg = jnp.sum(dhv * xhat, axis=0, keepdims=True)
        dxh = dhv * gv
        dx = rstd * (dxh - xhat * jnp.mean(dxh * xhat, axis=-1, keepdims=True))
        if rest:
            dx = dx + rest[0].astype(F32)
        return (dx, dx, dg) if emit_bf16 else (dx, dg)

    ins = [_rspec(x, tr, W, x_col // W), _bspec(g2), _rspec(dh, tr)]
    if dres is not None:
        ins.append(_rspec(dres, tr))
    outs = [(W, F32), (W, BF16)] if emit_bf16 else [(W, F32)]
    return _row_call(fn, rows, tr, ins, outs, [(1, W)], name=name)


def _pool_counts(i, tr):
    t = (i * tr + lax.broadcasted_iota(jnp.int32, (tr, 1), 0) + 1).astype(F32)
    return [jnp.minimum(t, float(w)) for w in POOL_WINDOWS]


def _pool_fwd(z, w_pool_bf, pool_scale, *, name, tr):
    rows = z.shape[0]
    ublk = ZC_U // POOL_WIDTH
    hpt = tr // HALO

    def fn(i, u, uprev, wp, ps):
        uprev = jnp.where(i > 0, uprev, 0.0)
        ext = jnp.concatenate([uprev, u], axis=0)
        s2 = ext + pltpu.roll(ext, 1, 0)
        s4 = s2 + pltpu.roll(s2, 2, 0)
        s8 = s4 + pltpu.roll(s4, 4, 0)
        s16 = s8 + pltpu.roll(s8, 8, 0)
        cnts = _pool_counts(i, tr)
        pooled, mixed = [], []
        for g, sw in enumerate((s2, s4, s8, s16)):
            lanes = slice(g * POOL_GROUP, (g + 1) * POOL_GROUP)
            pg = sw[HALO:, lanes] / cnts[g] - u[:, lanes]
            pooled.append(pg)
            mixed.append(jnp.dot(pg.astype(BF16), wp[g], preferred_element_type=F32))
        pooled = jnp.concatenate(pooled, axis=1)
        mixed = jnp.concatenate(mixed, axis=1)
        return pooled, mixed, mixed * ps

    ins = [_rspec(z, tr, POOL_WIDTH, ublk),
           (z, (HALO, POOL_WIDTH), lambda i: (jnp.maximum(i * hpt - 1, 0), ublk)),
           _bspec(w_pool_bf), _bspec(pool_scale.reshape(1, POOL_WIDTH))]
    return _row_call(fn, rows, tr, ins, [(POOL_WIDTH, BF16), (POOL_WIDTH, F32), (POOL_WIDTH, BF16)], name=name)


def _pool_bwd_mix(dpm, mixed, pooled, w_pool_bf, pool_scale, *, name, tr):
    rows = dpm.shape[0]

    def fn(i, dv, mv, pv, wp, ps):
        dv = dv.astype(F32)
        dscale = jnp.sum(dv * mv, axis=0, keepdims=True)
        dmix = (dv * ps).astype(BF16)
        cnts = _pool_counts(i, tr)
        dpool, dwp = [], []
        for g in range(len(POOL_WINDOWS)):
            lanes = slice(g * POOL_GROUP, (g + 1) * POOL_GROUP)
            dg = lax.dot_general(dmix[:, lanes], wp[g], (((1,), (1,)), ((), ())), preferred_element_type=F32)
            dpool.append(dg)
            dwp.append(lax.dot_general(pv[:, lanes], dmix[:, lanes], (((0,), (0,)), ((), ())),
                                       preferred_element_type=F32)[None])
        dpool = jnp.concatenate(dpool, axis=1)
        dpool_cnt = jnp.concatenate([dpool[:, g * POOL_GROUP:(g + 1) * POOL_GROUP] / cnts[g]
                                     for g in range(len(POOL_WINDOWS))], axis=1)
        return dpool, dpool_cnt, dscale, jnp.concatenate(dwp, axis=0)

    ins = [_rspec(dpm, tr), _rspec(mixed, tr), _rspec(pooled, tr), _bspec(w_pool_bf),
           _bspec(pool_scale.reshape(1, POOL_WIDTH))]
    return _row_call(fn, rows, tr, ins, [(POOL_WIDTH, F32), (POOL_WIDTH, F32)],
                     [(1, POOL_WIDTH), (len(POOL_WINDOWS), POOL_GROUP, POOL_GROUP)], name=name)


def _pool_bwd_window(dpool, dpool_cnt, *, name, tr):
    rows = dpool.shape[0]
    hpt = tr // HALO
    n_halo = rows // HALO
    n_tiles = rows // tr

    def fn(i, dp, dc, dnext):
        dnext = jnp.where(i < n_tiles - 1, dnext, 0.0)
        ext = jnp.concatenate([dc, dnext], axis=0)
        n = tr + HALO
        s2 = ext + pltpu.roll(ext, n - 1, 0)
        s4 = s2 + pltpu.roll(s2, n - 2, 0)
        s8 = s4 + pltpu.roll(s4, n - 4, 0)
        s16 = s8 + pltpu.roll(s8, n - 8, 0)
        out = []
        for g, sw in enumerate((s2, s4, s8, s16)):
            lanes = slice(g * POOL_GROUP, (g + 1) * POOL_GROUP)
            out.append(sw[:tr, lanes] - dp[:, lanes])
        return jnp.concatenate(out, axis=1)

    ins = [_rspec(dpool, tr), _rspec(dpool_cnt, tr),
           (dpool_cnt, (HALO, POOL_WIDTH), lambda i: (jnp.minimum((i + 1) * hpt, n_halo - 1), 0))]
    return _row_call(fn, rows, tr, ins, [(POOL_WIDTH, BF16)], name=name)[0]


def _causal_pairs(n, k_major):
    if k_major:
        pairs = [(qi, ki) for ki in range(n) for qi in range(ki, n)]
    else:
        pairs = [(qi, ki) for qi in range(n) for ki in range(qi + 1)]
    return (jnp.asarray(np.array([p[0] for p in pairs], np.int32)),
            jnp.asarray(np.array([p[1] for p in pairs], np.int32)), len(pairs))


def _masked_scores(q, k, qi, ki, blk):
    s = lax.dot_general(q, k, (((1,), (1,)), ((), ())), preferred_element_type=F32) * ATTN_SCALE

    def mask(sv):
        r = lax.broadcasted_iota(jnp.int32, (blk, blk), 0)
        c = lax.broadcasted_iota(jnp.int32, (blk, blk), 1)
        return jnp.where(c <= r, sv, NEG_INF)

    return lax.cond(qi == ki, mask, lambda sv: sv, s)


def _flash_fwd(q, k, v, *, name, blk):
    T = q.shape[0]
    n = T // blk
    qtab, ktab, n_pairs = _causal_pairs(n, k_major=False)

    def body(qt, kt, q_ref, k_ref, v_ref, o_ref, lse_ref, m_s, l_s, acc_s):
        p = pl.program_id(1)
        qi, ki = qt[p], kt[p]

        @pl.when(ki == 0)
        def _():
            m_s[...] = jnp.full_like(m_s, NEG_INF)
            l_s[...] = jnp.zeros_like(l_s)
            acc_s[...] = jnp.zeros_like(acc_s)

        s = _masked_scores(q_ref[...], k_ref[...], qi, ki, blk)
        m_prev = m_s[...]
        m_new = jnp.maximum(m_prev, jnp.max(s, axis=-1, keepdims=True))
        pr = jnp.exp(s - m_new)
        alpha = jnp.exp(m_prev - m_new)
        l_s[...] = alpha * l_s[...] + jnp.sum(pr, axis=-1, keepdims=True)
        acc_s[...] = alpha * acc_s[...] + jnp.dot(pr.astype(BF16), v_ref[...], preferred_element_type=F32)
        m_s[...] = m_new

        @pl.when(ki == qi)
        def _():
            o_ref[...] = (acc_s[...] / l_s[...]).astype(o_ref.dtype)
            lse_ref[...] = jnp.broadcast_to(m_s[...] + jnp.log(l_s[...]), lse_ref.shape)

    grid_spec = pltpu.PrefetchScalarGridSpec(
        num_scalar_prefetch=2, grid=(N_HEADS, n_pairs),
        in_specs=[pl.BlockSpec((blk, HEAD_PAD), lambda h, p, qt, kt: (qt[p], h)),
                  pl.BlockSpec((blk, HEAD_PAD), lambda h, p, qt, kt: (kt[p], h)),
                  pl.BlockSpec((blk, V_HEAD), lambda h, p, qt, kt: (kt[p], h))],
        out_specs=[pl.BlockSpec((blk, V_HEAD), lambda h, p, qt, kt: (qt[p], h)),
                   pl.BlockSpec((blk, LANES), lambda h, p, qt, kt: (qt[p], h))],
        scratch_shapes=[pltpu.VMEM((blk, 1), F32), pltpu.VMEM((blk, 1), F32), pltpu.VMEM((blk, V_HEAD), F32)])
    return pl.pallas_call(body, grid_spec=grid_spec,
                          out_shape=[jax.ShapeDtypeStruct((T, N_HEADS * V_HEAD), BF16),
                                     jax.ShapeDtypeStruct((T, N_HEADS * LANES), F32)],
                          name=name, compiler_params=_cparams("parallel", "arbitrary"))(qtab, ktab, q, k, v)


def _flash_bwd(q, k, v, o, lse, do, *, name, blk):
    T = q.shape[0]
    n = T // blk
    qtab, ktab, n_pairs = _causal_pairs(n, k_major=True)

    def body(qt, kt, q_ref, k_ref, v_ref, o_ref, lse_ref, do_ref, dq_ref, dk_ref, dv_ref, dq_s, dk_s, dv_s):
        p = pl.program_id(1)
        qi, ki = qt[p], kt[p]
        first = qi == ki

        @pl.when(p == 0)
        def _():
            dq_s[...] = jnp.zeros_like(dq_s)

        @pl.when(first)
        def _():
            dk_s[...] = jnp.zeros_like(dk_s)
            dv_s[...] = jnp.zeros_like(dv_s)

        qv, kv, dov = q_ref[...], k_ref[...], do_ref[...]
        s = _masked_scores(qv, kv, qi, ki, blk)
        pr = jnp.exp(s - lse_ref[...][:, :1])
        dp = lax.dot_general(dov, v_ref[...], (((1,), (1,)), ((), ())), preferred_element_type=F32)
        delta = jnp.sum(dov.astype(F32) * o_ref[...].astype(F32), axis=-1, keepdims=True)
        ds = (pr * (dp - delta) * ATTN_SCALE).astype(BF16)
        dv_s[...] += lax.dot_general(pr.astype(BF16), dov, (((0,), (0,)), ((), ())), preferred_element_type=F32)
        dk_s[...] += lax.dot_general(ds, qv, (((0,), (0,)), ((), ())), preferred_element_type=F32)
        rows = pl.ds(pl.multiple_of(qi * blk, blk), blk)
        dq_s[rows, :] += jnp.dot(ds, kv, preferred_element_type=F32)

        @pl.when(first)
        def _():
            dq_ref[...] = dq_s[rows, :].astype(dq_ref.dtype)

        @pl.when(qi == n - 1)
        def _():
            dk_ref[...] = dk_s[...].astype(dk_ref.dtype)
            dv_ref[...] = dv_s[...].astype(dv_ref.dtype)

    qmap = lambda h, p, qt, kt: (qt[p], h)
    kmap = lambda h, p, qt, kt: (kt[p], h)
    grid_spec = pltpu.PrefetchScalarGridSpec(
        num_scalar_prefetch=2, grid=(N_HEADS, n_pairs),
        in_specs=[pl.BlockSpec((blk, HEAD_PAD), qmap), pl.BlockSpec((blk, HEAD_PAD), kmap),
                  pl.BlockSpec((blk, V_HEAD), kmap), pl.BlockSpec((blk, V_HEAD), qmap),
                  pl.BlockSpec((blk, LANES), qmap), pl.BlockSpec((blk, V_HEAD), qmap)],
        out_specs=[pl.BlockSpec((blk, HEAD_PAD), kmap), pl.BlockSpec((blk, HEAD_PAD), kmap),
                   pl.BlockSpec((blk, V_HEAD), kmap)],
        scratch_shapes=[pltpu.VMEM((T, HEAD_PAD), F32), pltpu.VMEM((blk, HEAD_PAD), F32),
                        pltpu.VMEM((blk, V_HEAD), F32)])
    return pl.pallas_call(body, grid_spec=grid_spec,
                          out_shape=[jax.ShapeDtypeStruct((T, N_HEADS * HEAD_PAD), BF16),
                                     jax.ShapeDtypeStruct((T, N_HEADS * HEAD_PAD), BF16),
                                     jax.ShapeDtypeStruct((T, N_HEADS * V_HEAD), BF16)],
                          name=name, compiler_params=_cparams("arbitrary", "arbitrary"))(qtab, ktab, q, k, v, o, lse, do)


MESH_ID = pl.DeviceIdType.MESH
ANY_SPEC = pl.BlockSpec(memory_space=pl.ANY)


def _other_chips(x, y):
    out = []
    for dx, dy in ((1, 0), (0, 1), (1, 1)):
        px = x ^ dx if dx else x
        py = y ^ dy if dy else y
        out.append((px, py, 2 * px + py))
    return out


def _gather_weights(flat):
    rh = flat.shape[0] // 2

    def body(src2, out, send_sems, recv_sems, local_sem):
        x, y, c = lax.axis_index("x"), lax.axis_index("y"), lax.axis_index("c")
        me = 2 * x + y
        sib = (x, y, 1 - c)
        own = pltpu.make_async_copy(src2, out.at[me], local_sem)
        own.start()
        chips = _other_chips(x, y)
        sends = []
        for j, (px, py, pk) in enumerate(chips):
            cp = pltpu.make_async_remote_copy(src_ref=src2.at[c], dst_ref=out.at[me, c], send_sem=send_sems.at[j],
                                              recv_sem=recv_sems.at[j], device_id=(px, py, c), device_id_type=MESH_ID)
            cp.start()
            sends.append(cp)
        for j, (px, py, pk) in enumerate(chips):
            land = out.at[pk, c]
            pltpu.make_async_remote_copy(src_ref=land, dst_ref=land, send_sem=send_sems.at[j], recv_sem=recv_sems.at[j],
                                         device_id=(px, py, c), device_id_type=MESH_ID).wait_recv()
            fw = pltpu.make_async_remote_copy(src_ref=land, dst_ref=land, send_sem=send_sems.at[3 + j],
                                              recv_sem=recv_sems.at[3 + j], device_id=sib, device_id_type=MESH_ID)
            fw.start()
            sends.append(fw)
        for j, (px, py, pk) in enumerate(chips):
            land = out.at[pk, 1 - c]
            pltpu.make_async_remote_copy(src_ref=land, dst_ref=land, send_sem=send_sems.at[3 + j],
                                         recv_sem=recv_sems.at[3 + j], device_id=sib, device_id_type=MESH_ID).wait_recv()
        for cp in sends:
            cp.wait_send()
        own.wait()

    return pl.pallas_call(body, out_shape=jax.ShapeDtypeStruct((N_CHIPS, 2, rh, LANES), flat.dtype),
                          in_specs=[ANY_SPEC], out_specs=ANY_SPEC,
                          scratch_shapes=[pltpu.SemaphoreType.DMA((6,)), pltpu.SemaphoreType.DMA((6,)),
                                          pltpu.SemaphoreType.DMA(())],
                          name="gather_weights")(flat.reshape(2, rh, LANES))


def _swap_halves(g):
    rh = g.shape[2]

    def body(src, out, send_sem, recv_sem):
        x, y, c = lax.axis_index("x"), lax.axis_index("y"), lax.axis_index("c")
        cp = pltpu.make_async_remote_copy(src_ref=src.at[:, 1 - c], dst_ref=out, send_sem=send_sem, recv_sem=recv_sem,
                                          device_id=(x, y, 1 - c), device_id_type=MESH_ID)
        cp.start()
        cp.wait()

    return pl.pallas_call(body, out_shape=jax.ShapeDtypeStruct((N_CHIPS, rh, LANES), g.dtype),
                          in_specs=[ANY_SPEC], out_specs=ANY_SPEC,
                          scratch_shapes=[pltpu.SemaphoreType.DMA(()), pltpu.SemaphoreType.DMA(())],
                          name="grad_swap_halves")(g)


def _add_halves(g, got, *, tr):
    rh = g.shape[2]
    nt = rh // tr

    def body(g_ref, got_ref, o_ref):
        o_ref[...] = g_ref[0] + got_ref[...]

    return pl.pallas_call(body, grid=(N_CHIPS, nt),
                          in_specs=[pl.BlockSpec((1, 1, tr, LANES), lambda k, i: (k, lax.axis_index("c"), i, 0)),
                                    pl.BlockSpec((1, tr, LANES), lambda k, i: (k, i, 0))],
                          out_specs=pl.BlockSpec((1, tr, LANES), lambda k, i: (k, i, 0)),
                          out_shape=jax.ShapeDtypeStruct((N_CHIPS, rh, LANES), F32), name="grad_add_halves",
                          compiler_params=_cparams("parallel", "parallel"))(g, got)


def _scatter_partials(part):
    rh = part.shape[1]

    def body(src, out, send_sems, recv_sems, local_sem):
        x, y, c = lax.axis_index("x"), lax.axis_index("y"), lax.axis_index("c")
        me = 2 * x + y
        own = pltpu.make_async_copy(src.at[me], out.at[me], local_sem)
        own.start()
        chips = _other_chips(x, y)
        sends = []
        for j, (px, py, pk) in enumerate(chips):
            cp = pltpu.make_async_remote_copy(src_ref=src.at[pk], dst_ref=out.at[me], send_sem=send_sems.at[j],
                                              recv_sem=recv_sems.at[j], device_id=(px, py, c), device_id_type=MESH_ID)
            cp.start()
            sends.append(cp)
        for j, (px, py, pk) in enumerate(chips):
            land = out.at[pk]
            pltpu.make_async_remote_copy(src_ref=land, dst_ref=land, send_sem=send_sems.at[j], recv_sem=recv_sems.at[j],
                                         device_id=(px, py, c), device_id_type=MESH_ID).wait_recv()
        for cp in sends:
            cp.wait_send()
        own.wait()

    return pl.pallas_call(body, out_shape=jax.ShapeDtypeStruct((N_CHIPS, rh, LANES), part.dtype),
                          in_specs=[ANY_SPEC], out_specs=ANY_SPEC,
                          scratch_shapes=[pltpu.SemaphoreType.DMA((3,)), pltpu.SemaphoreType.DMA((3,)),
                                          pltpu.SemaphoreType.DMA(())],
                          name="grad_scatter_partials")(part)


def _sum_chips(q, *, tr):
    rh = q.shape[1]

    def body(q_ref, o_ref):
        o_ref[...] = ((q_ref[0] + q_ref[1]) + q_ref[2]) + q_ref[3]

    return pl.pallas_call(body, grid=(rh // tr,),
                          in_specs=[pl.BlockSpec((N_CHIPS, tr, LANES), lambda i: (0, i, 0))],
                          out_specs=pl.BlockSpec((tr, LANES), lambda i: (i, 0)),
                          out_shape=jax.ShapeDtypeStruct((rh, LANES), F32), name="grad_sum_chips",
                          compiler_params=_cparams("parallel"))(q)


def _join_halves(half):
    rh = half.shape[0]

    def body(src, out, send_sem, recv_sem, local_sem):
        x, y, c = lax.axis_index("x"), lax.axis_index("y"), lax.axis_index("c")
        own = pltpu.make_async_copy(src, out.at[c], local_sem)
        own.start()
        cp = pltpu.make_async_remote_copy(src_ref=src, dst_ref=out.at[c], send_sem=send_sem, recv_sem=recv_sem,
                                          device_id=(x, y, 1 - c), device_id_type=MESH_ID)
        cp.start()
        cp.wait_send()
        land = out.at[1 - c]
        pltpu.make_async_remote_copy(src_ref=land, dst_ref=land, send_sem=send_sem, recv_sem=recv_sem,
                                     device_id=(x, y, 1 - c), device_id_type=MESH_ID).wait_recv()
        own.wait()

    return pl.pallas_call(body, out_shape=jax.ShapeDtypeStruct((2, rh, LANES), half.dtype),
                          in_specs=[ANY_SPEC], out_specs=ANY_SPEC,
                          scratch_shapes=[pltpu.SemaphoreType.DMA(()), pltpu.SemaphoreType.DMA(()),
                                          pltpu.SemaphoreType.DMA(())],
                          name="grad_join_halves")(half)


def _allreduce_small(v):
    rows = v.shape[0]

    def body(v_ref, o_ref, buf, send_sems, recv_sems):
        x, y, c = lax.axis_index("x"), lax.axis_index("y"), lax.axis_index("c")
        me = 4 * x + 2 * y + c
        buf[me] = v_ref[...]
        sends = []
        for j in range(1, N_DEV):
            px, py, pc = x ^ ((j >> 2) & 1), y ^ ((j >> 1) & 1), c ^ (j & 1)
            cp = pltpu.make_async_remote_copy(src_ref=v_ref, dst_ref=buf.at[me], send_sem=send_sems.at[j - 1],
                                              recv_sem=recv_sems.at[j - 1], device_id=(px, py, pc), device_id_type=MESH_ID)
            cp.start()
            sends.append(cp)
        for j in range(1, N_DEV):
            px, py, pc = x ^ ((j >> 2) & 1), y ^ ((j >> 1) & 1), c ^ (j & 1)
            land = buf.at[4 * px + 2 * py + pc]
            pltpu.make_async_remote_copy(src_ref=land, dst_ref=land, send_sem=send_sems.at[j - 1],
                                         recv_sem=recv_sems.at[j - 1], device_id=(px, py, pc),
                                         device_id_type=MESH_ID).wait_recv()
        for cp in sends:
            cp.wait_send()
        acc = buf[0]
        for d in range(1, N_DEV):
            acc = acc + buf[d]
        o_ref[...] = acc

    vm = pl.BlockSpec(memory_space=pltpu.VMEM)
    return pl.pallas_call(body, out_shape=jax.ShapeDtypeStruct((rows, LANES), F32), in_specs=[vm], out_specs=vm,
                          scratch_shapes=[pltpu.VMEM((N_DEV, rows, LANES), F32), pltpu.SemaphoreType.DMA((N_DEV - 1,)),
                                          pltpu.SemaphoreType.DMA((N_DEV - 1,))],
                          name="allreduce_small")(v)


def _adamw(w, g, m, v, *, name):
    shape = w.shape
    cols = shape[-1] if w.ndim > 1 else shape[0]
    rows = w.size // cols
    w2, g2, m2, v2 = (t.reshape(rows, cols) for t in (w, g, m, v))
    tr = rows if rows <= 256 else _tile(rows, 256, 8)

    def fn(i, wv, gv, mv, vv):
        mn = ADAM_B1 * mv + (1.0 - ADAM_B1) * gv
        vn = ADAM_B2 * vv + (1.0 - ADAM_B2) * (gv * gv)
        m_hat = mn / (1.0 - ADAM_B1 ** ADAM_STEP)
        v_hat = vn / (1.0 - ADAM_B2 ** ADAM_STEP)
        delta = -ADAM_LR * (m_hat / (jnp.sqrt(v_hat) + ADAM_EPS) + ADAM_WD * wv)
        return delta, mn, vn

    ins = [_rspec(t, tr) for t in (w2, g2, m2, v2)]
    d, mn, vn = _row_call(fn, rows, tr, ins, [(cols, F32)] * 3, name=name)
    return d.reshape(shape), mn.reshape(shape), vn.reshape(shape)


def _rope_cols(w):
    z = jnp.zeros(w.shape[:-1] + (32,), w.dtype)
    return jnp.concatenate([w[..., :32], z, w[..., 32:], z], axis=-1)


def _rope_cols_inv(w):
    return jnp.concatenate([w[..., :32], w[..., 64:96]], axis=-1)


def _layer_layouts(W, i):
    w_in = W['w_in'][i]
    u, cq, ckv = w_in[:, :512], w_in[:, 512:1024], w_in[:, 1024:1280]
    kr, ga, gb = w_in[:, 1280:1344], w_in[:, 1344:2368], w_in[:, 2368:]
    L = {}
    L['w_in'] = jnp.concatenate([ga, gb, u, cq, ckv, _rope_cols(kr)], axis=1)
    wq = W['w_uq'][i]
    L['w_q'] = jnp.concatenate([wq[..., :QK_NOPE], _rope_cols(wq[..., QK_NOPE:])], axis=-1).reshape(Q_LORA, -1)
    wkv = W['w_ukv'][i]
    L['w_k'] = jnp.concatenate([wkv[..., :QK_NOPE], jnp.zeros_like(wkv[..., :LANES])], axis=-1).reshape(KV_LORA, -1)
    L['w_v'] = wkv[..., QK_NOPE:].reshape(KV_LORA, -1)
    L['w_gu'] = jnp.concatenate([W['w_gate'][i], W['w_up'][i]], axis=1)
    for n in ('w_a', 'w_b', 'w_o', 'w_down', 'w_ple_gate', 'w_ple'):
        L[n] = W[n][i]
    return L


def _layer_grads_to_reference_layout(G):
    d = G['w_in']
    ga, gb, u = d[:, ZC_GA:ZC_GB], d[:, ZC_GB:ZC_U], d[:, ZC_U:ZC_CQ]
    cq, ckv, kr = d[:, ZC_CQ:ZC_CKV], d[:, ZC_CKV:ZC_KR], _rope_cols_inv(d[:, ZC_KR:])
    out = {'w_in': jnp.concatenate([u, cq, ckv, kr, ga, gb], axis=1)}
    dq = G['w_q'].reshape(Q_LORA, N_HEADS, HEAD_PAD)
    out['w_uq'] = jnp.concatenate([dq[..., :QK_NOPE], _rope_cols_inv(dq[..., QK_NOPE:])], axis=-1)
    dk = G['w_k'].reshape(KV_LORA, N_HEADS, HEAD_PAD)[..., :QK_NOPE]
    dv = G['w_v'].reshape(KV_LORA, N_HEADS, V_HEAD)
    out['w_ukv'] = jnp.concatenate([dk, dv], axis=-1)
    out['w_gate'], out['w_up'] = G['w_gu'][:, :D_FF], G['w_gu'][:, D_FF:]
    for n in ('w_a', 'w_b', 'w_o', 'w_down', 'w_ple_gate', 'w_ple'):
        out[n] = G[n]
    return out


def _pack_rows(parts, row_mult):
    flat = jnp.concatenate([p.reshape(-1) for p in parts])
    n = flat.shape[0]
    per = LANES * row_mult
    padded = -(-n // per) * per
    return jnp.pad(flat, (0, padded - n)).reshape(-1, LANES)


def _unpack_rows(flat2d, shapes):
    flat = flat2d.reshape(-1)
    out, off = [], 0
    for s in shapes:
        n = int(np.prod(s))
        out.append(flat[off:off + n].reshape(s))
        off += n
    return out


def _layer_fwd(i, x, p_i, L, norms, w_pool_bf, pool_scale, cos_t, sin_t, tr, blk):
    sv = {'x': x}
    z, sv['h'] = _mm_nn(x, L['w_in'], name=f"l{i}_in_proj", out_dtype=F32, gain=norms['norm_mix'], emit_a=True, tn=1152)
    sv['z'] = z
    sv['pooled'], sv['mixed'], sv['pm'] = _pool_fwd(z, w_pool_bf, pool_scale, name=f"l{i}_pool", tr=tr)
    sv['ya'] = _mm_nn(sv['pm'], L['w_a'], name=f"l{i}_ya", out_dtype=BF16, tn=1024)

    def q_epi(acc, ct, st):
        return jnp.concatenate([acc[:, :QK_NOPE], _rope(acc[:, QK_NOPE:], ct, st)], axis=1)

    def k_epi(acc, kr, ct, st):
        return jnp.concatenate([acc[:, :QK_NOPE], _rope(kr, ct, st)], axis=1)

    rope_rows = [(cos_t, LANES, 0), (sin_t, LANES, 0)]
    sv['q'], sv['cqn'] = _mm_nn(z, L['w_q'], name=f"l{i}_q_proj", out_dtype=BF16, a_col=ZC_CQ, gain=norms['q_norm'],
                                emit_a=True, epi=q_epi, epi_rows=rope_rows, tn=HEAD_PAD)
    sv['k'], sv['ckvn'] = _mm_nn(z, L['w_k'], name=f"l{i}_k_proj", out_dtype=BF16, a_col=ZC_CKV, gain=norms['kv_norm'],
                                 emit_a=True, epi=k_epi, epi_rows=[(z, LANES, ZC_KR // LANES)] + rope_rows, tn=HEAD_PAD)
    sv['v'] = _mm_nn(sv['ckvn'], L['w_v'], name=f"l{i}_v_proj", out_dtype=BF16, tn=1024)
    sv['o'], sv['lse'] = _flash_fwd(sv['q'], sv['k'], sv['v'], name=f"l{i}_attn", blk=blk)
    sv['yb'] = _mm_nn(sv['o'], L['w_b'], name=f"l{i}_yb", out_dtype=BF16, tn=1024)

    def merge(_, ga, gb, ya, yb):
        return _sigmoid(ga) * ya.astype(F32) + _sigmoid(gb) * yb.astype(F32)

    T = x.shape[0]
    sv['merged'] = _row_call(merge, T, tr, [_rspec(z, tr, D_MODEL, 0), _rspec(z, tr, D_MODEL, 1), _rspec(sv['ya'], tr),
                                            _rspec(sv['yb'], tr)], [(D_MODEL, BF16)], name=f"l{i}_merge")[0]
    x1 = _mm_nn(sv['merged'], L['w_o'], name=f"l{i}_wo", out_dtype=F32, res=x, tn=1024)
    sv['x1'] = x1
    sv['gu'], sv['h2'] = _mm_nn(x1, L['w_gu'], name=f"l{i}_gate_up", out_dtype=BF16, gain=norms['norm_ffn'], emit_a=True,
                                tn=1408)

    def swiglu(_, g, u):
        g = g.astype(F32)
        return g * _sigmoid(g) * u.astype(F32)

    trw = _wide_rows(tr)
    sv['act'] = _row_call(swiglu, T, trw, [_rspec(sv['gu'], trw, D_FF, 0), _rspec(sv['gu'], trw, D_FF, 1)], [(D_FF, BF16)],
                          name=f"l{i}_swiglu")[0]
    x2 = _mm_nn(sv['act'], L['w_down'], name=f"l{i}_down", out_dtype=F32, res=x1, tn=1024)
    sv['x2'] = x2
    sv['logit'], sv['h3'] = _mm_nn(x2, L['w_ple_gate'], name=f"l{i}_ple_gate", out_dtype=F32, gain=norms['norm_ple'],
                                   emit_a=True, tn=1024)
    sv['pe'] = _mm_nn(p_i, L['w_ple'], name=f"l{i}_ple", out_dtype=F32, tn=1024)

    def ple(_, xv, lg, pe):
        return xv + _sigmoid(lg) * pe

    x3 = _row_call(ple, T, tr, [_rspec(x2, tr), _rspec(sv['logit'], tr), _rspec(sv['pe'], tr)], [(D_MODEL, F32)],
                   name=f"l{i}_ple_add")[0]
    return x3, sv


def _layer_bwd(i, dx3, sv, p_i, L, norms, w_pool_bf, pool_scale, cos_t, sin_t, tr, blk):
    T = dx3.shape[0]
    G = {}
    z = sv['z']

    def ple_bwd(_, d, lg, pe):
        g = _sigmoid(lg)
        return d * pe * g * (1.0 - g), d * g

    dlogit, dpe = _row_call(ple_bwd, T, tr, [_rspec(dx3, tr), _rspec(sv['logit'], tr), _rspec(sv['pe'], tr)],
                            [(D_MODEL, BF16), (D_MODEL, BF16)], name=f"l{i}_ple_bwd")
    G['w_ple_gate'] = _mm_tn(sv['h3'], dlogit, name=f"l{i}_dw_ple_gate", tn=1024)
    G['w_ple'] = _mm_tn(p_i, dpe, name=f"l{i}_dw_ple", tn=1024)
    dh3 = _mm_nt([(dlogit, L['w_ple_gate'])], name=f"l{i}_dh3", out_dtype=F32)
    dx2, dx2_bf, G['norm_ple'] = _rms_bwd_rows(sv['x2'], 0, norms['norm_ple'], dh3, dx3, name=f"l{i}_norm_ple_bwd", tr=tr)

    dact = _mm_nt([(dx2_bf, L['w_down'])], name=f"l{i}_dact", out_dtype=BF16)
    G['w_down'] = _mm_tn(sv['act'], dx2_bf, name=f"l{i}_dw_down", tk=1408, tn=1024)

    def swiglu_bwd(_, da, g, u):
        da, g, u = da.astype(F32), g.astype(F32), u.astype(F32)
        sg = _sigmoid(g)
        dg = da * u * sg * (1.0 + g * (1.0 - sg))
        du = da * g * sg
        return jnp.concatenate([dg, du], axis=1)

    trw = _wide_rows(tr)
    dgu = _row_call(swiglu_bwd, T, trw, [_rspec(dact, trw), _rspec(sv['gu'], trw, D_FF, 0), _rspec(sv['gu'], trw, D_FF, 1)],
                    [(2 * D_FF, BF16)], name=f"l{i}_swiglu_bwd")[0]
    G['w_gu'] = _mm_tn(sv['h2'], dgu, name=f"l{i}_dw_gate_up", tn=1408)
    dh2 = _mm_nt([(dgu, L['w_gu'])], name=f"l{i}_dh2", out_dtype=F32, tm=256)
    dx1, dx1_bf, G['norm_ffn'] = _rms_bwd_rows(sv['x1'], 0, norms['norm_ffn'], dh2, dx2, name=f"l{i}_norm_ffn_bwd", tr=tr)

    dmerged = _mm_nt([(dx1_bf, L['w_o'])], name=f"l{i}_dmerged", out_dtype=F32)
    G['w_o'] = _mm_tn(sv['merged'], dx1_bf, name=f"l{i}_dw_o", tn=1024)

    def merge_bwd(_, dm, ga, gb, ya, yb):
        sa, sb = _sigmoid(ga), _sigmoid(gb)
        ya, yb = ya.astype(F32), yb.astype(F32)
        return dm * ya * sa * (1.0 - sa), dm * yb * sb * (1.0 - sb), dm * sa, dm * sb

    dga, dgb, dya, dyb = _row_call(merge_bwd, T, tr, [_rspec(dmerged, tr), _rspec(z, tr, D_MODEL, 0), _rspec(z, tr, D_MODEL, 1),
                                                      _rspec(sv['ya'], tr), _rspec(sv['yb'], tr)],
                                   [(D_MODEL, BF16)] * 4, name=f"l{i}_merge_bwd")

    G['w_b'] = _mm_tn(sv['o'], dyb, name=f"l{i}_dw_b", tn=1024)
    do = _mm_nt([(dyb, L['w_b'])], name=f"l{i}_do", out_dtype=BF16)
    dq, dk, dv = _flash_bwd(sv['q'], sv['k'], sv['v'], sv['o'], sv['lse'], do, name=f"l{i}_attn_bwd", blk=blk)

    def dq_rope(_, d, ct, st):
        d = d.astype(F32)
        out = []
        for h in range(N_HEADS):
            out.append(d[:, h * HEAD_PAD:h * HEAD_PAD + QK_NOPE])
            out.append(_rope_bwd(d[:, h * HEAD_PAD + QK_NOPE:(h + 1) * HEAD_PAD], ct, st))
        return jnp.concatenate(out, axis=1)

    dq = _row_call(dq_rope, T, trw, [_rspec(dq, trw), _rspec(cos_t, trw), _rspec(sin_t, trw)], [(N_HEADS * HEAD_PAD, BF16)],
                   name=f"l{i}_dq_rope")[0]

    def dk_rope(_, d, ct, st):
        d = d.astype(F32)
        acc = d[:, QK_NOPE:HEAD_PAD]
        for h in range(1, N_HEADS):
            acc = acc + d[:, h * HEAD_PAD + QK_NOPE:(h + 1) * HEAD_PAD]
        return _rope_bwd(acc, ct, st)

    dkr = _row_call(dk_rope, T, tr, [_rspec(dk, tr), _rspec(cos_t, tr), _rspec(sin_t, tr)], [(LANES, BF16)],
                    name=f"l{i}_dk_rope")[0]
    G['w_q'] = _mm_tn(sv['cqn'], dq, name=f"l{i}_dw_q", tn=1024)
    G['w_k'] = _mm_tn(sv['ckvn'], dk, name=f"l{i}_dw_k", tn=1024)
    G['w_v'] = _mm_tn(sv['ckvn'], dv, name=f"l{i}_dw_v", tn=1024)
    dcqn = _mm_nt([(dq, L['w_q'])], name=f"l{i}_dcqn", out_dtype=F32)
    dckvn = _mm_nt([(dk, L['w_k']), (dv, L['w_v'])], name=f"l{i}_dckvn", out_dtype=F32)
    dcq, G['q_norm'] = _rms_bwd_rows(z, ZC_CQ, norms['q_norm'], dcqn, None, name=f"l{i}_q_norm_bwd", tr=tr, emit_bf16=False)
    dckv, G['kv_norm'] = _rms_bwd_rows(z, ZC_CKV, norms['kv_norm'], dckvn, None, name=f"l{i}_kv_norm_bwd", tr=tr,
                                       emit_bf16=False)

    G['w_a'] = _mm_tn(sv['pm'], dya, name=f"l{i}_dw_a", tn=1024)
    dpm = _mm_nt([(dya, L['w_a'])], name=f"l{i}_dpm", out_dtype=F32)
    dpool, dpool_cnt, G['pool_scale'], G['w_pool'] = _pool_bwd_mix(dpm, sv['mixed'], sv['pooled'], w_pool_bf, pool_scale,
                                                                   name=f"l{i}_pool_bwd_mix", tr=tr)
    du = _pool_bwd_window(dpool, dpool_cnt, name=f"l{i}_pool_bwd_window", tr=tr)

    def join(_, a, b, c, d, e, f):
        return jnp.concatenate([a, b, c, d.astype(BF16), e.astype(BF16), f], axis=1)

    dz = _row_call(join, T, trw, [_rspec(t, trw) for t in (dga, dgb, du, dcq, dckv, dkr)], [(Z_WIDTH, BF16)],
                   name=f"l{i}_dz_join")[0]
    G['w_in'] = _mm_tn(sv['h'], dz, name=f"l{i}_dw_in", tn=1152)
    dh = _mm_nt([(dz, L['w_in'])], name=f"l{i}_dh", out_dtype=F32)
    dx, G['norm_mix'] = _rms_bwd_rows(sv['x'], 0, norms['norm_mix'], dh, dx1, name=f"l{i}_norm_mix_bwd", tr=tr,
                                      emit_bf16=False)
    return dx, G


def kernel(x, p, positions, norm_mix, w_in, w_pool, pool_scale, q_norm, kv_norm, w_uq, w_ukv, w_a, w_b, w_o, norm_ffn, w_gate, w_up, w_down, norm_ple, w_ple_gate, w_ple, final_norm, loss_target, m_norm_mix, m_w_in, m_w_pool, m_pool_scale, m_q_norm, m_kv_norm, m_w_uq, m_w_ukv, m_w_a, m_w_b, m_w_o, m_norm_ffn, m_w_gate, m_w_up, m_w_down, m_norm_ple, m_w_ple_gate, m_w_ple, m_final_norm, v_norm_mix, v_w_in, v_w_pool, v_pool_scale, v_q_norm, v_kv_norm, v_w_uq, v_w_ukv, v_w_a, v_w_b, v_w_o, v_norm_ffn, v_w_gate, v_w_up, v_w_down, v_norm_ple, v_w_ple_gate, v_w_ple, v_final_norm):
    given = dict(locals())
    weights = {n: given[n] for n in WEIGHTS}
    T = x.shape[1]
    tr = min(512, max(T // 2, 8))
    blk = min(512, max(T // 4, 128))
    x0 = x.reshape(T, D_MODEL)
    target = loss_target.reshape(T, D_MODEL)

    names = list(SHARDED)
    shard_shapes = [weights[n].shape for n in names]
    flat = _pack_rows([weights[n].astype(BF16) for n in names], row_mult=1024)
    R = flat.shape[0]
    gathered = _gather_weights(flat).reshape(N_CHIPS, R, LANES)
    per_chip = [_unpack_rows(gathered[k], shard_shapes) for k in range(N_CHIPS)]
    W = {n: jnp.concatenate([per_chip[k][j] for k in range(N_CHIPS)], axis=SHARDED[n]) for j, n in enumerate(names)}
    layouts = [_layer_layouts(W, i) for i in range(DEPTH)]
    w_pool_bf = w_pool.astype(BF16)

    inv_freq = 1.0 / (ROPE_THETA ** (jnp.arange(0, QK_ROPE, 2, dtype=F32) / QK_ROPE))
    zero32 = jnp.zeros((32,), F32)
    freq_row = jnp.concatenate([inv_freq, zero32, inv_freq, zero32]).reshape(1, LANES)
    cos_mask = jnp.concatenate([jnp.ones((32,), F32), zero32, jnp.ones((32,), F32), zero32]).reshape(1, LANES)
    sin_sign = jnp.concatenate([-jnp.ones((32,), F32), zero32, jnp.ones((32,), F32), zero32]).reshape(1, LANES)

    def rope_tables(_, pos, fr, cm, ss):
        ang = pos.astype(F32) * fr
        return jnp.cos(ang) * cm, jnp.sin(ang) * ss

    pos_col = positions.reshape(T, 1)
    cos_t, sin_t = _row_call(rope_tables, T, tr, [_rspec(pos_col, tr), _bspec(freq_row), _bspec(cos_mask), _bspec(sin_sign)],
                             [(LANES, F32), (LANES, F32)], name="rope_tables")

    xs = x0
    saved = []
    for i in range(DEPTH):
        norms = {n: weights[n][i] for n in ('norm_mix', 'q_norm', 'kv_norm', 'norm_ffn', 'norm_ple')}
        xs, sv = _layer_fwd(i, xs, p[i, 0], layouts[i], norms, w_pool_bf[i], pool_scale[i], cos_t, sin_t, tr, blk)
        saved.append((sv, norms))

    def head(_, xv, tv, gv):
        rstd = lax.rsqrt(jnp.mean(xv * xv, axis=-1, keepdims=True) + EPS)
        xhat = xv * rstd
        err = xhat * gv - tv
        loss = 0.5 * jnp.sum(jnp.mean(err * err, axis=-1, keepdims=True), axis=0, keepdims=True)
        dy = err * (1.0 / D_MODEL)
        dg = jnp.sum(dy * xhat, axis=0, keepdims=True)
        dxh = dy * gv
        dx = rstd * (dxh - xhat * jnp.mean(dxh * xhat, axis=-1, keepdims=True))
        return dx, jnp.broadcast_to(loss, (1, LANES)), dg

    dx, loss_part, g_final = _row_call(head, T, tr, [_rspec(xs, tr), _rspec(target, tr), _bspec(final_norm.reshape(1, D_MODEL))],
                                       [(D_MODEL, F32)], [(1, LANES), (1, D_MODEL)], name="loss_head")
    loss = lax.psum(loss_part[0, 0], ("x", "y", "c"))

    layer_grads = [None] * DEPTH
    for i in reversed(range(DEPTH)):
        sv, norms = saved[i]
        dx, layer_grads[i] = _layer_bwd(i, dx, sv, p[i, 0], layouts[i], norms, w_pool_bf[i], pool_scale[i], cos_t, sin_t, tr,
                                        blk)
    grad_x = dx.reshape(x.shape)

    ref_layout = [_layer_grads_to_reference_layout(g) for g in layer_grads]
    local = {n: jnp.stack([ref_layout[i][n] for i in range(DEPTH)]) for n in names}
    for n in ('norm_mix', 'q_norm', 'kv_norm', 'norm_ffn', 'norm_ple', 'pool_scale'):
        local[n] = jnp.stack([layer_grads[i][n].reshape(-1) for i in range(DEPTH)])
    local['w_pool'] = jnp.stack([layer_grads[i]['w_pool'] for i in range(DEPTH)])
    local['final_norm'] = g_final.reshape(-1)

    send = []
    for k in range(N_CHIPS):
        parts = []
        for n in names:
            ax = SHARDED[n]
            size = local[n].shape[ax] // N_CHIPS
            parts.append(lax.slice_in_dim(local[n], k * size, (k + 1) * size, axis=ax))
        send.append(_pack_rows(parts, row_mult=1024))
    rh = R // 2
    trr = _tile(rh, 2048, 8)
    g_all = jnp.stack(send).reshape(N_CHIPS, 2, rh, LANES)
    part = _add_halves(g_all, _swap_halves(g_all), tr=trr)
    reduced_half = _sum_chips(_scatter_partials(part), tr=trr)
    reduced = _join_halves(reduced_half).reshape(R, LANES)
    grads = dict(zip(names, _unpack_rows(reduced, shard_shapes)))

    rep_shapes = [weights[n].shape for n in REPLICATED]
    rep = _allreduce_small(_pack_rows([local[n] for n in REPLICATED], row_mult=8))
    grads.update(zip(REPLICATED, _unpack_rows(rep, rep_shapes)))

    deltas, new_m, new_v = {}, {}, {}
    for n in WEIGHTS:
        deltas[n], new_m[n], new_v[n] = _adamw(weights[n], grads[n], given['m_' + n], given['v_' + n], name=f"adamw_{n}")
    return (loss, grad_x, *[grads[n] for n in WEIGHTS], *[deltas[n] for n in WEIGHTS], *[new_m[n] for n in WEIGHTS],
            *[new_v[n] for n in WEIGHTS])
```

```python
import functools

import numpy as np
import jax
import jax.numpy as jnp
from jax import lax
from jax.experimental import pallas as pl
from jax.experimental.pallas import tpu as pltpu

F32 = jnp.float32
BF16 = jnp.bfloat16

D_MODEL = 1024
DEPTH = 2
PLE_DIM = 256
POOL_WINDOWS = (2, 4, 8, 16)
POOL_GROUP = 128
POOL_WIDTH = 512
N_HEADS = 8
Q_LORA = 512
KV_LORA = 256
QK_NOPE = 128
QK_ROPE = 64
QK_HEAD = 192
V_HEAD = 128
D_FF = 2816
ROPE_THETA = 10000.0
EPS = 1e-6
ATTN_SCALE = QK_HEAD ** -0.5

ADAM_LR = 0.001
ADAM_B1 = 0.9
ADAM_B2 = 0.999
ADAM_EPS = 1e-08
ADAM_WD = 0.01
ADAM_STEP = 10

LANES = 128
HALO = 16
HEAD_PAD = 256
V7X_VMEM_BYTES = 64 * 1024 * 1024
VMEM_LIMIT = (V7X_VMEM_BYTES * 3) // 4
N_CHIPS = 4
N_DEV = 8
NEG_INF = -1e30

ZC_GA, ZC_GB, ZC_U, ZC_CQ, ZC_CKV, ZC_KR = 0, 1024, 2048, 2560, 3072, 3328
Z_WIDTH = 3456

WEIGHTS = ['norm_mix', 'w_in', 'w_pool', 'pool_scale', 'q_norm', 'kv_norm', 'w_uq', 'w_ukv', 'w_a', 'w_b', 'w_o',
           'norm_ffn', 'w_gate', 'w_up', 'w_down', 'norm_ple', 'w_ple_gate', 'w_ple', 'final_norm']
SHARDED = {'w_in': 2, 'w_uq': 1, 'w_ukv': 1, 'w_a': 2, 'w_b': 1, 'w_o': 1, 'w_gate': 2, 'w_up': 2, 'w_down': 1,
           'w_ple_gate': 1, 'w_ple': 2}
REPLICATED = [n for n in WEIGHTS if n not in SHARDED]


def _tile(n, target, mult=LANES):
    if n <= target:
        return n
    best = None
    for t in range(mult, target + 1, mult):
        if n % t == 0:
            best = t
    assert best is not None, (n, target)
    return best


def _cparams(*sem):
    return pltpu.CompilerParams(dimension_semantics=sem, vmem_limit_bytes=VMEM_LIMIT)


def _rope(t, cos_t, sin_t):
    return t * cos_t + pltpu.roll(t, 64, 1) * sin_t


def _rope_bwd(d, cos_t, sin_t):
    return d * cos_t + pltpu.roll(d * sin_t, 64, 1)


def _sigmoid(v):
    return 1.0 / (1.0 + jnp.exp(-v))


def _mm_nn(a, b, *, name, out_dtype, a_col=0, gain=None, emit_a=False, res=None, epi=None, epi_rows=(),
           tm=512, tn=512):
    M = a.shape[0]
    K, N = b.shape
    tm = min(tm, M)
    tn = _tile(N, tn)
    assert a_col % K == 0 and M % tm == 0
    a_blk = a_col // K
    stage = gain is not None or a.dtype != BF16 or emit_a
    n_rows = len(epi_rows)

    def body(*refs):
        refs = list(refs)
        a_ref, b_ref = refs[0], refs[1]
        pos = 2
        g_ref = res_ref = None
        if gain is not None:
            g_ref = refs[pos]; pos += 1
        if res is not None:
            res_ref = refs[pos]; pos += 1
        row_refs = refs[pos:pos + n_rows]; pos += n_rows
        o_ref = refs[pos]; pos += 1
        ah_ref = None
        if emit_a:
            ah_ref = refs[pos]; pos += 1
        abf = refs[pos] if stage else None

        if stage:
            @pl.when(pl.program_id(1) == 0)
            def _():
                av = a_ref[...].astype(F32)
                if gain is not None:
                    av = av * lax.rsqrt(jnp.mean(av * av, axis=-1, keepdims=True) + EPS) * g_ref[...]
                abf[...] = av.astype(BF16)
                if emit_a:
                    ah_ref[...] = abf[...]
            lhs = abf[...]
        else:
            lhs = a_ref[...]
        acc = jnp.dot(lhs, b_ref[...], preferred_element_type=F32)
        if res is not None:
            acc = acc + res_ref[...]
        if epi is not None:
            acc = epi(acc, *[r[...] for r in row_refs])
        o_ref[...] = acc.astype(o_ref.dtype)

    in_specs = [pl.BlockSpec((tm, K), lambda i, j: (i, a_blk)), pl.BlockSpec((K, tn), lambda i, j: (0, j))]
    args = [a, b]
    if gain is not None:
        in_specs.append(pl.BlockSpec((1, K), lambda i, j: (0, 0)))
        args.append(gain.reshape(1, K).astype(F32))
    if res is not None:
        in_specs.append(pl.BlockSpec((tm, tn), lambda i, j: (i, j)))
        args.append(res)
    for arr, w, blk in epi_rows:
        in_specs.append(pl.BlockSpec((tm, w), lambda i, j, blk=blk: (i, blk)))
        args.append(arr)
    out_shape = [jax.ShapeDtypeStruct((M, N), out_dtype)]
    out_specs = [pl.BlockSpec((tm, tn), lambda i, j: (i, j))]
    if emit_a:
        out_shape.append(jax.ShapeDtypeStruct((M, K), BF16))
        out_specs.append(pl.BlockSpec((tm, K), lambda i, j: (i, 0)))
    scratch = [pltpu.VMEM((tm, K), BF16)] if stage else []
    outs = pl.pallas_call(body, grid=(M // tm, N // tn), in_specs=in_specs, out_specs=out_specs,
                          out_shape=out_shape, scratch_shapes=scratch, name=name,
                          compiler_params=_cparams("parallel", "arbitrary"))(*args)
    return outs if emit_a else outs[0]


def _mm_nt(pairs, *, name, out_dtype, tm=512):
    M = pairs[0][0].shape[0]
    N = pairs[0][1].shape[0]
    tm = min(tm, M)
    n_p = len(pairs)

    def body(*refs):
        o_ref = refs[2 * n_p]
        acc = None
        for k in range(n_p):
            av = refs[2 * k][...].astype(BF16)
            part = lax.dot_general(av, refs[2 * k + 1][...], (((1,), (1,)), ((), ())), preferred_element_type=F32)
            acc = part if acc is None else acc + part
        o_ref[...] = acc.astype(o_ref.dtype)

    in_specs, args = [], []
    for a, b in pairs:
        assert a.shape[1] == b.shape[1] and b.shape[0] == N and a.shape[0] == M
        in_specs.append(pl.BlockSpec((tm, a.shape[1]), lambda i: (i, 0)))
        in_specs.append(pl.BlockSpec(b.shape, lambda i: (0, 0)))
        args += [a, b]
    return pl.pallas_call(body, grid=(M // tm,), in_specs=in_specs,
                          out_specs=pl.BlockSpec((tm, N), lambda i: (i, 0)),
                          out_shape=jax.ShapeDtypeStruct((M, N), out_dtype), name=name,
                          compiler_params=_cparams("parallel"))(*args)


def _mm_tn(a, b, *, name, a_col=0, a_w=None, tk=1024, tn=1152, tm=512):
    M = a.shape[0]
    a_w = a.shape[1] if a_w is None else a_w
    N = b.shape[1]
    tm = min(tm, M)
    tk = _tile(a_w, tk)
    tn = _tile(N, tn)
    assert a_col % tk == 0 and M % tm == 0
    a_blk0 = a_col // tk

    def body(a_ref, b_ref, o_ref):
        @pl.when(pl.program_id(2) == 0)
        def _():
            o_ref[...] = jnp.zeros_like(o_ref)
        o_ref[...] += lax.dot_general(a_ref[...].astype(BF16), b_ref[...].astype(BF16), (((0,), (0,)), ((), ())),
                                      preferred_element_type=F32)

    return pl.pallas_call(body, grid=(a_w // tk, N // tn, M // tm),
                          in_specs=[pl.BlockSpec((tm, tk), lambda k, j, m: (m, k + a_blk0)),
                                    pl.BlockSpec((tm, tn), lambda k, j, m: (m, j))],
                          out_specs=pl.BlockSpec((tk, tn), lambda k, j, m: (k, j)),
                          out_shape=jax.ShapeDtypeStruct((a_w, N), F32), name=name,
                          compiler_params=_cparams("parallel", "parallel", "arbitrary"))(a, b)


def _row_call(fn, rows, tr, ins, outs, accs=(), *, name):
    n_in, n_out, n_acc = len(ins), len(outs), len(accs)

    def body(*refs):
        i = pl.program_id(0)
        vals = fn(i, *[r[...] for r in refs[:n_in]])
        if not isinstance(vals, (tuple, list)):
            vals = (vals,)
        for r, v in zip(refs[n_in:n_in + n_out], vals[:n_out]):
            r[...] = v.astype(r.dtype)
        if n_acc:
            acc_refs = refs[n_in + n_out:]

            @pl.when(i == 0)
            def _():
                for r in acc_refs:
                    r[...] = jnp.zeros_like(r)
            for r, v in zip(acc_refs, vals[n_out:]):
                r[...] += v

    out_shape = [jax.ShapeDtypeStruct((rows, w), dt) for w, dt in outs]
    out_specs = [pl.BlockSpec((tr, w), lambda i: (i, 0)) for w, dt in outs]
    for s in accs:
        out_shape.append(jax.ShapeDtypeStruct(s, F32))
        out_specs.append(pl.BlockSpec(s, lambda i, n=len(s): (0,) * n))
    res = pl.pallas_call(body, grid=(rows // tr,), in_specs=[pl.BlockSpec(bs, im) for _, bs, im in ins],
                         out_specs=out_specs, out_shape=out_shape, name=name,
                         compiler_params=_cparams("arbitrary"))(*[a for a, _, _ in ins])
    return res


def _wide_rows(tr):
    return max(tr // 4, 8)


def _rspec(arr, tr, w=None, blk=0):
    w = arr.shape[1] if w is None else w
    return (arr, (tr, w), lambda i, blk=blk: (i, blk))


def _bspec(arr):
    return (arr, arr.shape, lambda i, n=arr.ndim: (0,) * n)


def _rms_bwd_rows(x, x_col, gain, dh, dres, *, name, tr, emit_bf16=True):
    rows = x.shape[0]
    W = gain.shape[-1]
    g2 = gain.reshape(1, W).astype(F32)

    def fn(i, xv, gv, dhv, *rest):
        xv = xv.astype(F32)
        dhv = dhv.astype(F32)
        rstd = lax.rsqrt(jnp.mean(xv * xv, axis=-1, keepdims=True) + EPS)
        xhat = xv * rstd
        dg = jnp.sum(dhv * xhat, axis=0, keepdims=True)
        dxh = dhv * gv
        dx = rstd * (dxh - xhat * jnp.mean(dxh * xhat, axis=-1, keepdims=True))
        if rest:
            dx = dx + rest[0].astype(F32)
        return (dx, dx, dg) if emit_bf16 else (dx, dg)

    ins = [_rspec(x, tr, W, x_col // W), _bspec(g2), _rspec(dh, tr)]
    if dres is not None:
        ins.append(_rspec(dres, tr))
    outs = [(W, F32), (W, BF16)] if emit_bf16 else [(W, F32)]
    return _row_call(fn, rows, tr, ins, outs, [(1, W)], name=name)


def _pool_counts(i, tr):
    t = (i * tr + lax.broadcasted_iota(jnp.int32, (tr, 1), 0) + 1).astype(F32)
    return [jnp.minimum(t, float(w)) for w in POOL_WINDOWS]


def _pool_fwd(z, w_pool_bf, pool_scale, *, name, tr):
    rows = z.shape[0]
    ublk = ZC_U // POOL_WIDTH
    hpt = tr // HALO

    def fn(i, u, uprev, wp, ps):
        uprev = jnp.where(i > 0, uprev, 0.0)
        ext = jnp.concatenate([uprev, u], axis=0)
        s2 = ext + pltpu.roll(ext, 1, 0)
        s4 = s2 + pltpu.roll(s2, 2, 0)
        s8 = s4 + pltpu.roll(s4, 4, 0)
        s16 = s8 + pltpu.roll(s8, 8, 0)
        cnts = _pool_counts(i, tr)
        pooled, mixed = [], []
        for g, sw in enumerate((s2, s4, s8, s16)):
            lanes = slice(g * POOL_GROUP, (g + 1) * POOL_GROUP)
            pg = sw[HALO:, lanes] / cnts[g] - u[:, lanes]
            pooled.append(pg)
            mixed.append(jnp.dot(pg.astype(BF16), wp[g], preferred_element_type=F32))
        pooled = jnp.concatenate(pooled, axis=1)
        mixed = jnp.concatenate(mixed, axis=1)
        return pooled, mixed, mixed * ps

    ins = [_rspec(z, tr, POOL_WIDTH, ublk),
           (z, (HALO, POOL_WIDTH), lambda i: (jnp.maximum(i * hpt - 1, 0), ublk)),
           _bspec(w_pool_bf), _bspec(pool_scale.reshape(1, POOL_WIDTH))]
    return _row_call(fn, rows, tr, ins, [(POOL_WIDTH, BF16), (POOL_WIDTH, F32), (POOL_WIDTH, BF16)], name=name)


def _pool_bwd_mix(dpm, mixed, pooled, w_pool_bf, pool_scale, *, name, tr):
    rows = dpm.shape[0]

    def fn(i, dv, mv, pv, wp, ps):
        dv = dv.astype(F32)
        dscale = jnp.sum(dv * mv, axis=0, keepdims=True)
        dmix = (dv * ps).astype(BF16)
        cnts = _pool_counts(i, tr)
        dpool, dwp = [], []
        for g in range(len(POOL_WINDOWS)):
            lanes = slice(g * POOL_GROUP, (g + 1) * POOL_GROUP)
            dg = lax.dot_general(dmix[:, lanes], wp[g], (((1,), (1,)), ((), ())), preferred_element_type=F32)
            dpool.append(dg)
            dwp.append(lax.dot_general(pv[:, lanes], dmix[:, lanes], (((0,), (0,)), ((), ())),
                                       preferred_element_type=F32)[None])
        dpool = jnp.concatenate(dpool, axis=1)
        dpool_cnt = jnp.concatenate([dpool[:, g * POOL_GROUP:(g + 1) * POOL_GROUP] / cnts[g]
                                     for g in range(len(POOL_WINDOWS))], axis=1)
        return dpool, dpool_cnt, dscale, jnp.concatenate(dwp, axis=0)

    ins = [_rspec(dpm, tr), _rspec(mixed, tr), _rspec(pooled, tr), _bspec(w_pool_bf),
           _bspec(pool_scale.reshape(1, POOL_WIDTH))]
    return _row_call(fn, rows, tr, ins, [(POOL_WIDTH, F32), (POOL_WIDTH, F32)],
                     [(1, POOL_WIDTH), (len(POOL_WINDOWS), POOL_GROUP, POOL_GROUP)], name=name)


def _pool_bwd_window(dpool, dpool_cnt, *, name, tr):
    rows = dpool.shape[0]
    hpt = tr // HALO
    n_halo = rows // HALO
    n_tiles = rows // tr

    def fn(i, dp, dc, dnext):
        dnext = jnp.where(i < n_tiles - 1, dnext, 0.0)
        ext = jnp.concatenate([dc, dnext], axis=0)
        n = tr + HALO
        s2 = ext + pltpu.roll(ext, n - 1, 0)
        s4 = s2 + pltpu.roll(s2, n - 2, 0)
        s8 = s4 + pltpu.roll(s4, n - 4, 0)
        s16 = s8 + pltpu.roll(s8, n - 8, 0)
        out = []
        for g, sw in enumerate((s2, s4, s8, s16)):
            lanes = slice(g * POOL_GROUP, (g + 1) * POOL_GROUP)
            out.append(sw[:tr, lanes] - dp[:, lanes])
        return jnp.concatenate(out, axis=1)

    ins = [_rspec(dpool, tr), _rspec(dpool_cnt, tr),
           (dpool_cnt, (HALO, POOL_WIDTH), lambda i: (jnp.minimum((i + 1) * hpt, n_halo - 1), 0))]
    return _row_call(fn, rows, tr, ins, [(POOL_WIDTH, BF16)], name=name)[0]


def _causal_pairs(n, k_major):
    if k_major:
        pairs = [(qi, ki) for ki in range(n) for qi in range(ki, n)]
    else:
        pairs = [(qi, ki) for qi in range(n) for ki in range(qi + 1)]
    return (jnp.asarray(np.array([p[0] for p in pairs], np.int32)),
            jnp.asarray(np.array([p[1] for p in pairs], np.int32)), len(pairs))


SUBLANES = 8
NT_DIMS = (((1,), (1,)), ((), ()))
TN_DIMS = (((0,), (0,)), ((), ()))


ATTN_BLOCK = 1024
QUERY_CHUNK = 256
EXP2_SCALE = ATTN_SCALE * 1.4426950408889634


def _scores_t(q_c, k, c, qc, diag):
    s = lax.dot_general(k, q_c, NT_DIMS, preferred_element_type=F32)
    if diag:
        key = lax.broadcasted_iota(jnp.int32, s.shape, 0)
        qry = lax.broadcasted_iota(jnp.int32, s.shape, 1) + c * qc
        s = jnp.where(key <= qry, s, NEG_INF)
    return s


def _flash_fwd(q, k, v, *, name, blk):
    T = q.shape[0]
    n = T // blk
    qc = min(QUERY_CHUNK, blk)
    qtab, ktab, n_pairs = _causal_pairs(n, k_major=False)

    def body(qt, kt, q_ref, k_ref, v_ref, o_ref, lse_ref, m_s, l_s, acc_s):
        p = pl.program_id(1)
        qi, ki = qt[p], kt[p]

        @pl.when(ki == 0)
        def _():
            m_s[...] = jnp.full_like(m_s, NEG_INF)
            l_s[...] = jnp.zeros_like(l_s)
            acc_s[...] = jnp.zeros_like(acc_s)

        def step(diag):
            kv, vv = k_ref[...], v_ref[...]
            chunks = [slice(c * qc, (c + 1) * qc) for c in range(blk // qc)]
            scores = [_scores_t(q_ref[rows, :], kv, c, qc, diag) for c, rows in enumerate(chunks)]
            probs, alphas = [], []
            for rows, s_t in zip(chunks, scores):
                m_prev = m_s[:, rows]
                m_new = jnp.maximum(m_prev, jnp.max(s_t, axis=0, keepdims=True))
                p_t = jnp.exp2((s_t - m_new) * EXP2_SCALE)
                alpha = jnp.exp2((m_prev - m_new) * EXP2_SCALE)
                l_s[:, rows] = alpha * l_s[:, rows] + jnp.sum(p_t, axis=0, keepdims=True)
                m_s[:, rows] = m_new
                probs.append(p_t.astype(BF16))
                alphas.append(alpha)
            for rows, p_t, alpha in zip(chunks, probs, alphas):
                acc_s[:, rows] = alpha * acc_s[:, rows] + lax.dot_general(vv, p_t, TN_DIMS, preferred_element_type=F32)

        @pl.when(ki != qi)
        def _():
            step(False)

        @pl.when(ki == qi)
        def _():
            step(True)
            o_ref[...] = (acc_s[...] / l_s[...]).T.astype(o_ref.dtype)
            lse2 = m_s[...] * EXP2_SCALE + jnp.log2(l_s[...])
            lse_ref[...] = jnp.broadcast_to(lse2, lse_ref.shape)

    grid_spec = pltpu.PrefetchScalarGridSpec(
        num_scalar_prefetch=2, grid=(N_HEADS, n_pairs),
        in_specs=[pl.BlockSpec((blk, HEAD_PAD), lambda h, p, qt, kt: (qt[p], h)),
                  pl.BlockSpec((blk, HEAD_PAD), lambda h, p, qt, kt: (kt[p], h)),
                  pl.BlockSpec((blk, V_HEAD), lambda h, p, qt, kt: (kt[p], h))],
        out_specs=[pl.BlockSpec((blk, V_HEAD), lambda h, p, qt, kt: (qt[p], h)),
                   pl.BlockSpec((SUBLANES, blk), lambda h, p, qt, kt: (h, qt[p]))],
        scratch_shapes=[pltpu.VMEM((1, blk), F32), pltpu.VMEM((1, blk), F32), pltpu.VMEM((V_HEAD, blk), F32)])
    return pl.pallas_call(body, grid_spec=grid_spec,
                          out_shape=[jax.ShapeDtypeStruct((T, N_HEADS * V_HEAD), BF16),
                                     jax.ShapeDtypeStruct((N_HEADS * SUBLANES, T), F32)],
                          name=name, compiler_params=_cparams("parallel", "arbitrary"))(qtab, ktab, q, k, v)


def _attn_delta(do, o, *, name, tr):
    T = do.shape[0]

    def body(do_ref, o_ref, d_ref):
        prod = do_ref[...].astype(F32) * o_ref[...].astype(F32)
        lane_head = lax.broadcasted_iota(jnp.int32, (tr, LANES), 1) // SUBLANES
        mat = jnp.zeros((tr, LANES), F32)
        for h in range(N_HEADS):
            d_h = jnp.sum(prod[:, h * V_HEAD:(h + 1) * V_HEAD], axis=1, keepdims=True)
            mat = jnp.where(lane_head == h, d_h, mat)
        d_ref[...] = mat.T[:N_HEADS * SUBLANES, :]

    return pl.pallas_call(body, grid=(T // tr,),
                          in_specs=[pl.BlockSpec((tr, N_HEADS * V_HEAD), lambda i: (i, 0)),
                                    pl.BlockSpec((tr, N_HEADS * V_HEAD), lambda i: (i, 0))],
                          out_specs=pl.BlockSpec((N_HEADS * SUBLANES, tr), lambda i: (0, i)),
                          out_shape=jax.ShapeDtypeStruct((N_HEADS * SUBLANES, T), F32), name=name,
                          compiler_params=_cparams("parallel"))(do, o)


def _flash_bwd(q, k, v, lse, delta, do, *, name, blk):
    T = q.shape[0]
    n = T // blk
    qc = min(QUERY_CHUNK, blk)
    qtab, ktab, n_pairs = _causal_pairs(n, k_major=True)

    def body(qt, kt, q_ref, k_ref, v_ref, lse_ref, delta_ref, do_ref, dq_ref, dk_ref, dv_ref, dq_s, dk_s, dv_s):
        p = pl.program_id(1)
        qi, ki = qt[p], kt[p]
        first = qi == ki

        @pl.when(p == 0)
        def _():
            dq_s[...] = jnp.zeros_like(dq_s)

        @pl.when(first)
        def _():
            dk_s[...] = jnp.zeros_like(dk_s)
            dv_s[...] = jnp.zeros_like(dv_s)

        def step(diag):
            kv, vv = k_ref[...], v_ref[...]
            chunks = [slice(c * qc, (c + 1) * qc) for c in range(blk // qc)]
            qs = [q_ref[rows, :] for rows in chunks]
            dos = [do_ref[rows, :] for rows in chunks]
            scores = [_scores_t(q_c, kv, c, qc, diag) for c, q_c in enumerate(qs)]
            dps = [lax.dot_general(vv, do_c, NT_DIMS, preferred_element_type=F32) for do_c in dos]
            probs, dss = [], []
            for rows, s_t, dp_t in zip(chunks, scores, dps):
                p_t = jnp.exp2(s_t * EXP2_SCALE - lse_ref[0:1, rows])
                dss.append((p_t * (dp_t - delta_ref[0:1, rows])).astype(BF16))
                probs.append(p_t.astype(BF16))
            dv_acc = dk_acc = None
            for p_t, ds_t, q_c, do_c in zip(probs, dss, qs, dos):
                dv_c = jnp.dot(p_t, do_c, preferred_element_type=F32)
                dk_c = jnp.dot(ds_t, q_c, preferred_element_type=F32)
                dv_acc = dv_c if dv_acc is None else dv_acc + dv_c
                dk_acc = dk_c if dk_acc is None else dk_acc + dk_c
            for rows, ds_t in zip(chunks, dss):
                dq_s[qi, :, rows] += lax.dot_general(kv, ds_t, TN_DIMS, preferred_element_type=F32)
            dv_s[...] += dv_acc
            dk_s[...] += dk_acc

        @pl.when(jnp.logical_not(first))
        def _():
            step(False)

        @pl.when(first)
        def _():
            step(True)
            dq_ref[...] = (dq_s[qi] * ATTN_SCALE).T.astype(dq_ref.dtype)

        @pl.when(qi == n - 1)
        def _():
            dk_ref[...] = (dk_s[...] * ATTN_SCALE).astype(dk_ref.dtype)
            dv_ref[...] = dv_s[...].astype(dv_ref.dtype)

    qmap = lambda h, p, qt, kt: (qt[p], h)
    kmap = lambda h, p, qt, kt: (kt[p], h)
    smap = lambda h, p, qt, kt: (h, qt[p])
    grid_spec = pltpu.PrefetchScalarGridSpec(
        num_scalar_prefetch=2, grid=(N_HEADS, n_pairs),
        in_specs=[pl.BlockSpec((blk, HEAD_PAD), qmap), pl.BlockSpec((blk, HEAD_PAD), kmap),
                  pl.BlockSpec((blk, V_HEAD), kmap), pl.BlockSpec((SUBLANES, blk), smap),
                  pl.BlockSpec((SUBLANES, blk), smap), pl.BlockSpec((blk, V_HEAD), qmap)],
        out_specs=[pl.BlockSpec((blk, HEAD_PAD), kmap), pl.BlockSpec((blk, HEAD_PAD), kmap),
                   pl.BlockSpec((blk, V_HEAD), kmap)],
        scratch_shapes=[pltpu.VMEM((n, HEAD_PAD, blk), F32), pltpu.VMEM((blk, HEAD_PAD), F32),
                        pltpu.VMEM((blk, V_HEAD), F32)])
    return pl.pallas_call(body, grid_spec=grid_spec,
                          out_shape=[jax.ShapeDtypeStruct((T, N_HEADS * HEAD_PAD), BF16),
                                     jax.ShapeDtypeStruct((T, N_HEADS * HEAD_PAD), BF16),
                                     jax.ShapeDtypeStruct((T, N_HEADS * V_HEAD), BF16)],
                          name=name, compiler_params=_cparams("arbitrary", "arbitrary"))(qtab, ktab, q, k, v, lse, delta, do)


MESH_ID = pl.DeviceIdType.MESH
ANY_SPEC = pl.BlockSpec(memory_space=pl.ANY)


def _other_chips(x, y):
    out = []
    for dx, dy in ((1, 0), (0, 1), (1, 1)):
        px = x ^ dx if dx else x
        py = y ^ dy if dy else y
        out.append((px, py, 2 * px + py))
    return out


def _gather_weights(flat):
    rh = flat.shape[0] // 2

    def body(src2, out, send_sems, recv_sems, local_sem):
        x, y, c = lax.axis_index("x"), lax.axis_index("y"), lax.axis_index("c")
        me = 2 * x + y
        sib = (x, y, 1 - c)
        own = pltpu.make_async_copy(src2, out.at[me], local_sem)
        own.start()
        chips = _other_chips(x, y)
        sends = []
        for j, (px, py, pk) in enumerate(chips):
            cp = pltpu.make_async_remote_copy(src_ref=src2.at[c], dst_ref=out.at[me, c], send_sem=send_sems.at[j],
                                              recv_sem=recv_sems.at[j], device_id=(px, py, c), device_id_type=MESH_ID)
            cp.start()
            sends.append(cp)
        for j, (px, py, pk) in enumerate(chips):
            land = out.at[pk, c]
            pltpu.make_async_remote_copy(src_ref=land, dst_ref=land, send_sem=send_sems.at[j], recv_sem=recv_sems.at[j],
                                         device_id=(px, py, c), device_id_type=MESH_ID).wait_recv()
            fw = pltpu.make_async_remote_copy(src_ref=land, dst_ref=land, send_sem=send_sems.at[3 + j],
                                              recv_sem=recv_sems.at[3 + j], device_id=sib, device_id_type=MESH_ID)
            fw.start()
            sends.append(fw)
        for j, (px, py, pk) in enumerate(chips):
            land = out.at[pk, 1 - c]
            pltpu.make_async_remote_copy(src_ref=land, dst_ref=land, send_sem=send_sems.at[3 + j],
                                         recv_sem=recv_sems.at[3 + j], device_id=sib, device_id_type=MESH_ID).wait_recv()
        for cp in sends:
            cp.wait_send()
        own.wait()

    return pl.pallas_call(body, out_shape=jax.ShapeDtypeStruct((N_CHIPS, 2, rh, LANES), flat.dtype),
                          in_specs=[ANY_SPEC], out_specs=ANY_SPEC,
                          scratch_shapes=[pltpu.SemaphoreType.DMA((6,)), pltpu.SemaphoreType.DMA((6,)),
                                          pltpu.SemaphoreType.DMA(())],
                          name="gather_weights")(flat.reshape(2, rh, LANES))


def _swap_halves(g):
    rh = g.shape[2]

    def body(src, out, send_sem, recv_sem):
        x, y, c = lax.axis_index("x"), lax.axis_index("y"), lax.axis_index("c")
        cp = pltpu.make_async_remote_copy(src_ref=src.at[:, 1 - c], dst_ref=out, send_sem=send_sem, recv_sem=recv_sem,
                                          device_id=(x, y, 1 - c), device_id_type=MESH_ID)
        cp.start()
        cp.wait()

    return pl.pallas_call(body, out_shape=jax.ShapeDtypeStruct((N_CHIPS, rh, LANES), g.dtype),
                          in_specs=[ANY_SPEC], out_specs=ANY_SPEC,
                          scratch_shapes=[pltpu.SemaphoreType.DMA(()), pltpu.SemaphoreType.DMA(())],
                          name="grad_swap_halves")(g)


def _add_halves(g, got, *, tr):
    rh = g.shape[2]
    nt = rh // tr

    def body(c_ref, g_ref, got_ref, o_ref):
        o_ref[...] = (g_ref[0] + got_ref[...]).astype(o_ref.dtype)

    grid_spec = pltpu.PrefetchScalarGridSpec(
        num_scalar_prefetch=1, grid=(N_CHIPS, nt),
        in_specs=[pl.BlockSpec((1, 1, tr, LANES), lambda k, i, c_ref: (k, c_ref[0], i, 0)),
                  pl.BlockSpec((1, tr, LANES), lambda k, i, c_ref: (k, i, 0))],
        out_specs=pl.BlockSpec((1, tr, LANES), lambda k, i, c_ref: (k, i, 0)))
    core = lax.axis_index("c").astype(jnp.int32).reshape(1)
    return pl.pallas_call(body, grid_spec=grid_spec,
                          out_shape=jax.ShapeDtypeStruct((N_CHIPS, rh, LANES), BF16), name="grad_add_halves",
                          compiler_params=_cparams("parallel", "parallel"))(core, g, got)


def _scatter_partials(part):
    rh = part.shape[1]

    def body(src, out, send_sems, recv_sems, local_sem):
        x, y, c = lax.axis_index("x"), lax.axis_index("y"), lax.axis_index("c")
        me = 2 * x + y
        own = pltpu.make_async_copy(src.at[me], out.at[me], local_sem)
        own.start()
        chips = _other_chips(x, y)
        sends = []
        for j, (px, py, pk) in enumerate(chips):
            cp = pltpu.make_async_remote_copy(src_ref=src.at[pk], dst_ref=out.at[me], send_sem=send_sems.at[j],
                                              recv_sem=recv_sems.at[j], device_id=(px, py, c), device_id_type=MESH_ID)
            cp.start()
            sends.append(cp)
        for j, (px, py, pk) in enumerate(chips):
            land = out.at[pk]
            pltpu.make_async_remote_copy(src_ref=land, dst_ref=land, send_sem=send_sems.at[j], recv_sem=recv_sems.at[j],
                                         device_id=(px, py, c), device_id_type=MESH_ID).wait_recv()
        for cp in sends:
            cp.wait_send()
        own.wait()

    return pl.pallas_call(body, out_shape=jax.ShapeDtypeStruct((N_CHIPS, rh, LANES), part.dtype),
                          in_specs=[ANY_SPEC], out_specs=ANY_SPEC,
                          scratch_shapes=[pltpu.SemaphoreType.DMA((3,)), pltpu.SemaphoreType.DMA((3,)),
                                          pltpu.SemaphoreType.DMA(())],
                          name="grad_scatter_partials")(part)


def _sum_chips(q, *, tr):
    rh = q.shape[1]

    def body(q_ref, o_ref):
        parts = [q_ref[k].astype(F32) for k in range(N_CHIPS)]
        o_ref[...] = ((parts[0] + parts[1]) + parts[2]) + parts[3]

    return pl.pallas_call(body, grid=(rh // tr,),
                          in_specs=[pl.BlockSpec((N_CHIPS, tr, LANES), lambda i: (0, i, 0))],
                          out_specs=pl.BlockSpec((tr, LANES), lambda i: (i, 0)),
                          out_shape=jax.ShapeDtypeStruct((rh, LANES), F32), name="grad_sum_chips",
                          compiler_params=_cparams("parallel"))(q)


def _join_halves(half):
    rh = half.shape[0]

    def body(src, out, send_sem, recv_sem, local_sem):
        x, y, c = lax.axis_index("x"), lax.axis_index("y"), lax.axis_index("c")
        own = pltpu.make_async_copy(src, out.at[c], local_sem)
        own.start()
        cp = pltpu.make_async_remote_copy(src_ref=src, dst_ref=out.at[c], send_sem=send_sem, recv_sem=recv_sem,
                                          device_id=(x, y, 1 - c), device_id_type=MESH_ID)
        cp.start()
        cp.wait_send()
        land = out.at[1 - c]
        pltpu.make_async_remote_copy(src_ref=land, dst_ref=land, send_sem=send_sem, recv_sem=recv_sem,
                                     device_id=(x, y, 1 - c), device_id_type=MESH_ID).wait_recv()
        own.wait()

    return pl.pallas_call(body, out_shape=jax.ShapeDtypeStruct((2, rh, LANES), half.dtype),
                          in_specs=[ANY_SPEC], out_specs=ANY_SPEC,
                          scratch_shapes=[pltpu.SemaphoreType.DMA(()), pltpu.SemaphoreType.DMA(()),
                                          pltpu.SemaphoreType.DMA(())],
                          name="grad_join_halves")(half)


def _allreduce_small(v):
    rows = v.shape[0]

    def body(v_ref, o_ref, buf, send_sems, recv_sems):
        x, y, c = lax.axis_index("x"), lax.axis_index("y"), lax.axis_index("c")
        me = 4 * x + 2 * y + c
        buf[me] = v_ref[...]
        sends = []
        for j in range(1, N_DEV):
            px, py, pc = x ^ ((j >> 2) & 1), y ^ ((j >> 1) & 1), c ^ (j & 1)
            cp = pltpu.make_async_remote_copy(src_ref=v_ref, dst_ref=buf.at[me], send_sem=send_sems.at[j - 1],
                                              recv_sem=recv_sems.at[j - 1], device_id=(px, py, pc), device_id_type=MESH_ID)
            cp.start()
            sends.append(cp)
        for j in range(1, N_DEV):
            px, py, pc = x ^ ((j >> 2) & 1), y ^ ((j >> 1) & 1), c ^ (j & 1)
            land = buf.at[4 * px + 2 * py + pc]
            pltpu.make_async_remote_copy(src_ref=land, dst_ref=land, send_sem=send_sems.at[j - 1],
                                         recv_sem=recv_sems.at[j - 1], device_id=(px, py, pc),
                                         device_id_type=MESH_ID).wait_recv()
        for cp in sends:
            cp.wait_send()
        acc = buf[0]
        for d in range(1, N_DEV):
            acc = acc + buf[d]
        o_ref[...] = acc

    vm = pl.BlockSpec(memory_space=pltpu.VMEM)
    return pl.pallas_call(body, out_shape=jax.ShapeDtypeStruct((rows, LANES), F32), in_specs=[vm], out_specs=vm,
                          scratch_shapes=[pltpu.VMEM((N_DEV, rows, LANES), F32), pltpu.SemaphoreType.DMA((N_DEV - 1,)),
                                          pltpu.SemaphoreType.DMA((N_DEV - 1,))],
                          name="allreduce_small")(v)


def _adamw(w, g, m, v, *, name):
    shape = w.shape
    cols = shape[-1] if w.ndim > 1 else shape[0]
    rows = w.size // cols
    w2, g2, m2, v2 = (t.reshape(rows, cols) for t in (w, g, m, v))
    tr = rows if rows <= 256 else _tile(rows, 256, 8)

    def fn(i, wv, gv, mv, vv):
        mn = ADAM_B1 * mv + (1.0 - ADAM_B1) * gv
        vn = ADAM_B2 * vv + (1.0 - ADAM_B2) * (gv * gv)
        m_hat = mn / (1.0 - ADAM_B1 ** ADAM_STEP)
        v_hat = vn / (1.0 - ADAM_B2 ** ADAM_STEP)
        delta = -ADAM_LR * (m_hat / (jnp.sqrt(v_hat) + ADAM_EPS) + ADAM_WD * wv)
        return delta, mn, vn

    ins = [_rspec(t, tr) for t in (w2, g2, m2, v2)]
    d, mn, vn = _row_call(fn, rows, tr, ins, [(cols, F32)] * 3, name=name)
    return d.reshape(shape), mn.reshape(shape), vn.reshape(shape)


def _rope_cols(w):
    z = jnp.zeros(w.shape[:-1] + (32,), w.dtype)
    return jnp.concatenate([w[..., :32], z, w[..., 32:], z], axis=-1)


def _rope_cols_inv(w):
    return jnp.concatenate([w[..., :32], w[..., 64:96]], axis=-1)


def _layer_layouts(W, i):
    w_in = W['w_in'][i]
    u, cq, ckv = w_in[:, :512], w_in[:, 512:1024], w_in[:, 1024:1280]
    kr, ga, gb = w_in[:, 1280:1344], w_in[:, 1344:2368], w_in[:, 2368:]
    L = {}
    L['w_in'] = jnp.concatenate([ga, gb, u, cq, ckv, _rope_cols(kr)], axis=1)
    wq = W['w_uq'][i]
    L['w_q'] = jnp.concatenate([wq[..., :QK_NOPE], _rope_cols(wq[..., QK_NOPE:])], axis=-1).reshape(Q_LORA, -1)
    wkv = W['w_ukv'][i]
    L['w_k'] = jnp.concatenate([wkv[..., :QK_NOPE], jnp.zeros_like(wkv[..., :LANES])], axis=-1).reshape(KV_LORA, -1)
    L['w_v'] = wkv[..., QK_NOPE:].reshape(KV_LORA, -1)
    L['w_gu'] = jnp.concatenate([W['w_gate'][i], W['w_up'][i]], axis=1)
    for n in ('w_a', 'w_b', 'w_o', 'w_down', 'w_ple_gate', 'w_ple'):
        L[n] = W[n][i]
    return L


def _layer_grads_to_reference_layout(G):
    d = G['w_in']
    ga, gb, u = d[:, ZC_GA:ZC_GB], d[:, ZC_GB:ZC_U], d[:, ZC_U:ZC_CQ]
    cq, ckv, kr = d[:, ZC_CQ:ZC_CKV], d[:, ZC_CKV:ZC_KR], _rope_cols_inv(d[:, ZC_KR:])
    out = {'w_in': jnp.concatenate([u, cq, ckv, kr, ga, gb], axis=1)}
    dq = G['w_q'].reshape(Q_LORA, N_HEADS, HEAD_PAD)
    out['w_uq'] = jnp.concatenate([dq[..., :QK_NOPE], _rope_cols_inv(dq[..., QK_NOPE:])], axis=-1)
    dk = G['w_k'].reshape(KV_LORA, N_HEADS, HEAD_PAD)[..., :QK_NOPE]
    dv = G['w_v'].reshape(KV_LORA, N_HEADS, V_HEAD)
    out['w_ukv'] = jnp.concatenate([dk, dv], axis=-1)
    out['w_gate'], out['w_up'] = G['w_gu'][:, :D_FF], G['w_gu'][:, D_FF:]
    for n in ('w_a', 'w_b', 'w_o', 'w_down', 'w_ple_gate', 'w_ple'):
        out[n] = G[n]
    return out


def _pack_rows(parts, row_mult):
    flat = jnp.concatenate([p.reshape(-1) for p in parts])
    n = flat.shape[0]
    per = LANES * row_mult
    padded = -(-n // per) * per
    return jnp.pad(flat, (0, padded - n)).reshape(-1, LANES)


def _unpack_rows(flat2d, shapes):
    flat = flat2d.reshape(-1)
    out, off = [], 0
    for s in shapes:
        n = int(np.prod(s))
        out.append(flat[off:off + n].reshape(s))
        off += n
    return out


def _layer_fwd(i, x, p_i, L, norms, w_pool_bf, pool_scale, cos_t, sin_t, tr, blk):
    sv = {'x': x}
    z, sv['h'] = _mm_nn(x, L['w_in'], name=f"l{i}_in_proj", out_dtype=F32, gain=norms['norm_mix'], emit_a=True, tn=1152)
    sv['z'] = z
    sv['pooled'], sv['mixed'], sv['pm'] = _pool_fwd(z, w_pool_bf, pool_scale, name=f"l{i}_pool", tr=tr)
    sv['ya'] = _mm_nn(sv['pm'], L['w_a'], name=f"l{i}_ya", out_dtype=BF16, tn=1024)

    def q_epi(acc, ct, st):
        return jnp.concatenate([acc[:, :QK_NOPE], _rope(acc[:, QK_NOPE:], ct, st)], axis=1)

    def k_epi(acc, kr, ct, st):
        return jnp.concatenate([acc[:, :QK_NOPE], _rope(kr, ct, st)], axis=1)

    rope_rows = [(cos_t, LANES, 0), (sin_t, LANES, 0)]
    sv['q'], sv['cqn'] = _mm_nn(z, L['w_q'], name=f"l{i}_q_proj", out_dtype=BF16, a_col=ZC_CQ, gain=norms['q_norm'],
                                emit_a=True, epi=q_epi, epi_rows=rope_rows, tn=HEAD_PAD)
    sv['k'], sv['ckvn'] = _mm_nn(z, L['w_k'], name=f"l{i}_k_proj", out_dtype=BF16, a_col=ZC_CKV, gain=norms['kv_norm'],
                                 emit_a=True, epi=k_epi, epi_rows=[(z, LANES, ZC_KR // LANES)] + rope_rows, tn=HEAD_PAD)
    sv['v'] = _mm_nn(sv['ckvn'], L['w_v'], name=f"l{i}_v_proj", out_dtype=BF16, tn=1024)
    sv['o'], sv['lse'] = _flash_fwd(sv['q'], sv['k'], sv['v'], name=f"l{i}_attn", blk=blk)
    sv['yb'] = _mm_nn(sv['o'], L['w_b'], name=f"l{i}_yb", out_dtype=BF16, tn=1024)

    def merge(_, ga, gb, ya, yb):
        return _sigmoid(ga) * ya.astype(F32) + _sigmoid(gb) * yb.astype(F32)

    T = x.shape[0]
    sv['merged'] = _row_call(merge, T, tr, [_rspec(z, tr, D_MODEL, 0), _rspec(z, tr, D_MODEL, 1), _rspec(sv['ya'], tr),
                                            _rspec(sv['yb'], tr)], [(D_MODEL, BF16)], name=f"l{i}_merge")[0]
    x1 = _mm_nn(sv['merged'], L['w_o'], name=f"l{i}_wo", out_dtype=F32, res=x, tn=1024)
    sv['x1'] = x1
    sv['gu'], sv['h2'] = _mm_nn(x1, L['w_gu'], name=f"l{i}_gate_up", out_dtype=BF16, gain=norms['norm_ffn'], emit_a=True,
                                tn=1408)

    def swiglu(_, g, u):
        g = g.astype(F32)
        return g * _sigmoid(g) * u.astype(F32)

    trw = _wide_rows(tr)
    sv['act'] = _row_call(swiglu, T, trw, [_rspec(sv['gu'], trw, D_FF, 0), _rspec(sv['gu'], trw, D_FF, 1)], [(D_FF, BF16)],
                          name=f"l{i}_swiglu")[0]
    x2 = _mm_nn(sv['act'], L['w_down'], name=f"l{i}_down", out_dtype=F32, res=x1, tn=1024)
    sv['x2'] = x2
    sv['logit'], sv['h3'] = _mm_nn(x2, L['w_ple_gate'], name=f"l{i}_ple_gate", out_dtype=F32, gain=norms['norm_ple'],
                                   emit_a=True, tn=1024)
    sv['pe'] = _mm_nn(p_i, L['w_ple'], name=f"l{i}_ple", out_dtype=F32, tn=1024)

    def ple(_, xv, lg, pe):
        return xv + _sigmoid(lg) * pe

    x3 = _row_call(ple, T, tr, [_rspec(x2, tr), _rspec(sv['logit'], tr), _rspec(sv['pe'], tr)], [(D_MODEL, F32)],
                   name=f"l{i}_ple_add")[0]
    return x3, sv


def _layer_bwd(i, dx3, sv, p_i, L, norms, w_pool_bf, pool_scale, cos_t, sin_t, tr, blk):
    T = dx3.shape[0]
    G = {}
    z = sv['z']

    def ple_bwd(_, d, lg, pe):
        g = _sigmoid(lg)
        return d * pe * g * (1.0 - g), d * g

    dlogit, dpe = _row_call(ple_bwd, T, tr, [_rspec(dx3, tr), _rspec(sv['logit'], tr), _rspec(sv['pe'], tr)],
                            [(D_MODEL, BF16), (D_MODEL, BF16)], name=f"l{i}_ple_bwd")
    G['w_ple_gate'] = _mm_tn(sv['h3'], dlogit, name=f"l{i}_dw_ple_gate", tn=1024)
    G['w_ple'] = _mm_tn(p_i, dpe, name=f"l{i}_dw_ple", tn=1024)
    dh3 = _mm_nt([(dlogit, L['w_ple_gate'])], name=f"l{i}_dh3", out_dtype=F32)
    dx2, dx2_bf, G['norm_ple'] = _rms_bwd_rows(sv['x2'], 0, norms['norm_ple'], dh3, dx3, name=f"l{i}_norm_ple_bwd", tr=tr)

    dact = _mm_nt([(dx2_bf, L['w_down'])], name=f"l{i}_dact", out_dtype=BF16)
    G['w_down'] = _mm_tn(sv['act'], dx2_bf, name=f"l{i}_dw_down", tk=1408, tn=1024)

    def swiglu_bwd(_, da, g, u):
        da, g, u = da.astype(F32), g.astype(F32), u.astype(F32)
        sg = _sigmoid(g)
        dg = da * u * sg * (1.0 + g * (1.0 - sg))
        du = da * g * sg
        return jnp.concatenate([dg, du], axis=1)

    trw = _wide_rows(tr)
    dgu = _row_call(swiglu_bwd, T, trw, [_rspec(dact, trw), _rspec(sv['gu'], trw, D_FF, 0), _rspec(sv['gu'], trw, D_FF, 1)],
                    [(2 * D_FF, BF16)], name=f"l{i}_swiglu_bwd")[0]
    G['w_gu'] = _mm_tn(sv['h2'], dgu, name=f"l{i}_dw_gate_up", tn=1408)
    dh2 = _mm_nt([(dgu, L['w_gu'])], name=f"l{i}_dh2", out_dtype=F32, tm=256)
    dx1, dx1_bf, G['norm_ffn'] = _rms_bwd_rows(sv['x1'], 0, norms['norm_ffn'], dh2, dx2, name=f"l{i}_norm_ffn_bwd", tr=tr)

    dmerged = _mm_nt([(dx1_bf, L['w_o'])], name=f"l{i}_dmerged", out_dtype=F32)
    G['w_o'] = _mm_tn(sv['merged'], dx1_bf, name=f"l{i}_dw_o", tn=1024)

    def merge_bwd(_, dm, ga, gb, ya, yb):
        sa, sb = _sigmoid(ga), _sigmoid(gb)
        ya, yb = ya.astype(F32), yb.astype(F32)
        return dm * ya * sa * (1.0 - sa), dm * yb * sb * (1.0 - sb), dm * sa, dm * sb

    dga, dgb, dya, dyb = _row_call(merge_bwd, T, tr, [_rspec(dmerged, tr), _rspec(z, tr, D_MODEL, 0), _rspec(z, tr, D_MODEL, 1),
                                                      _rspec(sv['ya'], tr), _rspec(sv['yb'], tr)],
                                   [(D_MODEL, BF16)] * 4, name=f"l{i}_merge_bwd")

    G['w_b'] = _mm_tn(sv['o'], dyb, name=f"l{i}_dw_b", tn=1024)
    do = _mm_nt([(dyb, L['w_b'])], name=f"l{i}_do", out_dtype=BF16)
    delta = _attn_delta(do, sv['o'], name=f"l{i}_attn_delta", tr=tr)
    dq, dk, dv = _flash_bwd(sv['q'], sv['k'], sv['v'], sv['lse'], delta, do, name=f"l{i}_attn_bwd", blk=blk)

    def dq_rope(_, d, ct, st):
        d = d.astype(F32)
        out = []
        for h in range(N_HEADS):
            out.append(d[:, h * HEAD_PAD:h * HEAD_PAD + QK_NOPE])
            out.append(_rope_bwd(d[:, h * HEAD_PAD + QK_NOPE:(h + 1) * HEAD_PAD], ct, st))
        return jnp.concatenate(out, axis=1)

    dq = _row_call(dq_rope, T, trw, [_rspec(dq, trw), _rspec(cos_t, trw), _rspec(sin_t, trw)], [(N_HEADS * HEAD_PAD, BF16)],
                   name=f"l{i}_dq_rope")[0]

    def dk_rope(_, d, ct, st):
        d = d.astype(F32)
        acc = d[:, QK_NOPE:HEAD_PAD]
        for h in range(1, N_HEADS):
            acc = acc + d[:, h * HEAD_PAD + QK_NOPE:(h + 1) * HEAD_PAD]
        return _rope_bwd(acc, ct, st)

    dkr = _row_call(dk_rope, T, tr, [_rspec(dk, tr), _rspec(cos_t, tr), _rspec(sin_t, tr)], [(LANES, BF16)],
                    name=f"l{i}_dk_rope")[0]
    G['w_q'] = _mm_tn(sv['cqn'], dq, name=f"l{i}_dw_q", tn=1024)
    G['w_k'] = _mm_tn(sv['ckvn'], dk, name=f"l{i}_dw_k", tn=1024)
    G['w_v'] = _mm_tn(sv['ckvn'], dv, name=f"l{i}_dw_v", tn=1024)
    dcqn = _mm_nt([(dq, L['w_q'])], name=f"l{i}_dcqn", out_dtype=F32)
    dckvn = _mm_nt([(dk, L['w_k']), (dv, L['w_v'])], name=f"l{i}_dckvn", out_dtype=F32)
    dcq, G['q_norm'] = _rms_bwd_rows(z, ZC_CQ, norms['q_norm'], dcqn, None, name=f"l{i}_q_norm_bwd", tr=tr, emit_bf16=False)
    dckv, G['kv_norm'] = _rms_bwd_rows(z, ZC_CKV, norms['kv_norm'], dckvn, None, name=f"l{i}_kv_norm_bwd", tr=tr,
                                       emit_bf16=False)

    G['w_a'] = _mm_tn(sv['pm'], dya, name=f"l{i}_dw_a", tn=1024)
    dpm = _mm_nt([(dya, L['w_a'])], name=f"l{i}_dpm", out_dtype=F32)
    dpool, dpool_cnt, G['pool_scale'], G['w_pool'] = _pool_bwd_mix(dpm, sv['mixed'], sv['pooled'], w_pool_bf, pool_scale,
                                                                   name=f"l{i}_pool_bwd_mix", tr=tr)
    du = _pool_bwd_window(dpool, dpool_cnt, name=f"l{i}_pool_bwd_window", tr=tr)

    def join(_, a, b, c, d, e, f):
        return jnp.concatenate([a, b, c, d.astype(BF16), e.astype(BF16), f], axis=1)

    dz = _row_call(join, T, trw, [_rspec(t, trw) for t in (dga, dgb, du, dcq, dckv, dkr)], [(Z_WIDTH, BF16)],
                   name=f"l{i}_dz_join")[0]
    G['w_in'] = _mm_tn(sv['h'], dz, name=f"l{i}_dw_in", tn=1152)
    dh = _mm_nt([(dz, L['w_in'])], name=f"l{i}_dh", out_dtype=F32)
    dx, G['norm_mix'] = _rms_bwd_rows(sv['x'], 0, norms['norm_mix'], dh, dx1, name=f"l{i}_norm_mix_bwd", tr=tr,
                                      emit_bf16=False)
    return dx, G


def kernel(x, p, positions, norm_mix, w_in, w_pool, pool_scale, q_norm, kv_norm, w_uq, w_ukv, w_a, w_b, w_o, norm_ffn, w_gate, w_up, w_down, norm_ple, w_ple_gate, w_ple, final_norm, loss_target, m_norm_mix, m_w_in, m_w_pool, m_pool_scale, m_q_norm, m_kv_norm, m_w_uq, m_w_ukv, m_w_a, m_w_b, m_w_o, m_norm_ffn, m_w_gate, m_w_up, m_w_down, m_norm_ple, m_w_ple_gate, m_w_ple, m_final_norm, v_norm_mix, v_w_in, v_w_pool, v_pool_scale, v_q_norm, v_kv_norm, v_w_uq, v_w_ukv, v_w_a, v_w_b, v_w_o, v_norm_ffn, v_w_gate, v_w_up, v_w_down, v_norm_ple, v_w_ple_gate, v_w_ple, v_final_norm):
    given = dict(locals())
    weights = {n: given[n] for n in WEIGHTS}
    T = x.shape[1]
    tr = min(512, max(T // 2, 8))
    blk = min(ATTN_BLOCK, max(T // 4, 128))
    x0 = x.reshape(T, D_MODEL)
    target = loss_target.reshape(T, D_MODEL)

    names = list(SHARDED)
    shard_shapes = [weights[n].shape for n in names]
    flat = _pack_rows([weights[n].astype(BF16) for n in names], row_mult=1024)
    R = flat.shape[0]
    gathered = _gather_weights(flat).reshape(N_CHIPS, R, LANES)
    per_chip = [_unpack_rows(gathered[k], shard_shapes) for k in range(N_CHIPS)]
    W = {n: jnp.concatenate([per_chip[k][j] for k in range(N_CHIPS)], axis=SHARDED[n]) for j, n in enumerate(names)}
    layouts = [_layer_layouts(W, i) for i in range(DEPTH)]
    w_pool_bf = w_pool.astype(BF16)

    inv_freq = 1.0 / (ROPE_THETA ** (jnp.arange(0, QK_ROPE, 2, dtype=F32) / QK_ROPE))
    zero32 = jnp.zeros((32,), F32)
    freq_row = jnp.concatenate([inv_freq, zero32, inv_freq, zero32]).reshape(1, LANES)
    cos_mask = jnp.concatenate([jnp.ones((32,), F32), zero32, jnp.ones((32,), F32), zero32]).reshape(1, LANES)
    sin_sign = jnp.concatenate([-jnp.ones((32,), F32), zero32, jnp.ones((32,), F32), zero32]).reshape(1, LANES)

    def rope_tables(_, pos, fr, cm, ss):
        ang = pos.astype(F32) * fr
        return jnp.cos(ang) * cm, jnp.sin(ang) * ss

    pos_col = positions.reshape(T, 1)
    cos_t, sin_t = _row_call(rope_tables, T, tr, [_rspec(pos_col, tr), _bspec(freq_row), _bspec(cos_mask), _bspec(sin_sign)],
                             [(LANES, F32), (LANES, F32)], name="rope_tables")

    xs = x0
    saved = []
    for i in range(DEPTH):
        norms = {n: weights[n][i] for n in ('norm_mix', 'q_norm', 'kv_norm', 'norm_ffn', 'norm_ple')}
        xs, sv = _layer_fwd(i, xs, p[i, 0], layouts[i], norms, w_pool_bf[i], pool_scale[i], cos_t, sin_t, tr, blk)
        saved.append((sv, norms))

    def head(_, xv, tv, gv):
        rstd = lax.rsqrt(jnp.mean(xv * xv, axis=-1, keepdims=True) + EPS)
        xhat = xv * rstd
        err = xhat * gv - tv
        loss = 0.5 * jnp.sum(jnp.mean(err * err, axis=-1, keepdims=True), axis=0, keepdims=True)
        dy = err * (1.0 / D_MODEL)
        dg = jnp.sum(dy * xhat, axis=0, keepdims=True)
        dxh = dy * gv
        dx = rstd * (dxh - xhat * jnp.mean(dxh * xhat, axis=-1, keepdims=True))
        return dx, jnp.broadcast_to(loss, (1, LANES)), dg

    dx, loss_part, g_final = _row_call(head, T, tr, [_rspec(xs, tr), _rspec(target, tr), _bspec(final_norm.reshape(1, D_MODEL))],
                                       [(D_MODEL, F32)], [(1, LANES), (1, D_MODEL)], name="loss_head")
    loss = lax.psum(loss_part[0, 0], ("x", "y", "c"))

    layer_grads = [None] * DEPTH
    for i in reversed(range(DEPTH)):
        sv, norms = saved[i]
        dx, layer_grads[i] = _layer_bwd(i, dx, sv, p[i, 0], layouts[i], norms, w_pool_bf[i], pool_scale[i], cos_t, sin_t, tr,
                                        blk)
    grad_x = dx.reshape(x.shape)

    ref_layout = [_layer_grads_to_reference_layout(g) for g in layer_grads]
    local = {n: jnp.stack([ref_layout[i][n] for i in range(DEPTH)]) for n in names}
    for n in ('norm_mix', 'q_norm', 'kv_norm', 'norm_ffn', 'norm_ple', 'pool_scale'):
        local[n] = jnp.stack([layer_grads[i][n].reshape(-1) for i in range(DEPTH)])
    local['w_pool'] = jnp.stack([layer_grads[i]['w_pool'] for i in range(DEPTH)])
    local['final_norm'] = g_final.reshape(-1)

    send = []
    for k in range(N_CHIPS):
        parts = []
        for n in names:
            ax = SHARDED[n]
            size = local[n].shape[ax] // N_CHIPS
            parts.append(lax.slice_in_dim(local[n], k * size, (k + 1) * size, axis=ax))
        send.append(_pack_rows(parts, row_mult=1024))
    rh = R // 2
    trr = _tile(rh, 2048, 8)
    g_all = jnp.stack(send).reshape(N_CHIPS, 2, rh, LANES)
    part = _add_halves(g_all, _swap_halves(g_all), tr=trr)
    reduced_half = _sum_chips(_scatter_partials(part), tr=trr)
    reduced = _join_halves(reduced_half).reshape(R, LANES)
    grads = dict(zip(names, _unpack_rows(reduced, shard_shapes)))

    rep_shapes = [weights[n].shape for n in REPLICATED]
    rep = _allreduce_small(_pack_rows([local[n] for n in REPLICATED], row_mult=8))
    grads.update(zip(REPLICATED, _unpack_rows(rep, rep_shapes)))

    deltas, new_m, new_v = {}, {}, {}
    for n in WEIGHTS:
        deltas[n], new_m[n], new_v[n] = _adamw(weights[n], grads[n], given['m_' + n], given['v_' + n], name=f"adamw_{n}")
    return (loss, grad_x, *[grads[n] for n in WEIGHTS], *[deltas[n] for n in WEIGHTS], *[new_m[n] for n in WEIGHTS],
            *[new_v[n] for n in WEIGHTS])
```

```python
import functools

import numpy as np
import jax
import jax.numpy as jnp
from jax import lax
from jax.experimental import pallas as pl
from jax.experimental.pallas import tpu as pltpu

F32 = jnp.float32
BF16 = jnp.bfloat16

D_MODEL = 1024
DEPTH = 2
PLE_DIM = 256
POOL_WINDOWS = (2, 4, 8, 16)
POOL_GROUP = 128
POOL_WIDTH = 512
N_HEADS = 8
Q_LORA = 512
KV_LORA = 256
QK_NOPE = 128
QK_ROPE = 64
QK_HEAD = 192
V_HEAD = 128
D_FF = 2816
ROPE_THETA = 10000.0
EPS = 1e-6
ATTN_SCALE = QK_HEAD ** -0.5

ADAM_LR = 0.001
ADAM_B1 = 0.9
ADAM_B2 = 0.999
ADAM_EPS = 1e-08
ADAM_WD = 0.01
ADAM_STEP = 10

LANES = 128
HALO = 16
HEAD_PAD = 256
V7X_VMEM_BYTES = 64 * 1024 * 1024
VMEM_LIMIT = (V7X_VMEM_BYTES * 3) // 4
N_CHIPS = 4
N_DEV = 8
NEG_INF = -1e30

ZC_GA, ZC_GB, ZC_U, ZC_CQ, ZC_CKV, ZC_KR = 0, 1024, 2048, 2560, 3072, 3328
Z_WIDTH = 3456

WEIGHTS = ['norm_mix', 'w_in', 'w_pool', 'pool_scale', 'q_norm', 'kv_norm', 'w_uq', 'w_ukv', 'w_a', 'w_b', 'w_o',
           'norm_ffn', 'w_gate', 'w_up', 'w_down', 'norm_ple', 'w_ple_gate', 'w_ple', 'final_norm']
SHARDED = {'w_in': 2, 'w_uq': 1, 'w_ukv': 1, 'w_a': 2, 'w_b': 1, 'w_o': 1, 'w_gate': 2, 'w_up': 2, 'w_down': 1,
           'w_ple_gate': 1, 'w_ple': 2}
REPLICATED = [n for n in WEIGHTS if n not in SHARDED]


def _tile(n, target, mult=LANES):
    if n <= target:
        return n
    best = None
    for t in range(mult, target + 1, mult):
        if n % t == 0:
            best = t
    assert best is not None, (n, target)
    return best


def _cparams(*sem):
    return pltpu.CompilerParams(dimension_semantics=sem, vmem_limit_bytes=VMEM_LIMIT)


def _rope(t, cos_t, sin_t):
    return t * cos_t + pltpu.roll(t, 64, 1) * sin_t


def _rope_bwd(d, cos_t, sin_t):
    return d * cos_t + pltpu.roll(d * sin_t, 64, 1)


def _sigmoid(v):
    return 1.0 / (1.0 + jnp.exp(-v))


def _mm_nn(a, b, *, name, outs, a_col=0, gain=None, emit_a=False, epi=None, epi_rows=(), tm=512):
    M = a.shape[0]
    K, N = b.shape
    tm = min(tm, M)
    assert a_col % K == 0 and M % tm == 0
    a_blk = a_col // K
    n_rows, n_out = len(epi_rows), len(outs)

    def body(*refs):
        a_ref, b_ref = refs[0], refs[1]
        pos = 2
        g_ref = None
        if gain is not None:
            g_ref = refs[pos]
            pos += 1
        row_refs = refs[pos:pos + n_rows]
        out_refs = refs[pos + n_rows:pos + n_rows + n_out]
        lhs = a_ref[...]
        if gain is not None:
            av = lhs.astype(F32)
            lhs = av * lax.rsqrt(jnp.mean(av * av, axis=-1, keepdims=True) + EPS) * g_ref[...]
        lhs = lhs.astype(BF16)
        if emit_a:
            refs[pos + n_rows + n_out][...] = lhs
        acc = jnp.dot(lhs, b_ref[...], preferred_element_type=F32)
        vals = (acc,) if epi is None else epi(acc, *[r[...] for r in row_refs])
        for r, v in zip(out_refs, vals):
            r[...] = v.astype(r.dtype)

    in_specs = [pl.BlockSpec((tm, K), lambda i: (i, a_blk)), pl.BlockSpec((K, N), lambda i: (0, 0))]
    args = [a, b]
    if gain is not None:
        in_specs.append(pl.BlockSpec((1, K), lambda i: (0, 0)))
        args.append(gain.reshape(1, K).astype(F32))
    for arr, w, blk in epi_rows:
        in_specs.append(pl.BlockSpec((tm, w), lambda i, blk=blk: (i, blk)))
        args.append(arr)
    out_shape = [jax.ShapeDtypeStruct((M, w), dt) for w, dt in outs]
    out_specs = [pl.BlockSpec((tm, w), lambda i: (i, 0)) for w, dt in outs]
    if emit_a:
        out_shape.append(jax.ShapeDtypeStruct((M, K), BF16))
        out_specs.append(pl.BlockSpec((tm, K), lambda i: (i, 0)))
    res = pl.pallas_call(body, grid=(M // tm,), in_specs=in_specs, out_specs=out_specs, out_shape=out_shape,
                         name=name, compiler_params=_cparams("parallel"))(*args)
    return res[0] if len(res) == 1 else res


def _mm_nt(pairs, *, name, outs, epi=None, epi_rows=(), consts=(), accs=(), tm=512):
    M = pairs[0][0].shape[0]
    N = pairs[0][1].shape[0]
    tm = min(tm, M)
    n_p, n_in, n_out, n_acc = len(pairs), len(epi_rows) + len(consts), len(outs), len(accs)

    def body(*refs):
        acc = None
        for k in range(n_p):
            av = refs[2 * k][...].astype(BF16)
            part = lax.dot_general(av, refs[2 * k + 1][...], NT_DIMS, preferred_element_type=F32)
            acc = part if acc is None else acc + part
        pos = 2 * n_p
        extra = [r[...] for r in refs[pos:pos + n_in]]
        pos += n_in
        vals = (acc,) if epi is None else epi(acc, *extra)
        for r, v in zip(refs[pos:pos + n_out], vals[:n_out]):
            r[...] = v.astype(r.dtype)
        if n_acc:
            acc_refs = refs[pos + n_out:pos + n_out + n_acc]

            @pl.when(pl.program_id(0) == 0)
            def _():
                for r in acc_refs:
                    r[...] = jnp.zeros_like(r)
            for r, v in zip(acc_refs, vals[n_out:]):
                r[...] += v

    in_specs, args = [], []
    for a, b in pairs:
        assert a.shape[1] == b.shape[1] and b.shape[0] == N and a.shape[0] == M
        in_specs.append(pl.BlockSpec((tm, a.shape[1]), lambda i: (i, 0)))
        in_specs.append(pl.BlockSpec(b.shape, lambda i: (0, 0)))
        args += [a, b]
    for arr, w, blk in epi_rows:
        in_specs.append(pl.BlockSpec((tm, w), lambda i, blk=blk: (i, blk)))
        args.append(arr)
    for arr in consts:
        in_specs.append(pl.BlockSpec(arr.shape, lambda i, n=arr.ndim: (0,) * n))
        args.append(arr)
    out_shape = [jax.ShapeDtypeStruct((M, w), dt) for w, dt in outs]
    out_specs = [pl.BlockSpec((tm, w), lambda i: (i, 0)) for w, dt in outs]
    for s in accs:
        out_shape.append(jax.ShapeDtypeStruct(s, F32))
        out_specs.append(pl.BlockSpec(s, lambda i, n=len(s): (0,) * n))
    res = pl.pallas_call(body, grid=(M // tm,), in_specs=in_specs, out_specs=out_specs, out_shape=out_shape, name=name,
                         compiler_params=_cparams("arbitrary" if n_acc else "parallel"))(*args)
    return res[0] if len(res) == 1 else res


def _rms_bwd_epi(with_res, emit_bf16):
    def epi(dh, xv, *rest):
        gv = rest[-1]
        xv = xv.astype(F32)
        rstd = lax.rsqrt(jnp.mean(xv * xv, axis=-1, keepdims=True) + EPS)
        xhat = xv * rstd
        dg = jnp.sum(dh * xhat, axis=0, keepdims=True)
        dxh = dh * gv
        dx = rstd * (dxh - xhat * jnp.mean(dxh * xhat, axis=-1, keepdims=True))
        if with_res:
            dx = dx + rest[0].astype(F32)
        return (dx, dx, dg) if emit_bf16 else (dx, dg)
    return epi


def _mm_tn(a, b, *, name, a_col=0, a_w=None, tk=1024, tn=1152, tm=512):
    M = a.shape[0]
    a_w = a.shape[1] if a_w is None else a_w
    N = b.shape[1]
    tm = min(tm, M)
    tk = _tile(a_w, tk)
    tn = _tile(N, tn)
    assert a_col % tk == 0 and M % tm == 0
    a_blk0 = a_col // tk

    def body(a_ref, b_ref, o_ref):
        @pl.when(pl.program_id(2) == 0)
        def _():
            o_ref[...] = jnp.zeros_like(o_ref)
        o_ref[...] += lax.dot_general(a_ref[...].astype(BF16), b_ref[...].astype(BF16), (((0,), (0,)), ((), ())),
                                      preferred_element_type=F32)

    return pl.pallas_call(body, grid=(a_w // tk, N // tn, M // tm),
                          in_specs=[pl.BlockSpec((tm, tk), lambda k, j, m: (m, k + a_blk0)),
                                    pl.BlockSpec((tm, tn), lambda k, j, m: (m, j))],
                          out_specs=pl.BlockSpec((tk, tn), lambda k, j, m: (k, j)),
                          out_shape=jax.ShapeDtypeStruct((a_w, N), F32), name=name,
                          compiler_params=_cparams("parallel", "parallel", "arbitrary"))(a, b)


def _row_call(fn, rows, tr, ins, outs, accs=(), *, name):
    n_in, n_out, n_acc = len(ins), len(outs), len(accs)

    def body(*refs):
        i = pl.program_id(0)
        vals = fn(i, *[r[...] for r in refs[:n_in]])
        if not isinstance(vals, (tuple, list)):
            vals = (vals,)
        for r, v in zip(refs[n_in:n_in + n_out], vals[:n_out]):
            r[...] = v.astype(r.dtype)
        if n_acc:
            acc_refs = refs[n_in + n_out:]

            @pl.when(i == 0)
            def _():
                for r in acc_refs:
                    r[...] = jnp.zeros_like(r)
            for r, v in zip(acc_refs, vals[n_out:]):
                r[...] += v

    out_shape = [jax.ShapeDtypeStruct((rows, w), dt) for w, dt in outs]
    out_specs = [pl.BlockSpec((tr, w), lambda i: (i, 0)) for w, dt in outs]
    for s in accs:
        out_shape.append(jax.ShapeDtypeStruct(s, F32))
        out_specs.append(pl.BlockSpec(s, lambda i, n=len(s): (0,) * n))
    res = pl.pallas_call(body, grid=(rows // tr,), in_specs=[pl.BlockSpec(bs, im) for _, bs, im in ins],
                         out_specs=out_specs, out_shape=out_shape, name=name,
                         compiler_params=_cparams("arbitrary"))(*[a for a, _, _ in ins])
    return res


def _wide_rows(tr):
    return max(tr // 4, 8)


def _rspec(arr, tr, w=None, blk=0):
    w = arr.shape[1] if w is None else w
    return (arr, (tr, w), lambda i, blk=blk: (i, blk))


def _bspec(arr):
    return (arr, arr.shape, lambda i, n=arr.ndim: (0,) * n)


def _pool_counts(i, tr):
    t = (i * tr + lax.broadcasted_iota(jnp.int32, (tr, 1), 0) + 1).astype(F32)
    return [jnp.minimum(t, float(w)) for w in POOL_WINDOWS]


def _pool_fwd(z, w_pool_bf, pool_scale, *, name, tr):
    rows = z.shape[0]
    ublk = ZC_U // POOL_WIDTH
    hpt = tr // HALO

    def fn(i, u, uprev, wp, ps):
        uprev = jnp.where(i > 0, uprev, 0.0)
        ext = jnp.concatenate([uprev, u], axis=0)
        s2 = ext + pltpu.roll(ext, 1, 0)
        s4 = s2 + pltpu.roll(s2, 2, 0)
        s8 = s4 + pltpu.roll(s4, 4, 0)
        s16 = s8 + pltpu.roll(s8, 8, 0)
        cnts = _pool_counts(i, tr)
        pooled, mixed = [], []
        for g, sw in enumerate((s2, s4, s8, s16)):
            lanes = slice(g * POOL_GROUP, (g + 1) * POOL_GROUP)
            pg = sw[HALO:, lanes] / cnts[g] - u[:, lanes]
            pooled.append(pg)
            mixed.append(jnp.dot(pg.astype(BF16), wp[g], preferred_element_type=F32))
        pooled = jnp.concatenate(pooled, axis=1)
        mixed = jnp.concatenate(mixed, axis=1)
        return pooled, mixed, mixed * ps

    ins = [_rspec(z, tr, POOL_WIDTH, ublk),
           (z, (HALO, POOL_WIDTH), lambda i: (jnp.maximum(i * hpt - 1, 0), ublk)),
           _bspec(w_pool_bf), _bspec(pool_scale.reshape(1, POOL_WIDTH))]
    return _row_call(fn, rows, tr, ins, [(POOL_WIDTH, BF16), (POOL_WIDTH, F32), (POOL_WIDTH, BF16)], name=name)


def _pool_bwd_mix(dpm, mixed, pooled, w_pool_bf, pool_scale, *, name, tr):
    rows = dpm.shape[0]

    def fn(i, dv, mv, pv, wp, ps):
        dv = dv.astype(F32)
        dscale = jnp.sum(dv * mv, axis=0, keepdims=True)
        dmix = (dv * ps).astype(BF16)
        cnts = _pool_counts(i, tr)
        dpool, dwp = [], []
        for g in range(len(POOL_WINDOWS)):
            lanes = slice(g * POOL_GROUP, (g + 1) * POOL_GROUP)
            dg = lax.dot_general(dmix[:, lanes], wp[g], (((1,), (1,)), ((), ())), preferred_element_type=F32)
            dpool.append(dg)
            dwp.append(lax.dot_general(pv[:, lanes], dmix[:, lanes], (((0,), (0,)), ((), ())),
                                       preferred_element_type=F32)[None])
        dpool = jnp.concatenate(dpool, axis=1)
        dpool_cnt = jnp.concatenate([dpool[:, g * POOL_GROUP:(g + 1) * POOL_GROUP] / cnts[g]
                                     for g in range(len(POOL_WINDOWS))], axis=1)
        return dpool, dpool_cnt, dscale, jnp.concatenate(dwp, axis=0)

    ins = [_rspec(dpm, tr), _rspec(mixed, tr), _rspec(pooled, tr), _bspec(w_pool_bf),
           _bspec(pool_scale.reshape(1, POOL_WIDTH))]
    return _row_call(fn, rows, tr, ins, [(POOL_WIDTH, F32), (POOL_WIDTH, F32)],
                     [(1, POOL_WIDTH), (len(POOL_WINDOWS), POOL_GROUP, POOL_GROUP)], name=name)


def _pool_bwd_window(dpool, dpool_cnt, *, name, tr):
    rows = dpool.shape[0]
    hpt = tr // HALO
    n_halo = rows // HALO
    n_tiles = rows // tr

    def fn(i, dp, dc, dnext):
        dnext = jnp.where(i < n_tiles - 1, dnext, 0.0)
        ext = jnp.concatenate([dc, dnext], axis=0)
        n = tr + HALO
        s2 = ext + pltpu.roll(ext, n - 1, 0)
        s4 = s2 + pltpu.roll(s2, n - 2, 0)
        s8 = s4 + pltpu.roll(s4, n - 4, 0)
        s16 = s8 + pltpu.roll(s8, n - 8, 0)
        out = []
        for g, sw in enumerate((s2, s4, s8, s16)):
            lanes = slice(g * POOL_GROUP, (g + 1) * POOL_GROUP)
            out.append(sw[:tr, lanes] - dp[:, lanes])
        return jnp.concatenate(out, axis=1)

    ins = [_rspec(dpool, tr), _rspec(dpool_cnt, tr),
           (dpool_cnt, (HALO, POOL_WIDTH), lambda i: (jnp.minimum((i + 1) * hpt, n_halo - 1), 0))]
    return _row_call(fn, rows, tr, ins, [(POOL_WIDTH, BF16)], name=name)[0]


def _causal_pairs(n, k_major):
    if k_major:
        pairs = [(qi, ki) for ki in range(n) for qi in range(ki, n)]
    else:
        pairs = [(qi, ki) for qi in range(n) for ki in range(qi + 1)]
    return (jnp.asarray(np.array([p[0] for p in pairs], np.int32)),
            jnp.asarray(np.array([p[1] for p in pairs], np.int32)), len(pairs))


SUBLANES = 8
NT_DIMS = (((1,), (1,)), ((), ()))
TN_DIMS = (((0,), (0,)), ((), ()))


ATTN_BLOCK = 1024
QUERY_CHUNK = 256
EXP2_SCALE = ATTN_SCALE * 1.4426950408889634


def _scores_t(q_c, k, c, qc, diag):
    s = lax.dot_general(k, q_c, NT_DIMS, preferred_element_type=F32)
    if diag:
        key = lax.broadcasted_iota(jnp.int32, s.shape, 0)
        qry = lax.broadcasted_iota(jnp.int32, s.shape, 1) + c * qc
        s = jnp.where(key <= qry, s, NEG_INF)
    return s


def _flash_fwd(q, k, v, *, name, blk):
    T = q.shape[0]
    n = T // blk
    qc = min(QUERY_CHUNK, blk)
    qtab, ktab, n_pairs = _causal_pairs(n, k_major=False)

    def body(qt, kt, q_ref, k_ref, v_ref, o_ref, lse_ref, m_s, l_s, acc_s):
        p = pl.program_id(1)
        qi, ki = qt[p], kt[p]

        @pl.when(ki == 0)
        def _():
            m_s[...] = jnp.full_like(m_s, NEG_INF)
            l_s[...] = jnp.zeros_like(l_s)
            acc_s[...] = jnp.zeros_like(acc_s)

        def step(diag):
            kv, vv = k_ref[...], v_ref[...]
            chunks = [slice(c * qc, (c + 1) * qc) for c in range(blk // qc)]
            scores = [_scores_t(q_ref[rows, :], kv, c, qc, diag) for c, rows in enumerate(chunks)]
            probs, alphas = [], []
            for rows, s_t in zip(chunks, scores):
                m_prev = m_s[:, rows]
                m_new = jnp.maximum(m_prev, jnp.max(s_t, axis=0, keepdims=True))
                p_t = jnp.exp2((s_t - m_new) * EXP2_SCALE)
                alpha = jnp.exp2((m_prev - m_new) * EXP2_SCALE)
                l_s[:, rows] = alpha * l_s[:, rows] + jnp.sum(p_t, axis=0, keepdims=True)
                m_s[:, rows] = m_new
                probs.append(p_t.astype(BF16))
                alphas.append(alpha)
            for rows, p_t, alpha in zip(chunks, probs, alphas):
                acc_s[:, rows] = alpha * acc_s[:, rows] + lax.dot_general(vv, p_t, TN_DIMS, preferred_element_type=F32)

        @pl.when(ki != qi)
        def _():
            step(False)

        @pl.when(ki == qi)
        def _():
            step(True)
            o_ref[...] = (acc_s[...] / l_s[...]).T.astype(o_ref.dtype)
            lse2 = m_s[...] * EXP2_SCALE + jnp.log2(l_s[...])
            lse_ref[...] = jnp.broadcast_to(lse2, lse_ref.shape)

    grid_spec = pltpu.PrefetchScalarGridSpec(
        num_scalar_prefetch=2, grid=(N_HEADS, n_pairs),
        in_specs=[pl.BlockSpec((blk, HEAD_PAD), lambda h, p, qt, kt: (qt[p], h)),
                  pl.BlockSpec((blk, HEAD_PAD), lambda h, p, qt, kt: (kt[p], h)),
                  pl.BlockSpec((blk, V_HEAD), lambda h, p, qt, kt: (kt[p], h))],
        out_specs=[pl.BlockSpec((blk, V_HEAD), lambda h, p, qt, kt: (qt[p], h)),
                   pl.BlockSpec((SUBLANES, blk), lambda h, p, qt, kt: (h, qt[p]))],
        scratch_shapes=[pltpu.VMEM((1, blk), F32), pltpu.VMEM((1, blk), F32), pltpu.VMEM((V_HEAD, blk), F32)])
    return pl.pallas_call(body, grid_spec=grid_spec,
                          out_shape=[jax.ShapeDtypeStruct((T, N_HEADS * V_HEAD), BF16),
                                     jax.ShapeDtypeStruct((N_HEADS * SUBLANES, T), F32)],
                          name=name, compiler_params=_cparams("parallel", "arbitrary"))(qtab, ktab, q, k, v)


def _attn_delta(do, o, *, name, tr):
    T = do.shape[0]

    def body(do_ref, o_ref, d_ref):
        prod = do_ref[...].astype(F32) * o_ref[...].astype(F32)
        lane_head = lax.broadcasted_iota(jnp.int32, (tr, LANES), 1) // SUBLANES
        mat = jnp.zeros((tr, LANES), F32)
        for h in range(N_HEADS):
            d_h = jnp.sum(prod[:, h * V_HEAD:(h + 1) * V_HEAD], axis=1, keepdims=True)
            mat = jnp.where(lane_head == h, d_h, mat)
        d_ref[...] = mat.T[:N_HEADS * SUBLANES, :]

    return pl.pallas_call(body, grid=(T // tr,),
                          in_specs=[pl.BlockSpec((tr, N_HEADS * V_HEAD), lambda i: (i, 0)),
                                    pl.BlockSpec((tr, N_HEADS * V_HEAD), lambda i: (i, 0))],
                          out_specs=pl.BlockSpec((N_HEADS * SUBLANES, tr), lambda i: (0, i)),
                          out_shape=jax.ShapeDtypeStruct((N_HEADS * SUBLANES, T), F32), name=name,
                          compiler_params=_cparams("parallel"))(do, o)


def _flash_bwd(q, k, v, lse, delta, do, *, name, blk):
    T = q.shape[0]
    n = T // blk
    qc = min(QUERY_CHUNK, blk)
    qtab, ktab, n_pairs = _causal_pairs(n, k_major=True)

    def body(qt, kt, q_ref, k_ref, v_ref, lse_ref, delta_ref, do_ref, dq_ref, dk_ref, dv_ref, dq_s, dk_s, dv_s):
        p = pl.program_id(1)
        qi, ki = qt[p], kt[p]
        first = qi == ki

        @pl.when(p == 0)
        def _():
            dq_s[...] = jnp.zeros_like(dq_s)

        @pl.when(first)
        def _():
            dk_s[...] = jnp.zeros_like(dk_s)
            dv_s[...] = jnp.zeros_like(dv_s)

        def step(diag):
            kv, vv = k_ref[...], v_ref[...]
            chunks = [slice(c * qc, (c + 1) * qc) for c in range(blk // qc)]
            qs = [q_ref[rows, :] for rows in chunks]
            dos = [do_ref[rows, :] for rows in chunks]
            scores = [_scores_t(q_c, kv, c, qc, diag) for c, q_c in enumerate(qs)]
            dps = [lax.dot_general(vv, do_c, NT_DIMS, preferred_element_type=F32) for do_c in dos]
            probs, dss = [], []
            for rows, s_t, dp_t in zip(chunks, scores, dps):
                p_t = jnp.exp2(s_t * EXP2_SCALE - lse_ref[0:1, rows])
                dss.append((p_t * (dp_t - delta_ref[0:1, rows])).astype(BF16))
                probs.append(p_t.astype(BF16))
            dv_acc = dk_acc = None
            for p_t, ds_t, q_c, do_c in zip(probs, dss, qs, dos):
                dv_c = jnp.dot(p_t, do_c, preferred_element_type=F32)
                dk_c = jnp.dot(ds_t, q_c, preferred_element_type=F32)
                dv_acc = dv_c if dv_acc is None else dv_acc + dv_c
                dk_acc = dk_c if dk_acc is None else dk_acc + dk_c
            for rows, ds_t in zip(chunks, dss):
                dq_s[qi, :, rows] += lax.dot_general(kv, ds_t, TN_DIMS, preferred_element_type=F32)
            dv_s[...] += dv_acc
            dk_s[...] += dk_acc

        @pl.when(jnp.logical_not(first))
        def _():
            step(False)

        @pl.when(first)
        def _():
            step(True)
            dq_ref[...] = (dq_s[qi] * ATTN_SCALE).T.astype(dq_ref.dtype)

        @pl.when(qi == n - 1)
        def _():
            dk_ref[...] = (dk_s[...] * ATTN_SCALE).astype(dk_ref.dtype)
            dv_ref[...] = dv_s[...].astype(dv_ref.dtype)

    qmap = lambda h, p, qt, kt: (qt[p], h)
    kmap = lambda h, p, qt, kt: (kt[p], h)
    smap = lambda h, p, qt, kt: (h, qt[p])
    grid_spec = pltpu.PrefetchScalarGridSpec(
        num_scalar_prefetch=2, grid=(N_HEADS, n_pairs),
        in_specs=[pl.BlockSpec((blk, HEAD_PAD), qmap), pl.BlockSpec((blk, HEAD_PAD), kmap),
                  pl.BlockSpec((blk, V_HEAD), kmap), pl.BlockSpec((SUBLANES, blk), smap),
                  pl.BlockSpec((SUBLANES, blk), smap), pl.BlockSpec((blk, V_HEAD), qmap)],
        out_specs=[pl.BlockSpec((blk, HEAD_PAD), kmap), pl.BlockSpec((blk, HEAD_PAD), kmap),
                   pl.BlockSpec((blk, V_HEAD), kmap)],
        scratch_shapes=[pltpu.VMEM((n, HEAD_PAD, blk), F32), pltpu.VMEM((blk, HEAD_PAD), F32),
                        pltpu.VMEM((blk, V_HEAD), F32)])
    return pl.pallas_call(body, grid_spec=grid_spec,
                          out_shape=[jax.ShapeDtypeStruct((T, N_HEADS * HEAD_PAD), BF16),
                                     jax.ShapeDtypeStruct((T, N_HEADS * HEAD_PAD), BF16),
                                     jax.ShapeDtypeStruct((T, N_HEADS * V_HEAD), BF16)],
                          name=name, compiler_params=_cparams("arbitrary", "arbitrary"))(qtab, ktab, q, k, v, lse, delta, do)


MESH_ID = pl.DeviceIdType.MESH
ANY_SPEC = pl.BlockSpec(memory_space=pl.ANY)


def _other_chips(x, y):
    out = []
    for dx, dy in ((1, 0), (0, 1), (1, 1)):
        px = x ^ dx if dx else x
        py = y ^ dy if dy else y
        out.append((px, py, 2 * px + py))
    return out


def _gather_weights(flat):
    rh = flat.shape[0] // 2

    def body(src2, out, send_sems, recv_sems):
        x, y, c = lax.axis_index("x"), lax.axis_index("y"), lax.axis_index("c")
        me = 2 * x + y
        sib = (x, y, 1 - c)
        chips = _other_chips(x, y)
        sends = []
        for j, (px, py, pk) in enumerate(chips):
            cp = pltpu.make_async_remote_copy(src_ref=src2.at[c], dst_ref=out.at[me, c], send_sem=send_sems.at[j],
                                              recv_sem=recv_sems.at[j], device_id=(px, py, c), device_id_type=MESH_ID)
            cp.start()
            sends.append(cp)
        for j, (px, py, pk) in enumerate(chips):
            land = out.at[pk, c]
            pltpu.make_async_remote_copy(src_ref=land, dst_ref=land, send_sem=send_sems.at[j], recv_sem=recv_sems.at[j],
                                         device_id=(px, py, c), device_id_type=MESH_ID).wait_recv()
            fw = pltpu.make_async_remote_copy(src_ref=land, dst_ref=land, send_sem=send_sems.at[3 + j],
                                              recv_sem=recv_sems.at[3 + j], device_id=sib, device_id_type=MESH_ID)
            fw.start()
            sends.append(fw)
        for j, (px, py, pk) in enumerate(chips):
            land = out.at[pk, 1 - c]
            pltpu.make_async_remote_copy(src_ref=land, dst_ref=land, send_sem=send_sems.at[3 + j],
                                         recv_sem=recv_sems.at[3 + j], device_id=sib, device_id_type=MESH_ID).wait_recv()
        for cp in sends:
            cp.wait_send()

    return pl.pallas_call(body, out_shape=jax.ShapeDtypeStruct((N_CHIPS, 2, rh, LANES), flat.dtype),
                          in_specs=[ANY_SPEC], out_specs=ANY_SPEC,
                          scratch_shapes=[pltpu.SemaphoreType.DMA((6,)), pltpu.SemaphoreType.DMA((6,))],
                          name="gather_weights")(flat.reshape(2, rh, LANES))


def _swap_halves(g):
    rh = g.shape[2]

    def body(src, out, send_sem, recv_sem):
        x, y, c = lax.axis_index("x"), lax.axis_index("y"), lax.axis_index("c")
        cp = pltpu.make_async_remote_copy(src_ref=src.at[:, 1 - c], dst_ref=out, send_sem=send_sem, recv_sem=recv_sem,
                                          device_id=(x, y, 1 - c), device_id_type=MESH_ID)
        cp.start()
        cp.wait()

    return pl.pallas_call(body, out_shape=jax.ShapeDtypeStruct((N_CHIPS, rh, LANES), g.dtype),
                          in_specs=[ANY_SPEC], out_specs=ANY_SPEC,
                          scratch_shapes=[pltpu.SemaphoreType.DMA(()), pltpu.SemaphoreType.DMA(())],
                          name="grad_swap_halves")(g)


def _add_halves(g, got, *, tr):
    rh = g.shape[2]
    nt = rh // tr

    def body(c_ref, g_ref, got_ref, o_ref):
        o_ref[...] = (g_ref[0] + got_ref[...]).astype(o_ref.dtype)

    grid_spec = pltpu.PrefetchScalarGridSpec(
        num_scalar_prefetch=1, grid=(N_CHIPS, nt),
        in_specs=[pl.BlockSpec((1, 1, tr, LANES), lambda k, i, c_ref: (k, c_ref[0], i, 0)),
                  pl.BlockSpec((1, tr, LANES), lambda k, i, c_ref: (k, i, 0))],
        out_specs=pl.BlockSpec((1, tr, LANES), lambda k, i, c_ref: (k, i, 0)))
    core = lax.axis_index("c").astype(jnp.int32).reshape(1)
    return pl.pallas_call(body, grid_spec=grid_spec,
                          out_shape=jax.ShapeDtypeStruct((N_CHIPS, rh, LANES), BF16), name="grad_add_halves",
                          compiler_params=_cparams("parallel", "parallel"))(core, g, got)


def _scatter_partials(part):
    rh = part.shape[1]

    def body(src, out, send_sems, recv_sems, local_sem):
        x, y, c = lax.axis_index("x"), lax.axis_index("y"), lax.axis_index("c")
        me = 2 * x + y
        own = pltpu.make_async_copy(src.at[me], out.at[me], local_sem)
        own.start()
        chips = _other_chips(x, y)
        sends = []
        for j, (px, py, pk) in enumerate(chips):
            cp = pltpu.make_async_remote_copy(src_ref=src.at[pk], dst_ref=out.at[me], send_sem=send_sems.at[j],
                                              recv_sem=recv_sems.at[j], device_id=(px, py, c), device_id_type=MESH_ID)
            cp.start()
            sends.append(cp)
        for j, (px, py, pk) in enumerate(chips):
            land = out.at[pk]
            pltpu.make_async_remote_copy(src_ref=land, dst_ref=land, send_sem=send_sems.at[j], recv_sem=recv_sems.at[j],
                                         device_id=(px, py, c), device_id_type=MESH_ID).wait_recv()
        for cp in sends:
            cp.wait_send()
        own.wait()

    return pl.pallas_call(body, out_shape=jax.ShapeDtypeStruct((N_CHIPS, rh, LANES), part.dtype),
                          in_specs=[ANY_SPEC], out_specs=ANY_SPEC,
                          scratch_shapes=[pltpu.SemaphoreType.DMA((3,)), pltpu.SemaphoreType.DMA((3,)),
                                          pltpu.SemaphoreType.DMA(())],
                          name="grad_scatter_partials")(part)


def _sum_chips(q, *, tr):
    rh = q.shape[1]

    def body(q_ref, o_ref):
        parts = [q_ref[k].astype(F32) for k in range(N_CHIPS)]
        o_ref[...] = ((parts[0] + parts[1]) + parts[2]) + parts[3]

    return pl.pallas_call(body, grid=(rh // tr,),
                          in_specs=[pl.BlockSpec((N_CHIPS, tr, LANES), lambda i: (0, i, 0))],
                          out_specs=pl.BlockSpec((tr, LANES), lambda i: (i, 0)),
                          out_shape=jax.ShapeDtypeStruct((rh, LANES), F32), name="grad_sum_chips",
                          compiler_params=_cparams("parallel"))(q)


def _send_half(half):
    def body(src, out, send_sem, recv_sem):
        x, y, c = lax.axis_index("x"), lax.axis_index("y"), lax.axis_index("c")
        cp = pltpu.make_async_remote_copy(src_ref=src, dst_ref=out, send_sem=send_sem, recv_sem=recv_sem,
                                          device_id=(x, y, 1 - c), device_id_type=MESH_ID)
        cp.start()
        cp.wait()

    return pl.pallas_call(body, out_shape=jax.ShapeDtypeStruct(half.shape, half.dtype),
                          in_specs=[ANY_SPEC], out_specs=ANY_SPEC,
                          scratch_shapes=[pltpu.SemaphoreType.DMA(()), pltpu.SemaphoreType.DMA(())],
                          name="grad_send_half")(half)


def _both_halves(own, got, core, *, tr):
    rh = own.shape[0]

    def body(c_ref, own_ref, got_ref, o_ref):
        mine = pl.program_id(0) == c_ref[0]
        o_ref[0] = jnp.where(mine, own_ref[...], got_ref[...])

    grid_spec = pltpu.PrefetchScalarGridSpec(
        num_scalar_prefetch=1, grid=(2, rh // tr),
        in_specs=[pl.BlockSpec((tr, LANES), lambda h, i, c_ref: (i, 0)),
                  pl.BlockSpec((tr, LANES), lambda h, i, c_ref: (i, 0))],
        out_specs=pl.BlockSpec((1, tr, LANES), lambda h, i, c_ref: (h, i, 0)))
    return pl.pallas_call(body, grid_spec=grid_spec, out_shape=jax.ShapeDtypeStruct((2, rh, LANES), own.dtype),
                          name="grad_both_halves", compiler_params=_cparams("parallel", "parallel"))(core.reshape(1), own, got)


def _allreduce_small(v):
    rows = v.shape[0]

    def body(v_ref, o_ref, buf, send_sems, recv_sems):
        x, y, c = lax.axis_index("x"), lax.axis_index("y"), lax.axis_index("c")
        me = 4 * x + 2 * y + c
        buf[me] = v_ref[...]
        sends = []
        for j in range(1, N_DEV):
            px, py, pc = x ^ ((j >> 2) & 1), y ^ ((j >> 1) & 1), c ^ (j & 1)
            cp = pltpu.make_async_remote_copy(src_ref=v_ref, dst_ref=buf.at[me], send_sem=send_sems.at[j - 1],
                                              recv_sem=recv_sems.at[j - 1], device_id=(px, py, pc), device_id_type=MESH_ID)
            cp.start()
            sends.append(cp)
        for j in range(1, N_DEV):
            px, py, pc = x ^ ((j >> 2) & 1), y ^ ((j >> 1) & 1), c ^ (j & 1)
            land = buf.at[4 * px + 2 * py + pc]
            pltpu.make_async_remote_copy(src_ref=land, dst_ref=land, send_sem=send_sems.at[j - 1],
                                         recv_sem=recv_sems.at[j - 1], device_id=(px, py, pc),
                                         device_id_type=MESH_ID).wait_recv()
        for cp in sends:
            cp.wait_send()
        acc = buf[0]
        for d in range(1, N_DEV):
            acc = acc + buf[d]
        o_ref[...] = acc

    vm = pl.BlockSpec(memory_space=pltpu.VMEM)
    return pl.pallas_call(body, out_shape=jax.ShapeDtypeStruct((rows, LANES), F32), in_specs=[vm], out_specs=vm,
                          scratch_shapes=[pltpu.VMEM((N_DEV, rows, LANES), F32), pltpu.SemaphoreType.DMA((N_DEV - 1,)),
                                          pltpu.SemaphoreType.DMA((N_DEV - 1,))],
                          name="allreduce_small")(v)


def _adamw(w, g, m, v, *, name):
    shape = w.shape
    cols = shape[-1] if w.ndim > 1 else shape[0]
    rows = w.size // cols
    w2, g2, m2, v2 = (t.reshape(rows, cols) for t in (w, g, m, v))
    tr = rows if rows <= 256 else _tile(rows, 256, 8)

    def fn(i, wv, gv, mv, vv):
        mn = ADAM_B1 * mv + (1.0 - ADAM_B1) * gv
        vn = ADAM_B2 * vv + (1.0 - ADAM_B2) * (gv * gv)
        m_hat = mn / (1.0 - ADAM_B1 ** ADAM_STEP)
        v_hat = vn / (1.0 - ADAM_B2 ** ADAM_STEP)
        delta = -ADAM_LR * (m_hat / (jnp.sqrt(v_hat) + ADAM_EPS) + ADAM_WD * wv)
        return delta, mn, vn

    ins = [_rspec(t, tr) for t in (w2, g2, m2, v2)]
    d, mn, vn = _row_call(fn, rows, tr, ins, [(cols, F32)] * 3, name=name)
    return d.reshape(shape), mn.reshape(shape), vn.reshape(shape)


def _rope_cols(w):
    z = jnp.zeros(w.shape[:-1] + (32,), w.dtype)
    return jnp.concatenate([w[..., :32], z, w[..., 32:], z], axis=-1)


def _rope_cols_inv(w):
    return jnp.concatenate([w[..., :32], w[..., 64:96]], axis=-1)


def _layer_layouts(W, i):
    w_in = W['w_in'][i]
    u, cq, ckv = w_in[:, :512], w_in[:, 512:1024], w_in[:, 1024:1280]
    kr, ga, gb = w_in[:, 1280:1344], w_in[:, 1344:2368], w_in[:, 2368:]
    L = {}
    L['w_in'] = jnp.concatenate([ga, gb, u, cq, ckv, _rope_cols(kr)], axis=1)
    wq = W['w_uq'][i]
    L['w_q'] = jnp.concatenate([wq[..., :QK_NOPE], _rope_cols(wq[..., QK_NOPE:])], axis=-1).reshape(Q_LORA, -1)
    wkv = W['w_ukv'][i]
    L['w_k'] = jnp.concatenate([wkv[..., :QK_NOPE], jnp.zeros_like(wkv[..., :LANES])], axis=-1).reshape(KV_LORA, -1)
    L['w_v'] = wkv[..., QK_NOPE:].reshape(KV_LORA, -1)
    L['w_gu'] = jnp.concatenate([W['w_gate'][i], W['w_up'][i]], axis=1)
    for n in ('w_a', 'w_b', 'w_o', 'w_down', 'w_ple_gate', 'w_ple'):
        L[n] = W[n][i]
    return L


def _layer_grads_to_reference_layout(G):
    d = G['w_in']
    ga, gb, u = d[:, ZC_GA:ZC_GB], d[:, ZC_GB:ZC_U], d[:, ZC_U:ZC_CQ]
    cq, ckv, kr = d[:, ZC_CQ:ZC_CKV], d[:, ZC_CKV:ZC_KR], _rope_cols_inv(d[:, ZC_KR:])
    out = {'w_in': jnp.concatenate([u, cq, ckv, kr, ga, gb], axis=1)}
    dq = G['w_q'].reshape(Q_LORA, N_HEADS, HEAD_PAD)
    out['w_uq'] = jnp.concatenate([dq[..., :QK_NOPE], _rope_cols_inv(dq[..., QK_NOPE:])], axis=-1)
    dk = G['w_k'].reshape(KV_LORA, N_HEADS, HEAD_PAD)[..., :QK_NOPE]
    dv = G['w_v'].reshape(KV_LORA, N_HEADS, V_HEAD)
    out['w_ukv'] = jnp.concatenate([dk, dv], axis=-1)
    out['w_gate'], out['w_up'] = G['w_gu'][:, :D_FF], G['w_gu'][:, D_FF:]
    for n in ('w_a', 'w_b', 'w_o', 'w_down', 'w_ple_gate', 'w_ple'):
        out[n] = G[n]
    return out


PACK_ROWS = 2048


def _split_chips(a, axis):
    size = a.shape[axis] // N_CHIPS
    a = a.reshape(a.shape[:axis] + (N_CHIPS, size) + a.shape[axis + 1:])
    return jnp.moveaxis(a, axis, 0)


def _merge_chips(a, axis):
    a = jnp.moveaxis(a, 0, axis)
    return a.reshape(a.shape[:axis] + (N_CHIPS * a.shape[axis + 1],) + a.shape[axis + 2:])


def _pack_rows(parts, row_mult):
    flat = jnp.concatenate([p.reshape(-1) for p in parts])
    n = flat.shape[0]
    per = LANES * row_mult
    padded = -(-n // per) * per
    return jnp.pad(flat, (0, padded - n)).reshape(-1, LANES)


def _unpack_rows(flat2d, shapes):
    flat = flat2d.reshape(-1)
    out, off = [], 0
    for s in shapes:
        n = int(np.prod(s))
        out.append(flat[off:off + n].reshape(s))
        off += n
    return out


def _layer_fwd(i, x, p_i, L, norms, w_pool_bf, pool_scale, cos_t, sin_t, tr, blk):
    sv = {'x': x}
    z, sv['h'] = _mm_nn(x, L['w_in'], name=f"l{i}_in_proj", outs=[(Z_WIDTH, F32)], gain=norms['norm_mix'], emit_a=True,
                        tm=tr // 2)
    sv['z'] = z
    sv['pooled'], sv['mixed'], sv['pm'] = _pool_fwd(z, w_pool_bf, pool_scale, name=f"l{i}_pool", tr=tr)
    sv['ya'] = _mm_nn(sv['pm'], L['w_a'], name=f"l{i}_ya", outs=[(D_MODEL, BF16)], tm=tr)

    def heads(acc, rope_part):
        out = []
        for h in range(N_HEADS):
            out.append(acc[:, h * HEAD_PAD:h * HEAD_PAD + QK_NOPE])
            out.append(rope_part(acc[:, h * HEAD_PAD + QK_NOPE:(h + 1) * HEAD_PAD]))
        return jnp.concatenate(out, axis=1)

    def q_epi(acc, ct, st):
        return (heads(acc, lambda t: _rope(t, ct, st)),)

    def k_epi(acc, kr, ct, st):
        k_pe = _rope(kr, ct, st)
        return (heads(acc, lambda t: k_pe),)

    rope_rows = [(cos_t, LANES, 0), (sin_t, LANES, 0)]
    qk_width = N_HEADS * HEAD_PAD
    sv['q'], sv['cqn'] = _mm_nn(z, L['w_q'], name=f"l{i}_q_proj", outs=[(qk_width, BF16)], a_col=ZC_CQ,
                                gain=norms['q_norm'], emit_a=True, epi=q_epi, epi_rows=rope_rows, tm=tr)
    sv['k'], sv['ckvn'] = _mm_nn(z, L['w_k'], name=f"l{i}_k_proj", outs=[(qk_width, BF16)], a_col=ZC_CKV,
                                 gain=norms['kv_norm'], emit_a=True, epi=k_epi,
                                 epi_rows=[(z, LANES, ZC_KR // LANES)] + rope_rows, tm=tr)
    sv['v'] = _mm_nn(sv['ckvn'], L['w_v'], name=f"l{i}_v_proj", outs=[(N_HEADS * V_HEAD, BF16)], tm=tr)
    sv['o'], sv['lse'] = _flash_fwd(sv['q'], sv['k'], sv['v'], name=f"l{i}_attn", blk=blk)

    def merge_epi(yb, ga, gb, ya):
        return yb, _sigmoid(ga) * ya.astype(F32) + _sigmoid(gb) * yb

    sv['yb'], sv['merged'] = _mm_nn(sv['o'], L['w_b'], name=f"l{i}_yb_merge", outs=[(D_MODEL, BF16), (D_MODEL, BF16)],
                                    epi=merge_epi, epi_rows=[(z, D_MODEL, 0), (z, D_MODEL, 1), (sv['ya'], D_MODEL, 0)],
                                    tm=tr)

    def add_epi(acc, res):
        return (acc + res,)

    x1 = _mm_nn(sv['merged'], L['w_o'], name=f"l{i}_wo", outs=[(D_MODEL, F32)], epi=add_epi, epi_rows=[(x, D_MODEL, 0)],
                tm=tr)
    sv['x1'] = x1

    def swiglu_epi(acc):
        g, u = acc[:, :D_FF], acc[:, D_FF:]
        return acc, g * _sigmoid(g) * u

    sv['gu'], sv['act'], sv['h2'] = _mm_nn(x1, L['w_gu'], name=f"l{i}_gate_up", outs=[(2 * D_FF, BF16), (D_FF, BF16)],
                                           gain=norms['norm_ffn'], emit_a=True, epi=swiglu_epi, tm=tr // 2)
    x2 = _mm_nn(sv['act'], L['w_down'], name=f"l{i}_down", outs=[(D_MODEL, F32)], epi=add_epi,
                epi_rows=[(x1, D_MODEL, 0)], tm=tr)
    sv['x2'] = x2
    sv['logit'], sv['h3'] = _mm_nn(x2, L['w_ple_gate'], name=f"l{i}_ple_gate", outs=[(D_MODEL, F32)],
                                   gain=norms['norm_ple'], emit_a=True, tm=tr)

    def ple_epi(pe, xv, lg):
        return pe, xv + _sigmoid(lg) * pe

    sv['pe'], x3 = _mm_nn(p_i, L['w_ple'], name=f"l{i}_ple", outs=[(D_MODEL, F32), (D_MODEL, F32)], epi=ple_epi,
                          epi_rows=[(x2, D_MODEL, 0), (sv['logit'], D_MODEL, 0)], tm=tr)
    return x3, sv


def _layer_bwd(i, dx3, sv, p_i, L, norms, w_pool_bf, pool_scale, cos_t, sin_t, tr, blk):
    T = dx3.shape[0]
    G = {}
    z = sv['z']

    def ple_bwd(_, d, lg, pe):
        g = _sigmoid(lg)
        return d * pe * g * (1.0 - g), d * g

    dlogit, dpe = _row_call(ple_bwd, T, tr, [_rspec(dx3, tr), _rspec(sv['logit'], tr), _rspec(sv['pe'], tr)],
                            [(D_MODEL, BF16), (D_MODEL, BF16)], name=f"l{i}_ple_bwd")
    G['w_ple_gate'] = _mm_tn(sv['h3'], dlogit, name=f"l{i}_dw_ple_gate", tn=1024)
    G['w_ple'] = _mm_tn(p_i, dpe, name=f"l{i}_dw_ple", tn=1024)
    def gain_row(n):
        return norms[n].reshape(1, -1).astype(F32)

    dx2, dx2_bf, G['norm_ple'] = _mm_nt([(dlogit, L['w_ple_gate'])], name=f"l{i}_dh3_norm_bwd",
                                        outs=[(D_MODEL, F32), (D_MODEL, BF16)], epi=_rms_bwd_epi(True, True),
                                        epi_rows=[(sv['x2'], D_MODEL, 0), (dx3, D_MODEL, 0)], consts=[gain_row('norm_ple')],
                                        accs=[(1, D_MODEL)], tm=tr)

    def swiglu_bwd_epi(da, gu):
        g, u = gu[:, :D_FF].astype(F32), gu[:, D_FF:].astype(F32)
        sg = _sigmoid(g)
        return (jnp.concatenate([da * u * sg * (1.0 + g * (1.0 - sg)), da * g * sg], axis=1),)

    dgu = _mm_nt([(dx2_bf, L['w_down'])], name=f"l{i}_dact_swiglu_bwd", outs=[(2 * D_FF, BF16)], epi=swiglu_bwd_epi,
                 epi_rows=[(sv['gu'], 2 * D_FF, 0)], tm=tr // 2)
    G['w_down'] = _mm_tn(sv['act'], dx2_bf, name=f"l{i}_dw_down", tk=1408, tn=1024)
    G['w_gu'] = _mm_tn(sv['h2'], dgu, name=f"l{i}_dw_gate_up", tn=1408)
    dx1, dx1_bf, G['norm_ffn'] = _mm_nt([(dgu, L['w_gu'])], name=f"l{i}_dh2_norm_bwd",
                                        outs=[(D_MODEL, F32), (D_MODEL, BF16)], epi=_rms_bwd_epi(True, True),
                                        epi_rows=[(sv['x1'], D_MODEL, 0), (dx2, D_MODEL, 0)], consts=[gain_row('norm_ffn')],
                                        accs=[(1, D_MODEL)], tm=tr // 2)

    def merge_bwd_epi(dm, ga, gb, ya, yb):
        sa, sb = _sigmoid(ga), _sigmoid(gb)
        ya, yb = ya.astype(F32), yb.astype(F32)
        return dm * ya * sa * (1.0 - sa), dm * yb * sb * (1.0 - sb), dm * sa, dm * sb

    dga, dgb, dya, dyb = _mm_nt([(dx1_bf, L['w_o'])], name=f"l{i}_dmerged_bwd", outs=[(D_MODEL, BF16)] * 4,
                                epi=merge_bwd_epi, epi_rows=[(z, D_MODEL, 0), (z, D_MODEL, 1), (sv['ya'], D_MODEL, 0),
                                                             (sv['yb'], D_MODEL, 0)], tm=tr)
    G['w_o'] = _mm_tn(sv['merged'], dx1_bf, name=f"l{i}_dw_o", tn=1024)

    G['w_b'] = _mm_tn(sv['o'], dyb, name=f"l{i}_dw_b", tn=1024)
    do = _mm_nt([(dyb, L['w_b'])], name=f"l{i}_do", outs=[(D_MODEL, BF16)], tm=tr)
    trw = _wide_rows(tr)
    delta = _attn_delta(do, sv['o'], name=f"l{i}_attn_delta", tr=tr)
    dq, dk, dv = _flash_bwd(sv['q'], sv['k'], sv['v'], sv['lse'], delta, do, name=f"l{i}_attn_bwd", blk=blk)

    def dq_rope(_, d, ct, st):
        d = d.astype(F32)
        out = []
        for h in range(N_HEADS):
            out.append(d[:, h * HEAD_PAD:h * HEAD_PAD + QK_NOPE])
            out.append(_rope_bwd(d[:, h * HEAD_PAD + QK_NOPE:(h + 1) * HEAD_PAD], ct, st))
        return jnp.concatenate(out, axis=1)

    dq = _row_call(dq_rope, T, trw, [_rspec(dq, trw), _rspec(cos_t, trw), _rspec(sin_t, trw)], [(N_HEADS * HEAD_PAD, BF16)],
                   name=f"l{i}_dq_rope")[0]

    def dk_rope(_, d, ct, st):
        d = d.astype(F32)
        acc = d[:, QK_NOPE:HEAD_PAD]
        for h in range(1, N_HEADS):
            acc = acc + d[:, h * HEAD_PAD + QK_NOPE:(h + 1) * HEAD_PAD]
        return _rope_bwd(acc, ct, st)

    dkr = _row_call(dk_rope, T, tr, [_rspec(dk, tr), _rspec(cos_t, tr), _rspec(sin_t, tr)], [(LANES, BF16)],
                    name=f"l{i}_dk_rope")[0]
    G['w_q'] = _mm_tn(sv['cqn'], dq, name=f"l{i}_dw_q", tn=1024)
    G['w_k'] = _mm_tn(sv['ckvn'], dk, name=f"l{i}_dw_k", tn=1024)
    G['w_v'] = _mm_tn(sv['ckvn'], dv, name=f"l{i}_dw_v", tn=1024)
    dcq, G['q_norm'] = _mm_nt([(dq, L['w_q'])], name=f"l{i}_dcq", outs=[(Q_LORA, F32)], epi=_rms_bwd_epi(False, False),
                              epi_rows=[(z, Q_LORA, ZC_CQ // Q_LORA)], consts=[gain_row('q_norm')], accs=[(1, Q_LORA)], tm=tr)
    dckv, G['kv_norm'] = _mm_nt([(dk, L['w_k']), (dv, L['w_v'])], name=f"l{i}_dckv", outs=[(KV_LORA, F32)],
                                epi=_rms_bwd_epi(False, False), epi_rows=[(z, KV_LORA, ZC_CKV // KV_LORA)],
                                consts=[gain_row('kv_norm')], accs=[(1, KV_LORA)], tm=tr)

    G['w_a'] = _mm_tn(sv['pm'], dya, name=f"l{i}_dw_a", tn=1024)
    dpm = _mm_nt([(dya, L['w_a'])], name=f"l{i}_dpm", outs=[(POOL_WIDTH, F32)], tm=tr)
    dpool, dpool_cnt, G['pool_scale'], G['w_pool'] = _pool_bwd_mix(dpm, sv['mixed'], sv['pooled'], w_pool_bf, pool_scale,
                                                                   name=f"l{i}_pool_bwd_mix", tr=tr)
    du = _pool_bwd_window(dpool, dpool_cnt, name=f"l{i}_pool_bwd_window", tr=tr)

    def join(_, a, b, c, d, e, f):
        return jnp.concatenate([a, b, c, d.astype(BF16), e.astype(BF16), f], axis=1)

    dz = _row_call(join, T, trw, [_rspec(t, trw) for t in (dga, dgb, du, dcq, dckv, dkr)], [(Z_WIDTH, BF16)],
                   name=f"l{i}_dz_join")[0]
    G['w_in'] = _mm_tn(sv['h'], dz, name=f"l{i}_dw_in", tn=1152)
    dx, G['norm_mix'] = _mm_nt([(dz, L['w_in'])], name=f"l{i}_dh_norm_bwd", outs=[(D_MODEL, F32)],
                               epi=_rms_bwd_epi(True, False), epi_rows=[(sv['x'], D_MODEL, 0), (dx1, D_MODEL, 0)],
                               consts=[gain_row('norm_mix')], accs=[(1, D_MODEL)], tm=tr)
    return dx, G


def kernel(x, p, positions, norm_mix, w_in, w_pool, pool_scale, q_norm, kv_norm, w_uq, w_ukv, w_a, w_b, w_o, norm_ffn, w_gate, w_up, w_down, norm_ple, w_ple_gate, w_ple, final_norm, loss_target, m_norm_mix, m_w_in, m_w_pool, m_pool_scale, m_q_norm, m_kv_norm, m_w_uq, m_w_ukv, m_w_a, m_w_b, m_w_o, m_norm_ffn, m_w_gate, m_w_up, m_w_down, m_norm_ple, m_w_ple_gate, m_w_ple, m_final_norm, v_norm_mix, v_w_in, v_w_pool, v_pool_scale, v_q_norm, v_kv_norm, v_w_uq, v_w_ukv, v_w_a, v_w_b, v_w_o, v_norm_ffn, v_w_gate, v_w_up, v_w_down, v_norm_ple, v_w_ple_gate, v_w_ple, v_final_norm):
    given = dict(locals())
    weights = {n: given[n] for n in WEIGHTS}
    T = x.shape[1]
    tr = min(512, max(T // 2, 8))
    blk = min(ATTN_BLOCK, max(T // 4, 128))
    x0 = x.reshape(T, D_MODEL)
    target = loss_target.reshape(T, D_MODEL)

    names = list(SHARDED)
    shard_shapes = [weights[n].shape for n in names]
    flat = _pack_rows([weights[n].astype(BF16) for n in names], row_mult=PACK_ROWS)
    R = flat.shape[0]
    chip = (2 * lax.axis_index("x") + lax.axis_index("y")).astype(jnp.int32)
    core = lax.axis_index("c").astype(jnp.int32)
    gathered = _gather_weights(flat).reshape(N_CHIPS, R, LANES)
    gathered = lax.dynamic_update_slice(gathered, flat.reshape(1, R, LANES), (chip, 0, 0))
    flat_all = gathered.reshape(N_CHIPS, R * LANES)
    W, off = {}, 0
    for n, shp in zip(names, shard_shapes):
        size = int(np.prod(shp))
        W[n] = _merge_chips(flat_all[:, off:off + size].reshape((N_CHIPS,) + tuple(shp)), SHARDED[n])
        off += size
    layouts = [_layer_layouts(W, i) for i in range(DEPTH)]
    w_pool_bf = w_pool.astype(BF16)

    inv_freq = 1.0 / (ROPE_THETA ** (jnp.arange(0, QK_ROPE, 2, dtype=F32) / QK_ROPE))
    zero32 = jnp.zeros((32,), F32)
    freq_row = jnp.concatenate([inv_freq, zero32, inv_freq, zero32]).reshape(1, LANES)
    cos_mask = jnp.concatenate([jnp.ones((32,), F32), zero32, jnp.ones((32,), F32), zero32]).reshape(1, LANES)
    sin_sign = jnp.concatenate([-jnp.ones((32,), F32), zero32, jnp.ones((32,), F32), zero32]).reshape(1, LANES)

    def rope_tables(_, pos, fr, cm, ss):
        ang = pos.astype(F32) * fr
        return jnp.cos(ang) * cm, jnp.sin(ang) * ss

    pos_col = positions.reshape(T, 1)
    cos_t, sin_t = _row_call(rope_tables, T, tr, [_rspec(pos_col, tr), _bspec(freq_row), _bspec(cos_mask), _bspec(sin_sign)],
                             [(LANES, F32), (LANES, F32)], name="rope_tables")

    xs = x0
    saved = []
    for i in range(DEPTH):
        norms = {n: weights[n][i] for n in ('norm_mix', 'q_norm', 'kv_norm', 'norm_ffn', 'norm_ple')}
        xs, sv = _layer_fwd(i, xs, p[i, 0], layouts[i], norms, w_pool_bf[i], pool_scale[i], cos_t, sin_t, tr, blk)
        saved.append((sv, norms))

    def head(_, xv, tv, gv):
        rstd = lax.rsqrt(jnp.mean(xv * xv, axis=-1, keepdims=True) + EPS)
        xhat = xv * rstd
        err = xhat * gv - tv
        loss = 0.5 * jnp.sum(jnp.mean(err * err, axis=-1, keepdims=True), axis=0, keepdims=True)
        dy = err * (1.0 / D_MODEL)
        dg = jnp.sum(dy * xhat, axis=0, keepdims=True)
        dxh = dy * gv
        dx = rstd * (dxh - xhat * jnp.mean(dxh * xhat, axis=-1, keepdims=True))
        return dx, jnp.broadcast_to(loss, (1, LANES)), dg

    dx, loss_part, g_final = _row_call(head, T, tr, [_rspec(xs, tr), _rspec(target, tr), _bspec(final_norm.reshape(1, D_MODEL))],
                                       [(D_MODEL, F32)], [(1, LANES), (1, D_MODEL)], name="loss_head")
    loss = lax.psum(loss_part[0, 0], ("x", "y", "c"))

    layer_grads = [None] * DEPTH
    for i in reversed(range(DEPTH)):
        sv, norms = saved[i]
        dx, layer_grads[i] = _layer_bwd(i, dx, sv, p[i, 0], layouts[i], norms, w_pool_bf[i], pool_scale[i], cos_t, sin_t, tr,
                                        blk)
    grad_x = dx.reshape(x.shape)

    ref_layout = [_layer_grads_to_reference_layout(g) for g in layer_grads]
    local = {n: jnp.stack([ref_layout[i][n] for i in range(DEPTH)]) for n in names}
    for n in ('norm_mix', 'q_norm', 'kv_norm', 'norm_ffn', 'norm_ple', 'pool_scale'):
        local[n] = jnp.stack([layer_grads[i][n].reshape(-1) for i in range(DEPTH)])
    local['w_pool'] = jnp.stack([layer_grads[i]['w_pool'] for i in range(DEPTH)])
    local['final_norm'] = g_final.reshape(-1)

    send = jnp.concatenate([_split_chips(local[n], SHARDED[n]).reshape(N_CHIPS, -1) for n in names], axis=1)
    send = jnp.pad(send, ((0, 0), (0, R * LANES - send.shape[1])))
    rh = R // 2
    trr = PACK_ROWS // 2
    g_all = send.reshape(N_CHIPS, 2, rh, LANES)
    part = _add_halves(g_all, _swap_halves(g_all), tr=trr)
    reduced_half = _sum_chips(_scatter_partials(part), tr=trr)
    reduced = _both_halves(reduced_half, _send_half(reduced_half), core, tr=trr).reshape(R, LANES)
    grads = dict(zip(names, _unpack_rows(reduced, shard_shapes)))

    rep_shapes = [weights[n].shape for n in REPLICATED]
    rep = _allreduce_small(_pack_rows([local[n] for n in REPLICATED], row_mult=8))
    grads.update(zip(REPLICATED, _unpack_rows(rep, rep_shapes)))

    deltas, new_m, new_v = {}, {}, {}
    for n in WEIGHTS:
        deltas[n], new_m[n], new_v[n] = _adamw(weights[n], grads[n], given['m_' + n], given['v_' + n], name=f"adamw_{n}")
    return (loss, grad_x, *[grads[n] for n in WEIGHTS], *[deltas[n] for n in WEIGHTS], *[new_m[n] for n in WEIGHTS],
            *[new_v[n] for n in WEIGHTS])
```

```python
import functools

import numpy as np
import jax
import jax.numpy as jnp
from jax import lax
from jax.experimental import pallas as pl
from jax.experimental.pallas import tpu as pltpu

F32 = jnp.float32
BF16 = jnp.bfloat16

D_MODEL = 1024
DEPTH = 2
PLE_DIM = 256
POOL_WINDOWS = (2, 4, 8, 16)
POOL_GROUP = 128
POOL_WIDTH = 512
N_HEADS = 8
Q_LORA = 512
KV_LORA = 256
QK_NOPE = 128
QK_ROPE = 64
QK_HEAD = 192
V_HEAD = 128
D_FF = 2816
ROPE_THETA = 10000.0
EPS = 1e-6
ATTN_SCALE = QK_HEAD ** -0.5

ADAM_LR = 0.001
ADAM_B1 = 0.9
ADAM_B2 = 0.999
ADAM_EPS = 1e-08
ADAM_WD = 0.01
ADAM_STEP = 10

LANES = 128
HALO = 16
HEAD_PAD = 256
V7X_VMEM_BYTES = 64 * 1024 * 1024
VMEM_LIMIT = (V7X_VMEM_BYTES * 3) // 4
N_CHIPS = 4
N_DEV = 8
NEG_INF = -1e30

ZC_GA, ZC_GB, ZC_U, ZC_CQ, ZC_CKV, ZC_KR = 0, 1024, 2048, 2560, 3072, 3328
Z_WIDTH = 3456

WEIGHTS = ['norm_mix', 'w_in', 'w_pool', 'pool_scale', 'q_norm', 'kv_norm', 'w_uq', 'w_ukv', 'w_a', 'w_b', 'w_o',
           'norm_ffn', 'w_gate', 'w_up', 'w_down', 'norm_ple', 'w_ple_gate', 'w_ple', 'final_norm']
SHARDED = {'w_in': 2, 'w_uq': 1, 'w_ukv': 1, 'w_a': 2, 'w_b': 1, 'w_o': 1, 'w_gate': 2, 'w_up': 2, 'w_down': 1,
           'w_ple_gate': 1, 'w_ple': 2}
REPLICATED = [n for n in WEIGHTS if n not in SHARDED]


def _tile(n, target, mult=LANES):
    if n <= target:
        return n
    best = None
    for t in range(mult, target + 1, mult):
        if n % t == 0:
            best = t
    assert best is not None, (n, target)
    return best


def _cparams(*sem):
    return pltpu.CompilerParams(dimension_semantics=sem, vmem_limit_bytes=VMEM_LIMIT)


def _rope(t, cos_t, sin_t):
    return t * cos_t + pltpu.roll(t, 64, 1) * sin_t


def _rope_bwd(d, cos_t, sin_t):
    return d * cos_t + pltpu.roll(d * sin_t, 64, 1)


def _sigmoid(v):
    return 1.0 / (1.0 + jnp.exp(-v))


def _mm_nn(a, b, *, name, outs, a_col=0, gain=None, emit_a=False, epi=None, epi_rows=(), tm=512):
    M = a.shape[0]
    K, N = b.shape
    tm = min(tm, M)
    assert a_col % K == 0 and M % tm == 0
    a_blk = a_col // K
    n_rows, n_out = len(epi_rows), len(outs)

    def body(*refs):
        a_ref, b_ref = refs[0], refs[1]
        pos = 2
        g_ref = None
        if gain is not None:
            g_ref = refs[pos]
            pos += 1
        row_refs = refs[pos:pos + n_rows]
        out_refs = refs[pos + n_rows:pos + n_rows + n_out]
        lhs = a_ref[...]
        if gain is not None:
            av = lhs.astype(F32)
            lhs = av * lax.rsqrt(jnp.mean(av * av, axis=-1, keepdims=True) + EPS) * g_ref[...]
        lhs = lhs.astype(BF16)
        if emit_a:
            refs[pos + n_rows + n_out][...] = lhs
        acc = jnp.dot(lhs, b_ref[...], preferred_element_type=F32)
        vals = (acc,) if epi is None else epi(acc, *[r[...] for r in row_refs])
        for r, v in zip(out_refs, vals):
            r[...] = v.astype(r.dtype)

    in_specs = [pl.BlockSpec((tm, K), lambda i: (i, a_blk)), pl.BlockSpec((K, N), lambda i: (0, 0))]
    args = [a, b]
    if gain is not None:
        in_specs.append(pl.BlockSpec((1, K), lambda i: (0, 0)))
        args.append(gain.reshape(1, K).astype(F32))
    for arr, w, blk in epi_rows:
        in_specs.append(pl.BlockSpec((tm, w), lambda i, blk=blk: (i, blk)))
        args.append(arr)
    out_shape = [jax.ShapeDtypeStruct((M, w), dt) for w, dt in outs]
    out_specs = [pl.BlockSpec((tm, w), lambda i: (i, 0)) for w, dt in outs]
    if emit_a:
        out_shape.append(jax.ShapeDtypeStruct((M, K), BF16))
        out_specs.append(pl.BlockSpec((tm, K), lambda i: (i, 0)))
    res = pl.pallas_call(body, grid=(M // tm,), in_specs=in_specs, out_specs=out_specs, out_shape=out_shape,
                         name=name, compiler_params=_cparams("parallel"))(*args)
    return res[0] if len(res) == 1 else res


def _mm_nt(pairs, *, name, outs, epi=None, epi_rows=(), consts=(), accs=(), tm=512):
    M = pairs[0][0].shape[0]
    N = pairs[0][1].shape[0]
    tm = min(tm, M)
    n_p, n_in, n_out, n_acc = len(pairs), len(epi_rows) + len(consts), len(outs), len(accs)

    def body(*refs):
        acc = None
        for k in range(n_p):
            av = refs[2 * k][...].astype(BF16)
            part = lax.dot_general(av, refs[2 * k + 1][...], NT_DIMS, preferred_element_type=F32)
            acc = part if acc is None else acc + part
        pos = 2 * n_p
        extra = [r[...] for r in refs[pos:pos + n_in]]
        pos += n_in
        vals = (acc,) if epi is None else epi(acc, *extra)
        for r, v in zip(refs[pos:pos + n_out], vals[:n_out]):
            r[...] = v.astype(r.dtype)
        if n_acc:
            acc_refs = refs[pos + n_out:pos + n_out + n_acc]

            @pl.when(pl.program_id(0) == 0)
            def _():
                for r in acc_refs:
                    r[...] = jnp.zeros_like(r)
            for r, v in zip(acc_refs, vals[n_out:]):
                r[...] += v

    in_specs, args = [], []
    for a, b in pairs:
        assert a.shape[1] == b.shape[1] and b.shape[0] == N and a.shape[0] == M
        in_specs.append(pl.BlockSpec((tm, a.shape[1]), lambda i: (i, 0)))
        in_specs.append(pl.BlockSpec(b.shape, lambda i: (0, 0)))
        args += [a, b]
    for arr, w, blk in epi_rows:
        in_specs.append(pl.BlockSpec((tm, w), lambda i, blk=blk: (i, blk)))
        args.append(arr)
    for arr in consts:
        in_specs.append(pl.BlockSpec(arr.shape, lambda i, n=arr.ndim: (0,) * n))
        args.append(arr)
    out_shape = [jax.ShapeDtypeStruct((M, w), dt) for w, dt in outs]
    out_specs = [pl.BlockSpec((tm, w), lambda i: (i, 0)) for w, dt in outs]
    for s in accs:
        out_shape.append(jax.ShapeDtypeStruct(s, F32))
        out_specs.append(pl.BlockSpec(s, lambda i, n=len(s): (0,) * n))
    res = pl.pallas_call(body, grid=(M // tm,), in_specs=in_specs, out_specs=out_specs, out_shape=out_shape, name=name,
                         compiler_params=_cparams("arbitrary" if n_acc else "parallel"))(*args)
    return res[0] if len(res) == 1 else res


def _rms_bwd_epi(with_res, emit_bf16):
    def epi(dh, xv, *rest):
        gv = rest[-1]
        xv = xv.astype(F32)
        rstd = lax.rsqrt(jnp.mean(xv * xv, axis=-1, keepdims=True) + EPS)
        xhat = xv * rstd
        dg = jnp.sum(dh * xhat, axis=0, keepdims=True)
        dxh = dh * gv
        dx = rstd * (dxh - xhat * jnp.mean(dxh * xhat, axis=-1, keepdims=True))
        if with_res:
            dx = dx + rest[0].astype(F32)
        return (dx, dx, dg) if emit_bf16 else (dx, dg)
    return epi


def _mm_tn(a, b, *, name, a_col=0, a_w=None, tk=1024, tn=1152, tm=512):
    M = a.shape[0]
    a_w = a.shape[1] if a_w is None else a_w
    N = b.shape[1]
    tm = min(tm, M)
    tk = _tile(a_w, tk)
    tn = _tile(N, tn)
    assert a_col % tk == 0 and M % tm == 0
    a_blk0 = a_col // tk

    def body(a_ref, b_ref, o_ref):
        @pl.when(pl.program_id(2) == 0)
        def _():
            o_ref[...] = jnp.zeros_like(o_ref)
        o_ref[...] += lax.dot_general(a_ref[...].astype(BF16), b_ref[...].astype(BF16), (((0,), (0,)), ((), ())),
                                      preferred_element_type=F32)

    return pl.pallas_call(body, grid=(a_w // tk, N // tn, M // tm),
                          in_specs=[pl.BlockSpec((tm, tk), lambda k, j, m: (m, k + a_blk0)),
                                    pl.BlockSpec((tm, tn), lambda k, j, m: (m, j))],
                          out_specs=pl.BlockSpec((tk, tn), lambda k, j, m: (k, j)),
                          out_shape=jax.ShapeDtypeStruct((a_w, N), F32), name=name,
                          compiler_params=_cparams("parallel", "parallel", "arbitrary"))(a, b)


def _row_call(fn, rows, tr, ins, outs, accs=(), *, name):
    n_in, n_out, n_acc = len(ins), len(outs), len(accs)

    def body(*refs):
        i = pl.program_id(0)
        vals = fn(i, *[r[...] for r in refs[:n_in]])
        if not isinstance(vals, (tuple, list)):
            vals = (vals,)
        for r, v in zip(refs[n_in:n_in + n_out], vals[:n_out]):
            r[...] = v.astype(r.dtype)
        if n_acc:
            acc_refs = refs[n_in + n_out:]

            @pl.when(i == 0)
            def _():
                for r in acc_refs:
                    r[...] = jnp.zeros_like(r)
            for r, v in zip(acc_refs, vals[n_out:]):
                r[...] += v

    out_shape = [jax.ShapeDtypeStruct((rows, w), dt) for w, dt in outs]
    out_specs = [pl.BlockSpec((tr, w), lambda i: (i, 0)) for w, dt in outs]
    for s in accs:
        out_shape.append(jax.ShapeDtypeStruct(s, F32))
        out_specs.append(pl.BlockSpec(s, lambda i, n=len(s): (0,) * n))
    res = pl.pallas_call(body, grid=(rows // tr,), in_specs=[pl.BlockSpec(bs, im) for _, bs, im in ins],
                         out_specs=out_specs, out_shape=out_shape, name=name,
                         compiler_params=_cparams("arbitrary"))(*[a for a, _, _ in ins])
    return res


def _wide_rows(tr):
    return max(tr // 4, 8)


def _rspec(arr, tr, w=None, blk=0):
    w = arr.shape[1] if w is None else w
    return (arr, (tr, w), lambda i, blk=blk: (i, blk))


def _bspec(arr):
    return (arr, arr.shape, lambda i, n=arr.ndim: (0,) * n)


def _pool_counts(i, tr):
    t = (i * tr + lax.broadcasted_iota(jnp.int32, (tr, 1), 0) + 1).astype(F32)
    return [jnp.minimum(t, float(w)) for w in POOL_WINDOWS]


def _pool_fwd(z, w_pool_bf, pool_scale, *, name, tr):
    rows = z.shape[0]
    ublk = ZC_U // POOL_WIDTH
    hpt = tr // HALO

    def fn(i, u, uprev, wp, ps):
        uprev = jnp.where(i > 0, uprev, 0.0)
        ext = jnp.concatenate([uprev, u], axis=0)
        s2 = ext + pltpu.roll(ext, 1, 0)
        s4 = s2 + pltpu.roll(s2, 2, 0)
        s8 = s4 + pltpu.roll(s4, 4, 0)
        s16 = s8 + pltpu.roll(s8, 8, 0)
        cnts = _pool_counts(i, tr)
        pooled, mixed = [], []
        for g, sw in enumerate((s2, s4, s8, s16)):
            lanes = slice(g * POOL_GROUP, (g + 1) * POOL_GROUP)
            pg = sw[HALO:, lanes] / cnts[g] - u[:, lanes]
            pooled.append(pg)
            mixed.append(jnp.dot(pg.astype(BF16), wp[g], preferred_element_type=F32))
        pooled = jnp.concatenate(pooled, axis=1)
        mixed = jnp.concatenate(mixed, axis=1)
        return pooled, mixed, mixed * ps

    ins = [_rspec(z, tr, POOL_WIDTH, ublk),
           (z, (HALO, POOL_WIDTH), lambda i: (jnp.maximum(i * hpt - 1, 0), ublk)),
           _bspec(w_pool_bf), _bspec(pool_scale.reshape(1, POOL_WIDTH))]
    return _row_call(fn, rows, tr, ins, [(POOL_WIDTH, BF16), (POOL_WIDTH, F32), (POOL_WIDTH, BF16)], name=name)


def _pool_bwd_mix(dpm, mixed, pooled, w_pool_bf, pool_scale, *, name, tr):
    rows = dpm.shape[0]

    def fn(i, dv, mv, pv, wp, ps):
        dv = dv.astype(F32)
        dscale = jnp.sum(dv * mv, axis=0, keepdims=True)
        dmix = (dv * ps).astype(BF16)
        cnts = _pool_counts(i, tr)
        dpool, dwp = [], []
        for g in range(len(POOL_WINDOWS)):
            lanes = slice(g * POOL_GROUP, (g + 1) * POOL_GROUP)
            dg = lax.dot_general(dmix[:, lanes], wp[g], (((1,), (1,)), ((), ())), preferred_element_type=F32)
            dpool.append(dg)
            dwp.append(lax.dot_general(pv[:, lanes], dmix[:, lanes], (((0,), (0,)), ((), ())),
                                       preferred_element_type=F32)[None])
        dpool = jnp.concatenate(dpool, axis=1)
        dpool_cnt = jnp.concatenate([dpool[:, g * POOL_GROUP:(g + 1) * POOL_GROUP] / cnts[g]
                                     for g in range(len(POOL_WINDOWS))], axis=1)
        return dpool, dpool_cnt, dscale, jnp.concatenate(dwp, axis=0)

    ins = [_rspec(dpm, tr), _rspec(mixed, tr), _rspec(pooled, tr), _bspec(w_pool_bf),
           _bspec(pool_scale.reshape(1, POOL_WIDTH))]
    return _row_call(fn, rows, tr, ins, [(POOL_WIDTH, F32), (POOL_WIDTH, F32)],
                     [(1, POOL_WIDTH), (len(POOL_WINDOWS), POOL_GROUP, POOL_GROUP)], name=name)


def _pool_bwd_window(dpool, dpool_cnt, *, name, tr):
    rows = dpool.shape[0]
    hpt = tr // HALO
    n_halo = rows // HALO
    n_tiles = rows // tr

    def fn(i, dp, dc, dnext):
        dnext = jnp.where(i < n_tiles - 1, dnext, 0.0)
        ext = jnp.concatenate([dc, dnext], axis=0)
        n = tr + HALO
        s2 = ext + pltpu.roll(ext, n - 1, 0)
        s4 = s2 + pltpu.roll(s2, n - 2, 0)
        s8 = s4 + pltpu.roll(s4, n - 4, 0)
        s16 = s8 + pltpu.roll(s8, n - 8, 0)
        out = []
        for g, sw in enumerate((s2, s4, s8, s16)):
            lanes = slice(g * POOL_GROUP, (g + 1) * POOL_GROUP)
            out.append(sw[:tr, lanes] - dp[:, lanes])
        return jnp.concatenate(out, axis=1)

    ins = [_rspec(dpool, tr), _rspec(dpool_cnt, tr),
           (dpool_cnt, (HALO, POOL_WIDTH), lambda i: (jnp.minimum((i + 1) * hpt, n_halo - 1), 0))]
    return _row_call(fn, rows, tr, ins, [(POOL_WIDTH, BF16)], name=name)[0]


def _causal_pairs(n, k_major):
    if k_major:
        pairs = [(qi, ki) for ki in range(n) for qi in range(ki, n)]
    else:
        pairs = [(qi, ki) for qi in range(n) for ki in range(qi + 1)]
    return (jnp.asarray(np.array([p[0] for p in pairs], np.int32)),
            jnp.asarray(np.array([p[1] for p in pairs], np.int32)), len(pairs))


SUBLANES = 8
NT_DIMS = (((1,), (1,)), ((), ()))
TN_DIMS = (((0,), (0,)), ((), ()))


ATTN_BLOCK = 1024
QUERY_CHUNK = 256
EXP2_SCALE = ATTN_SCALE * 1.4426950408889634


def _scores_t(q_c, k, c, qc, diag):
    s = lax.dot_general(k, q_c, NT_DIMS, preferred_element_type=F32)
    if diag:
        key = lax.broadcasted_iota(jnp.int32, s.shape, 0)
        qry = lax.broadcasted_iota(jnp.int32, s.shape, 1) + c * qc
        s = jnp.where(key <= qry, s, NEG_INF)
    return s


def _flash_fwd(q, k, v, *, name, blk):
    T = q.shape[0]
    n = T // blk
    qc = min(QUERY_CHUNK, blk)
    qtab, ktab, n_pairs = _causal_pairs(n, k_major=False)

    def body(qt, kt, q_ref, k_ref, v_ref, o_ref, lse_ref, m_s, l_s, acc_s):
        p = pl.program_id(1)
        qi, ki = qt[p], kt[p]

        @pl.when(ki == 0)
        def _():
            m_s[...] = jnp.full_like(m_s, NEG_INF)
            l_s[...] = jnp.zeros_like(l_s)
            acc_s[...] = jnp.zeros_like(acc_s)

        def step(diag):
            kv, vv = k_ref[...], v_ref[...]
            chunks = [slice(c * qc, (c + 1) * qc) for c in range(blk // qc)]
            scores = [_scores_t(q_ref[rows, :], kv, c, qc, diag) for c, rows in enumerate(chunks)]
            probs, alphas = [], []
            for rows, s_t in zip(chunks, scores):
                m_prev = m_s[:, rows]
                m_new = jnp.maximum(m_prev, jnp.max(s_t, axis=0, keepdims=True))
                p_t = jnp.exp2((s_t - m_new) * EXP2_SCALE)
                alpha = jnp.exp2((m_prev - m_new) * EXP2_SCALE)
                l_s[:, rows] = alpha * l_s[:, rows] + jnp.sum(p_t, axis=0, keepdims=True)
                m_s[:, rows] = m_new
                probs.append(p_t.astype(BF16))
                alphas.append(alpha)
            for rows, p_t, alpha in zip(chunks, probs, alphas):
                acc_s[:, rows] = alpha * acc_s[:, rows] + lax.dot_general(vv, p_t, TN_DIMS, preferred_element_type=F32)

        @pl.when(ki != qi)
        def _():
            step(False)

        @pl.when(ki == qi)
        def _():
            step(True)
            o_ref[...] = (acc_s[...] / l_s[...]).T.astype(o_ref.dtype)
            lse2 = m_s[...] * EXP2_SCALE + jnp.log2(l_s[...])
            lse_ref[...] = jnp.broadcast_to(lse2, lse_ref.shape)

    grid_spec = pltpu.PrefetchScalarGridSpec(
        num_scalar_prefetch=2, grid=(N_HEADS, n_pairs),
        in_specs=[pl.BlockSpec((blk, HEAD_PAD), lambda h, p, qt, kt: (qt[p], h)),
                  pl.BlockSpec((blk, HEAD_PAD), lambda h, p, qt, kt: (kt[p], h)),
                  pl.BlockSpec((blk, V_HEAD), lambda h, p, qt, kt: (kt[p], h))],
        out_specs=[pl.BlockSpec((blk, V_HEAD), lambda h, p, qt, kt: (qt[p], h)),
                   pl.BlockSpec((SUBLANES, blk), lambda h, p, qt, kt: (h, qt[p]))],
        scratch_shapes=[pltpu.VMEM((1, blk), F32), pltpu.VMEM((1, blk), F32), pltpu.VMEM((V_HEAD, blk), F32)])
    return pl.pallas_call(body, grid_spec=grid_spec,
                          out_shape=[jax.ShapeDtypeStruct((T, N_HEADS * V_HEAD), BF16),
                                     jax.ShapeDtypeStruct((N_HEADS * SUBLANES, T), F32)],
                          name=name, compiler_params=_cparams("parallel", "arbitrary"))(qtab, ktab, q, k, v)


def _attn_delta(do, o, *, name, tr):
    T = do.shape[0]

    def body(do_ref, o_ref, d_ref):
        prod = do_ref[...].astype(F32) * o_ref[...].astype(F32)
        lane_head = lax.broadcasted_iota(jnp.int32, (tr, LANES), 1) // SUBLANES
        mat = jnp.zeros((tr, LANES), F32)
        for h in range(N_HEADS):
            d_h = jnp.sum(prod[:, h * V_HEAD:(h + 1) * V_HEAD], axis=1, keepdims=True)
            mat = jnp.where(lane_head == h, d_h, mat)
        d_ref[...] = mat.T[:N_HEADS * SUBLANES, :]

    return pl.pallas_call(body, grid=(T // tr,),
                          in_specs=[pl.BlockSpec((tr, N_HEADS * V_HEAD), lambda i: (i, 0)),
                                    pl.BlockSpec((tr, N_HEADS * V_HEAD), lambda i: (i, 0))],
                          out_specs=pl.BlockSpec((N_HEADS * SUBLANES, tr), lambda i: (0, i)),
                          out_shape=jax.ShapeDtypeStruct((N_HEADS * SUBLANES, T), F32), name=name,
                          compiler_params=_cparams("parallel"))(do, o)


def _flash_bwd(q, k, v, lse, delta, do, cos_t, sin_t, *, name, blk):
    T = q.shape[0]
    n = T // blk
    qc = min(QUERY_CHUNK, blk)
    qtab, ktab, n_pairs = _causal_pairs(n, k_major=True)

    def body(qt, kt, q_ref, k_ref, v_ref, lse_ref, delta_ref, do_ref, cos_ref, sin_ref, dq_ref, dk_ref, dv_ref,
             dq_s, dk_s, dv_s):
        p = pl.program_id(1)
        qi, ki = qt[p], kt[p]
        first = qi == ki

        @pl.when(p == 0)
        def _():
            dq_s[...] = jnp.zeros_like(dq_s)

        @pl.when(first)
        def _():
            dk_s[...] = jnp.zeros_like(dk_s)
            dv_s[...] = jnp.zeros_like(dv_s)

        def step(diag):
            kv, vv = k_ref[...], v_ref[...]
            chunks = [slice(c * qc, (c + 1) * qc) for c in range(blk // qc)]
            qs = [q_ref[rows, :] for rows in chunks]
            dos = [do_ref[rows, :] for rows in chunks]
            scores = [_scores_t(q_c, kv, c, qc, diag) for c, q_c in enumerate(qs)]
            dps = [lax.dot_general(vv, do_c, NT_DIMS, preferred_element_type=F32) for do_c in dos]
            probs, dss = [], []
            for rows, s_t, dp_t in zip(chunks, scores, dps):
                p_t = jnp.exp2(s_t * EXP2_SCALE - lse_ref[0:1, rows])
                dss.append((p_t * (dp_t - delta_ref[0:1, rows])).astype(BF16))
                probs.append(p_t.astype(BF16))
            dv_acc = dk_acc = None
            for p_t, ds_t, q_c, do_c in zip(probs, dss, qs, dos):
                dv_c = jnp.dot(p_t, do_c, preferred_element_type=F32)
                dk_c = jnp.dot(ds_t, q_c, preferred_element_type=F32)
                dv_acc = dv_c if dv_acc is None else dv_acc + dv_c
                dk_acc = dk_c if dk_acc is None else dk_acc + dk_c
            for rows, ds_t in zip(chunks, dss):
                dq_s[qi, :, rows] += lax.dot_general(kv, ds_t, TN_DIMS, preferred_element_type=F32)
            dv_s[...] += dv_acc
            dk_s[...] += dk_acc

        @pl.when(jnp.logical_not(first))
        def _():
            step(False)

        @pl.when(first)
        def _():
            step(True)
            dq_t = (dq_s[qi] * ATTN_SCALE).T
            dq_ref[:, :QK_NOPE] = dq_t[:, :QK_NOPE].astype(dq_ref.dtype)
            dq_ref[:, QK_NOPE:] = _rope_bwd(dq_t[:, QK_NOPE:], cos_ref[...], sin_ref[...]).astype(dq_ref.dtype)

        @pl.when(qi == n - 1)
        def _():
            dk_ref[...] = (dk_s[...] * ATTN_SCALE).astype(dk_ref.dtype)
            dv_ref[...] = dv_s[...].astype(dv_ref.dtype)

    qmap = lambda h, p, qt, kt: (qt[p], h)
    kmap = lambda h, p, qt, kt: (kt[p], h)
    smap = lambda h, p, qt, kt: (h, qt[p])
    tmap = lambda h, p, qt, kt: (kt[p], 0)
    grid_spec = pltpu.PrefetchScalarGridSpec(
        num_scalar_prefetch=2, grid=(N_HEADS, n_pairs),
        in_specs=[pl.BlockSpec((blk, HEAD_PAD), qmap), pl.BlockSpec((blk, HEAD_PAD), kmap),
                  pl.BlockSpec((blk, V_HEAD), kmap), pl.BlockSpec((SUBLANES, blk), smap),
                  pl.BlockSpec((SUBLANES, blk), smap), pl.BlockSpec((blk, V_HEAD), qmap),
                  pl.BlockSpec((blk, LANES), tmap), pl.BlockSpec((blk, LANES), tmap)],
        out_specs=[pl.BlockSpec((blk, HEAD_PAD), kmap), pl.BlockSpec((blk, HEAD_PAD), kmap),
                   pl.BlockSpec((blk, V_HEAD), kmap)],
        scratch_shapes=[pltpu.VMEM((n, HEAD_PAD, blk), F32), pltpu.VMEM((blk, HEAD_PAD), F32),
                        pltpu.VMEM((blk, V_HEAD), F32)])
    return pl.pallas_call(body, grid_spec=grid_spec,
                          out_shape=[jax.ShapeDtypeStruct((T, N_HEADS * HEAD_PAD), BF16),
                                     jax.ShapeDtypeStruct((T, N_HEADS * HEAD_PAD), BF16),
                                     jax.ShapeDtypeStruct((T, N_HEADS * V_HEAD), BF16)],
                          name=name, compiler_params=_cparams("arbitrary", "arbitrary"))(
                              qtab, ktab, q, k, v, lse, delta, do, cos_t, sin_t)


MESH_ID = pl.DeviceIdType.MESH
ANY_SPEC = pl.BlockSpec(memory_space=pl.ANY)


def _other_chips(x, y):
    out = []
    for dx, dy in ((1, 0), (0, 1), (1, 1)):
        px = x ^ dx if dx else x
        py = y ^ dy if dy else y
        out.append((px, py, 2 * px + py))
    return out


def _gather_weights(flat):
    rh = flat.shape[0] // 2

    def body(src2, out, send_sems, recv_sems):
        x, y, c = lax.axis_index("x"), lax.axis_index("y"), lax.axis_index("c")
        me = 2 * x + y
        sib = (x, y, 1 - c)
        chips = _other_chips(x, y)
        sends = []
        for j, (px, py, pk) in enumerate(chips):
            cp = pltpu.make_async_remote_copy(src_ref=src2.at[c], dst_ref=out.at[me, c], send_sem=send_sems.at[j],
                                              recv_sem=recv_sems.at[j], device_id=(px, py, c), device_id_type=MESH_ID)
            cp.start()
            sends.append(cp)
        for j, (px, py, pk) in enumerate(chips):
            land = out.at[pk, c]
            pltpu.make_async_remote_copy(src_ref=land, dst_ref=land, send_sem=send_sems.at[j], recv_sem=recv_sems.at[j],
                                         device_id=(px, py, c), device_id_type=MESH_ID).wait_recv()
            fw = pltpu.make_async_remote_copy(src_ref=land, dst_ref=land, send_sem=send_sems.at[3 + j],
                                              recv_sem=recv_sems.at[3 + j], device_id=sib, device_id_type=MESH_ID)
            fw.start()
            sends.append(fw)
        for j, (px, py, pk) in enumerate(chips):
            land = out.at[pk, 1 - c]
            pltpu.make_async_remote_copy(src_ref=land, dst_ref=land, send_sem=send_sems.at[3 + j],
                                         recv_sem=recv_sems.at[3 + j], device_id=sib, device_id_type=MESH_ID).wait_recv()
        for cp in sends:
            cp.wait_send()

    return pl.pallas_call(body, out_shape=jax.ShapeDtypeStruct((N_CHIPS, 2, rh, LANES), flat.dtype),
                          in_specs=[ANY_SPEC], out_specs=ANY_SPEC,
                          scratch_shapes=[pltpu.SemaphoreType.DMA((6,)), pltpu.SemaphoreType.DMA((6,))],
                          name="gather_weights")(flat.reshape(2, rh, LANES))


def _swap_halves(g):
    rh = g.shape[2]

    def body(src, out, send_sem, recv_sem):
        x, y, c = lax.axis_index("x"), lax.axis_index("y"), lax.axis_index("c")
        cp = pltpu.make_async_remote_copy(src_ref=src.at[:, 1 - c], dst_ref=out, send_sem=send_sem, recv_sem=recv_sem,
                                          device_id=(x, y, 1 - c), device_id_type=MESH_ID)
        cp.start()
        cp.wait()

    return pl.pallas_call(body, out_shape=jax.ShapeDtypeStruct((N_CHIPS, rh, LANES), g.dtype),
                          in_specs=[ANY_SPEC], out_specs=ANY_SPEC,
                          scratch_shapes=[pltpu.SemaphoreType.DMA(()), pltpu.SemaphoreType.DMA(())],
                          name="grad_swap_halves")(g)


def _add_halves(g, got, *, tr):
    rh = g.shape[2]
    nt = rh // tr

    def body(c_ref, g_ref, got_ref, o_ref):
        o_ref[...] = (g_ref[0] + got_ref[...]).astype(o_ref.dtype)

    grid_spec = pltpu.PrefetchScalarGridSpec(
        num_scalar_prefetch=1, grid=(N_CHIPS, nt),
        in_specs=[pl.BlockSpec((1, 1, tr, LANES), lambda k, i, c_ref: (k, c_ref[0], i, 0)),
                  pl.BlockSpec((1, tr, LANES), lambda k, i, c_ref: (k, i, 0))],
        out_specs=pl.BlockSpec((1, tr, LANES), lambda k, i, c_ref: (k, i, 0)))
    core = lax.axis_index("c").astype(jnp.int32).reshape(1)
    return pl.pallas_call(body, grid_spec=grid_spec,
                          out_shape=jax.ShapeDtypeStruct((N_CHIPS, rh, LANES), BF16), name="grad_add_halves",
                          compiler_params=_cparams("parallel", "parallel"))(core, g, got)


def _scatter_partials(part):
    rh = part.shape[1]

    def body(src, out, send_sems, recv_sems, local_sem):
        x, y, c = lax.axis_index("x"), lax.axis_index("y"), lax.axis_index("c")
        me = 2 * x + y
        own = pltpu.make_async_copy(src.at[me], out.at[me], local_sem)
        own.start()
        chips = _other_chips(x, y)
        sends = []
        for j, (px, py, pk) in enumerate(chips):
            cp = pltpu.make_async_remote_copy(src_ref=src.at[pk], dst_ref=out.at[me], send_sem=send_sems.at[j],
                                              recv_sem=recv_sems.at[j], device_id=(px, py, c), device_id_type=MESH_ID)
            cp.start()
            sends.append(cp)
        for j, (px, py, pk) in enumerate(chips):
            land = out.at[pk]
            pltpu.make_async_remote_copy(src_ref=land, dst_ref=land, send_sem=send_sems.at[j], recv_sem=recv_sems.at[j],
                                         device_id=(px, py, c), device_id_type=MESH_ID).wait_recv()
        for cp in sends:
            cp.wait_send()
        own.wait()

    return pl.pallas_call(body, out_shape=jax.ShapeDtypeStruct((N_CHIPS, rh, LANES), part.dtype),
                          in_specs=[ANY_SPEC], out_specs=ANY_SPEC,
                          scratch_shapes=[pltpu.SemaphoreType.DMA((3,)), pltpu.SemaphoreType.DMA((3,)),
                                          pltpu.SemaphoreType.DMA(())],
                          name="grad_scatter_partials")(part)


def _sum_chips(q, *, tr):
    rh = q.shape[1]

    def body(q_ref, o_ref):
        parts = [q_ref[k].astype(F32) for k in range(N_CHIPS)]
        o_ref[...] = ((parts[0] + parts[1]) + parts[2]) + parts[3]

    return pl.pallas_call(body, grid=(rh // tr,),
                          in_specs=[pl.BlockSpec((N_CHIPS, tr, LANES), lambda i: (0, i, 0))],
                          out_specs=pl.BlockSpec((tr, LANES), lambda i: (i, 0)),
                          out_shape=jax.ShapeDtypeStruct((rh, LANES), F32), name="grad_sum_chips",
                          compiler_params=_cparams("parallel"))(q)


def _send_half(half):
    def body(src, out, send_sem, recv_sem):
        x, y, c = lax.axis_index("x"), lax.axis_index("y"), lax.axis_index("c")
        cp = pltpu.make_async_remote_copy(src_ref=src, dst_ref=out, send_sem=send_sem, recv_sem=recv_sem,
                                          device_id=(x, y, 1 - c), device_id_type=MESH_ID)
        cp.start()
        cp.wait()

    return pl.pallas_call(body, out_shape=jax.ShapeDtypeStruct(half.shape, half.dtype),
                          in_specs=[ANY_SPEC], out_specs=ANY_SPEC,
                          scratch_shapes=[pltpu.SemaphoreType.DMA(()), pltpu.SemaphoreType.DMA(())],
                          name="grad_send_half")(half)


def _both_halves(own, got, core, *, tr):
    rh = own.shape[0]

    def body(c_ref, own_ref, got_ref, o_ref):
        mine = pl.program_id(0) == c_ref[0]
        o_ref[0] = jnp.where(mine, own_ref[...], got_ref[...])

    grid_spec = pltpu.PrefetchScalarGridSpec(
        num_scalar_prefetch=1, grid=(2, rh // tr),
        in_specs=[pl.BlockSpec((tr, LANES), lambda h, i, c_ref: (i, 0)),
                  pl.BlockSpec((tr, LANES), lambda h, i, c_ref: (i, 0))],
        out_specs=pl.BlockSpec((1, tr, LANES), lambda h, i, c_ref: (h, i, 0)))
    return pl.pallas_call(body, grid_spec=grid_spec, out_shape=jax.ShapeDtypeStruct((2, rh, LANES), own.dtype),
                          name="grad_both_halves", compiler_params=_cparams("parallel", "parallel"))(core.reshape(1), own, got)


def _allreduce_small(v):
    rows = v.shape[0]

    def body(v_ref, o_ref, buf, send_sems, recv_sems):
        x, y, c = lax.axis_index("x"), lax.axis_index("y"), lax.axis_index("c")
        me = 4 * x + 2 * y + c
        buf[me] = v_ref[...]
        sends = []
        for j in range(1, N_DEV):
            px, py, pc = x ^ ((j >> 2) & 1), y ^ ((j >> 1) & 1), c ^ (j & 1)
            cp = pltpu.make_async_remote_copy(src_ref=v_ref, dst_ref=buf.at[me], send_sem=send_sems.at[j - 1],
                                              recv_sem=recv_sems.at[j - 1], device_id=(px, py, pc), device_id_type=MESH_ID)
            cp.start()
            sends.append(cp)
        for j in range(1, N_DEV):
            px, py, pc = x ^ ((j >> 2) & 1), y ^ ((j >> 1) & 1), c ^ (j & 1)
            land = buf.at[4 * px + 2 * py + pc]
            pltpu.make_async_remote_copy(src_ref=land, dst_ref=land, send_sem=send_sems.at[j - 1],
                                         recv_sem=recv_sems.at[j - 1], device_id=(px, py, pc),
                                         device_id_type=MESH_ID).wait_recv()
        for cp in sends:
            cp.wait_send()
        acc = buf[0]
        for d in range(1, N_DEV):
            acc = acc + buf[d]
        o_ref[...] = acc

    vm = pl.BlockSpec(memory_space=pltpu.VMEM)
    return pl.pallas_call(body, out_shape=jax.ShapeDtypeStruct((rows, LANES), F32), in_specs=[vm], out_specs=vm,
                          scratch_shapes=[pltpu.VMEM((N_DEV, rows, LANES), F32), pltpu.SemaphoreType.DMA((N_DEV - 1,)),
                                          pltpu.SemaphoreType.DMA((N_DEV - 1,))],
                          name="allreduce_small")(v)


def _adamw(w, g, m, v, *, name):
    shape = w.shape
    cols = shape[-1] if w.ndim > 1 else shape[0]
    rows = w.size // cols
    w2, g2, m2, v2 = (t.reshape(rows, cols) for t in (w, g, m, v))
    tr = rows if rows <= 256 else _tile(rows, 256, 8)

    def fn(i, wv, gv, mv, vv):
        mn = ADAM_B1 * mv + (1.0 - ADAM_B1) * gv
        vn = ADAM_B2 * vv + (1.0 - ADAM_B2) * (gv * gv)
        m_hat = mn / (1.0 - ADAM_B1 ** ADAM_STEP)
        v_hat = vn / (1.0 - ADAM_B2 ** ADAM_STEP)
        delta = -ADAM_LR * (m_hat / (jnp.sqrt(v_hat) + ADAM_EPS) + ADAM_WD * wv)
        return delta, mn, vn

    ins = [_rspec(t, tr) for t in (w2, g2, m2, v2)]
    d, mn, vn = _row_call(fn, rows, tr, ins, [(cols, F32)] * 3, name=name)
    return d.reshape(shape), mn.reshape(shape), vn.reshape(shape)


def _rope_cols(w):
    z = jnp.zeros(w.shape[:-1] + (32,), w.dtype)
    return jnp.concatenate([w[..., :32], z, w[..., 32:], z], axis=-1)


def _rope_cols_inv(w):
    return jnp.concatenate([w[..., :32], w[..., 64:96]], axis=-1)


def _layer_layouts(W, i):
    w_in = W['w_in'][i]
    u, cq, ckv = w_in[:, :512], w_in[:, 512:1024], w_in[:, 1024:1280]
    kr, ga, gb = w_in[:, 1280:1344], w_in[:, 1344:2368], w_in[:, 2368:]
    L = {}
    L['w_in'] = jnp.concatenate([ga, gb, u, cq, ckv, _rope_cols(kr)], axis=1)
    wq = W['w_uq'][i]
    L['w_q'] = jnp.concatenate([wq[..., :QK_NOPE], _rope_cols(wq[..., QK_NOPE:])], axis=-1).reshape(Q_LORA, -1)
    wkv = W['w_ukv'][i]
    L['w_k'] = jnp.concatenate([wkv[..., :QK_NOPE], jnp.zeros_like(wkv[..., :LANES])], axis=-1).reshape(KV_LORA, -1)
    L['w_v'] = wkv[..., QK_NOPE:].reshape(KV_LORA, -1)
    L['w_gu'] = jnp.concatenate([W['w_gate'][i], W['w_up'][i]], axis=1)
    for n in ('w_a', 'w_b', 'w_o', 'w_down', 'w_ple_gate', 'w_ple'):
        L[n] = W[n][i]
    return L


def _layer_grads_to_reference_layout(G):
    d = G['w_in']
    ga, gb, u = d[:, ZC_GA:ZC_GB], d[:, ZC_GB:ZC_U], d[:, ZC_U:ZC_CQ]
    cq, ckv, kr = d[:, ZC_CQ:ZC_CKV], d[:, ZC_CKV:ZC_KR], _rope_cols_inv(d[:, ZC_KR:])
    out = {'w_in': jnp.concatenate([u, cq, ckv, kr, ga, gb], axis=1)}
    dq = G['w_q'].reshape(Q_LORA, N_HEADS, HEAD_PAD)
    out['w_uq'] = jnp.concatenate([dq[..., :QK_NOPE], _rope_cols_inv(dq[..., QK_NOPE:])], axis=-1)
    dk = G['w_k'].reshape(KV_LORA, N_HEADS, HEAD_PAD)[..., :QK_NOPE]
    dv = G['w_v'].reshape(KV_LORA, N_HEADS, V_HEAD)
    out['w_ukv'] = jnp.concatenate([dk, dv], axis=-1)
    out['w_gate'], out['w_up'] = G['w_gu'][:, :D_FF], G['w_gu'][:, D_FF:]
    for n in ('w_a', 'w_b', 'w_o', 'w_down', 'w_ple_gate', 'w_ple'):
        out[n] = G[n]
    return out


PACK_ROWS = 2048


def _pack_rows(parts, row_mult):
    flat = jnp.concatenate([p.reshape(-1) for p in parts])
    n = flat.shape[0]
    per = LANES * row_mult
    padded = -(-n // per) * per
    return jnp.pad(flat, (0, padded - n)).reshape(-1, LANES)


def _unpack_rows(flat2d, shapes):
    flat = flat2d.reshape(-1)
    out, off = [], 0
    for s in shapes:
        n = int(np.prod(s))
        out.append(flat[off:off + n].reshape(s))
        off += n
    return out


def _layer_fwd(i, x, p_i, L, norms, w_pool_bf, pool_scale, cos_t, sin_t, tr, blk):
    sv = {'x': x}
    z, sv['h'] = _mm_nn(x, L['w_in'], name=f"l{i}_in_proj", outs=[(Z_WIDTH, F32)], gain=norms['norm_mix'], emit_a=True,
                        tm=tr // 2)
    sv['z'] = z
    sv['pooled'], sv['mixed'], sv['pm'] = _pool_fwd(z, w_pool_bf, pool_scale, name=f"l{i}_pool", tr=tr)
    sv['ya'] = _mm_nn(sv['pm'], L['w_a'], name=f"l{i}_ya", outs=[(D_MODEL, BF16)], tm=tr)

    def heads(acc, rope_part):
        out = []
        for h in range(N_HEADS):
            out.append(acc[:, h * HEAD_PAD:h * HEAD_PAD + QK_NOPE])
            out.append(rope_part(acc[:, h * HEAD_PAD + QK_NOPE:(h + 1) * HEAD_PAD]))
        return jnp.concatenate(out, axis=1)

    def q_epi(acc, ct, st):
        return (heads(acc, lambda t: _rope(t, ct, st)),)

    def k_epi(acc, kr, ct, st):
        k_pe = _rope(kr, ct, st)
        return (heads(acc, lambda t: k_pe),)

    rope_rows = [(cos_t, LANES, 0), (sin_t, LANES, 0)]
    qk_width = N_HEADS * HEAD_PAD
    sv['q'], sv['cqn'] = _mm_nn(z, L['w_q'], name=f"l{i}_q_proj", outs=[(qk_width, BF16)], a_col=ZC_CQ,
                                gain=norms['q_norm'], emit_a=True, epi=q_epi, epi_rows=rope_rows, tm=tr)
    sv['k'], sv['ckvn'] = _mm_nn(z, L['w_k'], name=f"l{i}_k_proj", outs=[(qk_width, BF16)], a_col=ZC_CKV,
                                 gain=norms['kv_norm'], emit_a=True, epi=k_epi,
                                 epi_rows=[(z, LANES, ZC_KR // LANES)] + rope_rows, tm=tr)
    sv['v'] = _mm_nn(sv['ckvn'], L['w_v'], name=f"l{i}_v_proj", outs=[(N_HEADS * V_HEAD, BF16)], tm=tr)
    sv['o'], sv['lse'] = _flash_fwd(sv['q'], sv['k'], sv['v'], name=f"l{i}_attn", blk=blk)

    def merge_epi(yb, ga, gb, ya):
        return yb, _sigmoid(ga) * ya.astype(F32) + _sigmoid(gb) * yb

    sv['yb'], sv['merged'] = _mm_nn(sv['o'], L['w_b'], name=f"l{i}_yb_merge", outs=[(D_MODEL, BF16), (D_MODEL, BF16)],
                                    epi=merge_epi, epi_rows=[(z, D_MODEL, 0), (z, D_MODEL, 1), (sv['ya'], D_MODEL, 0)],
                                    tm=tr)

    def add_epi(acc, res):
        return (acc + res,)

    x1 = _mm_nn(sv['merged'], L['w_o'], name=f"l{i}_wo", outs=[(D_MODEL, F32)], epi=add_epi, epi_rows=[(x, D_MODEL, 0)],
                tm=tr)
    sv['x1'] = x1

    def swiglu_epi(acc):
        g, u = acc[:, :D_FF], acc[:, D_FF:]
        return acc, g * _sigmoid(g) * u

    sv['gu'], sv['act'], sv['h2'] = _mm_nn(x1, L['w_gu'], name=f"l{i}_gate_up", outs=[(2 * D_FF, BF16), (D_FF, BF16)],
                                           gain=norms['norm_ffn'], emit_a=True, epi=swiglu_epi, tm=tr // 2)
    x2 = _mm_nn(sv['act'], L['w_down'], name=f"l{i}_down", outs=[(D_MODEL, F32)], epi=add_epi,
                epi_rows=[(x1, D_MODEL, 0)], tm=tr)
    sv['x2'] = x2
    sv['logit'], sv['h3'] = _mm_nn(x2, L['w_ple_gate'], name=f"l{i}_ple_gate", outs=[(D_MODEL, F32)],
                                   gain=norms['norm_ple'], emit_a=True, tm=tr)

    def ple_epi(pe, xv, lg):
        return pe, xv + _sigmoid(lg) * pe

    sv['pe'], x3 = _mm_nn(p_i, L['w_ple'], name=f"l{i}_ple", outs=[(D_MODEL, F32), (D_MODEL, F32)], epi=ple_epi,
                          epi_rows=[(x2, D_MODEL, 0), (sv['logit'], D_MODEL, 0)], tm=tr)
    return x3, sv


def _layer_bwd(i, dx3, sv, p_i, L, norms, w_pool_bf, pool_scale, cos_t, sin_t, tr, blk):
    T = dx3.shape[0]
    G = {}
    z = sv['z']

    def ple_bwd(_, d, lg, pe):
        g = _sigmoid(lg)
        return d * pe * g * (1.0 - g), d * g

    dlogit, dpe = _row_call(ple_bwd, T, tr, [_rspec(dx3, tr), _rspec(sv['logit'], tr), _rspec(sv['pe'], tr)],
                            [(D_MODEL, BF16), (D_MODEL, BF16)], name=f"l{i}_ple_bwd")
    G['w_ple_gate'] = _mm_tn(sv['h3'], dlogit, name=f"l{i}_dw_ple_gate", tn=1024)
    G['w_ple'] = _mm_tn(p_i, dpe, name=f"l{i}_dw_ple", tn=1024)
    def gain_row(n):
        return norms[n].reshape(1, -1).astype(F32)

    dx2, dx2_bf, G['norm_ple'] = _mm_nt([(dlogit, L['w_ple_gate'])], name=f"l{i}_dh3_norm_bwd",
                                        outs=[(D_MODEL, F32), (D_MODEL, BF16)], epi=_rms_bwd_epi(True, True),
                                        epi_rows=[(sv['x2'], D_MODEL, 0), (dx3, D_MODEL, 0)], consts=[gain_row('norm_ple')],
                                        accs=[(1, D_MODEL)], tm=tr)

    def swiglu_bwd_epi(da, gu):
        g, u = gu[:, :D_FF].astype(F32), gu[:, D_FF:].astype(F32)
        sg = _sigmoid(g)
        return (jnp.concatenate([da * u * sg * (1.0 + g * (1.0 - sg)), da * g * sg], axis=1),)

    dgu = _mm_nt([(dx2_bf, L['w_down'])], name=f"l{i}_dact_swiglu_bwd", outs=[(2 * D_FF, BF16)], epi=swiglu_bwd_epi,
                 epi_rows=[(sv['gu'], 2 * D_FF, 0)], tm=tr // 2)
    G['w_down'] = _mm_tn(sv['act'], dx2_bf, name=f"l{i}_dw_down", tk=1408, tn=1024)
    G['w_gu'] = _mm_tn(sv['h2'], dgu, name=f"l{i}_dw_gate_up", tn=1408)
    dx1, dx1_bf, G['norm_ffn'] = _mm_nt([(dgu, L['w_gu'])], name=f"l{i}_dh2_norm_bwd",
                                        outs=[(D_MODEL, F32), (D_MODEL, BF16)], epi=_rms_bwd_epi(True, True),
                                        epi_rows=[(sv['x1'], D_MODEL, 0), (dx2, D_MODEL, 0)], consts=[gain_row('norm_ffn')],
                                        accs=[(1, D_MODEL)], tm=tr // 2)

    def merge_bwd_epi(dm, ga, gb, ya, yb):
        sa, sb = _sigmoid(ga), _sigmoid(gb)
        ya, yb = ya.astype(F32), yb.astype(F32)
        return dm * ya * sa * (1.0 - sa), dm * yb * sb * (1.0 - sb), dm * sa, dm * sb

    dga, dgb, dya, dyb = _mm_nt([(dx1_bf, L['w_o'])], name=f"l{i}_dmerged_bwd", outs=[(D_MODEL, BF16)] * 4,
                                epi=merge_bwd_epi, epi_rows=[(z, D_MODEL, 0), (z, D_MODEL, 1), (sv['ya'], D_MODEL, 0),
                                                             (sv['yb'], D_MODEL, 0)], tm=tr)
    G['w_o'] = _mm_tn(sv['merged'], dx1_bf, name=f"l{i}_dw_o", tn=1024)

    G['w_b'] = _mm_tn(sv['o'], dyb, name=f"l{i}_dw_b", tn=1024)
    do = _mm_nt([(dyb, L['w_b'])], name=f"l{i}_do", outs=[(D_MODEL, BF16)], tm=tr)
    trw = _wide_rows(tr)
    delta = _attn_delta(do, sv['o'], name=f"l{i}_attn_delta", tr=tr)
    dq, dk, dv = _flash_bwd(sv['q'], sv['k'], sv['v'], sv['lse'], delta, do, cos_t, sin_t, name=f"l{i}_attn_bwd", blk=blk)

    def dk_rope(_, d, ct, st):
        d = d.astype(F32)
        acc = d[:, QK_NOPE:HEAD_PAD]
        for h in range(1, N_HEADS):
            acc = acc + d[:, h * HEAD_PAD + QK_NOPE:(h + 1) * HEAD_PAD]
        return _rope_bwd(acc, ct, st)

    dkr = _row_call(dk_rope, T, tr, [_rspec(dk, tr), _rspec(cos_t, tr), _rspec(sin_t, tr)], [(LANES, BF16)],
                    name=f"l{i}_dk_rope")[0]
    G['w_q'] = _mm_tn(sv['cqn'], dq, name=f"l{i}_dw_q", tn=1024)
    G['w_k'] = _mm_tn(sv['ckvn'], dk, name=f"l{i}_dw_k", tn=1024)
    G['w_v'] = _mm_tn(sv['ckvn'], dv, name=f"l{i}_dw_v", tn=1024)
    dcq, G['q_norm'] = _mm_nt([(dq, L['w_q'])], name=f"l{i}_dcq", outs=[(Q_LORA, F32)], epi=_rms_bwd_epi(False, False),
                              epi_rows=[(z, Q_LORA, ZC_CQ // Q_LORA)], consts=[gain_row('q_norm')], accs=[(1, Q_LORA)], tm=tr)
    dckv, G['kv_norm'] = _mm_nt([(dk, L['w_k']), (dv, L['w_v'])], name=f"l{i}_dckv", outs=[(KV_LORA, F32)],
                                epi=_rms_bwd_epi(False, False), epi_rows=[(z, KV_LORA, ZC_CKV // KV_LORA)],
                                consts=[gain_row('kv_norm')], accs=[(1, KV_LORA)], tm=tr)

    G['w_a'] = _mm_tn(sv['pm'], dya, name=f"l{i}_dw_a", tn=1024)
    dpm = _mm_nt([(dya, L['w_a'])], name=f"l{i}_dpm", outs=[(POOL_WIDTH, F32)], tm=tr)
    dpool, dpool_cnt, G['pool_scale'], G['w_pool'] = _pool_bwd_mix(dpm, sv['mixed'], sv['pooled'], w_pool_bf, pool_scale,
                                                                   name=f"l{i}_pool_bwd_mix", tr=tr)
    du = _pool_bwd_window(dpool, dpool_cnt, name=f"l{i}_pool_bwd_window", tr=tr)

    def join(_, a, b, c, d, e, f):
        return jnp.concatenate([a, b, c, d.astype(BF16), e.astype(BF16), f], axis=1)

    dz = _row_call(join, T, trw, [_rspec(t, trw) for t in (dga, dgb, du, dcq, dckv, dkr)], [(Z_WIDTH, BF16)],
                   name=f"l{i}_dz_join")[0]
    G['w_in'] = _mm_tn(sv['h'], dz, name=f"l{i}_dw_in", tn=1152)
    dx, G['norm_mix'] = _mm_nt([(dz, L['w_in'])], name=f"l{i}_dh_norm_bwd", outs=[(D_MODEL, F32)],
                               epi=_rms_bwd_epi(True, False), epi_rows=[(sv['x'], D_MODEL, 0), (dx1, D_MODEL, 0)],
                               consts=[gain_row('norm_mix')], accs=[(1, D_MODEL)], tm=tr)
    return dx, G


def kernel(x, p, positions, norm_mix, w_in, w_pool, pool_scale, q_norm, kv_norm, w_uq, w_ukv, w_a, w_b, w_o, norm_ffn, w_gate, w_up, w_down, norm_ple, w_ple_gate, w_ple, final_norm, loss_target, m_norm_mix, m_w_in, m_w_pool, m_pool_scale, m_q_norm, m_kv_norm, m_w_uq, m_w_ukv, m_w_a, m_w_b, m_w_o, m_norm_ffn, m_w_gate, m_w_up, m_w_down, m_norm_ple, m_w_ple_gate, m_w_ple, m_final_norm, v_norm_mix, v_w_in, v_w_pool, v_pool_scale, v_q_norm, v_kv_norm, v_w_uq, v_w_ukv, v_w_a, v_w_b, v_w_o, v_norm_ffn, v_w_gate, v_w_up, v_w_down, v_norm_ple, v_w_ple_gate, v_w_ple, v_final_norm):
    given = dict(locals())
    weights = {n: given[n] for n in WEIGHTS}
    T = x.shape[1]
    tr = min(512, max(T // 2, 8))
    blk = min(ATTN_BLOCK, max(T // 4, 128))
    x0 = x.reshape(T, D_MODEL)
    target = loss_target.reshape(T, D_MODEL)

    names = list(SHARDED)
    shard_shapes = [weights[n].shape for n in names]
    flat = _pack_rows([weights[n].astype(BF16) for n in names], row_mult=PACK_ROWS)
    R = flat.shape[0]
    chip = (2 * lax.axis_index("x") + lax.axis_index("y")).astype(jnp.int32)
    core = lax.axis_index("c").astype(jnp.int32)
    gathered = _gather_weights(flat).reshape(N_CHIPS, R, LANES)
    gathered = lax.dynamic_update_slice(gathered, flat.reshape(1, R, LANES), (chip, 0, 0))
    per_chip = [_unpack_rows(gathered[k], shard_shapes) for k in range(N_CHIPS)]
    W = {n: jnp.concatenate([per_chip[k][j] for k in range(N_CHIPS)], axis=SHARDED[n]) for j, n in enumerate(names)}
    layouts = [_layer_layouts(W, i) for i in range(DEPTH)]
    w_pool_bf = w_pool.astype(BF16)

    inv_freq = 1.0 / (ROPE_THETA ** (jnp.arange(0, QK_ROPE, 2, dtype=F32) / QK_ROPE))
    zero32 = jnp.zeros((32,), F32)
    freq_row = jnp.concatenate([inv_freq, zero32, inv_freq, zero32]).reshape(1, LANES)
    cos_mask = jnp.concatenate([jnp.ones((32,), F32), zero32, jnp.ones((32,), F32), zero32]).reshape(1, LANES)
    sin_sign = jnp.concatenate([-jnp.ones((32,), F32), zero32, jnp.ones((32,), F32), zero32]).reshape(1, LANES)

    def rope_tables(_, pos, fr, cm, ss):
        ang = pos.astype(F32) * fr
        return jnp.cos(ang) * cm, jnp.sin(ang) * ss

    pos_col = positions.reshape(T, 1)
    cos_t, sin_t = _row_call(rope_tables, T, tr, [_rspec(pos_col, tr), _bspec(freq_row), _bspec(cos_mask), _bspec(sin_sign)],
                             [(LANES, F32), (LANES, F32)], name="rope_tables")

    xs = x0
    saved = []
    for i in range(DEPTH):
        norms = {n: weights[n][i] for n in ('norm_mix', 'q_norm', 'kv_norm', 'norm_ffn', 'norm_ple')}
        xs, sv = _layer_fwd(i, xs, p[i, 0], layouts[i], norms, w_pool_bf[i], pool_scale[i], cos_t, sin_t, tr, blk)
        saved.append((sv, norms))

    def head(_, xv, tv, gv):
        rstd = lax.rsqrt(jnp.mean(xv * xv, axis=-1, keepdims=True) + EPS)
        xhat = xv * rstd
        err = xhat * gv - tv
        loss = 0.5 * jnp.sum(jnp.mean(err * err, axis=-1, keepdims=True), axis=0, keepdims=True)
        dy = err * (1.0 / D_MODEL)
        dg = jnp.sum(dy * xhat, axis=0, keepdims=True)
        dxh = dy * gv
        dx = rstd * (dxh - xhat * jnp.mean(dxh * xhat, axis=-1, keepdims=True))
        return dx, jnp.broadcast_to(loss, (1, LANES)), dg

    dx, loss_part, g_final = _row_call(head, T, tr, [_rspec(xs, tr), _rspec(target, tr), _bspec(final_norm.reshape(1, D_MODEL))],
                                       [(D_MODEL, F32)], [(1, LANES), (1, D_MODEL)], name="loss_head")
    loss = lax.psum(loss_part[0, 0], ("x", "y", "c"))

    layer_grads = [None] * DEPTH
    for i in reversed(range(DEPTH)):
        sv, norms = saved[i]
        dx, layer_grads[i] = _layer_bwd(i, dx, sv, p[i, 0], layouts[i], norms, w_pool_bf[i], pool_scale[i], cos_t, sin_t, tr,
                                        blk)
    grad_x = dx.reshape(x.shape)

    ref_layout = [_layer_grads_to_reference_layout(g) for g in layer_grads]
    local = {n: jnp.stack([ref_layout[i][n] for i in range(DEPTH)]) for n in names}
    for n in ('norm_mix', 'q_norm', 'kv_norm', 'norm_ffn', 'norm_ple', 'pool_scale'):
        local[n] = jnp.stack([layer_grads[i][n].reshape(-1) for i in range(DEPTH)])
    local['w_pool'] = jnp.stack([layer_grads[i]['w_pool'] for i in range(DEPTH)])
    local['final_norm'] = g_final.reshape(-1)

    send = []
    for k in range(N_CHIPS):
        parts = []
        for n in names:
            ax = SHARDED[n]
            size = local[n].shape[ax] // N_CHIPS
            parts.append(lax.slice_in_dim(local[n], k * size, (k + 1) * size, axis=ax))
        send.append(_pack_rows(parts, row_mult=PACK_ROWS))
    rh = R // 2
    trr = PACK_ROWS // 2
    g_all = jnp.stack(send).reshape(N_CHIPS, 2, rh, LANES)
    part = _add_halves(g_all, _swap_halves(g_all), tr=trr)
    reduced_half = _sum_chips(_scatter_partials(part), tr=trr)
    reduced = _both_halves(reduced_half, _send_half(reduced_half), core, tr=trr).reshape(R, LANES)
    grads = dict(zip(names, _unpack_rows(reduced, shard_shapes)))

    rep_shapes = [weights[n].shape for n in REPLICATED]
    rep = _allreduce_small(_pack_rows([local[n] for n in REPLICATED], row_mult=8))
    grads.update(zip(REPLICATED, _unpack_rows(rep, rep_shapes)))

    deltas, new_m, new_v = {}, {}, {}
    for n in WEIGHTS:
        deltas[n], new_m[n], new_v[n] = _adamw(weights[n], grads[n], given['m_' + n], given['v_' + n], name=f"adamw_{n}")
    return (loss, grad_x, *[grads[n] for n in WEIGHTS], *[deltas[n] for n in WEIGHTS], *[new_m[n] for n in WEIGHTS],
            *[new_v[n] for n in WEIGHTS])
```

```python
import functools

import numpy as np
import jax
import jax.numpy as jnp
from jax import lax
from jax.experimental import pallas as pl
from jax.experimental.pallas import tpu as pltpu

F32 = jnp.float32
BF16 = jnp.bfloat16

D_MODEL = 1024
DEPTH = 2
PLE_DIM = 256
POOL_WINDOWS = (2, 4, 8, 16)
POOL_GROUP = 128
POOL_WIDTH = 512
N_HEADS = 8
Q_LORA = 512
KV_LORA = 256
QK_NOPE = 128
QK_ROPE = 64
QK_HEAD = 192
V_HEAD = 128
D_FF = 2816
ROPE_THETA = 10000.0
EPS = 1e-6
ATTN_SCALE = QK_HEAD ** -0.5

ADAM_LR = 0.001
ADAM_B1 = 0.9
ADAM_B2 = 0.999
ADAM_EPS = 1e-08
ADAM_WD = 0.01
ADAM_STEP = 10

LANES = 128
HALO = 16
HEAD_PAD = 256
V7X_VMEM_BYTES = 64 * 1024 * 1024
VMEM_LIMIT = (V7X_VMEM_BYTES * 3) // 4
N_CHIPS = 4
N_DEV = 8
NEG_INF = -1e30

ZC_GA, ZC_GB, ZC_U, ZC_CQ, ZC_CKV, ZC_KR = 0, 1024, 2048, 2560, 3072, 3328
Z_WIDTH = 3456

WEIGHTS = ['norm_mix', 'w_in', 'w_pool', 'pool_scale', 'q_norm', 'kv_norm', 'w_uq', 'w_ukv', 'w_a', 'w_b', 'w_o',
           'norm_ffn', 'w_gate', 'w_up', 'w_down', 'norm_ple', 'w_ple_gate', 'w_ple', 'final_norm']
SHARDED = {'w_in': 2, 'w_uq': 1, 'w_ukv': 1, 'w_a': 2, 'w_b': 1, 'w_o': 1, 'w_gate': 2, 'w_up': 2, 'w_down': 1,
           'w_ple_gate': 1, 'w_ple': 2}
REPLICATED = [n for n in WEIGHTS if n not in SHARDED]


def _tile(n, target, mult=LANES):
    if n <= target:
        return n
    best = None
    for t in range(mult, target + 1, mult):
        if n % t == 0:
            best = t
    assert best is not None, (n, target)
    return best


def _cparams(*sem):
    return pltpu.CompilerParams(dimension_semantics=sem, vmem_limit_bytes=VMEM_LIMIT)


def _rope(t, cos_t, sin_t):
    return t * cos_t + pltpu.roll(t, 64, 1) * sin_t


def _rope_bwd(d, cos_t, sin_t):
    return d * cos_t + pltpu.roll(d * sin_t, 64, 1)


def _sigmoid(v):
    return 1.0 / (1.0 + jnp.exp(-v))


def _mm_nn(a, b, *, name, outs, a_col=0, gain=None, emit_a=False, epi=None, epi_rows=(), tm=512):
    M = a.shape[0]
    K, N = b.shape
    tm = min(tm, M)
    assert a_col % K == 0 and M % tm == 0
    a_blk = a_col // K
    n_rows, n_out = len(epi_rows), len(outs)

    def body(*refs):
        a_ref, b_ref = refs[0], refs[1]
        pos = 2
        g_ref = None
        if gain is not None:
            g_ref = refs[pos]
            pos += 1
        row_refs = refs[pos:pos + n_rows]
        out_refs = refs[pos + n_rows:pos + n_rows + n_out]
        lhs = a_ref[...]
        if gain is not None:
            av = lhs.astype(F32)
            lhs = av * lax.rsqrt(jnp.mean(av * av, axis=-1, keepdims=True) + EPS) * g_ref[...]
        lhs = lhs.astype(BF16)
        if emit_a:
            refs[pos + n_rows + n_out][...] = lhs
        acc = jnp.dot(lhs, b_ref[...], preferred_element_type=F32)
        vals = (acc,) if epi is None else epi(acc, *[r[...] for r in row_refs])
        for r, v in zip(out_refs, vals):
            r[...] = v.astype(r.dtype)

    in_specs = [pl.BlockSpec((tm, K), lambda i: (i, a_blk)), pl.BlockSpec((K, N), lambda i: (0, 0))]
    args = [a, b]
    if gain is not None:
        in_specs.append(pl.BlockSpec((1, K), lambda i: (0, 0)))
        args.append(gain.reshape(1, K).astype(F32))
    for arr, w, blk in epi_rows:
        in_specs.append(pl.BlockSpec((tm, w), lambda i, blk=blk: (i, blk)))
        args.append(arr)
    out_shape = [jax.ShapeDtypeStruct((M, w), dt) for w, dt in outs]
    out_specs = [pl.BlockSpec((tm, w), lambda i: (i, 0)) for w, dt in outs]
    if emit_a:
        out_shape.append(jax.ShapeDtypeStruct((M, K), BF16))
        out_specs.append(pl.BlockSpec((tm, K), lambda i: (i, 0)))
    res = pl.pallas_call(body, grid=(M // tm,), in_specs=in_specs, out_specs=out_specs, out_shape=out_shape,
                         name=name, compiler_params=_cparams("parallel"))(*args)
    return res[0] if len(res) == 1 else res


def _mm_nt(pairs, *, name, outs, epi=None, epi_rows=(), consts=(), accs=(), tm=512):
    M = pairs[0][0].shape[0]
    N = pairs[0][1].shape[0]
    tm = min(tm, M)
    n_p, n_in, n_out, n_acc = len(pairs), len(epi_rows) + len(consts), len(outs), len(accs)

    def body(*refs):
        acc = None
        for k in range(n_p):
            av = refs[2 * k][...].astype(BF16)
            part = lax.dot_general(av, refs[2 * k + 1][...], NT_DIMS, preferred_element_type=F32)
            acc = part if acc is None else acc + part
        pos = 2 * n_p
        extra = [r[...] for r in refs[pos:pos + n_in]]
        pos += n_in
        vals = (acc,) if epi is None else epi(acc, *extra)
        for r, v in zip(refs[pos:pos + n_out], vals[:n_out]):
            r[...] = v.astype(r.dtype)
        if n_acc:
            acc_refs = refs[pos + n_out:pos + n_out + n_acc]

            @pl.when(pl.program_id(0) == 0)
            def _():
                for r in acc_refs:
                    r[...] = jnp.zeros_like(r)
            for r, v in zip(acc_refs, vals[n_out:]):
                r[...] += v

    in_specs, args = [], []
    for a, b in pairs:
        assert a.shape[1] == b.shape[1] and b.shape[0] == N and a.shape[0] == M
        in_specs.append(pl.BlockSpec((tm, a.shape[1]), lambda i: (i, 0)))
        in_specs.append(pl.BlockSpec(b.shape, lambda i: (0, 0)))
        args += [a, b]
    for arr, w, blk in epi_rows:
        in_specs.append(pl.BlockSpec((tm, w), lambda i, blk=blk: (i, blk)))
        args.append(arr)
    for arr in consts:
        in_specs.append(pl.BlockSpec(arr.shape, lambda i, n=arr.ndim: (0,) * n))
        args.append(arr)
    out_shape = [jax.ShapeDtypeStruct((M, w), dt) for w, dt in outs]
    out_specs = [pl.BlockSpec((tm, w), lambda i: (i, 0)) for w, dt in outs]
    for s in accs:
        out_shape.append(jax.ShapeDtypeStruct(s, F32))
        out_specs.append(pl.BlockSpec(s, lambda i, n=len(s): (0,) * n))
    res = pl.pallas_call(body, grid=(M // tm,), in_specs=in_specs, out_specs=out_specs, out_shape=out_shape, name=name,
                         compiler_params=_cparams("arbitrary" if n_acc else "parallel"))(*args)
    return res[0] if len(res) == 1 else res


def _rms_bwd_epi(with_res, emit_bf16):
    def epi(dh, xv, *rest):
        gv = rest[-1]
        xv = xv.astype(F32)
        rstd = lax.rsqrt(jnp.mean(xv * xv, axis=-1, keepdims=True) + EPS)
        xhat = xv * rstd
        dg = jnp.sum(dh * xhat, axis=0, keepdims=True)
        dxh = dh * gv
        dx = rstd * (dxh - xhat * jnp.mean(dxh * xhat, axis=-1, keepdims=True))
        if with_res:
            dx = dx + rest[0].astype(F32)
        return (dx, dx, dg) if emit_bf16 else (dx, dg)
    return epi


def _mm_tn(a, b, *, name, a_col=0, a_w=None, tk=1024, tn=1152, tm=512):
    M = a.shape[0]
    a_w = a.shape[1] if a_w is None else a_w
    N = b.shape[1]
    tm = min(tm, M)
    tk = _tile(a_w, tk)
    tn = _tile(N, tn)
    assert a_col % tk == 0 and M % tm == 0
    a_blk0 = a_col // tk

    def body(a_ref, b_ref, o_ref):
        @pl.when(pl.program_id(2) == 0)
        def _():
            o_ref[...] = jnp.zeros_like(o_ref)
        o_ref[...] += lax.dot_general(a_ref[...].astype(BF16), b_ref[...].astype(BF16), (((0,), (0,)), ((), ())),
                                      preferred_element_type=F32)

    return pl.pallas_call(body, grid=(a_w // tk, N // tn, M // tm),
                          in_specs=[pl.BlockSpec((tm, tk), lambda k, j, m: (m, k + a_blk0)),
                                    pl.BlockSpec((tm, tn), lambda k, j, m: (m, j))],
                          out_specs=pl.BlockSpec((tk, tn), lambda k, j, m: (k, j)),
                          out_shape=jax.ShapeDtypeStruct((a_w, N), F32), name=name,
                          compiler_params=_cparams("parallel", "parallel", "arbitrary"))(a, b)


def _row_call(fn, rows, tr, ins, outs, accs=(), *, name):
    n_in, n_out, n_acc = len(ins), len(outs), len(accs)

    def body(*refs):
        i = pl.program_id(0)
        vals = fn(i, *[r[...] for r in refs[:n_in]])
        if not isinstance(vals, (tuple, list)):
            vals = (vals,)
        for r, v in zip(refs[n_in:n_in + n_out], vals[:n_out]):
            r[...] = v.astype(r.dtype)
        if n_acc:
            acc_refs = refs[n_in + n_out:]

            @pl.when(i == 0)
            def _():
                for r in acc_refs:
                    r[...] = jnp.zeros_like(r)
            for r, v in zip(acc_refs, vals[n_out:]):
                r[...] += v

    out_shape = [jax.ShapeDtypeStruct((rows, w), dt) for w, dt in outs]
    out_specs = [pl.BlockSpec((tr, w), lambda i: (i, 0)) for w, dt in outs]
    for s in accs:
        out_shape.append(jax.ShapeDtypeStruct(s, F32))
        out_specs.append(pl.BlockSpec(s, lambda i, n=len(s): (0,) * n))
    res = pl.pallas_call(body, grid=(rows // tr,), in_specs=[pl.BlockSpec(bs, im) for _, bs, im in ins],
                         out_specs=out_specs, out_shape=out_shape, name=name,
                         compiler_params=_cparams("arbitrary"))(*[a for a, _, _ in ins])
    return res


def _wide_rows(tr):
    return max(tr // 4, 8)


def _rspec(arr, tr, w=None, blk=0):
    w = arr.shape[1] if w is None else w
    return (arr, (tr, w), lambda i, blk=blk: (i, blk))


def _bspec(arr):
    return (arr, arr.shape, lambda i, n=arr.ndim: (0,) * n)


def _pool_counts(i, tr):
    t = (i * tr + lax.broadcasted_iota(jnp.int32, (tr, 1), 0) + 1).astype(F32)
    return [jnp.minimum(t, float(w)) for w in POOL_WINDOWS]


def _pool_fwd(z, w_pool_bf, pool_scale, *, name, tr):
    rows = z.shape[0]
    ublk = ZC_U // POOL_WIDTH
    hpt = tr // HALO

    def fn(i, u, uprev, wp, ps):
        uprev = jnp.where(i > 0, uprev, 0.0)
        ext = jnp.concatenate([uprev, u], axis=0)
        s2 = ext + pltpu.roll(ext, 1, 0)
        s4 = s2 + pltpu.roll(s2, 2, 0)
        s8 = s4 + pltpu.roll(s4, 4, 0)
        s16 = s8 + pltpu.roll(s8, 8, 0)
        cnts = _pool_counts(i, tr)
        pooled, mixed = [], []
        for g, sw in enumerate((s2, s4, s8, s16)):
            lanes = slice(g * POOL_GROUP, (g + 1) * POOL_GROUP)
            pg = sw[HALO:, lanes] / cnts[g] - u[:, lanes]
            pooled.append(pg)
            mixed.append(jnp.dot(pg.astype(BF16), wp[g], preferred_element_type=F32))
        pooled = jnp.concatenate(pooled, axis=1)
        mixed = jnp.concatenate(mixed, axis=1)
        return pooled, mixed, mixed * ps

    ins = [_rspec(z, tr, POOL_WIDTH, ublk),
           (z, (HALO, POOL_WIDTH), lambda i: (jnp.maximum(i * hpt - 1, 0), ublk)),
           _bspec(w_pool_bf), _bspec(pool_scale.reshape(1, POOL_WIDTH))]
    return _row_call(fn, rows, tr, ins, [(POOL_WIDTH, BF16), (POOL_WIDTH, F32), (POOL_WIDTH, BF16)], name=name)


def _pool_bwd_mix(dpm, mixed, pooled, w_pool_bf, pool_scale, *, name, tr):
    rows = dpm.shape[0]

    def fn(i, dv, mv, pv, wp, ps):
        dv = dv.astype(F32)
        dscale = jnp.sum(dv * mv, axis=0, keepdims=True)
        dmix = (dv * ps).astype(BF16)
        cnts = _pool_counts(i, tr)
        dpool, dwp = [], []
        for g in range(len(POOL_WINDOWS)):
            lanes = slice(g * POOL_GROUP, (g + 1) * POOL_GROUP)
            dg = lax.dot_general(dmix[:, lanes], wp[g], (((1,), (1,)), ((), ())), preferred_element_type=F32)
            dpool.append(dg)
            dwp.append(lax.dot_general(pv[:, lanes], dmix[:, lanes], (((0,), (0,)), ((), ())),
                                       preferred_element_type=F32)[None])
        dpool = jnp.concatenate(dpool, axis=1)
        dpool_cnt = jnp.concatenate([dpool[:, g * POOL_GROUP:(g + 1) * POOL_GROUP] / cnts[g]
                                     for g in range(len(POOL_WINDOWS))], axis=1)
        return dpool, dpool_cnt, dscale, jnp.concatenate(dwp, axis=0)

    ins = [_rspec(dpm, tr), _rspec(mixed, tr), _rspec(pooled, tr), _bspec(w_pool_bf),
           _bspec(pool_scale.reshape(1, POOL_WIDTH))]
    return _row_call(fn, rows, tr, ins, [(POOL_WIDTH, F32), (POOL_WIDTH, F32)],
                     [(1, POOL_WIDTH), (len(POOL_WINDOWS), POOL_GROUP, POOL_GROUP)], name=name)


def _pool_bwd_window(dpool, dpool_cnt, *, name, tr):
    rows = dpool.shape[0]
    hpt = tr // HALO
    n_halo = rows // HALO
    n_tiles = rows // tr

    def fn(i, dp, dc, dnext):
        dnext = jnp.where(i < n_tiles - 1, dnext, 0.0)
        ext = jnp.concatenate([dc, dnext], axis=0)
        n = tr + HALO
        s2 = ext + pltpu.roll(ext, n - 1, 0)
        s4 = s2 + pltpu.roll(s2, n - 2, 0)
        s8 = s4 + pltpu.roll(s4, n - 4, 0)
        s16 = s8 + pltpu.roll(s8, n - 8, 0)
        out = []
        for g, sw in enumerate((s2, s4, s8, s16)):
            lanes = slice(g * POOL_GROUP, (g + 1) * POOL_GROUP)
            out.append(sw[:tr, lanes] - dp[:, lanes])
        return jnp.concatenate(out, axis=1)

    ins = [_rspec(dpool, tr), _rspec(dpool_cnt, tr),
           (dpool_cnt, (HALO, POOL_WIDTH), lambda i: (jnp.minimum((i + 1) * hpt, n_halo - 1), 0))]
    return _row_call(fn, rows, tr, ins, [(POOL_WIDTH, BF16)], name=name)[0]


def _causal_pairs(n, k_major):
    if k_major:
        pairs = [(qi, ki) for ki in range(n) for qi in range(ki, n)]
    else:
        pairs = [(qi, ki) for qi in range(n) for ki in range(qi + 1)]
    return (jnp.asarray(np.array([p[0] for p in pairs], np.int32)),
            jnp.asarray(np.array([p[1] for p in pairs], np.int32)), len(pairs))


SUBLANES = 8
NT_DIMS = (((1,), (1,)), ((), ()))
TN_DIMS = (((0,), (0,)), ((), ()))


ATTN_BLOCK = 1024
QUERY_CHUNK = 256
EXP2_SCALE = ATTN_SCALE * 1.4426950408889634


def _scores_t(q_c, k, c, qc, diag):
    s = lax.dot_general(k, q_c, NT_DIMS, preferred_element_type=F32)
    if diag:
        key = lax.broadcasted_iota(jnp.int32, s.shape, 0)
        qry = lax.broadcasted_iota(jnp.int32, s.shape, 1) + c * qc
        s = jnp.where(key <= qry, s, NEG_INF)
    return s


def _flash_fwd(q, k, v, *, name, blk):
    T = q.shape[0]
    n = T // blk
    qc = min(QUERY_CHUNK, blk)
    qtab, ktab, n_pairs = _causal_pairs(n, k_major=False)

    def body(qt, kt, q_ref, k_ref, v_ref, o_ref, lse_ref, m_s, l_s, acc_s):
        p = pl.program_id(1)
        qi, ki = qt[p], kt[p]

        @pl.when(ki == 0)
        def _():
            m_s[...] = jnp.full_like(m_s, NEG_INF)
            l_s[...] = jnp.zeros_like(l_s)
            acc_s[...] = jnp.zeros_like(acc_s)

        def step(diag):
            kv, vv = k_ref[...], v_ref[...]
            chunks = [slice(c * qc, (c + 1) * qc) for c in range(blk // qc)]
            scores = [_scores_t(q_ref[rows, :], kv, c, qc, diag) for c, rows in enumerate(chunks)]
            probs, alphas = [], []
            for rows, s_t in zip(chunks, scores):
                m_prev = m_s[:, rows]
                m_new = jnp.maximum(m_prev, jnp.max(s_t, axis=0, keepdims=True))
                p_t = jnp.exp2((s_t - m_new) * EXP2_SCALE)
                alpha = jnp.exp2((m_prev - m_new) * EXP2_SCALE)
                l_s[:, rows] = alpha * l_s[:, rows] + jnp.sum(p_t, axis=0, keepdims=True)
                m_s[:, rows] = m_new
                probs.append(p_t.astype(BF16))
                alphas.append(alpha)
            for rows, p_t, alpha in zip(chunks, probs, alphas):
                acc_s[:, rows] = alpha * acc_s[:, rows] + lax.dot_general(vv, p_t, TN_DIMS, preferred_element_type=F32)

        @pl.when(ki != qi)
        def _():
            step(False)

        @pl.when(ki == qi)
        def _():
            step(True)
            o_ref[...] = (acc_s[...] / l_s[...]).T.astype(o_ref.dtype)
            lse2 = m_s[...] * EXP2_SCALE + jnp.log2(l_s[...])
            lse_ref[...] = jnp.broadcast_to(lse2, lse_ref.shape)

    grid_spec = pltpu.PrefetchScalarGridSpec(
        num_scalar_prefetch=2, grid=(N_HEADS, n_pairs),
        in_specs=[pl.BlockSpec((blk, HEAD_PAD), lambda h, p, qt, kt: (qt[p], h)),
                  pl.BlockSpec((blk, HEAD_PAD), lambda h, p, qt, kt: (kt[p], h)),
                  pl.BlockSpec((blk, V_HEAD), lambda h, p, qt, kt: (kt[p], h))],
        out_specs=[pl.BlockSpec((blk, V_HEAD), lambda h, p, qt, kt: (qt[p], h)),
                   pl.BlockSpec((SUBLANES, blk), lambda h, p, qt, kt: (h, qt[p]))],
        scratch_shapes=[pltpu.VMEM((1, blk), F32), pltpu.VMEM((1, blk), F32), pltpu.VMEM((V_HEAD, blk), F32)])
    return pl.pallas_call(body, grid_spec=grid_spec,
                          out_shape=[jax.ShapeDtypeStruct((T, N_HEADS * V_HEAD), BF16),
                                     jax.ShapeDtypeStruct((N_HEADS * SUBLANES, T), F32)],
                          name=name, compiler_params=_cparams("parallel", "arbitrary"))(qtab, ktab, q, k, v)


def _attn_delta(do, o, *, name, tr):
    T = do.shape[0]

    def body(do_ref, o_ref, d_ref):
        prod = do_ref[...].astype(F32) * o_ref[...].astype(F32)
        lane_head = lax.broadcasted_iota(jnp.int32, (tr, LANES), 1) // SUBLANES
        mat = jnp.zeros((tr, LANES), F32)
        for h in range(N_HEADS):
            d_h = jnp.sum(prod[:, h * V_HEAD:(h + 1) * V_HEAD], axis=1, keepdims=True)
            mat = jnp.where(lane_head == h, d_h, mat)
        d_ref[...] = mat.T[:N_HEADS * SUBLANES, :]

    return pl.pallas_call(body, grid=(T // tr,),
                          in_specs=[pl.BlockSpec((tr, N_HEADS * V_HEAD), lambda i: (i, 0)),
                                    pl.BlockSpec((tr, N_HEADS * V_HEAD), lambda i: (i, 0))],
                          out_specs=pl.BlockSpec((N_HEADS * SUBLANES, tr), lambda i: (0, i)),
                          out_shape=jax.ShapeDtypeStruct((N_HEADS * SUBLANES, T), F32), name=name,
                          compiler_params=_cparams("parallel"))(do, o)


def _flash_bwd(q, k, v, lse, delta, do, cos_t, sin_t, *, name, blk):
    T = q.shape[0]
    n = T // blk
    qc = min(QUERY_CHUNK, blk)
    qtab, ktab, n_pairs = _causal_pairs(n, k_major=True)

    def body(qt, kt, q_ref, k_ref, v_ref, lse_ref, delta_ref, do_ref, cos_ref, sin_ref, dq_ref, dk_ref, dv_ref,
             dq_s, dk_s, dv_s):
        p = pl.program_id(1)
        qi, ki = qt[p], kt[p]
        first = qi == ki

        @pl.when(p == 0)
        def _():
            dq_s[...] = jnp.zeros_like(dq_s)

        @pl.when(first)
        def _():
            dk_s[...] = jnp.zeros_like(dk_s)
            dv_s[...] = jnp.zeros_like(dv_s)

        def step(diag):
            kv, vv = k_ref[...], v_ref[...]
            chunks = [slice(c * qc, (c + 1) * qc) for c in range(blk // qc)]
            qs = [q_ref[rows, :] for rows in chunks]
            dos = [do_ref[rows, :] for rows in chunks]
            scores = [_scores_t(q_c, kv, c, qc, diag) for c, q_c in enumerate(qs)]
            dps = [lax.dot_general(vv, do_c, NT_DIMS, preferred_element_type=F32) for do_c in dos]
            probs, dss = [], []
            for rows, s_t, dp_t in zip(chunks, scores, dps):
                p_t = jnp.exp2(s_t * EXP2_SCALE - lse_ref[0:1, rows])
                dss.append((p_t * (dp_t - delta_ref[0:1, rows])).astype(BF16))
                probs.append(p_t.astype(BF16))
            dv_acc = dk_acc = None
            for p_t, ds_t, q_c, do_c in zip(probs, dss, qs, dos):
                dv_c = jnp.dot(p_t, do_c, preferred_element_type=F32)
                dk_c = jnp.dot(ds_t, q_c, preferred_element_type=F32)
                dv_acc = dv_c if dv_acc is None else dv_acc + dv_c
                dk_acc = dk_c if dk_acc is None else dk_acc + dk_c
            for rows, ds_t in zip(chunks, dss):
                dq_s[qi, :, rows] += lax.dot_general(kv, ds_t, TN_DIMS, preferred_element_type=F32)
            dv_s[...] += dv_acc
            dk_s[...] += dk_acc

        @pl.when(jnp.logical_not(first))
        def _():
            step(False)

        @pl.when(first)
        def _():
            step(True)
            dq_t = (dq_s[qi] * ATTN_SCALE).T
            dq_ref[:, :QK_NOPE] = dq_t[:, :QK_NOPE].astype(dq_ref.dtype)
            dq_ref[:, QK_NOPE:] = _rope_bwd(dq_t[:, QK_NOPE:], cos_ref[...], sin_ref[...]).astype(dq_ref.dtype)

        @pl.when(qi == n - 1)
        def _():
            dk_ref[...] = (dk_s[...] * ATTN_SCALE).astype(dk_ref.dtype)
            dv_ref[...] = dv_s[...].astype(dv_ref.dtype)

    qmap = lambda h, p, qt, kt: (qt[p], h)
    kmap = lambda h, p, qt, kt: (kt[p], h)
    smap = lambda h, p, qt, kt: (h, qt[p])
    tmap = lambda h, p, qt, kt: (kt[p], 0)
    grid_spec = pltpu.PrefetchScalarGridSpec(
        num_scalar_prefetch=2, grid=(N_HEADS, n_pairs),
        in_specs=[pl.BlockSpec((blk, HEAD_PAD), qmap), pl.BlockSpec((blk, HEAD_PAD), kmap),
                  pl.BlockSpec((blk, V_HEAD), kmap), pl.BlockSpec((SUBLANES, blk), smap),
                  pl.BlockSpec((SUBLANES, blk), smap), pl.BlockSpec((blk, V_HEAD), qmap),
                  pl.BlockSpec((blk, LANES), tmap), pl.BlockSpec((blk, LANES), tmap)],
        out_specs=[pl.BlockSpec((blk, HEAD_PAD), kmap), pl.BlockSpec((blk, HEAD_PAD), kmap),
                   pl.BlockSpec((blk, V_HEAD), kmap)],
        scratch_shapes=[pltpu.VMEM((n, HEAD_PAD, blk), F32), pltpu.VMEM((blk, HEAD_PAD), F32),
                        pltpu.VMEM((blk, V_HEAD), F32)])
    return pl.pallas_call(body, grid_spec=grid_spec,
                          out_shape=[jax.ShapeDtypeStruct((T, N_HEADS * HEAD_PAD), BF16),
                                     jax.ShapeDtypeStruct((T, N_HEADS * HEAD_PAD), BF16),
                                     jax.ShapeDtypeStruct((T, N_HEADS * V_HEAD), BF16)],
                          name=name, compiler_params=_cparams("arbitrary", "arbitrary"))(
                              qtab, ktab, q, k, v, lse, delta, do, cos_t, sin_t)


MESH_ID = pl.DeviceIdType.MESH
ANY_SPEC = pl.BlockSpec(memory_space=pl.ANY)


def _other_chips(x, y):
    out = []
    for dx, dy in ((1, 0), (0, 1), (1, 1)):
        px = x ^ dx if dx else x
        py = y ^ dy if dy else y
        out.append((px, py, 2 * px + py))
    return out


def _gather_weights(flat):
    rh = flat.shape[0] // 2

    def body(src2, out, send_sems, recv_sems):
        x, y, c = lax.axis_index("x"), lax.axis_index("y"), lax.axis_index("c")
        me = 2 * x + y
        sib = (x, y, 1 - c)
        chips = _other_chips(x, y)
        sends = []
        for j, (px, py, pk) in enumerate(chips):
            cp = pltpu.make_async_remote_copy(src_ref=src2.at[c], dst_ref=out.at[me, c], send_sem=send_sems.at[j],
                                              recv_sem=recv_sems.at[j], device_id=(px, py, c), device_id_type=MESH_ID)
            cp.start()
            sends.append(cp)
        for j, (px, py, pk) in enumerate(chips):
            land = out.at[pk, c]
            pltpu.make_async_remote_copy(src_ref=land, dst_ref=land, send_sem=send_sems.at[j], recv_sem=recv_sems.at[j],
                                         device_id=(px, py, c), device_id_type=MESH_ID).wait_recv()
            fw = pltpu.make_async_remote_copy(src_ref=land, dst_ref=land, send_sem=send_sems.at[3 + j],
                                              recv_sem=recv_sems.at[3 + j], device_id=sib, device_id_type=MESH_ID)
            fw.start()
            sends.append(fw)
        for j, (px, py, pk) in enumerate(chips):
            land = out.at[pk, 1 - c]
            pltpu.make_async_remote_copy(src_ref=land, dst_ref=land, send_sem=send_sems.at[3 + j],
                                         recv_sem=recv_sems.at[3 + j], device_id=sib, device_id_type=MESH_ID).wait_recv()
        for cp in sends:
            cp.wait_send()

    return pl.pallas_call(body, out_shape=jax.ShapeDtypeStruct((N_CHIPS, 2, rh, PACK_LANES), flat.dtype),
                          in_specs=[ANY_SPEC], out_specs=ANY_SPEC,
                          scratch_shapes=[pltpu.SemaphoreType.DMA((6,)), pltpu.SemaphoreType.DMA((6,))],
                          name="gather_weights")(flat.reshape(2, rh, PACK_LANES))


def _swap_halves(gs):
    rh = gs[0].shape[1]

    def body(*refs):
        srcs, out, send_sems, recv_sems = refs[:N_CHIPS], refs[N_CHIPS], refs[N_CHIPS + 1], refs[N_CHIPS + 2]
        x, y, c = lax.axis_index("x"), lax.axis_index("y"), lax.axis_index("c")
        copies = [pltpu.make_async_remote_copy(src_ref=srcs[k].at[1 - c], dst_ref=out.at[k], send_sem=send_sems.at[k],
                                               recv_sem=recv_sems.at[k], device_id=(x, y, 1 - c), device_id_type=MESH_ID)
                  for k in range(N_CHIPS)]
        for cp in copies:
            cp.start()
        for cp in copies:
            cp.wait()

    return pl.pallas_call(body, out_shape=jax.ShapeDtypeStruct((N_CHIPS, rh, PACK_LANES), gs[0].dtype),
                          in_specs=[ANY_SPEC] * N_CHIPS, out_specs=ANY_SPEC,
                          scratch_shapes=[pltpu.SemaphoreType.DMA((N_CHIPS,)), pltpu.SemaphoreType.DMA((N_CHIPS,))],
                          name="grad_swap_halves")(*gs)


def _add_halves(gs, got, core, *, tr):
    rh = gs[0].shape[1]

    def body(*refs):
        g_refs, got_ref, o_ref = refs[1:1 + N_CHIPS], refs[1 + N_CHIPS], refs[2 + N_CHIPS]
        for k in range(N_CHIPS):
            o_ref[k] = (g_refs[k][0] + got_ref[k]).astype(o_ref.dtype)

    grid_spec = pltpu.PrefetchScalarGridSpec(
        num_scalar_prefetch=1, grid=(rh // tr,),
        in_specs=[pl.BlockSpec((1, tr, PACK_LANES), lambda i, c_ref: (c_ref[0], i, 0))] * N_CHIPS
        + [pl.BlockSpec((N_CHIPS, tr, PACK_LANES), lambda i, c_ref: (0, i, 0))],
        out_specs=pl.BlockSpec((N_CHIPS, tr, PACK_LANES), lambda i, c_ref: (0, i, 0)))
    return pl.pallas_call(body, grid_spec=grid_spec,
                          out_shape=jax.ShapeDtypeStruct((N_CHIPS, rh, PACK_LANES), BF16), name="grad_add_halves",
                          compiler_params=_cparams("parallel"))(core.reshape(1), *gs, got)


def _scatter_partials(part):
    rh = part.shape[1]

    def body(src, out, send_sems, recv_sems, local_sem):
        x, y, c = lax.axis_index("x"), lax.axis_index("y"), lax.axis_index("c")
        me = 2 * x + y
        own = pltpu.make_async_copy(src.at[me], out.at[me], local_sem)
        own.start()
        chips = _other_chips(x, y)
        sends = []
        for j, (px, py, pk) in enumerate(chips):
            cp = pltpu.make_async_remote_copy(src_ref=src.at[pk], dst_ref=out.at[me], send_sem=send_sems.at[j],
                                              recv_sem=recv_sems.at[j], device_id=(px, py, c), device_id_type=MESH_ID)
            cp.start()
            sends.append(cp)
        for j, (px, py, pk) in enumerate(chips):
            land = out.at[pk]
            pltpu.make_async_remote_copy(src_ref=land, dst_ref=land, send_sem=send_sems.at[j], recv_sem=recv_sems.at[j],
                                         device_id=(px, py, c), device_id_type=MESH_ID).wait_recv()
        for cp in sends:
            cp.wait_send()
        own.wait()

    return pl.pallas_call(body, out_shape=jax.ShapeDtypeStruct((N_CHIPS, rh, PACK_LANES), part.dtype),
                          in_specs=[ANY_SPEC], out_specs=ANY_SPEC,
                          scratch_shapes=[pltpu.SemaphoreType.DMA((3,)), pltpu.SemaphoreType.DMA((3,)),
                                          pltpu.SemaphoreType.DMA(())],
                          name="grad_scatter_partials")(part)


def _sum_chips(q, *, tr):
    rh = q.shape[1]

    def body(q_ref, o_ref):
        parts = [q_ref[k].astype(F32) for k in range(N_CHIPS)]
        o_ref[...] = ((parts[0] + parts[1]) + parts[2]) + parts[3]

    return pl.pallas_call(body, grid=(rh // tr,),
                          in_specs=[pl.BlockSpec((N_CHIPS, tr, PACK_LANES), lambda i: (0, i, 0))],
                          out_specs=pl.BlockSpec((tr, PACK_LANES), lambda i: (i, 0)),
                          out_shape=jax.ShapeDtypeStruct((rh, PACK_LANES), F32), name="grad_sum_chips",
                          compiler_params=_cparams("parallel"))(q)


def _send_half(half):
    def body(src, out, send_sem, recv_sem):
        x, y, c = lax.axis_index("x"), lax.axis_index("y"), lax.axis_index("c")
        cp = pltpu.make_async_remote_copy(src_ref=src, dst_ref=out, send_sem=send_sem, recv_sem=recv_sem,
                                          device_id=(x, y, 1 - c), device_id_type=MESH_ID)
        cp.start()
        cp.wait()

    return pl.pallas_call(body, out_shape=jax.ShapeDtypeStruct(half.shape, half.dtype),
                          in_specs=[ANY_SPEC], out_specs=ANY_SPEC,
                          scratch_shapes=[pltpu.SemaphoreType.DMA(()), pltpu.SemaphoreType.DMA(())],
                          name="grad_send_half")(half)


def _both_halves(own, got, core, *, tr):
    rh = own.shape[0]

    def body(c_ref, own_ref, got_ref, o_ref):
        mine = pl.program_id(0) == c_ref[0]
        o_ref[0] = jnp.where(mine, own_ref[...], got_ref[...])

    grid_spec = pltpu.PrefetchScalarGridSpec(
        num_scalar_prefetch=1, grid=(2, rh // tr),
        in_specs=[pl.BlockSpec((tr, PACK_LANES), lambda h, i, c_ref: (i, 0)),
                  pl.BlockSpec((tr, PACK_LANES), lambda h, i, c_ref: (i, 0))],
        out_specs=pl.BlockSpec((1, tr, PACK_LANES), lambda h, i, c_ref: (h, i, 0)))
    return pl.pallas_call(body, grid_spec=grid_spec, out_shape=jax.ShapeDtypeStruct((2, rh, PACK_LANES), own.dtype),
                          name="grad_both_halves", compiler_params=_cparams("parallel", "parallel"))(core.reshape(1), own, got)


def _allreduce_small(v):
    rows = v.shape[0]

    def body(v_ref, o_ref, buf, send_sems, recv_sems):
        x, y, c = lax.axis_index("x"), lax.axis_index("y"), lax.axis_index("c")
        me = 4 * x + 2 * y + c
        buf[me] = v_ref[...]
        sends = []
        for j in range(1, N_DEV):
            px, py, pc = x ^ ((j >> 2) & 1), y ^ ((j >> 1) & 1), c ^ (j & 1)
            cp = pltpu.make_async_remote_copy(src_ref=v_ref, dst_ref=buf.at[me], send_sem=send_sems.at[j - 1],
                                              recv_sem=recv_sems.at[j - 1], device_id=(px, py, pc), device_id_type=MESH_ID)
            cp.start()
            sends.append(cp)
        for j in range(1, N_DEV):
            px, py, pc = x ^ ((j >> 2) & 1), y ^ ((j >> 1) & 1), c ^ (j & 1)
            land = buf.at[4 * px + 2 * py + pc]
            pltpu.make_async_remote_copy(src_ref=land, dst_ref=land, send_sem=send_sems.at[j - 1],
                                         recv_sem=recv_sems.at[j - 1], device_id=(px, py, pc),
                                         device_id_type=MESH_ID).wait_recv()
        for cp in sends:
            cp.wait_send()
        acc = buf[0]
        for d in range(1, N_DEV):
            acc = acc + buf[d]
        o_ref[...] = acc

    vm = pl.BlockSpec(memory_space=pltpu.VMEM)
    return pl.pallas_call(body, out_shape=jax.ShapeDtypeStruct((rows, LANES), F32), in_specs=[vm], out_specs=vm,
                          scratch_shapes=[pltpu.VMEM((N_DEV, rows, LANES), F32), pltpu.SemaphoreType.DMA((N_DEV - 1,)),
                                          pltpu.SemaphoreType.DMA((N_DEV - 1,))],
                          name="allreduce_small")(v)


def _adamw(w, g, m, v, *, name):
    shape = w.shape
    cols = shape[-1] if w.ndim > 1 else shape[0]
    rows = w.size // cols
    w2, g2, m2, v2 = (t.reshape(rows, cols) for t in (w, g, m, v))
    tr = rows if rows <= 256 else _tile(rows, 256, 8)

    def fn(i, wv, gv, mv, vv):
        mn = ADAM_B1 * mv + (1.0 - ADAM_B1) * gv
        vn = ADAM_B2 * vv + (1.0 - ADAM_B2) * (gv * gv)
        m_hat = mn / (1.0 - ADAM_B1 ** ADAM_STEP)
        v_hat = vn / (1.0 - ADAM_B2 ** ADAM_STEP)
        delta = -ADAM_LR * (m_hat / (jnp.sqrt(v_hat) + ADAM_EPS) + ADAM_WD * wv)
        return delta, mn, vn

    ins = [_rspec(t, tr) for t in (w2, g2, m2, v2)]
    d, mn, vn = _row_call(fn, rows, tr, ins, [(cols, F32)] * 3, name=name)
    return d.reshape(shape), mn.reshape(shape), vn.reshape(shape)


def _rope_cols(w):
    z = jnp.zeros(w.shape[:-1] + (32,), w.dtype)
    return jnp.concatenate([w[..., :32], z, w[..., 32:], z], axis=-1)


def _rope_cols_inv(w):
    return jnp.concatenate([w[..., :32], w[..., 64:96]], axis=-1)


def _layer_layouts(W, i):
    w_in = W['w_in'][i]
    u, cq, ckv = w_in[:, :512], w_in[:, 512:1024], w_in[:, 1024:1280]
    kr, ga, gb = w_in[:, 1280:1344], w_in[:, 1344:2368], w_in[:, 2368:]
    L = {}
    L['w_in'] = jnp.concatenate([ga, gb, u, cq, ckv, _rope_cols(kr)], axis=1)
    wq = W['w_uq'][i]
    L['w_q'] = jnp.concatenate([wq[..., :QK_NOPE], _rope_cols(wq[..., QK_NOPE:])], axis=-1).reshape(Q_LORA, -1)
    wkv = W['w_ukv'][i]
    L['w_k'] = jnp.concatenate([wkv[..., :QK_NOPE], jnp.zeros_like(wkv[..., :LANES])], axis=-1).reshape(KV_LORA, -1)
    L['w_v'] = wkv[..., QK_NOPE:].reshape(KV_LORA, -1)
    L['w_gu'] = jnp.concatenate([W['w_gate'][i], W['w_up'][i]], axis=1)
    for n in ('w_a', 'w_b', 'w_o', 'w_down', 'w_ple_gate', 'w_ple'):
        L[n] = W[n][i]
    return L


def _layer_grads_to_reference_layout(G):
    d = G['w_in']
    ga, gb, u = d[:, ZC_GA:ZC_GB], d[:, ZC_GB:ZC_U], d[:, ZC_U:ZC_CQ]
    cq, ckv, kr = d[:, ZC_CQ:ZC_CKV], d[:, ZC_CKV:ZC_KR], _rope_cols_inv(d[:, ZC_KR:])
    out = {'w_in': jnp.concatenate([u, cq, ckv, kr, ga, gb], axis=1)}
    dq = G['w_q'].reshape(Q_LORA, N_HEADS, HEAD_PAD)
    out['w_uq'] = jnp.concatenate([dq[..., :QK_NOPE], _rope_cols_inv(dq[..., QK_NOPE:])], axis=-1)
    dk = G['w_k'].reshape(KV_LORA, N_HEADS, HEAD_PAD)[..., :QK_NOPE]
    dv = G['w_v'].reshape(KV_LORA, N_HEADS, V_HEAD)
    out['w_ukv'] = jnp.concatenate([dk, dv], axis=-1)
    out['w_gate'], out['w_up'] = G['w_gu'][:, :D_FF], G['w_gu'][:, D_FF:]
    for n in ('w_a', 'w_b', 'w_o', 'w_down', 'w_ple_gate', 'w_ple'):
        out[n] = G[n]
    return out


PACK_ROWS = 2048


def _pack_rows(parts, row_mult):
    flat = jnp.concatenate([p.reshape(-1) for p in parts])
    n = flat.shape[0]
    per = LANES * row_mult
    padded = -(-n // per) * per
    return jnp.pad(flat, (0, padded - n)).reshape(-1, LANES)


def _unpack_rows(flat2d, shapes):
    flat = flat2d.reshape(-1)
    out, off = [], 0
    for s in shapes:
        n = int(np.prod(s))
        out.append(flat[off:off + n].reshape(s))
        off += n
    return out


PACK_LANES = 256


def _lane_blocks(cols):
    return -(-cols // PACK_LANES)


def _pack_blocks(parts, row_mult):
    blocks = []
    for p in parts:
        p2 = p.reshape(-1, p.shape[-1])
        cols = p2.shape[1]
        nb = _lane_blocks(cols)
        if nb * PACK_LANES != cols:
            p2 = jnp.pad(p2, ((0, 0), (0, nb * PACK_LANES - cols)))
        blocks += [p2[:, j * PACK_LANES:(j + 1) * PACK_LANES] for j in range(nb)]
    buf = jnp.concatenate(blocks, axis=0)
    rows = buf.shape[0]
    padded = -(-rows // row_mult) * row_mult
    return buf if padded == rows else jnp.pad(buf, ((0, padded - rows), (0, 0)))


def _unpack_blocks(buf, shapes):
    out, off = [], 0
    for s in shapes:
        rows, cols = int(np.prod(s[:-1])), s[-1]
        nb = _lane_blocks(cols)
        piece = jnp.concatenate([buf[off + j * rows:off + (j + 1) * rows] for j in range(nb)], axis=1)
        out.append(piece[:, :cols].reshape(s))
        off += nb * rows
    return out


def _layer_fwd(i, x, p_i, L, norms, w_pool_bf, pool_scale, cos_t, sin_t, tr, blk):
    sv = {'x': x}
    z, sv['h'] = _mm_nn(x, L['w_in'], name=f"l{i}_in_proj", outs=[(Z_WIDTH, F32)], gain=norms['norm_mix'], emit_a=True,
                        tm=tr // 2)
    sv['z'] = z
    sv['pooled'], sv['mixed'], sv['pm'] = _pool_fwd(z, w_pool_bf, pool_scale, name=f"l{i}_pool", tr=tr)
    sv['ya'] = _mm_nn(sv['pm'], L['w_a'], name=f"l{i}_ya", outs=[(D_MODEL, BF16)], tm=tr)

    def heads(acc, rope_part):
        out = []
        for h in range(N_HEADS):
            out.append(acc[:, h * HEAD_PAD:h * HEAD_PAD + QK_NOPE])
            out.append(rope_part(acc[:, h * HEAD_PAD + QK_NOPE:(h + 1) * HEAD_PAD]))
        return jnp.concatenate(out, axis=1)

    def q_epi(acc, ct, st):
        return (heads(acc, lambda t: _rope(t, ct, st)),)

    def k_epi(acc, kr, ct, st):
        k_pe = _rope(kr, ct, st)
        return (heads(acc, lambda t: k_pe),)

    rope_rows = [(cos_t, LANES, 0), (sin_t, LANES, 0)]
    qk_width = N_HEADS * HEAD_PAD
    sv['q'], sv['cqn'] = _mm_nn(z, L['w_q'], name=f"l{i}_q_proj", outs=[(qk_width, BF16)], a_col=ZC_CQ,
                                gain=norms['q_norm'], emit_a=True, epi=q_epi, epi_rows=rope_rows, tm=tr)
    sv['k'], sv['ckvn'] = _mm_nn(z, L['w_k'], name=f"l{i}_k_proj", outs=[(qk_width, BF16)], a_col=ZC_CKV,
                                 gain=norms['kv_norm'], emit_a=True, epi=k_epi,
                                 epi_rows=[(z, LANES, ZC_KR // LANES)] + rope_rows, tm=tr)
    sv['v'] = _mm_nn(sv['ckvn'], L['w_v'], name=f"l{i}_v_proj", outs=[(N_HEADS * V_HEAD, BF16)], tm=tr)
    sv['o'], sv['lse'] = _flash_fwd(sv['q'], sv['k'], sv['v'], name=f"l{i}_attn", blk=blk)

    def merge_epi(yb, ga, gb, ya):
        return yb, _sigmoid(ga) * ya.astype(F32) + _sigmoid(gb) * yb

    sv['yb'], sv['merged'] = _mm_nn(sv['o'], L['w_b'], name=f"l{i}_yb_merge", outs=[(D_MODEL, BF16), (D_MODEL, BF16)],
                                    epi=merge_epi, epi_rows=[(z, D_MODEL, 0), (z, D_MODEL, 1), (sv['ya'], D_MODEL, 0)],
                                    tm=tr)

    def add_epi(acc, res):
        return (acc + res,)

    x1 = _mm_nn(sv['merged'], L['w_o'], name=f"l{i}_wo", outs=[(D_MODEL, F32)], epi=add_epi, epi_rows=[(x, D_MODEL, 0)],
                tm=tr)
    sv['x1'] = x1

    def swiglu_epi(acc):
        g, u = acc[:, :D_FF], acc[:, D_FF:]
        return acc, g * _sigmoid(g) * u

    sv['gu'], sv['act'], sv['h2'] = _mm_nn(x1, L['w_gu'], name=f"l{i}_gate_up", outs=[(2 * D_FF, BF16), (D_FF, BF16)],
                                           gain=norms['norm_ffn'], emit_a=True, epi=swiglu_epi, tm=tr // 2)
    x2 = _mm_nn(sv['act'], L['w_down'], name=f"l{i}_down", outs=[(D_MODEL, F32)], epi=add_epi,
                epi_rows=[(x1, D_MODEL, 0)], tm=tr)
    sv['x2'] = x2
    sv['logit'], sv['h3'] = _mm_nn(x2, L['w_ple_gate'], name=f"l{i}_ple_gate", outs=[(D_MODEL, F32)],
                                   gain=norms['norm_ple'], emit_a=True, tm=tr)

    def ple_epi(pe, xv, lg):
        return pe, xv + _sigmoid(lg) * pe

    sv['pe'], x3 = _mm_nn(p_i, L['w_ple'], name=f"l{i}_ple", outs=[(D_MODEL, F32), (D_MODEL, F32)], epi=ple_epi,
                          epi_rows=[(x2, D_MODEL, 0), (sv['logit'], D_MODEL, 0)], tm=tr)
    return x3, sv


def _layer_bwd(i, dx3, sv, p_i, L, norms, w_pool_bf, pool_scale, cos_t, sin_t, tr, blk):
    T = dx3.shape[0]
    G = {}
    z = sv['z']

    def ple_bwd(_, d, lg, pe):
        g = _sigmoid(lg)
        return d * pe * g * (1.0 - g), d * g

    dlogit, dpe = _row_call(ple_bwd, T, tr, [_rspec(dx3, tr), _rspec(sv['logit'], tr), _rspec(sv['pe'], tr)],
                            [(D_MODEL, BF16), (D_MODEL, BF16)], name=f"l{i}_ple_bwd")
    G['w_ple_gate'] = _mm_tn(sv['h3'], dlogit, name=f"l{i}_dw_ple_gate", tn=1024)
    G['w_ple'] = _mm_tn(p_i, dpe, name=f"l{i}_dw_ple", tn=1024)
    def gain_row(n):
        return norms[n].reshape(1, -1).astype(F32)

    dx2, dx2_bf, G['norm_ple'] = _mm_nt([(dlogit, L['w_ple_gate'])], name=f"l{i}_dh3_norm_bwd",
                                        outs=[(D_MODEL, F32), (D_MODEL, BF16)], epi=_rms_bwd_epi(True, True),
                                        epi_rows=[(sv['x2'], D_MODEL, 0), (dx3, D_MODEL, 0)], consts=[gain_row('norm_ple')],
                                        accs=[(1, D_MODEL)], tm=tr)

    def swiglu_bwd_epi(da, gu):
        g, u = gu[:, :D_FF].astype(F32), gu[:, D_FF:].astype(F32)
        sg = _sigmoid(g)
        return (jnp.concatenate([da * u * sg * (1.0 + g * (1.0 - sg)), da * g * sg], axis=1),)

    dgu = _mm_nt([(dx2_bf, L['w_down'])], name=f"l{i}_dact_swiglu_bwd", outs=[(2 * D_FF, BF16)], epi=swiglu_bwd_epi,
                 epi_rows=[(sv['gu'], 2 * D_FF, 0)], tm=tr // 2)
    G['w_down'] = _mm_tn(sv['act'], dx2_bf, name=f"l{i}_dw_down", tk=1408, tn=1024)
    G['w_gu'] = _mm_tn(sv['h2'], dgu, name=f"l{i}_dw_gate_up", tn=1408)
    dx1, dx1_bf, G['norm_ffn'] = _mm_nt([(dgu, L['w_gu'])], name=f"l{i}_dh2_norm_bwd",
                                        outs=[(D_MODEL, F32), (D_MODEL, BF16)], epi=_rms_bwd_epi(True, True),
                                        epi_rows=[(sv['x1'], D_MODEL, 0), (dx2, D_MODEL, 0)], consts=[gain_row('norm_ffn')],
                                        accs=[(1, D_MODEL)], tm=tr // 2)

    def merge_bwd_epi(dm, ga, gb, ya, yb):
        sa, sb = _sigmoid(ga), _sigmoid(gb)
        ya, yb = ya.astype(F32), yb.astype(F32)
        return dm * ya * sa * (1.0 - sa), dm * yb * sb * (1.0 - sb), dm * sa, dm * sb

    dga, dgb, dya, dyb = _mm_nt([(dx1_bf, L['w_o'])], name=f"l{i}_dmerged_bwd", outs=[(D_MODEL, BF16)] * 4,
                                epi=merge_bwd_epi, epi_rows=[(z, D_MODEL, 0), (z, D_MODEL, 1), (sv['ya'], D_MODEL, 0),
                                                             (sv['yb'], D_MODEL, 0)], tm=tr)
    G['w_o'] = _mm_tn(sv['merged'], dx1_bf, name=f"l{i}_dw_o", tn=1024)

    G['w_b'] = _mm_tn(sv['o'], dyb, name=f"l{i}_dw_b", tn=1024)
    do = _mm_nt([(dyb, L['w_b'])], name=f"l{i}_do", outs=[(D_MODEL, BF16)], tm=tr)
    trw = _wide_rows(tr)
    delta = _attn_delta(do, sv['o'], name=f"l{i}_attn_delta", tr=tr)
    dq, dk, dv = _flash_bwd(sv['q'], sv['k'], sv['v'], sv['lse'], delta, do, cos_t, sin_t, name=f"l{i}_attn_bwd", blk=blk)

    def dk_rope(_, d, ct, st):
        d = d.astype(F32)
        acc = d[:, QK_NOPE:HEAD_PAD]
        for h in range(1, N_HEADS):
            acc = acc + d[:, h * HEAD_PAD + QK_NOPE:(h + 1) * HEAD_PAD]
        return _rope_bwd(acc, ct, st)

    dkr = _row_call(dk_rope, T, tr, [_rspec(dk, tr), _rspec(cos_t, tr), _rspec(sin_t, tr)], [(LANES, BF16)],
                    name=f"l{i}_dk_rope")[0]
    G['w_q'] = _mm_tn(sv['cqn'], dq, name=f"l{i}_dw_q", tn=1024)
    G['w_k'] = _mm_tn(sv['ckvn'], dk, name=f"l{i}_dw_k", tn=1024)
    G['w_v'] = _mm_tn(sv['ckvn'], dv, name=f"l{i}_dw_v", tn=1024)
    dcq, G['q_norm'] = _mm_nt([(dq, L['w_q'])], name=f"l{i}_dcq", outs=[(Q_LORA, F32)], epi=_rms_bwd_epi(False, False),
                              epi_rows=[(z, Q_LORA, ZC_CQ // Q_LORA)], consts=[gain_row('q_norm')], accs=[(1, Q_LORA)], tm=tr)
    dckv, G['kv_norm'] = _mm_nt([(dk, L['w_k']), (dv, L['w_v'])], name=f"l{i}_dckv", outs=[(KV_LORA, F32)],
                                epi=_rms_bwd_epi(False, False), epi_rows=[(z, KV_LORA, ZC_CKV // KV_LORA)],
                                consts=[gain_row('kv_norm')], accs=[(1, KV_LORA)], tm=tr)

    G['w_a'] = _mm_tn(sv['pm'], dya, name=f"l{i}_dw_a", tn=1024)
    dpm = _mm_nt([(dya, L['w_a'])], name=f"l{i}_dpm", outs=[(POOL_WIDTH, F32)], tm=tr)
    dpool, dpool_cnt, G['pool_scale'], G['w_pool'] = _pool_bwd_mix(dpm, sv['mixed'], sv['pooled'], w_pool_bf, pool_scale,
                                                                   name=f"l{i}_pool_bwd_mix", tr=tr)
    du = _pool_bwd_window(dpool, dpool_cnt, name=f"l{i}_pool_bwd_window", tr=tr)

    def join(_, a, b, c, d, e, f):
        return jnp.concatenate([a, b, c, d.astype(BF16), e.astype(BF16), f], axis=1)

    dz = _row_call(join, T, trw, [_rspec(t, trw) for t in (dga, dgb, du, dcq, dckv, dkr)], [(Z_WIDTH, BF16)],
                   name=f"l{i}_dz_join")[0]
    G['w_in'] = _mm_tn(sv['h'], dz, name=f"l{i}_dw_in", tn=1152)
    dx, G['norm_mix'] = _mm_nt([(dz, L['w_in'])], name=f"l{i}_dh_norm_bwd", outs=[(D_MODEL, F32)],
                               epi=_rms_bwd_epi(True, False), epi_rows=[(sv['x'], D_MODEL, 0), (dx1, D_MODEL, 0)],
                               consts=[gain_row('norm_mix')], accs=[(1, D_MODEL)], tm=tr)
    return dx, G


def kernel(x, p, positions, norm_mix, w_in, w_pool, pool_scale, q_norm, kv_norm, w_uq, w_ukv, w_a, w_b, w_o, norm_ffn, w_gate, w_up, w_down, norm_ple, w_ple_gate, w_ple, final_norm, loss_target, m_norm_mix, m_w_in, m_w_pool, m_pool_scale, m_q_norm, m_kv_norm, m_w_uq, m_w_ukv, m_w_a, m_w_b, m_w_o, m_norm_ffn, m_w_gate, m_w_up, m_w_down, m_norm_ple, m_w_ple_gate, m_w_ple, m_final_norm, v_norm_mix, v_w_in, v_w_pool, v_pool_scale, v_q_norm, v_kv_norm, v_w_uq, v_w_ukv, v_w_a, v_w_b, v_w_o, v_norm_ffn, v_w_gate, v_w_up, v_w_down, v_norm_ple, v_w_ple_gate, v_w_ple, v_final_norm):
    given = dict(locals())
    weights = {n: given[n] for n in WEIGHTS}
    T = x.shape[1]
    tr = min(512, max(T // 2, 8))
    blk = min(ATTN_BLOCK, max(T // 4, 128))
    x0 = x.reshape(T, D_MODEL)
    target = loss_target.reshape(T, D_MODEL)

    names = list(SHARDED)
    shard_shapes = [weights[n].shape for n in names]
    flat = _pack_blocks([weights[n].astype(BF16) for n in names], row_mult=PACK_ROWS)
    R = flat.shape[0]
    chip = (2 * lax.axis_index("x") + lax.axis_index("y")).astype(jnp.int32)
    core = lax.axis_index("c").astype(jnp.int32)
    gathered = _gather_weights(flat).reshape(N_CHIPS, R, PACK_LANES)
    gathered = lax.dynamic_update_slice(gathered, flat.reshape(1, R, PACK_LANES), (chip, 0, 0))
    per_chip = [_unpack_blocks(gathered[k], shard_shapes) for k in range(N_CHIPS)]
    W = {n: jnp.concatenate([per_chip[k][j] for k in range(N_CHIPS)], axis=SHARDED[n]) for j, n in enumerate(names)}
    layouts = [_layer_layouts(W, i) for i in range(DEPTH)]
    w_pool_bf = w_pool.astype(BF16)

    inv_freq = 1.0 / (ROPE_THETA ** (jnp.arange(0, QK_ROPE, 2, dtype=F32) / QK_ROPE))
    zero32 = jnp.zeros((32,), F32)
    freq_row = jnp.concatenate([inv_freq, zero32, inv_freq, zero32]).reshape(1, LANES)
    cos_mask = jnp.concatenate([jnp.ones((32,), F32), zero32, jnp.ones((32,), F32), zero32]).reshape(1, LANES)
    sin_sign = jnp.concatenate([-jnp.ones((32,), F32), zero32, jnp.ones((32,), F32), zero32]).reshape(1, LANES)

    def rope_tables(_, pos, fr, cm, ss):
        ang = pos.astype(F32) * fr
        return jnp.cos(ang) * cm, jnp.sin(ang) * ss

    pos_col = positions.reshape(T, 1)
    cos_t, sin_t = _row_call(rope_tables, T, tr, [_rspec(pos_col, tr), _bspec(freq_row), _bspec(cos_mask), _bspec(sin_sign)],
                             [(LANES, F32), (LANES, F32)], name="rope_tables")

    xs = x0
    saved = []
    for i in range(DEPTH):
        norms = {n: weights[n][i] for n in ('norm_mix', 'q_norm', 'kv_norm', 'norm_ffn', 'norm_ple')}
        xs, sv = _layer_fwd(i, xs, p[i, 0], layouts[i], norms, w_pool_bf[i], pool_scale[i], cos_t, sin_t, tr, blk)
        saved.append((sv, norms))

    def head(_, xv, tv, gv):
        rstd = lax.rsqrt(jnp.mean(xv * xv, axis=-1, keepdims=True) + EPS)
        xhat = xv * rstd
        err = xhat * gv - tv
        loss = 0.5 * jnp.sum(jnp.mean(err * err, axis=-1, keepdims=True), axis=0, keepdims=True)
        dy = err * (1.0 / D_MODEL)
        dg = jnp.sum(dy * xhat, axis=0, keepdims=True)
        dxh = dy * gv
        dx = rstd * (dxh - xhat * jnp.mean(dxh * xhat, axis=-1, keepdims=True))
        return dx, jnp.broadcast_to(loss, (1, LANES)), dg

    dx, loss_part, g_final = _row_call(head, T, tr, [_rspec(xs, tr), _rspec(target, tr), _bspec(final_norm.reshape(1, D_MODEL))],
                                       [(D_MODEL, F32)], [(1, LANES), (1, D_MODEL)], name="loss_head")
    loss = lax.psum(loss_part[0, 0], ("x", "y", "c"))

    layer_grads = [None] * DEPTH
    for i in reversed(range(DEPTH)):
        sv, norms = saved[i]
        dx, layer_grads[i] = _layer_bwd(i, dx, sv, p[i, 0], layouts[i], norms, w_pool_bf[i], pool_scale[i], cos_t, sin_t, tr,
                                        blk)
    grad_x = dx.reshape(x.shape)

    ref_layout = [_layer_grads_to_reference_layout(g) for g in layer_grads]
    local = {n: jnp.stack([ref_layout[i][n] for i in range(DEPTH)]) for n in names}
    for n in ('norm_mix', 'q_norm', 'kv_norm', 'norm_ffn', 'norm_ple', 'pool_scale'):
        local[n] = jnp.stack([layer_grads[i][n].reshape(-1) for i in range(DEPTH)])
    local['w_pool'] = jnp.stack([layer_grads[i]['w_pool'] for i in range(DEPTH)])
    local['final_norm'] = g_final.reshape(-1)

    send = []
    for k in range(N_CHIPS):
        parts = []
        for n in names:
            ax = SHARDED[n]
            size = local[n].shape[ax] // N_CHIPS
            parts.append(lax.slice_in_dim(local[n], k * size, (k + 1) * size, axis=ax))
        send.append(_pack_blocks(parts, row_mult=PACK_ROWS))
    rh = R // 2
    trr = PACK_ROWS // 2
    send = [g.reshape(2, rh, PACK_LANES) for g in send]
    part = _add_halves(send, _swap_halves(send), core, tr=trr)
    reduced_half = _sum_chips(_scatter_partials(part), tr=trr)
    reduced = _both_halves(reduced_half, _send_half(reduced_half), core, tr=trr).reshape(R, PACK_LANES)
    grads = dict(zip(names, _unpack_blocks(reduced, shard_shapes)))

    rep_shapes = [weights[n].shape for n in REPLICATED]
    rep = _allreduce_small(_pack_rows([local[n] for n in REPLICATED], row_mult=8))
    grads.update(zip(REPLICATED, _unpack_rows(rep, rep_shapes)))

    deltas, new_m, new_v = {}, {}, {}
    for n in WEIGHTS:
        deltas[n], new_m[n], new_v[n] = _adamw(weights[n], grads[n], given['m_' + n], given['v_' + n], name=f"adamw_{n}")
    return (loss, grad_x, *[grads[n] for n in WEIGHTS], *[deltas[n] for n in WEIGHTS], *[new_m[n] for n in WEIGHTS],
            *[new_v[n] for n in WEIGHTS])
```

```python
import functools

import numpy as np
import jax
import jax.numpy as jnp
from jax import lax
from jax.experimental import pallas as pl
from jax.experimental.pallas import tpu as pltpu

F32 = jnp.float32
BF16 = jnp.bfloat16

D_MODEL = 1024
DEPTH = 2
PLE_DIM = 256
POOL_WINDOWS = (2, 4, 8, 16)
POOL_GROUP = 128
POOL_WIDTH = 512
N_HEADS = 8
Q_LORA = 512
KV_LORA = 256
QK_NOPE = 128
QK_ROPE = 64
QK_HEAD = 192
V_HEAD = 128
D_FF = 2816
ROPE_THETA = 10000.0
EPS = 1e-6
ATTN_SCALE = QK_HEAD ** -0.5

ADAM_LR = 0.001
ADAM_B1 = 0.9
ADAM_B2 = 0.999
ADAM_EPS = 1e-08
ADAM_WD = 0.01
ADAM_STEP = 10

LANES = 128
HALO = 16
HEAD_PAD = 256
V7X_VMEM_BYTES = 64 * 1024 * 1024
VMEM_LIMIT = (V7X_VMEM_BYTES * 3) // 4
N_CHIPS = 4
N_DEV = 8
NEG_INF = -1e30

ZC_GA, ZC_GB, ZC_U, ZC_CQ, ZC_CKV, ZC_KR = 0, 1024, 2048, 2560, 3072, 3328
Z_WIDTH = 3456

WEIGHTS = ['norm_mix', 'w_in', 'w_pool', 'pool_scale', 'q_norm', 'kv_norm', 'w_uq', 'w_ukv', 'w_a', 'w_b', 'w_o',
           'norm_ffn', 'w_gate', 'w_up', 'w_down', 'norm_ple', 'w_ple_gate', 'w_ple', 'final_norm']
SHARDED = {'w_in': 2, 'w_uq': 1, 'w_ukv': 1, 'w_a': 2, 'w_b': 1, 'w_o': 1, 'w_gate': 2, 'w_up': 2, 'w_down': 1,
           'w_ple_gate': 1, 'w_ple': 2}
REPLICATED = [n for n in WEIGHTS if n not in SHARDED]


def _tile(n, target, mult=LANES):
    if n <= target:
        return n
    best = None
    for t in range(mult, target + 1, mult):
        if n % t == 0:
            best = t
    assert best is not None, (n, target)
    return best


def _cparams(*sem):
    return pltpu.CompilerParams(dimension_semantics=sem, vmem_limit_bytes=VMEM_LIMIT)


def _rope(t, cos_t, sin_t):
    return t * cos_t + pltpu.roll(t, 64, 1) * sin_t


def _rope_bwd(d, cos_t, sin_t):
    return d * cos_t + pltpu.roll(d * sin_t, 64, 1)


def _sigmoid(v):
    return 1.0 / (1.0 + jnp.exp(-v))


def _mm_nn(a, b, *, name, outs, a_col=0, gain=None, emit_a=False, epi=None, epi_rows=(), tm=512):
    M = a.shape[0]
    K, N = b.shape
    tm = min(tm, M)
    assert a_col % K == 0 and M % tm == 0
    a_blk = a_col // K
    n_rows, n_out = len(epi_rows), len(outs)

    def body(*refs):
        a_ref, b_ref = refs[0], refs[1]
        pos = 2
        g_ref = None
        if gain is not None:
            g_ref = refs[pos]
            pos += 1
        row_refs = refs[pos:pos + n_rows]
        out_refs = refs[pos + n_rows:pos + n_rows + n_out]
        lhs = a_ref[...]
        if gain is not None:
            av = lhs.astype(F32)
            lhs = av * lax.rsqrt(jnp.mean(av * av, axis=-1, keepdims=True) + EPS) * g_ref[...]
        lhs = lhs.astype(BF16)
        if emit_a:
            refs[pos + n_rows + n_out][...] = lhs
        acc = jnp.dot(lhs, b_ref[...], preferred_element_type=F32)
        vals = (acc,) if epi is None else epi(acc, *[r[...] for r in row_refs])
        for r, v in zip(out_refs, vals):
            r[...] = v.astype(r.dtype)

    in_specs = [pl.BlockSpec((tm, K), lambda i: (i, a_blk)), pl.BlockSpec((K, N), lambda i: (0, 0))]
    args = [a, b]
    if gain is not None:
        in_specs.append(pl.BlockSpec((1, K), lambda i: (0, 0)))
        args.append(gain.reshape(1, K).astype(F32))
    for arr, w, blk in epi_rows:
        in_specs.append(pl.BlockSpec((tm, w), lambda i, blk=blk: (i, blk)))
        args.append(arr)
    out_shape = [jax.ShapeDtypeStruct((M, w), dt) for w, dt in outs]
    out_specs = [pl.BlockSpec((tm, w), lambda i: (i, 0)) for w, dt in outs]
    if emit_a:
        out_shape.append(jax.ShapeDtypeStruct((M, K), BF16))
        out_specs.append(pl.BlockSpec((tm, K), lambda i: (i, 0)))
    res = pl.pallas_call(body, grid=(M // tm,), in_specs=in_specs, out_specs=out_specs, out_shape=out_shape,
                         name=name, compiler_params=_cparams("parallel"))(*args)
    return res[0] if len(res) == 1 else res


def _mm_nt(pairs, *, name, outs, epi=None, epi_rows=(), consts=(), accs=(), tm=512, into=None, wide0=None):
    M = pairs[0][0].shape[0]
    N = pairs[0][1].shape[0]
    tm = min(tm, M)
    n_p, n_in, n_out, n_acc = len(pairs), len(epi_rows) + len(consts), len(outs), len(accs)

    def body(*refs):
        acc = None
        for k in range(n_p):
            av = refs[2 * k][...].astype(BF16)
            part = lax.dot_general(av, refs[2 * k + 1][...], NT_DIMS, preferred_element_type=F32)
            acc = part if acc is None else acc + part
        pos = 2 * n_p
        extra = [r[...] for r in refs[pos:pos + n_in]]
        pos += n_in + (1 if into is not None else 0)
        vals = (acc,) if epi is None else epi(acc, *extra)
        for r, v in zip(refs[pos:pos + n_out], vals[:n_out]):
            r[...] = v.astype(r.dtype)
        if n_acc:
            acc_refs = refs[pos + n_out:pos + n_out + n_acc]

            @pl.when(pl.program_id(0) == 0)
            def _():
                for r in acc_refs:
                    r[...] = jnp.zeros_like(r)
            for r, v in zip(acc_refs, vals[n_out:]):
                r[...] += v

    in_specs, args = [], []
    for a, b in pairs:
        assert a.shape[1] == b.shape[1] and b.shape[0] == N and a.shape[0] == M
        in_specs.append(pl.BlockSpec((tm, a.shape[1]), lambda i: (i, 0)))
        in_specs.append(pl.BlockSpec(b.shape, lambda i: (0, 0)))
        args += [a, b]
    for arr, w, blk in epi_rows:
        in_specs.append(pl.BlockSpec((tm, w), lambda i, blk=blk: (i, blk)))
        args.append(arr)
    for arr in consts:
        in_specs.append(pl.BlockSpec(arr.shape, lambda i, n=arr.ndim: (0,) * n))
        args.append(arr)
    out_shape = [jax.ShapeDtypeStruct((M, w), dt) for w, dt in outs]
    out_specs = [pl.BlockSpec((tm, w), lambda i: (i, 0)) for w, dt in outs]
    for s in accs:
        out_shape.append(jax.ShapeDtypeStruct(s, F32))
        out_specs.append(pl.BlockSpec(s, lambda i, n=len(s): (0,) * n))
    aliases = _into_column_block(into, tm, out_shape, out_specs, in_specs, args) if into is not None else {}
    if wide0 is not None:
        out_shape[0] = jax.ShapeDtypeStruct((M, wide0), outs[0][1])
    res = pl.pallas_call(body, grid=(M // tm,), in_specs=in_specs, out_specs=out_specs, out_shape=out_shape, name=name,
                         input_output_aliases=aliases,
                         compiler_params=_cparams("arbitrary" if n_acc else "parallel"))(*args)
    return res[0] if len(res) == 1 else res


def _rms_bwd_epi(with_res, emit_bf16):
    def epi(dh, xv, *rest):
        gv = rest[-1]
        xv = xv.astype(F32)
        rstd = lax.rsqrt(jnp.mean(xv * xv, axis=-1, keepdims=True) + EPS)
        xhat = xv * rstd
        dg = jnp.sum(dh * xhat, axis=0, keepdims=True)
        dxh = dh * gv
        dx = rstd * (dxh - xhat * jnp.mean(dxh * xhat, axis=-1, keepdims=True))
        if with_res:
            dx = dx + rest[0].astype(F32)
        return (dx, dx, dg) if emit_bf16 else (dx, dg)
    return epi


def _mm_tn(a, b, *, name, a_col=0, a_w=None, tk=1024, tn=1152, tm=512):
    M = a.shape[0]
    a_w = a.shape[1] if a_w is None else a_w
    N = b.shape[1]
    tm = min(tm, M)
    tk = _tile(a_w, tk)
    tn = _tile(N, tn)
    assert a_col % tk == 0 and M % tm == 0
    a_blk0 = a_col // tk

    def body(a_ref, b_ref, o_ref):
        @pl.when(pl.program_id(2) == 0)
        def _():
            o_ref[...] = jnp.zeros_like(o_ref)
        o_ref[...] += lax.dot_general(a_ref[...].astype(BF16), b_ref[...].astype(BF16), (((0,), (0,)), ((), ())),
                                      preferred_element_type=F32)

    return pl.pallas_call(body, grid=(a_w // tk, N // tn, M // tm),
                          in_specs=[pl.BlockSpec((tm, tk), lambda k, j, m: (m, k + a_blk0)),
                                    pl.BlockSpec((tm, tn), lambda k, j, m: (m, j))],
                          out_specs=pl.BlockSpec((tk, tn), lambda k, j, m: (k, j)),
                          out_shape=jax.ShapeDtypeStruct((a_w, N), F32), name=name,
                          compiler_params=_cparams("parallel", "parallel", "arbitrary"))(a, b)


def _into_column_block(into, tile, out_shape, out_specs, in_specs, args):
    buf, width, blk = into
    out_shape[0] = jax.ShapeDtypeStruct(buf.shape, buf.dtype)
    out_specs[0] = pl.BlockSpec((tile, width), lambda i: (i, blk))
    in_specs.append(pl.BlockSpec(memory_space=pl.ANY))
    args.append(buf)
    return {len(args) - 1: 0}


def _row_call(fn, rows, tr, ins, outs, accs=(), *, name, into=None):
    n_in, n_out, n_acc = len(ins), len(outs), len(accs)
    first_out = n_in + (1 if into is not None else 0)

    def body(*refs):
        i = pl.program_id(0)
        vals = fn(i, *[r[...] for r in refs[:n_in]])
        if not isinstance(vals, (tuple, list)):
            vals = (vals,)
        for r, v in zip(refs[first_out:first_out + n_out], vals[:n_out]):
            r[...] = v.astype(r.dtype)
        if n_acc:
            acc_refs = refs[first_out + n_out:]

            @pl.when(i == 0)
            def _():
                for r in acc_refs:
                    r[...] = jnp.zeros_like(r)
            for r, v in zip(acc_refs, vals[n_out:]):
                r[...] += v

    out_shape = [jax.ShapeDtypeStruct((rows, w), dt) for w, dt in outs]
    out_specs = [pl.BlockSpec((tr, w), lambda i: (i, 0)) for w, dt in outs]
    for s in accs:
        out_shape.append(jax.ShapeDtypeStruct(s, F32))
        out_specs.append(pl.BlockSpec(s, lambda i, n=len(s): (0,) * n))
    in_specs = [pl.BlockSpec(bs, im) for _, bs, im in ins]
    args = [a for a, _, _ in ins]
    aliases = _into_column_block(into, tr, out_shape, out_specs, in_specs, args) if into is not None else {}
    res = pl.pallas_call(body, grid=(rows // tr,), in_specs=in_specs, out_specs=out_specs, out_shape=out_shape, name=name,
                         input_output_aliases=aliases, compiler_params=_cparams("arbitrary"))(*args)
    return res


def _rspec(arr, tr, w=None, blk=0):
    w = arr.shape[1] if w is None else w
    return (arr, (tr, w), lambda i, blk=blk: (i, blk))


def _bspec(arr):
    return (arr, arr.shape, lambda i, n=arr.ndim: (0,) * n)


def _pool_counts(i, tr):
    t = (i * tr + lax.broadcasted_iota(jnp.int32, (tr, 1), 0) + 1).astype(F32)
    return [jnp.minimum(t, float(w)) for w in POOL_WINDOWS]


def _pool_fwd(z, w_pool_bf, pool_scale, *, name, tr):
    rows = z.shape[0]
    ublk = ZC_U // POOL_WIDTH
    hpt = tr // HALO

    def fn(i, u, uprev, wp, ps):
        uprev = jnp.where(i > 0, uprev, 0.0)
        ext = jnp.concatenate([uprev, u], axis=0)
        s2 = ext + pltpu.roll(ext, 1, 0)
        s4 = s2 + pltpu.roll(s2, 2, 0)
        s8 = s4 + pltpu.roll(s4, 4, 0)
        s16 = s8 + pltpu.roll(s8, 8, 0)
        cnts = _pool_counts(i, tr)
        pooled, mixed = [], []
        for g, sw in enumerate((s2, s4, s8, s16)):
            lanes = slice(g * POOL_GROUP, (g + 1) * POOL_GROUP)
            pg = sw[HALO:, lanes] / cnts[g] - u[:, lanes]
            pooled.append(pg)
            mixed.append(jnp.dot(pg.astype(BF16), wp[g], preferred_element_type=F32))
        pooled = jnp.concatenate(pooled, axis=1)
        mixed = jnp.concatenate(mixed, axis=1)
        return pooled, mixed, mixed * ps

    ins = [_rspec(z, tr, POOL_WIDTH, ublk),
           (z, (HALO, POOL_WIDTH), lambda i: (jnp.maximum(i * hpt - 1, 0), ublk)),
           _bspec(w_pool_bf), _bspec(pool_scale.reshape(1, POOL_WIDTH))]
    return _row_call(fn, rows, tr, ins, [(POOL_WIDTH, BF16), (POOL_WIDTH, F32), (POOL_WIDTH, BF16)], name=name)


def _pool_bwd_mix(dpm, mixed, pooled, w_pool_bf, pool_scale, *, name, tr):
    rows = dpm.shape[0]

    def fn(i, dv, mv, pv, wp, ps):
        dv = dv.astype(F32)
        dscale = jnp.sum(dv * mv, axis=0, keepdims=True)
        dmix = (dv * ps).astype(BF16)
        cnts = _pool_counts(i, tr)
        dpool, dwp = [], []
        for g in range(len(POOL_WINDOWS)):
            lanes = slice(g * POOL_GROUP, (g + 1) * POOL_GROUP)
            dg = lax.dot_general(dmix[:, lanes], wp[g], (((1,), (1,)), ((), ())), preferred_element_type=F32)
            dpool.append(dg)
            dwp.append(lax.dot_general(pv[:, lanes], dmix[:, lanes], (((0,), (0,)), ((), ())),
                                       preferred_element_type=F32)[None])
        dpool = jnp.concatenate(dpool, axis=1)
        dpool_cnt = jnp.concatenate([dpool[:, g * POOL_GROUP:(g + 1) * POOL_GROUP] / cnts[g]
                                     for g in range(len(POOL_WINDOWS))], axis=1)
        return dpool, dpool_cnt, dscale, jnp.concatenate(dwp, axis=0)

    ins = [_rspec(dpm, tr), _rspec(mixed, tr), _rspec(pooled, tr), _bspec(w_pool_bf),
           _bspec(pool_scale.reshape(1, POOL_WIDTH))]
    return _row_call(fn, rows, tr, ins, [(POOL_WIDTH, F32), (POOL_WIDTH, F32)],
                     [(1, POOL_WIDTH), (len(POOL_WINDOWS), POOL_GROUP, POOL_GROUP)], name=name)


def _pool_bwd_window(dpool, dpool_cnt, *, name, tr, into):
    rows = dpool.shape[0]
    hpt = tr // HALO
    n_halo = rows // HALO
    n_tiles = rows // tr

    def fn(i, dp, dc, dnext):
        dnext = jnp.where(i < n_tiles - 1, dnext, 0.0)
        ext = jnp.concatenate([dc, dnext], axis=0)
        n = tr + HALO
        s2 = ext + pltpu.roll(ext, n - 1, 0)
        s4 = s2 + pltpu.roll(s2, n - 2, 0)
        s8 = s4 + pltpu.roll(s4, n - 4, 0)
        s16 = s8 + pltpu.roll(s8, n - 8, 0)
        out = []
        for g, sw in enumerate((s2, s4, s8, s16)):
            lanes = slice(g * POOL_GROUP, (g + 1) * POOL_GROUP)
            out.append(sw[:tr, lanes] - dp[:, lanes])
        return jnp.concatenate(out, axis=1)

    ins = [_rspec(dpool, tr), _rspec(dpool_cnt, tr),
           (dpool_cnt, (HALO, POOL_WIDTH), lambda i: (jnp.minimum((i + 1) * hpt, n_halo - 1), 0))]
    return _row_call(fn, rows, tr, ins, [(POOL_WIDTH, BF16)], name=name, into=into)[0]


def _causal_pairs(n, k_major):
    if k_major:
        pairs = [(qi, ki) for ki in range(n) for qi in range(ki, n)]
    else:
        pairs = [(qi, ki) for qi in range(n) for ki in range(qi + 1)]
    return (jnp.asarray(np.array([p[0] for p in pairs], np.int32)),
            jnp.asarray(np.array([p[1] for p in pairs], np.int32)), len(pairs))


SUBLANES = 8
NT_DIMS = (((1,), (1,)), ((), ()))
TN_DIMS = (((0,), (0,)), ((), ()))


ATTN_BLOCK = 1024
QUERY_CHUNK = 256
LOG2_E = 1.4426950408889634
EXP2_SCALE = ATTN_SCALE * LOG2_E


def _scores_t(q_c, k, c, qc, diag):
    s = lax.dot_general(k, q_c, NT_DIMS, preferred_element_type=F32)
    if diag:
        key = lax.broadcasted_iota(jnp.int32, s.shape, 0)
        qry = lax.broadcasted_iota(jnp.int32, s.shape, 1) + c * qc
        s = jnp.where(key <= qry, s, NEG_INF)
    return s


def _flash_fwd(q, k, v, *, name, blk):
    T = q.shape[0]
    n = T // blk
    qc = min(QUERY_CHUNK, blk)
    qtab, ktab, n_pairs = _causal_pairs(n, k_major=False)

    def body(qt, kt, q_ref, k_ref, v_ref, o_ref, lse_ref, m_s, l_s, acc_s):
        p = pl.program_id(1)
        qi, ki = qt[p], kt[p]

        @pl.when(ki == 0)
        def _():
            m_s[...] = jnp.full_like(m_s, NEG_INF)
            l_s[...] = jnp.zeros_like(l_s)
            acc_s[...] = jnp.zeros_like(acc_s)

        def step(diag):
            kv, vv = k_ref[...], v_ref[...]
            chunks = [slice(c * qc, (c + 1) * qc) for c in range(blk // qc)]
            scores = [_scores_t(q_ref[rows, :], kv, c, qc, diag) for c, rows in enumerate(chunks)]
            probs, alphas = [], []
            for rows, s_t in zip(chunks, scores):
                m_prev = m_s[:, rows]
                m_new = jnp.maximum(m_prev, jnp.max(s_t, axis=0, keepdims=True))
                p_t = jnp.exp2(s_t - m_new)
                alpha = jnp.exp2(m_prev - m_new)
                l_s[:, rows] = alpha * l_s[:, rows] + jnp.sum(p_t, axis=0, keepdims=True)
                m_s[:, rows] = m_new
                probs.append(p_t.astype(BF16))
                alphas.append(alpha)
            for rows, p_t, alpha in zip(chunks, probs, alphas):
                acc_s[:, rows] = alpha * acc_s[:, rows] + lax.dot_general(vv, p_t, TN_DIMS, preferred_element_type=F32)

        @pl.when(ki != qi)
        def _():
            step(False)

        @pl.when(ki == qi)
        def _():
            step(True)
            o_ref[...] = (acc_s[...] / l_s[...]).T.astype(o_ref.dtype)
            lse2 = m_s[...] + jnp.log2(l_s[...])
            lse_ref[...] = jnp.broadcast_to(lse2, lse_ref.shape)

    grid_spec = pltpu.PrefetchScalarGridSpec(
        num_scalar_prefetch=2, grid=(N_HEADS, n_pairs),
        in_specs=[pl.BlockSpec((blk, HEAD_PAD), lambda h, p, qt, kt: (qt[p], h)),
                  pl.BlockSpec((blk, HEAD_PAD), lambda h, p, qt, kt: (kt[p], h)),
                  pl.BlockSpec((blk, V_HEAD), lambda h, p, qt, kt: (kt[p], h))],
        out_specs=[pl.BlockSpec((blk, V_HEAD), lambda h, p, qt, kt: (qt[p], h)),
                   pl.BlockSpec((SUBLANES, blk), lambda h, p, qt, kt: (h, qt[p]))],
        scratch_shapes=[pltpu.VMEM((1, blk), F32), pltpu.VMEM((1, blk), F32), pltpu.VMEM((V_HEAD, blk), F32)])
    return pl.pallas_call(body, grid_spec=grid_spec,
                          out_shape=[jax.ShapeDtypeStruct((T, N_HEADS * V_HEAD), BF16),
                                     jax.ShapeDtypeStruct((N_HEADS * SUBLANES, T), F32)],
                          name=name, compiler_params=_cparams("parallel", "arbitrary"))(qtab, ktab, q, k, v)


def _attn_delta(do, o, *, name, tr):
    T = do.shape[0]

    def body(do_ref, o_ref, d_ref):
        prod = do_ref[...].astype(F32) * o_ref[...].astype(F32)
        lane_head = lax.broadcasted_iota(jnp.int32, (tr, LANES), 1) // SUBLANES
        mat = jnp.zeros((tr, LANES), F32)
        for h in range(N_HEADS):
            d_h = jnp.sum(prod[:, h * V_HEAD:(h + 1) * V_HEAD], axis=1, keepdims=True)
            mat = jnp.where(lane_head == h, d_h, mat)
        d_ref[...] = mat.T[:N_HEADS * SUBLANES, :]

    return pl.pallas_call(body, grid=(T // tr,),
                          in_specs=[pl.BlockSpec((tr, N_HEADS * V_HEAD), lambda i: (i, 0)),
                                    pl.BlockSpec((tr, N_HEADS * V_HEAD), lambda i: (i, 0))],
                          out_specs=pl.BlockSpec((N_HEADS * SUBLANES, tr), lambda i: (0, i)),
                          out_shape=jax.ShapeDtypeStruct((N_HEADS * SUBLANES, T), F32), name=name,
                          compiler_params=_cparams("parallel"))(do, o)


def _flash_bwd(q, k, v, lse, delta, do, cos_t, sin_t, *, name, blk):
    T = q.shape[0]
    n = T // blk
    qc = min(QUERY_CHUNK, blk)
    qtab, ktab, n_pairs = _causal_pairs(n, k_major=True)

    def body(qt, kt, q_ref, k_ref, v_ref, lse_ref, delta_ref, do_ref, cos_ref, sin_ref, dq_ref, dk_ref, dv_ref,
             dq_s, dk_s, dv_s):
        p = pl.program_id(1)
        qi, ki = qt[p], kt[p]
        first = qi == ki

        @pl.when(p == 0)
        def _():
            dq_s[...] = jnp.zeros_like(dq_s)

        @pl.when(first)
        def _():
            dk_s[...] = jnp.zeros_like(dk_s)
            dv_s[...] = jnp.zeros_like(dv_s)

        def step(diag):
            kv, vv = k_ref[...], v_ref[...]
            chunks = [slice(c * qc, (c + 1) * qc) for c in range(blk // qc)]
            qs = [q_ref[rows, :] for rows in chunks]
            dos = [do_ref[rows, :] for rows in chunks]
            scores = [_scores_t(q_c, kv, c, qc, diag) for c, q_c in enumerate(qs)]
            dps = [lax.dot_general(vv, do_c, NT_DIMS, preferred_element_type=F32) for do_c in dos]
            probs, dss = [], []
            for rows, s_t, dp_t in zip(chunks, scores, dps):
                p_t = jnp.exp2(s_t - lse_ref[0:1, rows])
                dss.append((p_t * (dp_t - delta_ref[0:1, rows])).astype(BF16))
                probs.append(p_t.astype(BF16))
            dv_acc = dk_acc = None
            for p_t, ds_t, q_c, do_c in zip(probs, dss, qs, dos):
                dv_c = jnp.dot(p_t, do_c, preferred_element_type=F32)
                dk_c = jnp.dot(ds_t, q_c, preferred_element_type=F32)
                dv_acc = dv_c if dv_acc is None else dv_acc + dv_c
                dk_acc = dk_c if dk_acc is None else dk_acc + dk_c
            for rows, ds_t in zip(chunks, dss):
                dq_s[qi, :, rows] += lax.dot_general(kv, ds_t, TN_DIMS, preferred_element_type=F32)
            dv_s[...] += dv_acc
            dk_s[...] += dk_acc

        @pl.when(jnp.logical_not(first))
        def _():
            step(False)

        @pl.when(first)
        def _():
            step(True)
            dq_t = (dq_s[qi] * ATTN_SCALE).T
            dq_ref[:, :QK_NOPE] = dq_t[:, :QK_NOPE].astype(dq_ref.dtype)
            dq_ref[:, QK_NOPE:] = _rope_bwd(dq_t[:, QK_NOPE:], cos_ref[...], sin_ref[...]).astype(dq_ref.dtype)

        @pl.when(qi == n - 1)
        def _():
            dk_ref[...] = (dk_s[...] * (1.0 / LOG2_E)).astype(dk_ref.dtype)
            dv_ref[...] = dv_s[...].astype(dv_ref.dtype)

    qmap = lambda h, p, qt, kt: (qt[p], h)
    kmap = lambda h, p, qt, kt: (kt[p], h)
    smap = lambda h, p, qt, kt: (h, qt[p])
    tmap = lambda h, p, qt, kt: (kt[p], 0)
    grid_spec = pltpu.PrefetchScalarGridSpec(
        num_scalar_prefetch=2, grid=(N_HEADS, n_pairs),
        in_specs=[pl.BlockSpec((blk, HEAD_PAD), qmap), pl.BlockSpec((blk, HEAD_PAD), kmap),
                  pl.BlockSpec((blk, V_HEAD), kmap), pl.BlockSpec((SUBLANES, blk), smap),
                  pl.BlockSpec((SUBLANES, blk), smap), pl.BlockSpec((blk, V_HEAD), qmap),
                  pl.BlockSpec((blk, LANES), tmap), pl.BlockSpec((blk, LANES), tmap)],
        out_specs=[pl.BlockSpec((blk, HEAD_PAD), kmap), pl.BlockSpec((blk, HEAD_PAD), kmap),
                   pl.BlockSpec((blk, V_HEAD), kmap)],
        scratch_shapes=[pltpu.VMEM((n, HEAD_PAD, blk), F32), pltpu.VMEM((blk, HEAD_PAD), F32),
                        pltpu.VMEM((blk, V_HEAD), F32)])
    return pl.pallas_call(body, grid_spec=grid_spec,
                          out_shape=[jax.ShapeDtypeStruct((T, N_HEADS * HEAD_PAD), BF16),
                                     jax.ShapeDtypeStruct((T, N_HEADS * HEAD_PAD), BF16),
                                     jax.ShapeDtypeStruct((T, N_HEADS * V_HEAD), BF16)],
                          name=name, compiler_params=_cparams("arbitrary", "arbitrary"))(
                              qtab, ktab, q, k, v, lse, delta, do, cos_t, sin_t)


MESH_ID = pl.DeviceIdType.MESH
ANY_SPEC = pl.BlockSpec(memory_space=pl.ANY)


def _other_chips(x, y):
    out = []
    for dx, dy in ((1, 0), (0, 1), (1, 1)):
        px = x ^ dx if dx else x
        py = y ^ dy if dy else y
        out.append((px, py, 2 * px + py))
    return out


def _gather_weights(flat):
    rh = flat.shape[0] // 2

    def body(src2, out, send_sems, recv_sems):
        x, y, c = lax.axis_index("x"), lax.axis_index("y"), lax.axis_index("c")
        me = 2 * x + y
        sib = (x, y, 1 - c)
        chips = _other_chips(x, y)
        sends = []
        for j, (px, py, pk) in enumerate(chips):
            cp = pltpu.make_async_remote_copy(src_ref=src2.at[c], dst_ref=out.at[me, c], send_sem=send_sems.at[j],
                                              recv_sem=recv_sems.at[j], device_id=(px, py, c), device_id_type=MESH_ID)
            cp.start()
            sends.append(cp)
        for j, (px, py, pk) in enumerate(chips):
            land = out.at[pk, c]
            pltpu.make_async_remote_copy(src_ref=land, dst_ref=land, send_sem=send_sems.at[j], recv_sem=recv_sems.at[j],
                                         device_id=(px, py, c), device_id_type=MESH_ID).wait_recv()
            fw = pltpu.make_async_remote_copy(src_ref=land, dst_ref=land, send_sem=send_sems.at[3 + j],
                                              recv_sem=recv_sems.at[3 + j], device_id=sib, device_id_type=MESH_ID)
            fw.start()
            sends.append(fw)
        for j, (px, py, pk) in enumerate(chips):
            land = out.at[pk, 1 - c]
            pltpu.make_async_remote_copy(src_ref=land, dst_ref=land, send_sem=send_sems.at[3 + j],
                                         recv_sem=recv_sems.at[3 + j], device_id=sib, device_id_type=MESH_ID).wait_recv()
        for cp in sends:
            cp.wait_send()

    return pl.pallas_call(body, out_shape=jax.ShapeDtypeStruct((N_CHIPS, 2, rh, PACK_LANES), flat.dtype),
                          in_specs=[ANY_SPEC], out_specs=ANY_SPEC,
                          scratch_shapes=[pltpu.SemaphoreType.DMA((6,)), pltpu.SemaphoreType.DMA((6,))],
                          name="gather_weights")(flat.reshape(2, rh, PACK_LANES))


def _swap_halves(gs):
    rh = gs[0].shape[1]

    def body(*refs):
        srcs, out, send_sems, recv_sems = refs[:N_CHIPS], refs[N_CHIPS], refs[N_CHIPS + 1], refs[N_CHIPS + 2]
        x, y, c = lax.axis_index("x"), lax.axis_index("y"), lax.axis_index("c")
        copies = [pltpu.make_async_remote_copy(src_ref=srcs[k].at[1 - c], dst_ref=out.at[k], send_sem=send_sems.at[k],
                                               recv_sem=recv_sems.at[k], device_id=(x, y, 1 - c), device_id_type=MESH_ID)
                  for k in range(N_CHIPS)]
        for cp in copies:
            cp.start()
        for cp in copies:
            cp.wait()

    return pl.pallas_call(body, out_shape=jax.ShapeDtypeStruct((N_CHIPS, rh, PACK_LANES), gs[0].dtype),
                          in_specs=[ANY_SPEC] * N_CHIPS, out_specs=ANY_SPEC,
                          scratch_shapes=[pltpu.SemaphoreType.DMA((N_CHIPS,)), pltpu.SemaphoreType.DMA((N_CHIPS,))],
                          name="grad_swap_halves")(*gs)


def _add_halves(gs, got, core, *, tr):
    rh = gs[0].shape[1]

    def body(*refs):
        g_refs, got_ref, o_ref = refs[1:1 + N_CHIPS], refs[1 + N_CHIPS], refs[2 + N_CHIPS]
        for k in range(N_CHIPS):
            o_ref[k] = (g_refs[k][0] + got_ref[k]).astype(o_ref.dtype)

    grid_spec = pltpu.PrefetchScalarGridSpec(
        num_scalar_prefetch=1, grid=(rh // tr,),
        in_specs=[pl.BlockSpec((1, tr, PACK_LANES), lambda i, c_ref: (c_ref[0], i, 0))] * N_CHIPS
        + [pl.BlockSpec((N_CHIPS, tr, PACK_LANES), lambda i, c_ref: (0, i, 0))],
        out_specs=pl.BlockSpec((N_CHIPS, tr, PACK_LANES), lambda i, c_ref: (0, i, 0)))
    return pl.pallas_call(body, grid_spec=grid_spec,
                          out_shape=jax.ShapeDtypeStruct((N_CHIPS, rh, PACK_LANES), BF16), name="grad_add_halves",
                          compiler_params=_cparams("parallel"))(core.reshape(1), *gs, got)


def _scatter_partials(part):
    rh = part.shape[1]

    def body(src, out, send_sems, recv_sems, local_sem):
        x, y, c = lax.axis_index("x"), lax.axis_index("y"), lax.axis_index("c")
        me = 2 * x + y
        own = pltpu.make_async_copy(src.at[me], out.at[me], local_sem)
        own.start()
        chips = _other_chips(x, y)
        sends = []
        for j, (px, py, pk) in enumerate(chips):
            cp = pltpu.make_async_remote_copy(src_ref=src.at[pk], dst_ref=out.at[me], send_sem=send_sems.at[j],
                                              recv_sem=recv_sems.at[j], device_id=(px, py, c), device_id_type=MESH_ID)
            cp.start()
            sends.append(cp)
        for j, (px, py, pk) in enumerate(chips):
            land = out.at[pk]
            pltpu.make_async_remote_copy(src_ref=land, dst_ref=land, send_sem=send_sems.at[j], recv_sem=recv_sems.at[j],
                                         device_id=(px, py, c), device_id_type=MESH_ID).wait_recv()
        for cp in sends:
            cp.wait_send()
        own.wait()

    return pl.pallas_call(body, out_shape=jax.ShapeDtypeStruct((N_CHIPS, rh, PACK_LANES), part.dtype),
                          in_specs=[ANY_SPEC], out_specs=ANY_SPEC,
                          scratch_shapes=[pltpu.SemaphoreType.DMA((3,)), pltpu.SemaphoreType.DMA((3,)),
                                          pltpu.SemaphoreType.DMA(())],
                          name="grad_scatter_partials")(part)


def _sum_chips(q, *, tr):
    rh = q.shape[1]

    def body(q_ref, o_ref):
        parts = [q_ref[k].astype(F32) for k in range(N_CHIPS)]
        o_ref[...] = ((parts[0] + parts[1]) + parts[2]) + parts[3]

    return pl.pallas_call(body, grid=(rh // tr,),
                          in_specs=[pl.BlockSpec((N_CHIPS, tr, PACK_LANES), lambda i: (0, i, 0))],
                          out_specs=pl.BlockSpec((tr, PACK_LANES), lambda i: (i, 0)),
                          out_shape=jax.ShapeDtypeStruct((rh, PACK_LANES), F32), name="grad_sum_chips",
                          compiler_params=_cparams("parallel"))(q)


def _send_half(half):
    def body(src, out, send_sem, recv_sem):
        x, y, c = lax.axis_index("x"), lax.axis_index("y"), lax.axis_index("c")
        cp = pltpu.make_async_remote_copy(src_ref=src, dst_ref=out, send_sem=send_sem, recv_sem=recv_sem,
                                          device_id=(x, y, 1 - c), device_id_type=MESH_ID)
        cp.start()
        cp.wait()

    return pl.pallas_call(body, out_shape=jax.ShapeDtypeStruct(half.shape, half.dtype),
                          in_specs=[ANY_SPEC], out_specs=ANY_SPEC,
                          scratch_shapes=[pltpu.SemaphoreType.DMA(()), pltpu.SemaphoreType.DMA(())],
                          name="grad_send_half")(half)


def _both_halves(own, got, core, *, tr):
    rh = own.shape[0]

    def body(c_ref, own_ref, got_ref, o_ref):
        mine = pl.program_id(0) == c_ref[0]
        o_ref[0] = jnp.where(mine, own_ref[...], got_ref[...])

    grid_spec = pltpu.PrefetchScalarGridSpec(
        num_scalar_prefetch=1, grid=(2, rh // tr),
        in_specs=[pl.BlockSpec((tr, PACK_LANES), lambda h, i, c_ref: (i, 0)),
                  pl.BlockSpec((tr, PACK_LANES), lambda h, i, c_ref: (i, 0))],
        out_specs=pl.BlockSpec((1, tr, PACK_LANES), lambda h, i, c_ref: (h, i, 0)))
    return pl.pallas_call(body, grid_spec=grid_spec, out_shape=jax.ShapeDtypeStruct((2, rh, PACK_LANES), own.dtype),
                          name="grad_both_halves", compiler_params=_cparams("parallel", "parallel"))(core.reshape(1), own, got)


def _allreduce_small(v):
    rows = v.shape[0]

    def body(v_ref, o_ref, buf, send_sems, recv_sems):
        x, y, c = lax.axis_index("x"), lax.axis_index("y"), lax.axis_index("c")
        me = 4 * x + 2 * y + c
        buf[me] = v_ref[...]
        sends = []
        for j in range(1, N_DEV):
            px, py, pc = x ^ ((j >> 2) & 1), y ^ ((j >> 1) & 1), c ^ (j & 1)
            cp = pltpu.make_async_remote_copy(src_ref=v_ref, dst_ref=buf.at[me], send_sem=send_sems.at[j - 1],
                                              recv_sem=recv_sems.at[j - 1], device_id=(px, py, pc), device_id_type=MESH_ID)
            cp.start()
            sends.append(cp)
        for j in range(1, N_DEV):
            px, py, pc = x ^ ((j >> 2) & 1), y ^ ((j >> 1) & 1), c ^ (j & 1)
            land = buf.at[4 * px + 2 * py + pc]
            pltpu.make_async_remote_copy(src_ref=land, dst_ref=land, send_sem=send_sems.at[j - 1],
                                         recv_sem=recv_sems.at[j - 1], device_id=(px, py, pc),
                                         device_id_type=MESH_ID).wait_recv()
        for cp in sends:
            cp.wait_send()
        acc = buf[0]
        for d in range(1, N_DEV):
            acc = acc + buf[d]
        o_ref[...] = acc

    vm = pl.BlockSpec(memory_space=pltpu.VMEM)
    return pl.pallas_call(body, out_shape=jax.ShapeDtypeStruct((rows, LANES), F32), in_specs=[vm], out_specs=vm,
                          scratch_shapes=[pltpu.VMEM((N_DEV, rows, LANES), F32), pltpu.SemaphoreType.DMA((N_DEV - 1,)),
                                          pltpu.SemaphoreType.DMA((N_DEV - 1,))],
                          name="allreduce_small")(v)


def _adamw(w, g, m, v, *, name):
    shape = w.shape
    cols = shape[-1] if w.ndim > 1 else shape[0]
    rows = w.size // cols
    w2, g2, m2, v2 = (t.reshape(rows, cols) for t in (w, g, m, v))
    tr = rows if rows <= 256 else _tile(rows, 256, 8)

    def fn(i, wv, gv, mv, vv):
        mn = ADAM_B1 * mv + (1.0 - ADAM_B1) * gv
        vn = ADAM_B2 * vv + (1.0 - ADAM_B2) * (gv * gv)
        m_hat = mn / (1.0 - ADAM_B1 ** ADAM_STEP)
        v_hat = vn / (1.0 - ADAM_B2 ** ADAM_STEP)
        delta = -ADAM_LR * (m_hat / (jnp.sqrt(v_hat) + ADAM_EPS) + ADAM_WD * wv)
        return delta, mn, vn

    ins = [_rspec(t, tr) for t in (w2, g2, m2, v2)]
    d, mn, vn = _row_call(fn, rows, tr, ins, [(cols, F32)] * 3, name=name)
    return d.reshape(shape), mn.reshape(shape), vn.reshape(shape)


def _rope_cols(w):
    z = jnp.zeros(w.shape[:-1] + (32,), w.dtype)
    return jnp.concatenate([w[..., :32], z, w[..., 32:], z], axis=-1)


def _rope_cols_inv(w):
    return jnp.concatenate([w[..., :32], w[..., 64:96]], axis=-1)


def _layer_layouts(W, i):
    w_in = W['w_in'][i]
    u, cq, ckv = w_in[:, :512], w_in[:, 512:1024], w_in[:, 1024:1280]
    kr, ga, gb = w_in[:, 1280:1344], w_in[:, 1344:2368], w_in[:, 2368:]
    L = {}
    L['w_in'] = jnp.concatenate([ga, gb, u, cq, ckv, _rope_cols(kr)], axis=1)
    wq = W['w_uq'][i]
    L['w_q'] = jnp.concatenate([wq[..., :QK_NOPE], _rope_cols(wq[..., QK_NOPE:])], axis=-1).reshape(Q_LORA, -1)
    wkv = W['w_ukv'][i]
    L['w_k'] = jnp.concatenate([wkv[..., :QK_NOPE], jnp.zeros_like(wkv[..., :LANES])], axis=-1).reshape(KV_LORA, -1)
    L['w_v'] = wkv[..., QK_NOPE:].reshape(KV_LORA, -1)
    L['w_gu'] = jnp.concatenate([W['w_gate'][i], W['w_up'][i]], axis=1)
    for n in ('w_a', 'w_b', 'w_o', 'w_down', 'w_ple_gate', 'w_ple'):
        L[n] = W[n][i]
    return L


def _layer_grads_to_reference_layout(G):
    d = G['w_in']
    ga, gb, u = d[:, ZC_GA:ZC_GB], d[:, ZC_GB:ZC_U], d[:, ZC_U:ZC_CQ]
    cq, ckv, kr = d[:, ZC_CQ:ZC_CKV], d[:, ZC_CKV:ZC_KR], _rope_cols_inv(d[:, ZC_KR:])
    out = {'w_in': jnp.concatenate([u, cq, ckv, kr, ga, gb], axis=1)}
    dq = G['w_q'].reshape(Q_LORA, N_HEADS, HEAD_PAD)
    out['w_uq'] = jnp.concatenate([dq[..., :QK_NOPE], _rope_cols_inv(dq[..., QK_NOPE:])], axis=-1)
    dk = G['w_k'].reshape(KV_LORA, N_HEADS, HEAD_PAD)[..., :QK_NOPE]
    dv = G['w_v'].reshape(KV_LORA, N_HEADS, V_HEAD)
    out['w_ukv'] = jnp.concatenate([dk, dv], axis=-1)
    out['w_gate'], out['w_up'] = G['w_gu'][:, :D_FF], G['w_gu'][:, D_FF:]
    for n in ('w_a', 'w_b', 'w_o', 'w_down', 'w_ple_gate', 'w_ple'):
        out[n] = G[n]
    return out


PACK_ROWS = 2048


def _pack_rows(parts, row_mult):
    flat = jnp.concatenate([p.reshape(-1) for p in parts])
    n = flat.shape[0]
    per = LANES * row_mult
    padded = -(-n // per) * per
    return jnp.pad(flat, (0, padded - n)).reshape(-1, LANES)


def _unpack_rows(flat2d, shapes):
    flat = flat2d.reshape(-1)
    out, off = [], 0
    for s in shapes:
        n = int(np.prod(s))
        out.append(flat[off:off + n].reshape(s))
        off += n
    return out


PACK_LANES = 256


def _lane_blocks(cols):
    return -(-cols // PACK_LANES)


def _pack_blocks(parts, row_mult):
    blocks = []
    for p in parts:
        p2 = p.reshape(-1, p.shape[-1])
        cols = p2.shape[1]
        nb = _lane_blocks(cols)
        if nb * PACK_LANES != cols:
            p2 = jnp.pad(p2, ((0, 0), (0, nb * PACK_LANES - cols)))
        blocks += [p2[:, j * PACK_LANES:(j + 1) * PACK_LANES] for j in range(nb)]
    buf = jnp.concatenate(blocks, axis=0)
    rows = buf.shape[0]
    padded = -(-rows // row_mult) * row_mult
    return buf if padded == rows else jnp.pad(buf, ((0, padded - rows), (0, 0)))


def _unpack_blocks(buf, shapes):
    out, off = [], 0
    for s in shapes:
        rows, cols = int(np.prod(s[:-1])), s[-1]
        nb = _lane_blocks(cols)
        piece = jnp.concatenate([buf[off + j * rows:off + (j + 1) * rows] for j in range(nb)], axis=1)
        out.append(piece[:, :cols].reshape(s))
        off += nb * rows
    return out


def _layer_fwd(i, x, p_i, L, norms, w_pool_bf, pool_scale, cos_t, sin_t, tr, blk):
    sv = {'x': x}
    z, sv['h'] = _mm_nn(x, L['w_in'], name=f"l{i}_in_proj", outs=[(Z_WIDTH, F32)], gain=norms['norm_mix'], emit_a=True,
                        tm=tr // 2)
    sv['z'] = z
    sv['pooled'], sv['mixed'], sv['pm'] = _pool_fwd(z, w_pool_bf, pool_scale, name=f"l{i}_pool", tr=tr)
    sv['ya'] = _mm_nn(sv['pm'], L['w_a'], name=f"l{i}_ya", outs=[(D_MODEL, BF16)], tm=tr)

    def heads(acc, rope_part):
        out = []
        for h in range(N_HEADS):
            out.append(acc[:, h * HEAD_PAD:h * HEAD_PAD + QK_NOPE])
            out.append(rope_part(acc[:, h * HEAD_PAD + QK_NOPE:(h + 1) * HEAD_PAD]))
        return jnp.concatenate(out, axis=1)

    def q_epi(acc, ct, st):
        return (heads(acc * EXP2_SCALE, lambda t: _rope(t, ct, st)),)

    def k_epi(acc, kr, ct, st):
        k_pe = _rope(kr, ct, st)
        return (heads(acc, lambda t: k_pe),)

    rope_rows = [(cos_t, LANES, 0), (sin_t, LANES, 0)]
    qk_width = N_HEADS * HEAD_PAD
    sv['q'], sv['cqn'] = _mm_nn(z, L['w_q'], name=f"l{i}_q_proj", outs=[(qk_width, BF16)], a_col=ZC_CQ,
                                gain=norms['q_norm'], emit_a=True, epi=q_epi, epi_rows=rope_rows, tm=tr)
    sv['k'], sv['ckvn'] = _mm_nn(z, L['w_k'], name=f"l{i}_k_proj", outs=[(qk_width, BF16)], a_col=ZC_CKV,
                                 gain=norms['kv_norm'], emit_a=True, epi=k_epi,
                                 epi_rows=[(z, LANES, ZC_KR // LANES)] + rope_rows, tm=tr)
    sv['v'] = _mm_nn(sv['ckvn'], L['w_v'], name=f"l{i}_v_proj", outs=[(N_HEADS * V_HEAD, BF16)], tm=tr)
    sv['o'], sv['lse'] = _flash_fwd(sv['q'], sv['k'], sv['v'], name=f"l{i}_attn", blk=blk)

    def merge_epi(yb, ga, gb, ya):
        return yb, _sigmoid(ga) * ya.astype(F32) + _sigmoid(gb) * yb

    sv['yb'], sv['merged'] = _mm_nn(sv['o'], L['w_b'], name=f"l{i}_yb_merge", outs=[(D_MODEL, BF16), (D_MODEL, BF16)],
                                    epi=merge_epi, epi_rows=[(z, D_MODEL, 0), (z, D_MODEL, 1), (sv['ya'], D_MODEL, 0)],
                                    tm=tr)

    def add_epi(acc, res):
        return (acc + res,)

    x1 = _mm_nn(sv['merged'], L['w_o'], name=f"l{i}_wo", outs=[(D_MODEL, F32)], epi=add_epi, epi_rows=[(x, D_MODEL, 0)],
                tm=tr)
    sv['x1'] = x1

    def swiglu_epi(acc):
        g, u = acc[:, :D_FF], acc[:, D_FF:]
        return acc, g * _sigmoid(g) * u

    sv['gu'], sv['act'], sv['h2'] = _mm_nn(x1, L['w_gu'], name=f"l{i}_gate_up", outs=[(2 * D_FF, BF16), (D_FF, BF16)],
                                           gain=norms['norm_ffn'], emit_a=True, epi=swiglu_epi, tm=tr // 2)
    x2 = _mm_nn(sv['act'], L['w_down'], name=f"l{i}_down", outs=[(D_MODEL, F32)], epi=add_epi,
                epi_rows=[(x1, D_MODEL, 0)], tm=tr)
    sv['x2'] = x2
    sv['logit'], sv['h3'] = _mm_nn(x2, L['w_ple_gate'], name=f"l{i}_ple_gate", outs=[(D_MODEL, F32)],
                                   gain=norms['norm_ple'], emit_a=True, tm=tr)

    def ple_epi(pe, xv, lg):
        return pe, xv + _sigmoid(lg) * pe

    sv['pe'], x3 = _mm_nn(p_i, L['w_ple'], name=f"l{i}_ple", outs=[(D_MODEL, F32), (D_MODEL, F32)], epi=ple_epi,
                          epi_rows=[(x2, D_MODEL, 0), (sv['logit'], D_MODEL, 0)], tm=tr)
    return x3, sv


def _layer_bwd(i, dx3, sv, p_i, L, norms, w_pool_bf, pool_scale, cos_t, sin_t, tr, blk):
    T = dx3.shape[0]
    G = {}
    z = sv['z']

    def ple_bwd(_, d, lg, pe):
        g = _sigmoid(lg)
        return d * pe * g * (1.0 - g), d * g

    dlogit, dpe = _row_call(ple_bwd, T, tr, [_rspec(dx3, tr), _rspec(sv['logit'], tr), _rspec(sv['pe'], tr)],
                            [(D_MODEL, BF16), (D_MODEL, BF16)], name=f"l{i}_ple_bwd")
    G['w_ple_gate'] = _mm_tn(sv['h3'], dlogit, name=f"l{i}_dw_ple_gate", tn=1024)
    G['w_ple'] = _mm_tn(p_i, dpe, name=f"l{i}_dw_ple", tn=1024)
    def gain_row(n):
        return norms[n].reshape(1, -1).astype(F32)

    dx2, dx2_bf, G['norm_ple'] = _mm_nt([(dlogit, L['w_ple_gate'])], name=f"l{i}_dh3_norm_bwd",
                                        outs=[(D_MODEL, F32), (D_MODEL, BF16)], epi=_rms_bwd_epi(True, True),
                                        epi_rows=[(sv['x2'], D_MODEL, 0), (dx3, D_MODEL, 0)], consts=[gain_row('norm_ple')],
                                        accs=[(1, D_MODEL)], tm=tr)

    def swiglu_bwd_epi(da, gu):
        g, u = gu[:, :D_FF].astype(F32), gu[:, D_FF:].astype(F32)
        sg = _sigmoid(g)
        return (jnp.concatenate([da * u * sg * (1.0 + g * (1.0 - sg)), da * g * sg], axis=1),)

    dgu = _mm_nt([(dx2_bf, L['w_down'])], name=f"l{i}_dact_swiglu_bwd", outs=[(2 * D_FF, BF16)], epi=swiglu_bwd_epi,
                 epi_rows=[(sv['gu'], 2 * D_FF, 0)], tm=tr // 2)
    G['w_down'] = _mm_tn(sv['act'], dx2_bf, name=f"l{i}_dw_down", tk=1408, tn=1024)
    G['w_gu'] = _mm_tn(sv['h2'], dgu, name=f"l{i}_dw_gate_up", tn=1408)
    dx1, dx1_bf, G['norm_ffn'] = _mm_nt([(dgu, L['w_gu'])], name=f"l{i}_dh2_norm_bwd",
                                        outs=[(D_MODEL, F32), (D_MODEL, BF16)], epi=_rms_bwd_epi(True, True),
                                        epi_rows=[(sv['x1'], D_MODEL, 0), (dx2, D_MODEL, 0)], consts=[gain_row('norm_ffn')],
                                        accs=[(1, D_MODEL)], tm=tr // 2)

    def merge_bwd_epi(dm, ga, gb, ya, yb):
        sa, sb = _sigmoid(ga), _sigmoid(gb)
        ya, yb = ya.astype(F32), yb.astype(F32)
        d_gates = jnp.concatenate([dm * ya * sa * (1.0 - sa), dm * yb * sb * (1.0 - sb)], axis=1)
        return d_gates, dm * sa, dm * sb

    dz, dya, dyb = _mm_nt([(dx1_bf, L['w_o'])], name=f"l{i}_dmerged_bwd",
                          outs=[(2 * D_MODEL, BF16), (D_MODEL, BF16), (D_MODEL, BF16)], epi=merge_bwd_epi,
                          epi_rows=[(z, D_MODEL, 0), (z, D_MODEL, 1), (sv['ya'], D_MODEL, 0), (sv['yb'], D_MODEL, 0)],
                          tm=tr, wide0=Z_WIDTH)
    G['w_o'] = _mm_tn(sv['merged'], dx1_bf, name=f"l{i}_dw_o", tn=1024)

    G['w_b'] = _mm_tn(sv['o'], dyb, name=f"l{i}_dw_b", tn=1024)
    do = _mm_nt([(dyb, L['w_b'])], name=f"l{i}_do", outs=[(D_MODEL, BF16)], tm=tr)
    delta = _attn_delta(do, sv['o'], name=f"l{i}_attn_delta", tr=tr)
    dq, dk, dv = _flash_bwd(sv['q'], sv['k'], sv['v'], sv['lse'], delta, do, cos_t, sin_t, name=f"l{i}_attn_bwd", blk=blk)

    def dk_rope(_, d, ct, st):
        d = d.astype(F32)
        acc = d[:, QK_NOPE:HEAD_PAD]
        for h in range(1, N_HEADS):
            acc = acc + d[:, h * HEAD_PAD + QK_NOPE:(h + 1) * HEAD_PAD]
        return _rope_bwd(acc, ct, st)

    dz = _row_call(dk_rope, T, tr, [_rspec(dk, tr), _rspec(cos_t, tr), _rspec(sin_t, tr)], [(LANES, BF16)],
                   name=f"l{i}_dk_rope", into=(dz, LANES, ZC_KR // LANES))[0]
    G['w_q'] = _mm_tn(sv['cqn'], dq, name=f"l{i}_dw_q", tn=1024)
    G['w_k'] = _mm_tn(sv['ckvn'], dk, name=f"l{i}_dw_k", tn=1024)
    G['w_v'] = _mm_tn(sv['ckvn'], dv, name=f"l{i}_dw_v", tn=1024)
    dz, G['q_norm'] = _mm_nt([(dq, L['w_q'])], name=f"l{i}_dcq", outs=[(Q_LORA, BF16)], epi=_rms_bwd_epi(False, False),
                             epi_rows=[(z, Q_LORA, ZC_CQ // Q_LORA)], consts=[gain_row('q_norm')], accs=[(1, Q_LORA)], tm=tr,
                             into=(dz, Q_LORA, ZC_CQ // Q_LORA))
    dz, G['kv_norm'] = _mm_nt([(dk, L['w_k']), (dv, L['w_v'])], name=f"l{i}_dckv", outs=[(KV_LORA, BF16)],
                              epi=_rms_bwd_epi(False, False), epi_rows=[(z, KV_LORA, ZC_CKV // KV_LORA)],
                              consts=[gain_row('kv_norm')], accs=[(1, KV_LORA)], tm=tr,
                              into=(dz, KV_LORA, ZC_CKV // KV_LORA))

    G['w_a'] = _mm_tn(sv['pm'], dya, name=f"l{i}_dw_a", tn=1024)
    dpm = _mm_nt([(dya, L['w_a'])], name=f"l{i}_dpm", outs=[(POOL_WIDTH, F32)], tm=tr)
    dpool, dpool_cnt, G['pool_scale'], G['w_pool'] = _pool_bwd_mix(dpm, sv['mixed'], sv['pooled'], w_pool_bf, pool_scale,
                                                                   name=f"l{i}_pool_bwd_mix", tr=tr)
    dz = _pool_bwd_window(dpool, dpool_cnt, name=f"l{i}_pool_bwd_window", tr=tr, into=(dz, POOL_WIDTH, ZC_U // POOL_WIDTH))

    G['w_in'] = _mm_tn(sv['h'], dz, name=f"l{i}_dw_in", tn=1152)
    dx, G['norm_mix'] = _mm_nt([(dz, L['w_in'])], name=f"l{i}_dh_norm_bwd", outs=[(D_MODEL, F32)],
                               epi=_rms_bwd_epi(True, False), epi_rows=[(sv['x'], D_MODEL, 0), (dx1, D_MODEL, 0)],
                               consts=[gain_row('norm_mix')], accs=[(1, D_MODEL)], tm=tr)
    return dx, G


def kernel(x, p, positions, norm_mix, w_in, w_pool, pool_scale, q_norm, kv_norm, w_uq, w_ukv, w_a, w_b, w_o, norm_ffn, w_gate, w_up, w_down, norm_ple, w_ple_gate, w_ple, final_norm, loss_target, m_norm_mix, m_w_in, m_w_pool, m_pool_scale, m_q_norm, m_kv_norm, m_w_uq, m_w_ukv, m_w_a, m_w_b, m_w_o, m_norm_ffn, m_w_gate, m_w_up, m_w_down, m_norm_ple, m_w_ple_gate, m_w_ple, m_final_norm, v_norm_mix, v_w_in, v_w_pool, v_pool_scale, v_q_norm, v_kv_norm, v_w_uq, v_w_ukv, v_w_a, v_w_b, v_w_o, v_norm_ffn, v_w_gate, v_w_up, v_w_down, v_norm_ple, v_w_ple_gate, v_w_ple, v_final_norm):
    given = dict(locals())
    weights = {n: given[n] for n in WEIGHTS}
    T = x.shape[1]
    tr = min(512, max(T // 2, 8))
    blk = min(ATTN_BLOCK, max(T // 4, 128))
    x0 = x.reshape(T, D_MODEL)
    target = loss_target.reshape(T, D_MODEL)

    names = list(SHARDED)
    shard_shapes = [weights[n].shape for n in names]
    flat = _pack_blocks([weights[n].astype(BF16) for n in names], row_mult=PACK_ROWS)
    R = flat.shape[0]
    chip = (2 * lax.axis_index("x") + lax.axis_index("y")).astype(jnp.int32)
    core = lax.axis_index("c").astype(jnp.int32)
    gathered = _gather_weights(flat).reshape(N_CHIPS, R, PACK_LANES)
    gathered = lax.dynamic_update_slice(gathered, flat.reshape(1, R, PACK_LANES), (chip, 0, 0))
    per_chip = [_unpack_blocks(gathered[k], shard_shapes) for k in range(N_CHIPS)]
    W = {n: jnp.concatenate([per_chip[k][j] for k in range(N_CHIPS)], axis=SHARDED[n]) for j, n in enumerate(names)}
    layouts = [_layer_layouts(W, i) for i in range(DEPTH)]
    w_pool_bf = w_pool.astype(BF16)

    inv_freq = 1.0 / (ROPE_THETA ** (jnp.arange(0, QK_ROPE, 2, dtype=F32) / QK_ROPE))
    zero32 = jnp.zeros((32,), F32)
    freq_row = jnp.concatenate([inv_freq, zero32, inv_freq, zero32]).reshape(1, LANES)
    cos_mask = jnp.concatenate([jnp.ones((32,), F32), zero32, jnp.ones((32,), F32), zero32]).reshape(1, LANES)
    sin_sign = jnp.concatenate([-jnp.ones((32,), F32), zero32, jnp.ones((32,), F32), zero32]).reshape(1, LANES)

    def rope_tables(_, pos, fr, cm, ss):
        ang = pos.astype(F32) * fr
        return jnp.cos(ang) * cm, jnp.sin(ang) * ss

    pos_col = positions.reshape(T, 1)
    cos_t, sin_t = _row_call(rope_tables, T, tr, [_rspec(pos_col, tr), _bspec(freq_row), _bspec(cos_mask), _bspec(sin_sign)],
                             [(LANES, F32), (LANES, F32)], name="rope_tables")

    xs = x0
    saved = []
    for i in range(DEPTH):
        norms = {n: weights[n][i] for n in ('norm_mix', 'q_norm', 'kv_norm', 'norm_ffn', 'norm_ple')}
        xs, sv = _layer_fwd(i, xs, p[i, 0], layouts[i], norms, w_pool_bf[i], pool_scale[i], cos_t, sin_t, tr, blk)
        saved.append((sv, norms))

    def head(_, xv, tv, gv):
        rstd = lax.rsqrt(jnp.mean(xv * xv, axis=-1, keepdims=True) + EPS)
        xhat = xv * rstd
        err = xhat * gv - tv
        loss = 0.5 * jnp.sum(jnp.mean(err * err, axis=-1, keepdims=True), axis=0, keepdims=True)
        dy = err * (1.0 / D_MODEL)
        dg = jnp.sum(dy * xhat, axis=0, keepdims=True)
        dxh = dy * gv
        dx = rstd * (dxh - xhat * jnp.mean(dxh * xhat, axis=-1, keepdims=True))
        return dx, jnp.broadcast_to(loss, (1, LANES)), dg

    dx, loss_part, g_final = _row_call(head, T, tr, [_rspec(xs, tr), _rspec(target, tr), _bspec(final_norm.reshape(1, D_MODEL))],
                                       [(D_MODEL, F32)], [(1, LANES), (1, D_MODEL)], name="loss_head")
    loss = lax.psum(loss_part[0, 0], ("x", "y", "c"))

    layer_grads = [None] * DEPTH
    for i in reversed(range(DEPTH)):
        sv, norms = saved[i]
        dx, layer_grads[i] = _layer_bwd(i, dx, sv, p[i, 0], layouts[i], norms, w_pool_bf[i], pool_scale[i], cos_t, sin_t, tr,
                                        blk)
    grad_x = dx.reshape(x.shape)

    ref_layout = [_layer_grads_to_reference_layout(g) for g in layer_grads]
    local = {n: jnp.stack([ref_layout[i][n] for i in range(DEPTH)]) for n in names}
    for n in ('norm_mix', 'q_norm', 'kv_norm', 'norm_ffn', 'norm_ple', 'pool_scale'):
        local[n] = jnp.stack([layer_grads[i][n].reshape(-1) for i in range(DEPTH)])
    local['w_pool'] = jnp.stack([layer_grads[i]['w_pool'] for i in range(DEPTH)])
    local['final_norm'] = g_final.reshape(-1)

    send = []
    for k in range(N_CHIPS):
        parts = []
        for n in names:
            ax = SHARDED[n]
            size = local[n].shape[ax] // N_CHIPS
            parts.append(lax.slice_in_dim(local[n], k * size, (k + 1) * size, axis=ax))
        send.append(_pack_blocks(parts, row_mult=PACK_ROWS))
    rh = R // 2
    trr = PACK_ROWS // 2
    send = [g.reshape(2, rh, PACK_LANES) for g in send]
    part = _add_halves(send, _swap_halves(send), core, tr=trr)
    reduced_half = _sum_chips(_scatter_partials(part), tr=trr)
    reduced = _both_halves(reduced_half, _send_half(reduced_half), core, tr=trr).reshape(R, PACK_LANES)
    grads = dict(zip(names, _unpack_blocks(reduced, shard_shapes)))

    rep_shapes = [weights[n].shape for n in REPLICATED]
    rep = _allreduce_small(_pack_rows([local[n] for n in REPLICATED], row_mult=8))
    grads.update(zip(REPLICATED, _unpack_rows(rep, rep_shapes)))

    deltas, new_m, new_v = {}, {}, {}
    for n in WEIGHTS:
        deltas[n], new_m[n], new_v[n] = _adamw(weights[n], grads[n], given['m_' + n], given['v_' + n], name=f"adamw_{n}")
    return (loss, grad_x, *[grads[n] for n in WEIGHTS], *[deltas[n] for n in WEIGHTS], *[new_m[n] for n in WEIGHTS],
            *[new_v[n] for n in WEIGHTS])
```

```python
import functools

import numpy as np
import jax
import jax.numpy as jnp
from jax import lax
from jax.experimental import pallas as pl
from jax.experimental.pallas import tpu as pltpu

F32 = jnp.float32
BF16 = jnp.bfloat16

D_MODEL = 1024
DEPTH = 2
PLE_DIM = 256
POOL_WINDOWS = (2, 4, 8, 16)
POOL_GROUP = 128
POOL_WIDTH = 512
N_HEADS = 8
Q_LORA = 512
KV_LORA = 256
QK_NOPE = 128
QK_ROPE = 64
QK_HEAD = 192
V_HEAD = 128
D_FF = 2816
ROPE_THETA = 10000.0
EPS = 1e-6
ATTN_SCALE = QK_HEAD ** -0.5

ADAM_LR = 0.001
ADAM_B1 = 0.9
ADAM_B2 = 0.999
ADAM_EPS = 1e-08
ADAM_WD = 0.01
ADAM_STEP = 10

LANES = 128
HALO = 16
HEAD_PAD = 256
V7X_VMEM_BYTES = 64 * 1024 * 1024
VMEM_LIMIT = (V7X_VMEM_BYTES * 3) // 4
N_CHIPS = 4
N_DEV = 8
NEG_INF = -1e30

ZC_GA, ZC_GB, ZC_U, ZC_CQ, ZC_CKV, ZC_KR = 0, 1024, 2048, 2560, 3072, 3328
Z_WIDTH = 3456

WEIGHTS = ['norm_mix', 'w_in', 'w_pool', 'pool_scale', 'q_norm', 'kv_norm', 'w_uq', 'w_ukv', 'w_a', 'w_b', 'w_o',
           'norm_ffn', 'w_gate', 'w_up', 'w_down', 'norm_ple', 'w_ple_gate', 'w_ple', 'final_norm']
SHARDED = {'w_in': 2, 'w_uq': 1, 'w_ukv': 1, 'w_a': 2, 'w_b': 1, 'w_o': 1, 'w_gate': 2, 'w_up': 2, 'w_down': 1,
           'w_ple_gate': 1, 'w_ple': 2}
REPLICATED = [n for n in WEIGHTS if n not in SHARDED]


def _tile(n, target, mult=LANES):
    if n <= target:
        return n
    best = None
    for t in range(mult, target + 1, mult):
        if n % t == 0:
            best = t
    assert best is not None, (n, target)
    return best


def _cparams(*sem):
    return pltpu.CompilerParams(dimension_semantics=sem, vmem_limit_bytes=VMEM_LIMIT)


def _rope(t, cos_t, sin_t):
    return t * cos_t + pltpu.roll(t, 64, 1) * sin_t


def _rope_bwd(d, cos_t, sin_t):
    return d * cos_t + pltpu.roll(d * sin_t, 64, 1)


def _sigmoid(v):
    return 1.0 / (1.0 + jnp.exp(-v))


def _mm_nn(a, b, *, name, outs, a_col=0, gain=None, emit_a=False, epi=None, epi_rows=(), tm=512):
    M = a.shape[0]
    K, N = b.shape
    tm = min(tm, M)
    assert a_col % K == 0 and M % tm == 0
    a_blk = a_col // K
    n_rows, n_out = len(epi_rows), len(outs)

    def body(*refs):
        a_ref, b_ref = refs[0], refs[1]
        pos = 2
        g_ref = None
        if gain is not None:
            g_ref = refs[pos]
            pos += 1
        row_refs = refs[pos:pos + n_rows]
        out_refs = refs[pos + n_rows:pos + n_rows + n_out]
        lhs = a_ref[...]
        if gain is not None:
            av = lhs.astype(F32)
            lhs = av * lax.rsqrt(jnp.mean(av * av, axis=-1, keepdims=True) + EPS) * g_ref[...]
        lhs = lhs.astype(BF16)
        if emit_a:
            refs[pos + n_rows + n_out][...] = lhs
        acc = jnp.dot(lhs, b_ref[...], preferred_element_type=F32)
        vals = (acc,) if epi is None else epi(acc, *[r[...] for r in row_refs])
        for r, v in zip(out_refs, vals):
            r[...] = v.astype(r.dtype)

    in_specs = [pl.BlockSpec((tm, K), lambda i: (i, a_blk)), pl.BlockSpec((K, N), lambda i: (0, 0))]
    args = [a, b]
    if gain is not None:
        in_specs.append(pl.BlockSpec((1, K), lambda i: (0, 0)))
        args.append(gain.reshape(1, K).astype(F32))
    for arr, w, blk in epi_rows:
        in_specs.append(pl.BlockSpec((tm, w), lambda i, blk=blk: (i, blk)))
        args.append(arr)
    out_shape = [jax.ShapeDtypeStruct((M, w), dt) for w, dt in outs]
    out_specs = [pl.BlockSpec((tm, w), lambda i: (i, 0)) for w, dt in outs]
    if emit_a:
        out_shape.append(jax.ShapeDtypeStruct((M, K), BF16))
        out_specs.append(pl.BlockSpec((tm, K), lambda i: (i, 0)))
    res = pl.pallas_call(body, grid=(M // tm,), in_specs=in_specs, out_specs=out_specs, out_shape=out_shape,
                         name=name, compiler_params=_cparams("parallel"))(*args)
    return res[0] if len(res) == 1 else res


def _mm_nt(pairs, *, name, outs, epi=None, epi_rows=(), consts=(), accs=(), tm=512, into=None, wide0=None):
    M = pairs[0][0].shape[0]
    N = pairs[0][1].shape[0]
    tm = min(tm, M)
    n_p, n_in, n_out, n_acc = len(pairs), len(epi_rows) + len(consts), len(outs), len(accs)

    def body(*refs):
        acc = None
        for k in range(n_p):
            av = refs[2 * k][...].astype(BF16)
            part = lax.dot_general(av, refs[2 * k + 1][...], NT_DIMS, preferred_element_type=F32)
            acc = part if acc is None else acc + part
        pos = 2 * n_p
        extra = [r[...] for r in refs[pos:pos + n_in]]
        pos += n_in + (1 if into is not None else 0)
        vals = (acc,) if epi is None else epi(acc, *extra)
        for r, v in zip(refs[pos:pos + n_out], vals[:n_out]):
            r[...] = v.astype(r.dtype)
        if n_acc:
            acc_refs = refs[pos + n_out:pos + n_out + n_acc]

            @pl.when(pl.program_id(0) == 0)
            def _():
                for r in acc_refs:
                    r[...] = jnp.zeros_like(r)
            for r, v in zip(acc_refs, vals[n_out:]):
                r[...] += v

    in_specs, args = [], []
    for a, b in pairs:
        assert a.shape[1] == b.shape[1] and b.shape[0] == N and a.shape[0] == M
        in_specs.append(pl.BlockSpec((tm, a.shape[1]), lambda i: (i, 0)))
        in_specs.append(pl.BlockSpec(b.shape, lambda i: (0, 0)))
        args += [a, b]
    for arr, w, blk in epi_rows:
        in_specs.append(pl.BlockSpec((tm, w), lambda i, blk=blk: (i, blk)))
        args.append(arr)
    for arr in consts:
        in_specs.append(pl.BlockSpec(arr.shape, lambda i, n=arr.ndim: (0,) * n))
        args.append(arr)
    out_shape = [jax.ShapeDtypeStruct((M, w), dt) for w, dt in outs]
    out_specs = [pl.BlockSpec((tm, w), lambda i: (i, 0)) for w, dt in outs]
    for s in accs:
        out_shape.append(jax.ShapeDtypeStruct(s, F32))
        out_specs.append(pl.BlockSpec(s, lambda i, n=len(s): (0,) * n))
    aliases = _into_column_block(into, tm, out_shape, out_specs, in_specs, args) if into is not None else {}
    if wide0 is not None:
        out_shape[0] = jax.ShapeDtypeStruct((M, wide0), outs[0][1])
    res = pl.pallas_call(body, grid=(M // tm,), in_specs=in_specs, out_specs=out_specs, out_shape=out_shape, name=name,
                         input_output_aliases=aliases,
                         compiler_params=_cparams("arbitrary" if n_acc else "parallel"))(*args)
    return res[0] if len(res) == 1 else res


def _rms_bwd_epi(with_res, emit_bf16):
    def epi(dh, xv, *rest):
        gv = rest[-1]
        xv = xv.astype(F32)
        rstd = lax.rsqrt(jnp.mean(xv * xv, axis=-1, keepdims=True) + EPS)
        xhat = xv * rstd
        dg = jnp.sum(dh * xhat, axis=0, keepdims=True)
        dxh = dh * gv
        dx = rstd * (dxh - xhat * jnp.mean(dxh * xhat, axis=-1, keepdims=True))
        if with_res:
            dx = dx + rest[0].astype(F32)
        return (dx, dx, dg) if emit_bf16 else (dx, dg)
    return epi


def _mm_tn(a, b, *, name, a_col=0, a_w=None, tk=1024, tn=1152, tm=512):
    M = a.shape[0]
    a_w = a.shape[1] if a_w is None else a_w
    N = b.shape[1]
    tm = min(tm, M)
    tk = _tile(a_w, tk)
    tn = _tile(N, tn)
    assert a_col % tk == 0 and M % tm == 0
    a_blk0 = a_col // tk

    def body(a_ref, b_ref, o_ref):
        @pl.when(pl.program_id(2) == 0)
        def _():
            o_ref[...] = jnp.zeros_like(o_ref)
        o_ref[...] += lax.dot_general(a_ref[...].astype(BF16), b_ref[...].astype(BF16), (((0,), (0,)), ((), ())),
                                      preferred_element_type=F32)

    return pl.pallas_call(body, grid=(a_w // tk, N // tn, M // tm),
                          in_specs=[pl.BlockSpec((tm, tk), lambda k, j, m: (m, k + a_blk0)),
                                    pl.BlockSpec((tm, tn), lambda k, j, m: (m, j))],
                          out_specs=pl.BlockSpec((tk, tn), lambda k, j, m: (k, j)),
                          out_shape=jax.ShapeDtypeStruct((a_w, N), F32), name=name,
                          compiler_params=_cparams("parallel", "parallel", "arbitrary"))(a, b)


def _into_column_block(into, tile, out_shape, out_specs, in_specs, args):
    buf, width, blk = into
    out_shape[0] = jax.ShapeDtypeStruct(buf.shape, buf.dtype)
    out_specs[0] = pl.BlockSpec((tile, width), lambda i: (i, blk))
    in_specs.append(pl.BlockSpec(memory_space=pl.ANY))
    args.append(buf)
    return {len(args) - 1: 0}


def _row_call(fn, rows, tr, ins, outs, accs=(), *, name, into=None):
    n_in, n_out, n_acc = len(ins), len(outs), len(accs)
    first_out = n_in + (1 if into is not None else 0)

    def body(*refs):
        i = pl.program_id(0)
        vals = fn(i, *[r[...] for r in refs[:n_in]])
        if not isinstance(vals, (tuple, list)):
            vals = (vals,)
        for r, v in zip(refs[first_out:first_out + n_out], vals[:n_out]):
            r[...] = v.astype(r.dtype)
        if n_acc:
            acc_refs = refs[first_out + n_out:]

            @pl.when(i == 0)
            def _():
                for r in acc_refs:
                    r[...] = jnp.zeros_like(r)
            for r, v in zip(acc_refs, vals[n_out:]):
                r[...] += v

    out_shape = [jax.ShapeDtypeStruct((rows, w), dt) for w, dt in outs]
    out_specs = [pl.BlockSpec((tr, w), lambda i: (i, 0)) for w, dt in outs]
    for s in accs:
        out_shape.append(jax.ShapeDtypeStruct(s, F32))
        out_specs.append(pl.BlockSpec(s, lambda i, n=len(s): (0,) * n))
    in_specs = [pl.BlockSpec(bs, im) for _, bs, im in ins]
    args = [a for a, _, _ in ins]
    aliases = _into_column_block(into, tr, out_shape, out_specs, in_specs, args) if into is not None else {}
    res = pl.pallas_call(body, grid=(rows // tr,), in_specs=in_specs, out_specs=out_specs, out_shape=out_shape, name=name,
                         input_output_aliases=aliases, compiler_params=_cparams("arbitrary"))(*args)
    return res


def _rspec(arr, tr, w=None, blk=0):
    w = arr.shape[1] if w is None else w
    return (arr, (tr, w), lambda i, blk=blk: (i, blk))


def _bspec(arr):
    return (arr, arr.shape, lambda i, n=arr.ndim: (0,) * n)


def _pool_counts(i, tr):
    t = (i * tr + lax.broadcasted_iota(jnp.int32, (tr, 1), 0) + 1).astype(F32)
    return [jnp.minimum(t, float(w)) for w in POOL_WINDOWS]


def _pool_fwd(z, w_pool_bf, pool_scale, *, name, tr):
    rows = z.shape[0]
    ublk = ZC_U // POOL_WIDTH
    hpt = tr // HALO

    def fn(i, u, uprev, wp, ps):
        uprev = jnp.where(i > 0, uprev, 0.0)
        ext = jnp.concatenate([uprev, u], axis=0)
        s2 = ext + pltpu.roll(ext, 1, 0)
        s4 = s2 + pltpu.roll(s2, 2, 0)
        s8 = s4 + pltpu.roll(s4, 4, 0)
        s16 = s8 + pltpu.roll(s8, 8, 0)
        cnts = _pool_counts(i, tr)
        pooled, mixed = [], []
        for g, sw in enumerate((s2, s4, s8, s16)):
            lanes = slice(g * POOL_GROUP, (g + 1) * POOL_GROUP)
            pg = sw[HALO:, lanes] / cnts[g] - u[:, lanes]
            pooled.append(pg)
            mixed.append(jnp.dot(pg.astype(BF16), wp[g], preferred_element_type=F32))
        pooled = jnp.concatenate(pooled, axis=1)
        mixed = jnp.concatenate(mixed, axis=1)
        return pooled, mixed, mixed * ps

    ins = [_rspec(z, tr, POOL_WIDTH, ublk),
           (z, (HALO, POOL_WIDTH), lambda i: (jnp.maximum(i * hpt - 1, 0), ublk)),
           _bspec(w_pool_bf), _bspec(pool_scale.reshape(1, POOL_WIDTH))]
    return _row_call(fn, rows, tr, ins, [(POOL_WIDTH, BF16), (POOL_WIDTH, F32), (POOL_WIDTH, BF16)], name=name)


def _pool_bwd_mix(dpm, mixed, pooled, w_pool_bf, pool_scale, *, name, tr):
    rows = dpm.shape[0]

    def fn(i, dv, mv, pv, wp, ps):
        dv = dv.astype(F32)
        dscale = jnp.sum(dv * mv, axis=0, keepdims=True)
        dmix = (dv * ps).astype(BF16)
        cnts = _pool_counts(i, tr)
        dpool, dwp = [], []
        for g in range(len(POOL_WINDOWS)):
            lanes = slice(g * POOL_GROUP, (g + 1) * POOL_GROUP)
            dg = lax.dot_general(dmix[:, lanes], wp[g], (((1,), (1,)), ((), ())), preferred_element_type=F32)
            dpool.append(dg)
            dwp.append(lax.dot_general(pv[:, lanes], dmix[:, lanes], (((0,), (0,)), ((), ())),
                                       preferred_element_type=F32)[None])
        dpool = jnp.concatenate(dpool, axis=1)
        dpool_cnt = jnp.concatenate([dpool[:, g * POOL_GROUP:(g + 1) * POOL_GROUP] / cnts[g]
                                     for g in range(len(POOL_WINDOWS))], axis=1)
        return dpool, dpool_cnt, dscale, jnp.concatenate(dwp, axis=0)

    ins = [_rspec(dpm, tr), _rspec(mixed, tr), _rspec(pooled, tr), _bspec(w_pool_bf),
           _bspec(pool_scale.reshape(1, POOL_WIDTH))]
    return _row_call(fn, rows, tr, ins, [(POOL_WIDTH, F32), (POOL_WIDTH, F32)],
                     [(1, POOL_WIDTH), (len(POOL_WINDOWS), POOL_GROUP, POOL_GROUP)], name=name)


def _pool_bwd_window(dpool, dpool_cnt, *, name, tr, into):
    rows = dpool.shape[0]
    hpt = tr // HALO
    n_halo = rows // HALO
    n_tiles = rows // tr

    def fn(i, dp, dc, dnext):
        dnext = jnp.where(i < n_tiles - 1, dnext, 0.0)
        ext = jnp.concatenate([dc, dnext], axis=0)
        n = tr + HALO
        s2 = ext + pltpu.roll(ext, n - 1, 0)
        s4 = s2 + pltpu.roll(s2, n - 2, 0)
        s8 = s4 + pltpu.roll(s4, n - 4, 0)
        s16 = s8 + pltpu.roll(s8, n - 8, 0)
        out = []
        for g, sw in enumerate((s2, s4, s8, s16)):
            lanes = slice(g * POOL_GROUP, (g + 1) * POOL_GROUP)
            out.append(sw[:tr, lanes] - dp[:, lanes])
        return jnp.concatenate(out, axis=1)

    ins = [_rspec(dpool, tr), _rspec(dpool_cnt, tr),
           (dpool_cnt, (HALO, POOL_WIDTH), lambda i: (jnp.minimum((i + 1) * hpt, n_halo - 1), 0))]
    return _row_call(fn, rows, tr, ins, [(POOL_WIDTH, BF16)], name=name, into=into)[0]


def _causal_pairs(n, k_major):
    if k_major:
        pairs = [(qi, ki) for ki in range(n) for qi in range(ki, n)]
    else:
        pairs = [(qi, ki) for qi in range(n) for ki in range(qi + 1)]
    return (jnp.asarray(np.array([p[0] for p in pairs], np.int32)),
            jnp.asarray(np.array([p[1] for p in pairs], np.int32)), len(pairs))


SUBLANES = 8
NT_DIMS = (((1,), (1,)), ((), ()))
TN_DIMS = (((0,), (0,)), ((), ()))


ATTN_BLOCK = 1024
ATTN_BLOCK_FWD = 2048
QUERY_CHUNK = 256
LOG2_E = 1.4426950408889634
EXP2_SCALE = ATTN_SCALE * LOG2_E


def _scores_t(q_c, k, c, qc, diag):
    s = lax.dot_general(k, q_c, NT_DIMS, preferred_element_type=F32)
    if diag:
        key = lax.broadcasted_iota(jnp.int32, s.shape, 0)
        qry = lax.broadcasted_iota(jnp.int32, s.shape, 1) + c * qc
        s = jnp.where(key <= qry, s, NEG_INF)
    return s


def _flash_fwd(q, k, v, *, name, blk):
    T = q.shape[0]
    n = T // blk
    qc = min(QUERY_CHUNK, blk)
    qtab, ktab, n_pairs = _causal_pairs(n, k_major=False)

    def body(qt, kt, q_ref, k_ref, v_ref, o_ref, lse_ref, m_s, l_s, acc_s):
        p = pl.program_id(1)
        qi, ki = qt[p], kt[p]

        @pl.when(ki == 0)
        def _():
            m_s[...] = jnp.full_like(m_s, NEG_INF)
            l_s[...] = jnp.zeros_like(l_s)
            acc_s[...] = jnp.zeros_like(acc_s)

        def step(diag):
            kv, vv = k_ref[...], v_ref[...]
            chunks = [slice(c * qc, (c + 1) * qc) for c in range(blk // qc)]
            scores = [_scores_t(q_ref[rows, :], kv, c, qc, diag) for c, rows in enumerate(chunks)]
            probs, alphas = [], []
            for rows, s_t in zip(chunks, scores):
                m_prev = m_s[:, rows]
                m_new = jnp.maximum(m_prev, jnp.max(s_t, axis=0, keepdims=True))
                p_t = jnp.exp2(s_t - m_new)
                alpha = jnp.exp2(m_prev - m_new)
                l_s[:, rows] = alpha * l_s[:, rows] + jnp.sum(p_t, axis=0, keepdims=True)
                m_s[:, rows] = m_new
                probs.append(p_t.astype(BF16))
                alphas.append(alpha)
            for rows, p_t, alpha in zip(chunks, probs, alphas):
                acc_s[:, rows] = alpha * acc_s[:, rows] + lax.dot_general(vv, p_t, TN_DIMS, preferred_element_type=F32)

        @pl.when(ki != qi)
        def _():
            step(False)

        @pl.when(ki == qi)
        def _():
            step(True)
            o_ref[...] = (acc_s[...] / l_s[...]).T.astype(o_ref.dtype)
            lse2 = m_s[...] + jnp.log2(l_s[...])
            lse_ref[...] = jnp.broadcast_to(lse2, lse_ref.shape)

    grid_spec = pltpu.PrefetchScalarGridSpec(
        num_scalar_prefetch=2, grid=(N_HEADS, n_pairs),
        in_specs=[pl.BlockSpec((blk, HEAD_PAD), lambda h, p, qt, kt: (qt[p], h)),
                  pl.BlockSpec((blk, HEAD_PAD), lambda h, p, qt, kt: (kt[p], h)),
                  pl.BlockSpec((blk, V_HEAD), lambda h, p, qt, kt: (kt[p], h))],
        out_specs=[pl.BlockSpec((blk, V_HEAD), lambda h, p, qt, kt: (qt[p], h)),
                   pl.BlockSpec((SUBLANES, blk), lambda h, p, qt, kt: (h, qt[p]))],
        scratch_shapes=[pltpu.VMEM((1, blk), F32), pltpu.VMEM((1, blk), F32), pltpu.VMEM((V_HEAD, blk), F32)])
    return pl.pallas_call(body, grid_spec=grid_spec,
                          out_shape=[jax.ShapeDtypeStruct((T, N_HEADS * V_HEAD), BF16),
                                     jax.ShapeDtypeStruct((N_HEADS * SUBLANES, T), F32)],
                          name=name, compiler_params=_cparams("parallel", "arbitrary"))(qtab, ktab, q, k, v)


def _attn_delta(do, o, *, name, tr):
    T = do.shape[0]

    def body(do_ref, o_ref, d_ref):
        prod = do_ref[...].astype(F32) * o_ref[...].astype(F32)
        lane_head = lax.broadcasted_iota(jnp.int32, (tr, LANES), 1) // SUBLANES
        mat = jnp.zeros((tr, LANES), F32)
        for h in range(N_HEADS):
            d_h = jnp.sum(prod[:, h * V_HEAD:(h + 1) * V_HEAD], axis=1, keepdims=True)
            mat = jnp.where(lane_head == h, d_h, mat)
        d_ref[...] = mat.T[:N_HEADS * SUBLANES, :]

    return pl.pallas_call(body, grid=(T // tr,),
                          in_specs=[pl.BlockSpec((tr, N_HEADS * V_HEAD), lambda i: (i, 0)),
                                    pl.BlockSpec((tr, N_HEADS * V_HEAD), lambda i: (i, 0))],
                          out_specs=pl.BlockSpec((N_HEADS * SUBLANES, tr), lambda i: (0, i)),
                          out_shape=jax.ShapeDtypeStruct((N_HEADS * SUBLANES, T), F32), name=name,
                          compiler_params=_cparams("parallel"))(do, o)


def _flash_bwd(q, k, v, lse, delta, do, cos_t, sin_t, *, name, blk):
    T = q.shape[0]
    n = T // blk
    qc = min(QUERY_CHUNK, blk)
    qtab, ktab, n_pairs = _causal_pairs(n, k_major=True)

    def body(qt, kt, q_ref, k_ref, v_ref, lse_ref, delta_ref, do_ref, cos_ref, sin_ref, dq_ref, dk_ref, dv_ref,
             dq_s, dk_s, dv_s):
        p = pl.program_id(1)
        qi, ki = qt[p], kt[p]
        first = qi == ki

        @pl.when(p == 0)
        def _():
            dq_s[...] = jnp.zeros_like(dq_s)

        @pl.when(first)
        def _():
            dk_s[...] = jnp.zeros_like(dk_s)
            dv_s[...] = jnp.zeros_like(dv_s)

        def step(diag):
            kv, vv = k_ref[...], v_ref[...]
            chunks = [slice(c * qc, (c + 1) * qc) for c in range(blk // qc)]
            qs = [q_ref[rows, :] for rows in chunks]
            dos = [do_ref[rows, :] for rows in chunks]
            scores = [_scores_t(q_c, kv, c, qc, diag) for c, q_c in enumerate(qs)]
            dps = [lax.dot_general(vv, do_c, NT_DIMS, preferred_element_type=F32) for do_c in dos]
            probs, dss = [], []
            for rows, s_t, dp_t in zip(chunks, scores, dps):
                p_t = jnp.exp2(s_t - lse_ref[0:1, rows])
                dss.append((p_t * (dp_t - delta_ref[0:1, rows])).astype(BF16))
                probs.append(p_t.astype(BF16))
            dv_acc = dk_acc = None
            for p_t, ds_t, q_c, do_c in zip(probs, dss, qs, dos):
                dv_c = jnp.dot(p_t, do_c, preferred_element_type=F32)
                dk_c = jnp.dot(ds_t, q_c, preferred_element_type=F32)
                dv_acc = dv_c if dv_acc is None else dv_acc + dv_c
                dk_acc = dk_c if dk_acc is None else dk_acc + dk_c
            for rows, ds_t in zip(chunks, dss):
                dq_s[qi, :, rows] += lax.dot_general(kv, ds_t, TN_DIMS, preferred_element_type=F32)
            dv_s[...] += dv_acc
            dk_s[...] += dk_acc

        @pl.when(jnp.logical_not(first))
        def _():
            step(False)

        @pl.when(first)
        def _():
            step(True)
            dq_t = (dq_s[qi] * ATTN_SCALE).T
            dq_ref[:, :QK_NOPE] = dq_t[:, :QK_NOPE].astype(dq_ref.dtype)
            dq_ref[:, QK_NOPE:] = _rope_bwd(dq_t[:, QK_NOPE:], cos_ref[...], sin_ref[...]).astype(dq_ref.dtype)

        @pl.when(qi == n - 1)
        def _():
            dk_ref[...] = (dk_s[...] * (1.0 / LOG2_E)).astype(dk_ref.dtype)
            dv_ref[...] = dv_s[...].astype(dv_ref.dtype)

    qmap = lambda h, p, qt, kt: (qt[p], h)
    kmap = lambda h, p, qt, kt: (kt[p], h)
    smap = lambda h, p, qt, kt: (h, qt[p])
    tmap = lambda h, p, qt, kt: (kt[p], 0)
    grid_spec = pltpu.PrefetchScalarGridSpec(
        num_scalar_prefetch=2, grid=(N_HEADS, n_pairs),
        in_specs=[pl.BlockSpec((blk, HEAD_PAD), qmap), pl.BlockSpec((blk, HEAD_PAD), kmap),
                  pl.BlockSpec((blk, V_HEAD), kmap), pl.BlockSpec((SUBLANES, blk), smap),
                  pl.BlockSpec((SUBLANES, blk), smap), pl.BlockSpec((blk, V_HEAD), qmap),
                  pl.BlockSpec((blk, LANES), tmap), pl.BlockSpec((blk, LANES), tmap)],
        out_specs=[pl.BlockSpec((blk, HEAD_PAD), kmap), pl.BlockSpec((blk, HEAD_PAD), kmap),
                   pl.BlockSpec((blk, V_HEAD), kmap)],
        scratch_shapes=[pltpu.VMEM((n, HEAD_PAD, blk), F32), pltpu.VMEM((blk, HEAD_PAD), F32),
                        pltpu.VMEM((blk, V_HEAD), F32)])
    return pl.pallas_call(body, grid_spec=grid_spec,
                          out_shape=[jax.ShapeDtypeStruct((T, N_HEADS * HEAD_PAD), BF16),
                                     jax.ShapeDtypeStruct((T, N_HEADS * HEAD_PAD), BF16),
                                     jax.ShapeDtypeStruct((T, N_HEADS * V_HEAD), BF16)],
                          name=name, compiler_params=_cparams("arbitrary", "arbitrary"))(
                              qtab, ktab, q, k, v, lse, delta, do, cos_t, sin_t)


MESH_ID = pl.DeviceIdType.MESH
ANY_SPEC = pl.BlockSpec(memory_space=pl.ANY)


def _other_chips(x, y):
    out = []
    for dx, dy in ((1, 0), (0, 1), (1, 1)):
        px = x ^ dx if dx else x
        py = y ^ dy if dy else y
        out.append((px, py, 2 * px + py))
    return out


def _gather_weights(flat):
    rh = flat.shape[0] // 2
    rq = rh // 2

    def body(src2, out, send_sems, recv_sems):
        x, y, c = lax.axis_index("x"), lax.axis_index("y"), lax.axis_index("c")
        me = 2 * x + y
        sib = (x, y, 1 - c)
        (xx, xy, kx), (yx, yy, ky), (_, _, kd) = _other_chips(x, y)
        x_nbr, y_nbr = (xx, xy, c), (yx, yy, c)
        first, last = pl.ds(0, rq), pl.ds(rq, rq)

        def copy(j, src, dst, to):
            return pltpu.make_async_remote_copy(src_ref=src, dst_ref=dst, send_sem=send_sems.at[j], recv_sem=recv_sems.at[j],
                                                device_id=to, device_id_type=MESH_ID)

        def arrived(j, land):
            copy(j, land, land, sib).wait_recv()

        sends = [copy(0, src2.at[c], out.at[me, c], x_nbr), copy(1, src2.at[c], out.at[me, c], y_nbr)]
        for cp in sends:
            cp.start()
        landings = [(0, out.at[kx, c]), (1, out.at[ky, c]), (2, out.at[kd, c, first]), (3, out.at[kd, c, last])]
        relays = {0: (2, out.at[kx, c, first], y_nbr), 1: (3, out.at[ky, c, last], x_nbr)}
        for j, land in landings:
            arrived(j, land)
            if j in relays:
                rj, piece, to = relays[j]
                sends.append(copy(rj, piece, piece, to))
                sends[-1].start()
            sends.append(copy(4 + j, land, land, sib))
            sends[-1].start()
        for j, land in [(0, out.at[kx, 1 - c]), (1, out.at[ky, 1 - c]), (2, out.at[kd, 1 - c, first]),
                        (3, out.at[kd, 1 - c, last])]:
            arrived(4 + j, land)
        for cp in sends:
            cp.wait_send()

    return pl.pallas_call(body, out_shape=jax.ShapeDtypeStruct((N_CHIPS, 2, rh, PACK_LANES), flat.dtype),
                          in_specs=[ANY_SPEC], out_specs=ANY_SPEC,
                          scratch_shapes=[pltpu.SemaphoreType.DMA((8,)), pltpu.SemaphoreType.DMA((8,))],
                          name="gather_weights")(flat.reshape(2, rh, PACK_LANES))


def _swap_halves(gs):
    rh = gs[0].shape[1]

    def body(*refs):
        srcs, out, send_sems, recv_sems = refs[:N_CHIPS], refs[N_CHIPS], refs[N_CHIPS + 1], refs[N_CHIPS + 2]
        x, y, c = lax.axis_index("x"), lax.axis_index("y"), lax.axis_index("c")
        copies = [pltpu.make_async_remote_copy(src_ref=srcs[k].at[1 - c], dst_ref=out.at[k], send_sem=send_sems.at[k],
                                               recv_sem=recv_sems.at[k], device_id=(x, y, 1 - c), device_id_type=MESH_ID)
                  for k in range(N_CHIPS)]
        for cp in copies:
            cp.start()
        for cp in copies:
            cp.wait()

    return pl.pallas_call(body, out_shape=jax.ShapeDtypeStruct((N_CHIPS, rh, PACK_LANES), gs[0].dtype),
                          in_specs=[ANY_SPEC] * N_CHIPS, out_specs=ANY_SPEC,
                          scratch_shapes=[pltpu.SemaphoreType.DMA((N_CHIPS,)), pltpu.SemaphoreType.DMA((N_CHIPS,))],
                          name="grad_swap_halves")(*gs)


def _add_halves(gs, got, core, *, tr):
    rh = gs[0].shape[1]

    def body(*refs):
        g_refs, got_ref, o_ref = refs[1:1 + N_CHIPS], refs[1 + N_CHIPS], refs[2 + N_CHIPS]
        for k in range(N_CHIPS):
            o_ref[k] = (g_refs[k][0] + got_ref[k]).astype(o_ref.dtype)

    grid_spec = pltpu.PrefetchScalarGridSpec(
        num_scalar_prefetch=1, grid=(rh // tr,),
        in_specs=[pl.BlockSpec((1, tr, PACK_LANES), lambda i, c_ref: (c_ref[0], i, 0))] * N_CHIPS
        + [pl.BlockSpec((N_CHIPS, tr, PACK_LANES), lambda i, c_ref: (0, i, 0))],
        out_specs=pl.BlockSpec((N_CHIPS, tr, PACK_LANES), lambda i, c_ref: (0, i, 0)))
    return pl.pallas_call(body, grid_spec=grid_spec,
                          out_shape=jax.ShapeDtypeStruct((N_CHIPS, rh, PACK_LANES), BF16), name="grad_add_halves",
                          compiler_params=_cparams("parallel"))(core.reshape(1), *gs, got)


def _scatter_partials(part):
    rh = part.shape[1]

    def body(src, out, send_sems, recv_sems, local_sem):
        x, y, c = lax.axis_index("x"), lax.axis_index("y"), lax.axis_index("c")
        me = 2 * x + y
        own = pltpu.make_async_copy(src.at[me], out.at[me], local_sem)
        own.start()
        chips = _other_chips(x, y)
        sends = []
        for j, (px, py, pk) in enumerate(chips):
            cp = pltpu.make_async_remote_copy(src_ref=src.at[pk], dst_ref=out.at[me], send_sem=send_sems.at[j],
                                              recv_sem=recv_sems.at[j], device_id=(px, py, c), device_id_type=MESH_ID)
            cp.start()
            sends.append(cp)
        for j, (px, py, pk) in enumerate(chips):
            land = out.at[pk]
            pltpu.make_async_remote_copy(src_ref=land, dst_ref=land, send_sem=send_sems.at[j], recv_sem=recv_sems.at[j],
                                         device_id=(px, py, c), device_id_type=MESH_ID).wait_recv()
        for cp in sends:
            cp.wait_send()
        own.wait()

    return pl.pallas_call(body, out_shape=jax.ShapeDtypeStruct((N_CHIPS, rh, PACK_LANES), part.dtype),
                          in_specs=[ANY_SPEC], out_specs=ANY_SPEC,
                          scratch_shapes=[pltpu.SemaphoreType.DMA((3,)), pltpu.SemaphoreType.DMA((3,)),
                                          pltpu.SemaphoreType.DMA(())],
                          name="grad_scatter_partials")(part)


def _sum_chips(q, *, tr):
    rh = q.shape[1]

    def body(q_ref, o_ref):
        parts = [q_ref[k].astype(F32) for k in range(N_CHIPS)]
        o_ref[...] = ((parts[0] + parts[1]) + parts[2]) + parts[3]

    return pl.pallas_call(body, grid=(rh // tr,),
                          in_specs=[pl.BlockSpec((N_CHIPS, tr, PACK_LANES), lambda i: (0, i, 0))],
                          out_specs=pl.BlockSpec((tr, PACK_LANES), lambda i: (i, 0)),
                          out_shape=jax.ShapeDtypeStruct((rh, PACK_LANES), F32), name="grad_sum_chips",
                          compiler_params=_cparams("parallel"))(q)


def _send_half(half):
    def body(src, out, send_sem, recv_sem):
        x, y, c = lax.axis_index("x"), lax.axis_index("y"), lax.axis_index("c")
        cp = pltpu.make_async_remote_copy(src_ref=src, dst_ref=out, send_sem=send_sem, recv_sem=recv_sem,
                                          device_id=(x, y, 1 - c), device_id_type=MESH_ID)
        cp.start()
        cp.wait()

    return pl.pallas_call(body, out_shape=jax.ShapeDtypeStruct(half.shape, half.dtype),
                          in_specs=[ANY_SPEC], out_specs=ANY_SPEC,
                          scratch_shapes=[pltpu.SemaphoreType.DMA(()), pltpu.SemaphoreType.DMA(())],
                          name="grad_send_half")(half)


def _both_halves(own, got, core, *, tr):
    rh = own.shape[0]

    def body(c_ref, own_ref, got_ref, o_ref):
        mine = pl.program_id(0) == c_ref[0]
        o_ref[0] = jnp.where(mine, own_ref[...], got_ref[...])

    grid_spec = pltpu.PrefetchScalarGridSpec(
        num_scalar_prefetch=1, grid=(2, rh // tr),
        in_specs=[pl.BlockSpec((tr, PACK_LANES), lambda h, i, c_ref: (i, 0)),
                  pl.BlockSpec((tr, PACK_LANES), lambda h, i, c_ref: (i, 0))],
        out_specs=pl.BlockSpec((1, tr, PACK_LANES), lambda h, i, c_ref: (h, i, 0)))
    return pl.pallas_call(body, grid_spec=grid_spec, out_shape=jax.ShapeDtypeStruct((2, rh, PACK_LANES), own.dtype),
                          name="grad_both_halves", compiler_params=_cparams("parallel", "parallel"))(core.reshape(1), own, got)


def _allreduce_small(v):
    rows = v.shape[0]

    def body(v_ref, o_ref, buf, send_sems, recv_sems):
        x, y, c = lax.axis_index("x"), lax.axis_index("y"), lax.axis_index("c")
        me = 4 * x + 2 * y + c
        buf[me] = v_ref[...]
        sends = []
        for j in range(1, N_DEV):
            px, py, pc = x ^ ((j >> 2) & 1), y ^ ((j >> 1) & 1), c ^ (j & 1)
            cp = pltpu.make_async_remote_copy(src_ref=v_ref, dst_ref=buf.at[me], send_sem=send_sems.at[j - 1],
                                              recv_sem=recv_sems.at[j - 1], device_id=(px, py, pc), device_id_type=MESH_ID)
            cp.start()
            sends.append(cp)
        for j in range(1, N_DEV):
            px, py, pc = x ^ ((j >> 2) & 1), y ^ ((j >> 1) & 1), c ^ (j & 1)
            land = buf.at[4 * px + 2 * py + pc]
            pltpu.make_async_remote_copy(src_ref=land, dst_ref=land, send_sem=send_sems.at[j - 1],
                                         recv_sem=recv_sems.at[j - 1], device_id=(px, py, pc),
                                         device_id_type=MESH_ID).wait_recv()
        for cp in sends:
            cp.wait_send()
        acc = buf[0]
        for d in range(1, N_DEV):
            acc = acc + buf[d]
        o_ref[...] = acc

    vm = pl.BlockSpec(memory_space=pltpu.VMEM)
    return pl.pallas_call(body, out_shape=jax.ShapeDtypeStruct((rows, LANES), F32), in_specs=[vm], out_specs=vm,
                          scratch_shapes=[pltpu.VMEM((N_DEV, rows, LANES), F32), pltpu.SemaphoreType.DMA((N_DEV - 1,)),
                                          pltpu.SemaphoreType.DMA((N_DEV - 1,))],
                          name="allreduce_small")(v)


def _adamw(w, g, m, v, *, name):
    shape = w.shape
    cols = shape[-1] if w.ndim > 1 else shape[0]
    rows = w.size // cols
    w2, g2, m2, v2 = (t.reshape(rows, cols) for t in (w, g, m, v))
    tr = rows if rows <= 256 else _tile(rows, 256, 8)

    def fn(i, wv, gv, mv, vv):
        mn = ADAM_B1 * mv + (1.0 - ADAM_B1) * gv
        vn = ADAM_B2 * vv + (1.0 - ADAM_B2) * (gv * gv)
        m_hat = mn / (1.0 - ADAM_B1 ** ADAM_STEP)
        v_hat = vn / (1.0 - ADAM_B2 ** ADAM_STEP)
        delta = -ADAM_LR * (m_hat / (jnp.sqrt(v_hat) + ADAM_EPS) + ADAM_WD * wv)
        return delta, mn, vn

    ins = [_rspec(t, tr) for t in (w2, g2, m2, v2)]
    d, mn, vn = _row_call(fn, rows, tr, ins, [(cols, F32)] * 3, name=name)
    return d.reshape(shape), mn.reshape(shape), vn.reshape(shape)


def _rope_cols(w):
    z = jnp.zeros(w.shape[:-1] + (32,), w.dtype)
    return jnp.concatenate([w[..., :32], z, w[..., 32:], z], axis=-1)


def _rope_cols_inv(w):
    return jnp.concatenate([w[..., :32], w[..., 64:96]], axis=-1)


def _layer_layouts(W, i):
    w_in = W['w_in'][i]
    u, cq, ckv = w_in[:, :512], w_in[:, 512:1024], w_in[:, 1024:1280]
    kr, ga, gb = w_in[:, 1280:1344], w_in[:, 1344:2368], w_in[:, 2368:]
    L = {}
    L['w_in'] = jnp.concatenate([ga, gb, u, cq, ckv, _rope_cols(kr)], axis=1)
    wq = W['w_uq'][i]
    L['w_q'] = jnp.concatenate([wq[..., :QK_NOPE], _rope_cols(wq[..., QK_NOPE:])], axis=-1).reshape(Q_LORA, -1)
    wkv = W['w_ukv'][i]
    L['w_k'] = jnp.concatenate([wkv[..., :QK_NOPE], jnp.zeros_like(wkv[..., :LANES])], axis=-1).reshape(KV_LORA, -1)
    L['w_v'] = wkv[..., QK_NOPE:].reshape(KV_LORA, -1)
    L['w_gu'] = jnp.concatenate([W['w_gate'][i], W['w_up'][i]], axis=1)
    for n in ('w_a', 'w_b', 'w_o', 'w_down', 'w_ple_gate', 'w_ple'):
        L[n] = W[n][i]
    return L


def _layer_grads_to_reference_layout(G):
    d = G['w_in']
    ga, gb, u = d[:, ZC_GA:ZC_GB], d[:, ZC_GB:ZC_U], d[:, ZC_U:ZC_CQ]
    cq, ckv, kr = d[:, ZC_CQ:ZC_CKV], d[:, ZC_CKV:ZC_KR], _rope_cols_inv(d[:, ZC_KR:])
    out = {'w_in': jnp.concatenate([u, cq, ckv, kr, ga, gb], axis=1)}
    dq = G['w_q'].reshape(Q_LORA, N_HEADS, HEAD_PAD)
    out['w_uq'] = jnp.concatenate([dq[..., :QK_NOPE], _rope_cols_inv(dq[..., QK_NOPE:])], axis=-1)
    dk = G['w_k'].reshape(KV_LORA, N_HEADS, HEAD_PAD)[..., :QK_NOPE]
    dv = G['w_v'].reshape(KV_LORA, N_HEADS, V_HEAD)
    out['w_ukv'] = jnp.concatenate([dk, dv], axis=-1)
    out['w_gate'], out['w_up'] = G['w_gu'][:, :D_FF], G['w_gu'][:, D_FF:]
    for n in ('w_a', 'w_b', 'w_o', 'w_down', 'w_ple_gate', 'w_ple'):
        out[n] = G[n]
    return out


PACK_ROWS = 2048


def _pack_rows(parts, row_mult):
    flat = jnp.concatenate([p.reshape(-1) for p in parts])
    n = flat.shape[0]
    per = LANES * row_mult
    padded = -(-n // per) * per
    return jnp.pad(flat, (0, padded - n)).reshape(-1, LANES)


def _unpack_rows(flat2d, shapes):
    flat = flat2d.reshape(-1)
    out, off = [], 0
    for s in shapes:
        n = int(np.prod(s))
        out.append(flat[off:off + n].reshape(s))
        off += n
    return out


PACK_LANES = 256


def _lane_blocks(cols):
    return -(-cols // PACK_LANES)


def _pack_blocks(parts, row_mult):
    blocks = []
    for p in parts:
        p2 = p.reshape(-1, p.shape[-1])
        cols = p2.shape[1]
        nb = _lane_blocks(cols)
        if nb * PACK_LANES != cols:
            p2 = jnp.pad(p2, ((0, 0), (0, nb * PACK_LANES - cols)))
        blocks += [p2[:, j * PACK_LANES:(j + 1) * PACK_LANES] for j in range(nb)]
    buf = jnp.concatenate(blocks, axis=0)
    rows = buf.shape[0]
    padded = -(-rows // row_mult) * row_mult
    return buf if padded == rows else jnp.pad(buf, ((0, padded - rows), (0, 0)))


def _unpack_blocks(buf, shapes):
    out, off = [], 0
    for s in shapes:
        rows, cols = int(np.prod(s[:-1])), s[-1]
        nb = _lane_blocks(cols)
        piece = jnp.concatenate([buf[off + j * rows:off + (j + 1) * rows] for j in range(nb)], axis=1)
        out.append(piece[:, :cols].reshape(s))
        off += nb * rows
    return out


def _layer_fwd(i, x, p_i, L, norms, w_pool_bf, pool_scale, cos_t, sin_t, tr, blk):
    sv = {'x': x}
    z, sv['h'] = _mm_nn(x, L['w_in'], name=f"l{i}_in_proj", outs=[(Z_WIDTH, F32)], gain=norms['norm_mix'], emit_a=True,
                        tm=tr // 2)
    sv['z'] = z
    sv['pooled'], sv['mixed'], sv['pm'] = _pool_fwd(z, w_pool_bf, pool_scale, name=f"l{i}_pool", tr=tr)
    sv['ya'] = _mm_nn(sv['pm'], L['w_a'], name=f"l{i}_ya", outs=[(D_MODEL, BF16)], tm=tr)

    def heads(acc, rope_part):
        out = []
        for h in range(N_HEADS):
            out.append(acc[:, h * HEAD_PAD:h * HEAD_PAD + QK_NOPE])
            out.append(rope_part(acc[:, h * HEAD_PAD + QK_NOPE:(h + 1) * HEAD_PAD]))
        return jnp.concatenate(out, axis=1)

    def q_epi(acc, ct, st):
        return (heads(acc * EXP2_SCALE, lambda t: _rope(t, ct, st)),)

    def k_epi(acc, kr, ct, st):
        k_pe = _rope(kr, ct, st)
        return (heads(acc, lambda t: k_pe),)

    rope_rows = [(cos_t, LANES, 0), (sin_t, LANES, 0)]
    qk_width = N_HEADS * HEAD_PAD
    sv['q'], sv['cqn'] = _mm_nn(z, L['w_q'], name=f"l{i}_q_proj", outs=[(qk_width, BF16)], a_col=ZC_CQ,
                                gain=norms['q_norm'], emit_a=True, epi=q_epi, epi_rows=rope_rows, tm=tr)
    sv['k'], sv['ckvn'] = _mm_nn(z, L['w_k'], name=f"l{i}_k_proj", outs=[(qk_width, BF16)], a_col=ZC_CKV,
                                 gain=norms['kv_norm'], emit_a=True, epi=k_epi,
                                 epi_rows=[(z, LANES, ZC_KR // LANES)] + rope_rows, tm=tr)
    sv['v'] = _mm_nn(sv['ckvn'], L['w_v'], name=f"l{i}_v_proj", outs=[(N_HEADS * V_HEAD, BF16)], tm=tr)
    sv['o'], sv['lse'] = _flash_fwd(sv['q'], sv['k'], sv['v'], name=f"l{i}_attn",
                                    blk=min(ATTN_BLOCK_FWD, max(blk * ATTN_BLOCK_FWD // ATTN_BLOCK, 128)))

    def merge_epi(yb, ga, gb, ya):
        return yb, _sigmoid(ga) * ya.astype(F32) + _sigmoid(gb) * yb

    sv['yb'], sv['merged'] = _mm_nn(sv['o'], L['w_b'], name=f"l{i}_yb_merge", outs=[(D_MODEL, BF16), (D_MODEL, BF16)],
                                    epi=merge_epi, epi_rows=[(z, D_MODEL, 0), (z, D_MODEL, 1), (sv['ya'], D_MODEL, 0)],
                                    tm=tr)

    def add_epi(acc, res):
        return (acc + res,)

    x1 = _mm_nn(sv['merged'], L['w_o'], name=f"l{i}_wo", outs=[(D_MODEL, F32)], epi=add_epi, epi_rows=[(x, D_MODEL, 0)],
                tm=tr)
    sv['x1'] = x1

    def swiglu_epi(acc):
        g, u = acc[:, :D_FF], acc[:, D_FF:]
        return acc, g * _sigmoid(g) * u

    sv['gu'], sv['act'], sv['h2'] = _mm_nn(x1, L['w_gu'], name=f"l{i}_gate_up", outs=[(2 * D_FF, BF16), (D_FF, BF16)],
                                           gain=norms['norm_ffn'], emit_a=True, epi=swiglu_epi, tm=tr // 2)
    x2 = _mm_nn(sv['act'], L['w_down'], name=f"l{i}_down", outs=[(D_MODEL, F32)], epi=add_epi,
                epi_rows=[(x1, D_MODEL, 0)], tm=tr)
    sv['x2'] = x2
    sv['logit'], sv['h3'] = _mm_nn(x2, L['w_ple_gate'], name=f"l{i}_ple_gate", outs=[(D_MODEL, F32)],
                                   gain=norms['norm_ple'], emit_a=True, tm=tr)

    def ple_epi(pe, xv, lg):
        return pe, xv + _sigmoid(lg) * pe

    sv['pe'], x3 = _mm_nn(p_i, L['w_ple'], name=f"l{i}_ple", outs=[(D_MODEL, F32), (D_MODEL, F32)], epi=ple_epi,
                          epi_rows=[(x2, D_MODEL, 0), (sv['logit'], D_MODEL, 0)], tm=tr)
    return x3, sv


def _layer_bwd(i, dx3, sv, p_i, L, norms, w_pool_bf, pool_scale, cos_t, sin_t, tr, blk):
    T = dx3.shape[0]
    G = {}
    z = sv['z']

    def ple_bwd(_, d, lg, pe):
        g = _sigmoid(lg)
        return d * pe * g * (1.0 - g), d * g

    dlogit, dpe = _row_call(ple_bwd, T, tr, [_rspec(dx3, tr), _rspec(sv['logit'], tr), _rspec(sv['pe'], tr)],
                            [(D_MODEL, BF16), (D_MODEL, BF16)], name=f"l{i}_ple_bwd")
    G['w_ple_gate'] = _mm_tn(sv['h3'], dlogit, name=f"l{i}_dw_ple_gate", tn=1024)
    G['w_ple'] = _mm_tn(p_i, dpe, name=f"l{i}_dw_ple", tn=1024)
    def gain_row(n):
        return norms[n].reshape(1, -1).astype(F32)

    dx2, dx2_bf, G['norm_ple'] = _mm_nt([(dlogit, L['w_ple_gate'])], name=f"l{i}_dh3_norm_bwd",
                                        outs=[(D_MODEL, F32), (D_MODEL, BF16)], epi=_rms_bwd_epi(True, True),
                                        epi_rows=[(sv['x2'], D_MODEL, 0), (dx3, D_MODEL, 0)], consts=[gain_row('norm_ple')],
                                        accs=[(1, D_MODEL)], tm=tr)

    def swiglu_bwd_epi(da, gu):
        g, u = gu[:, :D_FF].astype(F32), gu[:, D_FF:].astype(F32)
        sg = _sigmoid(g)
        return (jnp.concatenate([da * u * sg * (1.0 + g * (1.0 - sg)), da * g * sg], axis=1),)

    dgu = _mm_nt([(dx2_bf, L['w_down'])], name=f"l{i}_dact_swiglu_bwd", outs=[(2 * D_FF, BF16)], epi=swiglu_bwd_epi,
                 epi_rows=[(sv['gu'], 2 * D_FF, 0)], tm=tr // 2)
    G['w_down'] = _mm_tn(sv['act'], dx2_bf, name=f"l{i}_dw_down", tk=1408, tn=1024)
    G['w_gu'] = _mm_tn(sv['h2'], dgu, name=f"l{i}_dw_gate_up", tn=1408)
    dx1, dx1_bf, G['norm_ffn'] = _mm_nt([(dgu, L['w_gu'])], name=f"l{i}_dh2_norm_bwd",
                                        outs=[(D_MODEL, F32), (D_MODEL, BF16)], epi=_rms_bwd_epi(True, True),
                                        epi_rows=[(sv['x1'], D_MODEL, 0), (dx2, D_MODEL, 0)], consts=[gain_row('norm_ffn')],
                                        accs=[(1, D_MODEL)], tm=tr // 2)

    def merge_bwd_epi(dm, ga, gb, ya, yb):
        sa, sb = _sigmoid(ga), _sigmoid(gb)
        ya, yb = ya.astype(F32), yb.astype(F32)
        d_gates = jnp.concatenate([dm * ya * sa * (1.0 - sa), dm * yb * sb * (1.0 - sb)], axis=1)
        return d_gates, dm * sa, dm * sb

    dz, dya, dyb = _mm_nt([(dx1_bf, L['w_o'])], name=f"l{i}_dmerged_bwd",
                          outs=[(2 * D_MODEL, BF16), (D_MODEL, BF16), (D_MODEL, BF16)], epi=merge_bwd_epi,
                          epi_rows=[(z, D_MODEL, 0), (z, D_MODEL, 1), (sv['ya'], D_MODEL, 0), (sv['yb'], D_MODEL, 0)],
                          tm=tr, wide0=Z_WIDTH)
    G['w_o'] = _mm_tn(sv['merged'], dx1_bf, name=f"l{i}_dw_o", tn=1024)

    G['w_b'] = _mm_tn(sv['o'], dyb, name=f"l{i}_dw_b", tn=1024)
    do = _mm_nt([(dyb, L['w_b'])], name=f"l{i}_do", outs=[(D_MODEL, BF16)], tm=tr)
    delta = _attn_delta(do, sv['o'], name=f"l{i}_attn_delta", tr=tr)
    dq, dk, dv = _flash_bwd(sv['q'], sv['k'], sv['v'], sv['lse'], delta, do, cos_t, sin_t, name=f"l{i}_attn_bwd", blk=blk)

    def dk_rope(_, d, ct, st):
        d = d.astype(F32)
        acc = d[:, QK_NOPE:HEAD_PAD]
        for h in range(1, N_HEADS):
            acc = acc + d[:, h * HEAD_PAD + QK_NOPE:(h + 1) * HEAD_PAD]
        return _rope_bwd(acc, ct, st)

    dz = _row_call(dk_rope, T, tr, [_rspec(dk, tr), _rspec(cos_t, tr), _rspec(sin_t, tr)], [(LANES, BF16)],
                   name=f"l{i}_dk_rope", into=(dz, LANES, ZC_KR // LANES))[0]
    G['w_q'] = _mm_tn(sv['cqn'], dq, name=f"l{i}_dw_q", tn=1024)
    G['w_k'] = _mm_tn(sv['ckvn'], dk, name=f"l{i}_dw_k", tn=1024)
    G['w_v'] = _mm_tn(sv['ckvn'], dv, name=f"l{i}_dw_v", tn=1024)
    dz, G['q_norm'] = _mm_nt([(dq, L['w_q'])], name=f"l{i}_dcq", outs=[(Q_LORA, BF16)], epi=_rms_bwd_epi(False, False),
                             epi_rows=[(z, Q_LORA, ZC_CQ // Q_LORA)], consts=[gain_row('q_norm')], accs=[(1, Q_LORA)], tm=tr,
                             into=(dz, Q_LORA, ZC_CQ // Q_LORA))
    dz, G['kv_norm'] = _mm_nt([(dk, L['w_k']), (dv, L['w_v'])], name=f"l{i}_dckv", outs=[(KV_LORA, BF16)],
                              epi=_rms_bwd_epi(False, False), epi_rows=[(z, KV_LORA, ZC_CKV // KV_LORA)],
                              consts=[gain_row('kv_norm')], accs=[(1, KV_LORA)], tm=tr,
                              into=(dz, KV_LORA, ZC_CKV // KV_LORA))

    G['w_a'] = _mm_tn(sv['pm'], dya, name=f"l{i}_dw_a", tn=1024)
    dpm = _mm_nt([(dya, L['w_a'])], name=f"l{i}_dpm", outs=[(POOL_WIDTH, F32)], tm=tr)
    dpool, dpool_cnt, G['pool_scale'], G['w_pool'] = _pool_bwd_mix(dpm, sv['mixed'], sv['pooled'], w_pool_bf, pool_scale,
                                                                   name=f"l{i}_pool_bwd_mix", tr=tr)
    dz = _pool_bwd_window(dpool, dpool_cnt, name=f"l{i}_pool_bwd_window", tr=tr, into=(dz, POOL_WIDTH, ZC_U // POOL_WIDTH))

    G['w_in'] = _mm_tn(sv['h'], dz, name=f"l{i}_dw_in", tn=1152)
    dx, G['norm_mix'] = _mm_nt([(dz, L['w_in'])], name=f"l{i}_dh_norm_bwd", outs=[(D_MODEL, F32)],
                               epi=_rms_bwd_epi(True, False), epi_rows=[(sv['x'], D_MODEL, 0), (dx1, D_MODEL, 0)],
                               consts=[gain_row('norm_mix')], accs=[(1, D_MODEL)], tm=tr)
    return dx, G


def kernel(x, p, positions, norm_mix, w_in, w_pool, pool_scale, q_norm, kv_norm, w_uq, w_ukv, w_a, w_b, w_o, norm_ffn, w_gate, w_up, w_down, norm_ple, w_ple_gate, w_ple, final_norm, loss_target, m_norm_mix, m_w_in, m_w_pool, m_pool_scale, m_q_norm, m_kv_norm, m_w_uq, m_w_ukv, m_w_a, m_w_b, m_w_o, m_norm_ffn, m_w_gate, m_w_up, m_w_down, m_norm_ple, m_w_ple_gate, m_w_ple, m_final_norm, v_norm_mix, v_w_in, v_w_pool, v_pool_scale, v_q_norm, v_kv_norm, v_w_uq, v_w_ukv, v_w_a, v_w_b, v_w_o, v_norm_ffn, v_w_gate, v_w_up, v_w_down, v_norm_ple, v_w_ple_gate, v_w_ple, v_final_norm):
    given = dict(locals())
    weights = {n: given[n] for n in WEIGHTS}
    T = x.shape[1]
    tr = min(512, max(T // 2, 8))
    blk = min(ATTN_BLOCK, max(T // 4, 128))
    x0 = x.reshape(T, D_MODEL)
    target = loss_target.reshape(T, D_MODEL)

    names = list(SHARDED)
    shard_shapes = [weights[n].shape for n in names]
    flat = _pack_blocks([weights[n].astype(BF16) for n in names], row_mult=PACK_ROWS)
    R = flat.shape[0]
    chip = (2 * lax.axis_index("x") + lax.axis_index("y")).astype(jnp.int32)
    core = lax.axis_index("c").astype(jnp.int32)
    gathered = _gather_weights(flat).reshape(N_CHIPS, R, PACK_LANES)
    gathered = lax.dynamic_update_slice(gathered, flat.reshape(1, R, PACK_LANES), (chip, 0, 0))
    per_chip = [_unpack_blocks(gathered[k], shard_shapes) for k in range(N_CHIPS)]
    W = {n: jnp.concatenate([per_chip[k][j] for k in range(N_CHIPS)], axis=SHARDED[n]) for j, n in enumerate(names)}
    layouts = [_layer_layouts(W, i) for i in range(DEPTH)]
    w_pool_bf = w_pool.astype(BF16)

    inv_freq = 1.0 / (ROPE_THETA ** (jnp.arange(0, QK_ROPE, 2, dtype=F32) / QK_ROPE))
    zero32 = jnp.zeros((32,), F32)
    freq_row = jnp.concatenate([inv_freq, zero32, inv_freq, zero32]).reshape(1, LANES)
    cos_mask = jnp.concatenate([jnp.ones((32,), F32), zero32, jnp.ones((32,), F32), zero32]).reshape(1, LANES)
    sin_sign = jnp.concatenate([-jnp.ones((32,), F32), zero32, jnp.ones((32,), F32), zero32]).reshape(1, LANES)

    def rope_tables(_, pos, fr, cm, ss):
        ang = pos.astype(F32) * fr
        return jnp.cos(ang) * cm, jnp.sin(ang) * ss

    pos_col = positions.reshape(T, 1)
    cos_t, sin_t = _row_call(rope_tables, T, tr, [_rspec(pos_col, tr), _bspec(freq_row), _bspec(cos_mask), _bspec(sin_sign)],
                             [(LANES, F32), (LANES, F32)], name="rope_tables")

    xs = x0
    saved = []
    for i in range(DEPTH):
        norms = {n: weights[n][i] for n in ('norm_mix', 'q_norm', 'kv_norm', 'norm_ffn', 'norm_ple')}
        xs, sv = _layer_fwd(i, xs, p[i, 0], layouts[i], norms, w_pool_bf[i], pool_scale[i], cos_t, sin_t, tr, blk)
        saved.append((sv, norms))

    def head(_, xv, tv, gv):
        rstd = lax.rsqrt(jnp.mean(xv * xv, axis=-1, keepdims=True) + EPS)
        xhat = xv * rstd
        err = xhat * gv - tv
        loss = 0.5 * jnp.sum(jnp.mean(err * err, axis=-1, keepdims=True), axis=0, keepdims=True)
        dy = err * (1.0 / D_MODEL)
        dg = jnp.sum(dy * xhat, axis=0, keepdims=True)
        dxh = dy * gv
        dx = rstd * (dxh - xhat * jnp.mean(dxh * xhat, axis=-1, keepdims=True))
        return dx, jnp.broadcast_to(loss, (1, LANES)), dg

    dx, loss_part, g_final = _row_call(head, T, tr, [_rspec(xs, tr), _rspec(target, tr), _bspec(final_norm.reshape(1, D_MODEL))],
                                       [(D_MODEL, F32)], [(1, LANES), (1, D_MODEL)], name="loss_head")
    loss = lax.psum(loss_part[0, 0], ("x", "y", "c"))

    layer_grads = [None] * DEPTH
    for i in reversed(range(DEPTH)):
        sv, norms = saved[i]
        dx, layer_grads[i] = _layer_bwd(i, dx, sv, p[i, 0], layouts[i], norms, w_pool_bf[i], pool_scale[i], cos_t, sin_t, tr,
                                        blk)
    grad_x = dx.reshape(x.shape)

    ref_layout = [_layer_grads_to_reference_layout(g) for g in layer_grads]
    local = {n: jnp.stack([ref_layout[i][n] for i in range(DEPTH)]) for n in names}
    for n in ('norm_mix', 'q_norm', 'kv_norm', 'norm_ffn', 'norm_ple', 'pool_scale'):
        local[n] = jnp.stack([layer_grads[i][n].reshape(-1) for i in range(DEPTH)])
    local['w_pool'] = jnp.stack([layer_grads[i]['w_pool'] for i in range(DEPTH)])
    local['final_norm'] = g_final.reshape(-1)

    send = []
    for k in range(N_CHIPS):
        parts = []
        for n in names:
            ax = SHARDED[n]
            size = local[n].shape[ax] // N_CHIPS
            parts.append(lax.slice_in_dim(local[n], k * size, (k + 1) * size, axis=ax))
        send.append(_pack_blocks(parts, row_mult=PACK_ROWS))
    rh = R // 2
    trr = PACK_ROWS // 2
    send = [g.reshape(2, rh, PACK_LANES) for g in send]
    part = _add_halves(send, _swap_halves(send), core, tr=trr)
    reduced_half = _sum_chips(_scatter_partials(part), tr=trr)
    reduced = _both_halves(reduced_half, _send_half(reduced_half), core, tr=trr).reshape(R, PACK_LANES)
    grads = dict(zip(names, _unpack_blocks(reduced, shard_shapes)))

    rep_shapes = [weights[n].shape for n in REPLICATED]
    rep = _allreduce_small(_pack_rows([local[n] for n in REPLICATED], row_mult=8))
    grads.update(zip(REPLICATED, _unpack_rows(rep, rep_shapes)))

    deltas, new_m, new_v = {}, {}, {}
    for n in WEIGHTS:
        deltas[n], new_m[n], new_v[n] = _adamw(weights[n], grads[n], given['m_' + n], given['v_' + n], name=f"adamw_{n}")
    return (loss, grad_x, *[grads[n] for n in WEIGHTS], *[deltas[n] for n in WEIGHTS], *[new_m[n] for n in WEIGHTS],
            *[new_v[n] for n in WEIGHTS])
```

```python
import functools

import numpy as np
import jax
import jax.numpy as jnp
from jax import lax
from jax.experimental import pallas as pl
from jax.experimental.pallas import tpu as pltpu

F32 = jnp.float32
BF16 = jnp.bfloat16

D_MODEL = 1024
DEPTH = 2
PLE_DIM = 256
POOL_WINDOWS = (2, 4, 8, 16)
POOL_GROUP = 128
POOL_WIDTH = 512
N_HEADS = 8
Q_LORA = 512
KV_LORA = 256
QK_NOPE = 128
QK_ROPE = 64
QK_HEAD = 192
V_HEAD = 128
D_FF = 2816
ROPE_THETA = 10000.0
EPS = 1e-6
ATTN_SCALE = QK_HEAD ** -0.5

ADAM_LR = 0.001
ADAM_B1 = 0.9
ADAM_B2 = 0.999
ADAM_EPS = 1e-08
ADAM_WD = 0.01
ADAM_STEP = 10

LANES = 128
HALO = 16
HEAD_PAD = 256
V7X_VMEM_BYTES = 64 * 1024 * 1024
VMEM_LIMIT = (V7X_VMEM_BYTES * 3) // 4
N_CHIPS = 4
N_DEV = 8
NEG_INF = -1e30

ZC_GA, ZC_GB, ZC_U, ZC_CQ, ZC_CKV, ZC_KR = 0, 1024, 2048, 2560, 3072, 3328
Z_WIDTH = 3456

WEIGHTS = ['norm_mix', 'w_in', 'w_pool', 'pool_scale', 'q_norm', 'kv_norm', 'w_uq', 'w_ukv', 'w_a', 'w_b', 'w_o',
           'norm_ffn', 'w_gate', 'w_up', 'w_down', 'norm_ple', 'w_ple_gate', 'w_ple', 'final_norm']
SHARDED = {'w_in': 2, 'w_uq': 1, 'w_ukv': 1, 'w_a': 2, 'w_b': 1, 'w_o': 1, 'w_gate': 2, 'w_up': 2, 'w_down': 1,
           'w_ple_gate': 1, 'w_ple': 2}
REPLICATED = [n for n in WEIGHTS if n not in SHARDED]


def _tile(n, target, mult=LANES):
    if n <= target:
        return n
    best = None
    for t in range(mult, target + 1, mult):
        if n % t == 0:
            best = t
    assert best is not None, (n, target)
    return best


def _cparams(*sem):
    return pltpu.CompilerParams(dimension_semantics=sem, vmem_limit_bytes=VMEM_LIMIT)


def _rope(t, cos_t, sin_t):
    return t * cos_t + pltpu.roll(t, 64, 1) * sin_t


def _rope_bwd(d, cos_t, sin_t):
    return d * cos_t + pltpu.roll(d * sin_t, 64, 1)


def _sigmoid(v):
    return 1.0 / (1.0 + jnp.exp(-v))


def _mm_nn(a, b, *, name, outs, a_col=0, gain=None, emit_a=False, epi=None, epi_rows=(), tm=512):
    M = a.shape[0]
    K, N = b.shape
    tm = min(tm, M)
    assert a_col % K == 0 and M % tm == 0
    a_blk = a_col // K
    n_rows, n_out = len(epi_rows), len(outs)

    def body(*refs):
        a_ref, b_ref = refs[0], refs[1]
        pos = 2
        g_ref = None
        if gain is not None:
            g_ref = refs[pos]
            pos += 1
        row_refs = refs[pos:pos + n_rows]
        out_refs = refs[pos + n_rows:pos + n_rows + n_out]
        lhs = a_ref[...]
        if gain is not None:
            av = lhs.astype(F32)
            lhs = av * lax.rsqrt(jnp.mean(av * av, axis=-1, keepdims=True) + EPS) * g_ref[...]
        lhs = lhs.astype(BF16)
        if emit_a:
            refs[pos + n_rows + n_out][...] = lhs
        acc = jnp.dot(lhs, b_ref[...], preferred_element_type=F32)
        vals = (acc,) if epi is None else epi(acc, *[r[...] for r in row_refs])
        for r, v in zip(out_refs, vals):
            r[...] = v.astype(r.dtype)

    in_specs = [pl.BlockSpec((tm, K), lambda i: (i, a_blk)), pl.BlockSpec((K, N), lambda i: (0, 0))]
    args = [a, b]
    if gain is not None:
        in_specs.append(pl.BlockSpec((1, K), lambda i: (0, 0)))
        args.append(gain.reshape(1, K).astype(F32))
    for arr, w, blk in epi_rows:
        in_specs.append(pl.BlockSpec((tm, w), lambda i, blk=blk: (i, blk)))
        args.append(arr)
    out_shape = [jax.ShapeDtypeStruct((M, w), dt) for w, dt in outs]
    out_specs = [pl.BlockSpec((tm, w), lambda i: (i, 0)) for w, dt in outs]
    if emit_a:
        out_shape.append(jax.ShapeDtypeStruct((M, K), BF16))
        out_specs.append(pl.BlockSpec((tm, K), lambda i: (i, 0)))
    res = pl.pallas_call(body, grid=(M // tm,), in_specs=in_specs, out_specs=out_specs, out_shape=out_shape,
                         name=name, compiler_params=_cparams("parallel"))(*args)
    return res[0] if len(res) == 1 else res


def _mm_nt(pairs, *, name, outs, epi=None, epi_rows=(), consts=(), accs=(), tm=512, into=None, wide0=None):
    M = pairs[0][0].shape[0]
    N = pairs[0][1].shape[0]
    tm = min(tm, M)
    n_p, n_in, n_out, n_acc = len(pairs), len(epi_rows) + len(consts), len(outs), len(accs)

    def body(*refs):
        acc = None
        for k in range(n_p):
            av = refs[2 * k][...].astype(BF16)
            part = lax.dot_general(av, refs[2 * k + 1][...], NT_DIMS, preferred_element_type=F32)
            acc = part if acc is None else acc + part
        pos = 2 * n_p
        extra = [r[...] for r in refs[pos:pos + n_in]]
        pos += n_in + (1 if into is not None else 0)
        vals = (acc,) if epi is None else epi(acc, *extra)
        for r, v in zip(refs[pos:pos + n_out], vals[:n_out]):
            r[...] = v.astype(r.dtype)
        if n_acc:
            acc_refs = refs[pos + n_out:pos + n_out + n_acc]

            @pl.when(pl.program_id(0) == 0)
            def _():
                for r in acc_refs:
                    r[...] = jnp.zeros_like(r)
            for r, v in zip(acc_refs, vals[n_out:]):
                r[...] += v

    in_specs, args = [], []
    for a, b in pairs:
        assert a.shape[1] == b.shape[1] and b.shape[0] == N and a.shape[0] == M
        in_specs.append(pl.BlockSpec((tm, a.shape[1]), lambda i: (i, 0)))
        in_specs.append(pl.BlockSpec(b.shape, lambda i: (0, 0)))
        args += [a, b]
    for arr, w, blk in epi_rows:
        in_specs.append(pl.BlockSpec((tm, w), lambda i, blk=blk: (i, blk)))
        args.append(arr)
    for arr in consts:
        in_specs.append(pl.BlockSpec(arr.shape, lambda i, n=arr.ndim: (0,) * n))
        args.append(arr)
    out_shape = [jax.ShapeDtypeStruct((M, w), dt) for w, dt in outs]
    out_specs = [pl.BlockSpec((tm, w), lambda i: (i, 0)) for w, dt in outs]
    for s in accs:
        out_shape.append(jax.ShapeDtypeStruct(s, F32))
        out_specs.append(pl.BlockSpec(s, lambda i, n=len(s): (0,) * n))
    aliases = _into_column_block(into, tm, out_shape, out_specs, in_specs, args) if into is not None else {}
    if wide0 is not None:
        out_shape[0] = jax.ShapeDtypeStruct((M, wide0), outs[0][1])
    res = pl.pallas_call(body, grid=(M // tm,), in_specs=in_specs, out_specs=out_specs, out_shape=out_shape, name=name,
                         input_output_aliases=aliases,
                         compiler_params=_cparams("arbitrary" if n_acc else "parallel"))(*args)
    return res[0] if len(res) == 1 else res


def _rms_bwd_epi(with_res, emit_bf16):
    def epi(dh, xv, *rest):
        gv = rest[-1]
        xv = xv.astype(F32)
        rstd = lax.rsqrt(jnp.mean(xv * xv, axis=-1, keepdims=True) + EPS)
        xhat = xv * rstd
        dg = jnp.sum(dh * xhat, axis=0, keepdims=True)
        dxh = dh * gv
        dx = rstd * (dxh - xhat * jnp.mean(dxh * xhat, axis=-1, keepdims=True))
        if with_res:
            dx = dx + rest[0].astype(F32)
        return (dx, dx, dg) if emit_bf16 else (dx, dg)
    return epi


def _mm_tn(a, b, *, name, a_col=0, a_w=None, tk=1024, tn=1152, tm=512):
    M = a.shape[0]
    a_w = a.shape[1] if a_w is None else a_w
    N = b.shape[1]
    tm = min(tm, M)
    tk = _tile(a_w, tk)
    tn = _tile(N, tn)
    assert a_col % tk == 0 and M % tm == 0
    a_blk0 = a_col // tk

    def body(a_ref, b_ref, o_ref):
        @pl.when(pl.program_id(2) == 0)
        def _():
            o_ref[...] = jnp.zeros_like(o_ref)
        o_ref[...] += lax.dot_general(a_ref[...].astype(BF16), b_ref[...].astype(BF16), (((0,), (0,)), ((), ())),
                                      preferred_element_type=F32)

    return pl.pallas_call(body, grid=(a_w // tk, N // tn, M // tm),
                          in_specs=[pl.BlockSpec((tm, tk), lambda k, j, m: (m, k + a_blk0)),
                                    pl.BlockSpec((tm, tn), lambda k, j, m: (m, j))],
                          out_specs=pl.BlockSpec((tk, tn), lambda k, j, m: (k, j)),
                          out_shape=jax.ShapeDtypeStruct((a_w, N), F32), name=name,
                          compiler_params=_cparams("parallel", "parallel", "arbitrary"))(a, b)


def _into_column_block(into, tile, out_shape, out_specs, in_specs, args):
    buf, width, blk = into
    out_shape[0] = jax.ShapeDtypeStruct(buf.shape, buf.dtype)
    out_specs[0] = pl.BlockSpec((tile, width), lambda i: (i, blk))
    in_specs.append(pl.BlockSpec(memory_space=pl.ANY))
    args.append(buf)
    return {len(args) - 1: 0}


def _row_call(fn, rows, tr, ins, outs, accs=(), *, name, into=None):
    n_in, n_out, n_acc = len(ins), len(outs), len(accs)
    first_out = n_in + (1 if into is not None else 0)

    def body(*refs):
        i = pl.program_id(0)
        vals = fn(i, *[r[...] for r in refs[:n_in]])
        if not isinstance(vals, (tuple, list)):
            vals = (vals,)
        for r, v in zip(refs[first_out:first_out + n_out], vals[:n_out]):
            r[...] = v.astype(r.dtype)
        if n_acc:
            acc_refs = refs[first_out + n_out:]

            @pl.when(i == 0)
            def _():
                for r in acc_refs:
                    r[...] = jnp.zeros_like(r)
            for r, v in zip(acc_refs, vals[n_out:]):
                r[...] += v

    out_shape = [jax.ShapeDtypeStruct((rows, w), dt) for w, dt in outs]
    out_specs = [pl.BlockSpec((tr, w), lambda i: (i, 0)) for w, dt in outs]
    for s in accs:
        out_shape.append(jax.ShapeDtypeStruct(s, F32))
        out_specs.append(pl.BlockSpec(s, lambda i, n=len(s): (0,) * n))
    in_specs = [pl.BlockSpec(bs, im) for _, bs, im in ins]
    args = [a for a, _, _ in ins]
    aliases = _into_column_block(into, tr, out_shape, out_specs, in_specs, args) if into is not None else {}
    res = pl.pallas_call(body, grid=(rows // tr,), in_specs=in_specs, out_specs=out_specs, out_shape=out_shape, name=name,
                         input_output_aliases=aliases, compiler_params=_cparams("arbitrary"))(*args)
    return res


def _rspec(arr, tr, w=None, blk=0):
    w = arr.shape[1] if w is None else w
    return (arr, (tr, w), lambda i, blk=blk: (i, blk))


def _bspec(arr):
    return (arr, arr.shape, lambda i, n=arr.ndim: (0,) * n)


def _pool_counts(i, tr):
    t = (i * tr + lax.broadcasted_iota(jnp.int32, (tr, 1), 0) + 1).astype(F32)
    return [jnp.minimum(t, float(w)) for w in POOL_WINDOWS]


def _pool_fwd(z, w_pool_bf, pool_scale, *, name, tr):
    rows = z.shape[0]
    ublk = ZC_U // POOL_WIDTH
    hpt = tr // HALO

    def fn(i, u, uprev, wp, ps):
        uprev = jnp.where(i > 0, uprev, 0.0)
        ext = jnp.concatenate([uprev, u], axis=0)
        s2 = ext + pltpu.roll(ext, 1, 0)
        s4 = s2 + pltpu.roll(s2, 2, 0)
        s8 = s4 + pltpu.roll(s4, 4, 0)
        s16 = s8 + pltpu.roll(s8, 8, 0)
        cnts = _pool_counts(i, tr)
        pooled, mixed = [], []
        for g, sw in enumerate((s2, s4, s8, s16)):
            lanes = slice(g * POOL_GROUP, (g + 1) * POOL_GROUP)
            pg = sw[HALO:, lanes] / cnts[g] - u[:, lanes]
            pooled.append(pg)
            mixed.append(jnp.dot(pg.astype(BF16), wp[g], preferred_element_type=F32))
        pooled = jnp.concatenate(pooled, axis=1)
        mixed = jnp.concatenate(mixed, axis=1)
        return pooled, mixed, mixed * ps

    ins = [_rspec(z, tr, POOL_WIDTH, ublk),
           (z, (HALO, POOL_WIDTH), lambda i: (jnp.maximum(i * hpt - 1, 0), ublk)),
           _bspec(w_pool_bf), _bspec(pool_scale.reshape(1, POOL_WIDTH))]
    return _row_call(fn, rows, tr, ins, [(POOL_WIDTH, BF16), (POOL_WIDTH, F32), (POOL_WIDTH, BF16)], name=name)


def _pool_bwd_mix(dpm, mixed, pooled, w_pool_bf, pool_scale, *, name, tr):
    rows = dpm.shape[0]

    def fn(i, dv, mv, pv, wp, ps):
        dv = dv.astype(F32)
        dscale = jnp.sum(dv * mv, axis=0, keepdims=True)
        dmix = (dv * ps).astype(BF16)
        cnts = _pool_counts(i, tr)
        dpool, dwp = [], []
        for g in range(len(POOL_WINDOWS)):
            lanes = slice(g * POOL_GROUP, (g + 1) * POOL_GROUP)
            dg = lax.dot_general(dmix[:, lanes], wp[g], (((1,), (1,)), ((), ())), preferred_element_type=F32)
            dpool.append(dg)
            dwp.append(lax.dot_general(pv[:, lanes], dmix[:, lanes], (((0,), (0,)), ((), ())),
                                       preferred_element_type=F32)[None])
        dpool = jnp.concatenate(dpool, axis=1)
        dpool_cnt = jnp.concatenate([dpool[:, g * POOL_GROUP:(g + 1) * POOL_GROUP] / cnts[g]
                                     for g in range(len(POOL_WINDOWS))], axis=1)
        return dpool, dpool_cnt, dscale, jnp.concatenate(dwp, axis=0)

    ins = [_rspec(dpm, tr), _rspec(mixed, tr), _rspec(pooled, tr), _bspec(w_pool_bf),
           _bspec(pool_scale.reshape(1, POOL_WIDTH))]
    return _row_call(fn, rows, tr, ins, [(POOL_WIDTH, F32), (POOL_WIDTH, F32)],
                     [(1, POOL_WIDTH), (len(POOL_WINDOWS), POOL_GROUP, POOL_GROUP)], name=name)


def _pool_bwd_window(dpool, dpool_cnt, *, name, tr, into):
    rows = dpool.shape[0]
    hpt = tr // HALO
    n_halo = rows // HALO
    n_tiles = rows // tr

    def fn(i, dp, dc, dnext):
        dnext = jnp.where(i < n_tiles - 1, dnext, 0.0)
        ext = jnp.concatenate([dc, dnext], axis=0)
        n = tr + HALO
        s2 = ext + pltpu.roll(ext, n - 1, 0)
        s4 = s2 + pltpu.roll(s2, n - 2, 0)
        s8 = s4 + pltpu.roll(s4, n - 4, 0)
        s16 = s8 + pltpu.roll(s8, n - 8, 0)
        out = []
        for g, sw in enumerate((s2, s4, s8, s16)):
            lanes = slice(g * POOL_GROUP, (g + 1) * POOL_GROUP)
            out.append(sw[:tr, lanes] - dp[:, lanes])
        return jnp.concatenate(out, axis=1)

    ins = [_rspec(dpool, tr), _rspec(dpool_cnt, tr),
           (dpool_cnt, (HALO, POOL_WIDTH), lambda i: (jnp.minimum((i + 1) * hpt, n_halo - 1), 0))]
    return _row_call(fn, rows, tr, ins, [(POOL_WIDTH, BF16)], name=name, into=into)[0]


def _causal_pairs(n, k_major):
    if k_major:
        pairs = [(qi, ki) for ki in range(n) for qi in range(ki, n)]
    else:
        pairs = [(qi, ki) for qi in range(n) for ki in range(qi + 1)]
    return (jnp.asarray(np.array([p[0] for p in pairs], np.int32)),
            jnp.asarray(np.array([p[1] for p in pairs], np.int32)), len(pairs))


SUBLANES = 8
NT_DIMS = (((1,), (1,)), ((), ()))
TN_DIMS = (((0,), (0,)), ((), ()))


ATTN_BLOCK = 1024
ATTN_BLOCK_FWD = 2048
QUERY_CHUNK = 256
LOG2_E = 1.4426950408889634
EXP2_SCALE = ATTN_SCALE * LOG2_E


def _scores_t(q_c, k, c, qc, diag):
    s = lax.dot_general(k, q_c, NT_DIMS, preferred_element_type=F32)
    if diag:
        key = lax.broadcasted_iota(jnp.int32, s.shape, 0)
        qry = lax.broadcasted_iota(jnp.int32, s.shape, 1) + c * qc
        s = jnp.where(key <= qry, s, NEG_INF)
    return s


def _flash_fwd(q, k, v, *, name, blk):
    T = q.shape[0]
    n = T // blk
    qc = min(QUERY_CHUNK, blk)
    qtab, ktab, n_pairs = _causal_pairs(n, k_major=False)

    def body(qt, kt, q_ref, k_ref, v_ref, o_ref, lse_ref, m_s, l_s, acc_s):
        p = pl.program_id(1)
        qi, ki = qt[p], kt[p]

        @pl.when(ki == 0)
        def _():
            m_s[...] = jnp.full_like(m_s, NEG_INF)
            l_s[...] = jnp.zeros_like(l_s)
            acc_s[...] = jnp.zeros_like(acc_s)

        def step(diag):
            kv, vv = k_ref[...], v_ref[...]
            chunks = [slice(c * qc, (c + 1) * qc) for c in range(blk // qc)]
            keys = [(c + 1) * qc if diag else blk for c in range(blk // qc)]
            scores = [_scores_t(q_ref[rows, :], kv[:n], c, qc, diag) for c, (rows, n) in enumerate(zip(chunks, keys))]
            probs, alphas = [], []
            for rows, s_t in zip(chunks, scores):
                m_prev = m_s[:, rows]
                m_new = jnp.maximum(m_prev, jnp.max(s_t, axis=0, keepdims=True))
                p_t = jnp.exp2(s_t - m_new)
                alpha = jnp.exp2(m_prev - m_new)
                l_s[:, rows] = alpha * l_s[:, rows] + jnp.sum(p_t, axis=0, keepdims=True)
                m_s[:, rows] = m_new
                probs.append(p_t.astype(BF16))
                alphas.append(alpha)
            for rows, n, p_t, alpha in zip(chunks, keys, probs, alphas):
                acc_s[:, rows] = alpha * acc_s[:, rows] + lax.dot_general(vv[:n], p_t, TN_DIMS,
                                                                          preferred_element_type=F32)

        @pl.when(ki != qi)
        def _():
            step(False)

        @pl.when(ki == qi)
        def _():
            step(True)
            o_ref[...] = (acc_s[...] / l_s[...]).T.astype(o_ref.dtype)
            lse2 = m_s[...] + jnp.log2(l_s[...])
            lse_ref[...] = jnp.broadcast_to(lse2, lse_ref.shape)

    grid_spec = pltpu.PrefetchScalarGridSpec(
        num_scalar_prefetch=2, grid=(N_HEADS, n_pairs),
        in_specs=[pl.BlockSpec((blk, HEAD_PAD), lambda h, p, qt, kt: (qt[p], h)),
                  pl.BlockSpec((blk, HEAD_PAD), lambda h, p, qt, kt: (kt[p], h)),
                  pl.BlockSpec((blk, V_HEAD), lambda h, p, qt, kt: (kt[p], h))],
        out_specs=[pl.BlockSpec((blk, V_HEAD), lambda h, p, qt, kt: (qt[p], h)),
                   pl.BlockSpec((SUBLANES, blk), lambda h, p, qt, kt: (h, qt[p]))],
        scratch_shapes=[pltpu.VMEM((1, blk), F32), pltpu.VMEM((1, blk), F32), pltpu.VMEM((V_HEAD, blk), F32)])
    return pl.pallas_call(body, grid_spec=grid_spec,
                          out_shape=[jax.ShapeDtypeStruct((T, N_HEADS * V_HEAD), BF16),
                                     jax.ShapeDtypeStruct((N_HEADS * SUBLANES, T), F32)],
                          name=name, compiler_params=_cparams("parallel", "arbitrary"))(qtab, ktab, q, k, v)


def _attn_delta(do, o, *, name, tr):
    T = do.shape[0]

    def body(do_ref, o_ref, d_ref):
        prod = do_ref[...].astype(F32) * o_ref[...].astype(F32)
        lane_head = lax.broadcasted_iota(jnp.int32, (tr, LANES), 1) // SUBLANES
        mat = jnp.zeros((tr, LANES), F32)
        for h in range(N_HEADS):
            d_h = jnp.sum(prod[:, h * V_HEAD:(h + 1) * V_HEAD], axis=1, keepdims=True)
            mat = jnp.where(lane_head == h, d_h, mat)
        d_ref[...] = mat.T[:N_HEADS * SUBLANES, :]

    return pl.pallas_call(body, grid=(T // tr,),
                          in_specs=[pl.BlockSpec((tr, N_HEADS * V_HEAD), lambda i: (i, 0)),
                                    pl.BlockSpec((tr, N_HEADS * V_HEAD), lambda i: (i, 0))],
                          out_specs=pl.BlockSpec((N_HEADS * SUBLANES, tr), lambda i: (0, i)),
                          out_shape=jax.ShapeDtypeStruct((N_HEADS * SUBLANES, T), F32), name=name,
                          compiler_params=_cparams("parallel"))(do, o)


def _flash_bwd(q, k, v, lse, delta, do, cos_t, sin_t, *, name, blk):
    T = q.shape[0]
    n = T // blk
    qc = min(QUERY_CHUNK, blk)
    qtab, ktab, n_pairs = _causal_pairs(n, k_major=True)

    def body(qt, kt, q_ref, k_ref, v_ref, lse_ref, delta_ref, do_ref, cos_ref, sin_ref, dq_ref, dk_ref, dv_ref,
             dq_s, dk_s, dv_s):
        p = pl.program_id(1)
        qi, ki = qt[p], kt[p]
        first = qi == ki

        @pl.when(p == 0)
        def _():
            dq_s[...] = jnp.zeros_like(dq_s)

        @pl.when(first)
        def _():
            dk_s[...] = jnp.zeros_like(dk_s)
            dv_s[...] = jnp.zeros_like(dv_s)

        def step(diag):
            kv, vv = k_ref[...], v_ref[...]
            chunks = [slice(c * qc, (c + 1) * qc) for c in range(blk // qc)]
            qs = [q_ref[rows, :] for rows in chunks]
            dos = [do_ref[rows, :] for rows in chunks]
            keys = [(c + 1) * qc if diag else blk for c in range(blk // qc)]
            scores = [_scores_t(q_c, kv[:n], c, qc, diag) for c, (q_c, n) in enumerate(zip(qs, keys))]
            dps = [lax.dot_general(vv[:n], do_c, NT_DIMS, preferred_element_type=F32) for do_c, n in zip(dos, keys)]
            probs, dss = [], []
            for rows, s_t, dp_t in zip(chunks, scores, dps):
                p_t = jnp.exp2(s_t - lse_ref[0:1, rows])
                dss.append((p_t * (dp_t - delta_ref[0:1, rows])).astype(BF16))
                probs.append(p_t.astype(BF16))
            dv_acc = dk_acc = None
            for n, p_t, ds_t, q_c, do_c in zip(keys, probs, dss, qs, dos):
                dv_c = jnp.dot(p_t, do_c, preferred_element_type=F32)
                dk_c = jnp.dot(ds_t, q_c, preferred_element_type=F32)
                if diag:
                    dv_s[:n, :] += dv_c
                    dk_s[:n, :] += dk_c
                else:
                    dv_acc = dv_c if dv_acc is None else dv_acc + dv_c
                    dk_acc = dk_c if dk_acc is None else dk_acc + dk_c
            for rows, n, ds_t in zip(chunks, keys, dss):
                dq_s[qi, :, rows] += lax.dot_general(kv[:n], ds_t, TN_DIMS, preferred_element_type=F32)
            if not diag:
                dv_s[...] += dv_acc
                dk_s[...] += dk_acc

        @pl.when(jnp.logical_not(first))
        def _():
            step(False)

        @pl.when(first)
        def _():
            step(True)
            dq_t = (dq_s[qi] * ATTN_SCALE).T
            dq_ref[:, :QK_NOPE] = dq_t[:, :QK_NOPE].astype(dq_ref.dtype)
            dq_ref[:, QK_NOPE:] = _rope_bwd(dq_t[:, QK_NOPE:], cos_ref[...], sin_ref[...]).astype(dq_ref.dtype)

        @pl.when(qi == n - 1)
        def _():
            dk_ref[...] = (dk_s[...] * (1.0 / LOG2_E)).astype(dk_ref.dtype)
            dv_ref[...] = dv_s[...].astype(dv_ref.dtype)

    qmap = lambda h, p, qt, kt: (qt[p], h)
    kmap = lambda h, p, qt, kt: (kt[p], h)
    smap = lambda h, p, qt, kt: (h, qt[p])
    tmap = lambda h, p, qt, kt: (kt[p], 0)
    grid_spec = pltpu.PrefetchScalarGridSpec(
        num_scalar_prefetch=2, grid=(N_HEADS, n_pairs),
        in_specs=[pl.BlockSpec((blk, HEAD_PAD), qmap), pl.BlockSpec((blk, HEAD_PAD), kmap),
                  pl.BlockSpec((blk, V_HEAD), kmap), pl.BlockSpec((SUBLANES, blk), smap),
                  pl.BlockSpec((SUBLANES, blk), smap), pl.BlockSpec((blk, V_HEAD), qmap),
                  pl.BlockSpec((blk, LANES), tmap), pl.BlockSpec((blk, LANES), tmap)],
        out_specs=[pl.BlockSpec((blk, HEAD_PAD), kmap), pl.BlockSpec((blk, HEAD_PAD), kmap),
                   pl.BlockSpec((blk, V_HEAD), kmap)],
        scratch_shapes=[pltpu.VMEM((n, HEAD_PAD, blk), F32), pltpu.VMEM((blk, HEAD_PAD), F32),
                        pltpu.VMEM((blk, V_HEAD), F32)])
    return pl.pallas_call(body, grid_spec=grid_spec,
                          out_shape=[jax.ShapeDtypeStruct((T, N_HEADS * HEAD_PAD), BF16),
                                     jax.ShapeDtypeStruct((T, N_HEADS * HEAD_PAD), BF16),
                                     jax.ShapeDtypeStruct((T, N_HEADS * V_HEAD), BF16)],
                          name=name, compiler_params=_cparams("arbitrary", "arbitrary"))(
                              qtab, ktab, q, k, v, lse, delta, do, cos_t, sin_t)


MESH_ID = pl.DeviceIdType.MESH
ANY_SPEC = pl.BlockSpec(memory_space=pl.ANY)


def _other_chips(x, y):
    out = []
    for dx, dy in ((1, 0), (0, 1), (1, 1)):
        px = x ^ dx if dx else x
        py = y ^ dy if dy else y
        out.append((px, py, 2 * px + py))
    return out


def _gather_weights(flat):
    rh = flat.shape[0] // 2
    rq = rh // 2

    def body(src2, out, send_sems, recv_sems):
        x, y, c = lax.axis_index("x"), lax.axis_index("y"), lax.axis_index("c")
        me = 2 * x + y
        sib = (x, y, 1 - c)
        (xx, xy, kx), (yx, yy, ky), (_, _, kd) = _other_chips(x, y)
        x_nbr, y_nbr = (xx, xy, c), (yx, yy, c)
        first, last = pl.ds(0, rq), pl.ds(rq, rq)

        def copy(j, src, dst, to):
            return pltpu.make_async_remote_copy(src_ref=src, dst_ref=dst, send_sem=send_sems.at[j], recv_sem=recv_sems.at[j],
                                                device_id=to, device_id_type=MESH_ID)

        def arrived(j, land):
            copy(j, land, land, sib).wait_recv()

        sends = [copy(0, src2.at[c], out.at[me, c], x_nbr), copy(1, src2.at[c], out.at[me, c], y_nbr)]
        for cp in sends:
            cp.start()
        landings = [(0, out.at[kx, c]), (1, out.at[ky, c]), (2, out.at[kd, c, first]), (3, out.at[kd, c, last])]
        relays = {0: (2, out.at[kx, c, first], y_nbr), 1: (3, out.at[ky, c, last], x_nbr)}
        for j, land in landings:
            arrived(j, land)
            if j in relays:
                rj, piece, to = relays[j]
                sends.append(copy(rj, piece, piece, to))
                sends[-1].start()
            sends.append(copy(4 + j, land, land, sib))
            sends[-1].start()
        for j, land in [(0, out.at[kx, 1 - c]), (1, out.at[ky, 1 - c]), (2, out.at[kd, 1 - c, first]),
                        (3, out.at[kd, 1 - c, last])]:
            arrived(4 + j, land)
        for cp in sends:
            cp.wait_send()

    return pl.pallas_call(body, out_shape=jax.ShapeDtypeStruct((N_CHIPS, 2, rh, PACK_LANES), flat.dtype),
                          in_specs=[ANY_SPEC], out_specs=ANY_SPEC,
                          scratch_shapes=[pltpu.SemaphoreType.DMA((8,)), pltpu.SemaphoreType.DMA((8,))],
                          name="gather_weights")(flat.reshape(2, rh, PACK_LANES))


def _swap_halves(gs):
    rh = gs[0].shape[1]

    def body(*refs):
        srcs, out, send_sems, recv_sems = refs[:N_CHIPS], refs[N_CHIPS], refs[N_CHIPS + 1], refs[N_CHIPS + 2]
        x, y, c = lax.axis_index("x"), lax.axis_index("y"), lax.axis_index("c")
        copies = [pltpu.make_async_remote_copy(src_ref=srcs[k].at[1 - c], dst_ref=out.at[k], send_sem=send_sems.at[k],
                                               recv_sem=recv_sems.at[k], device_id=(x, y, 1 - c), device_id_type=MESH_ID)
                  for k in range(N_CHIPS)]
        for cp in copies:
            cp.start()
        for cp in copies:
            cp.wait()

    return pl.pallas_call(body, out_shape=jax.ShapeDtypeStruct((N_CHIPS, rh, PACK_LANES), gs[0].dtype),
                          in_specs=[ANY_SPEC] * N_CHIPS, out_specs=ANY_SPEC,
                          scratch_shapes=[pltpu.SemaphoreType.DMA((N_CHIPS,)), pltpu.SemaphoreType.DMA((N_CHIPS,))],
                          name="grad_swap_halves")(*gs)


def _add_halves(gs, got, core, *, tr):
    rh = gs[0].shape[1]

    def body(*refs):
        g_refs, got_ref, o_ref = refs[1:1 + N_CHIPS], refs[1 + N_CHIPS], refs[2 + N_CHIPS]
        for k in range(N_CHIPS):
            o_ref[k] = (g_refs[k][0] + got_ref[k]).astype(o_ref.dtype)

    grid_spec = pltpu.PrefetchScalarGridSpec(
        num_scalar_prefetch=1, grid=(rh // tr,),
        in_specs=[pl.BlockSpec((1, tr, PACK_LANES), lambda i, c_ref: (c_ref[0], i, 0))] * N_CHIPS
        + [pl.BlockSpec((N_CHIPS, tr, PACK_LANES), lambda i, c_ref: (0, i, 0))],
        out_specs=pl.BlockSpec((N_CHIPS, tr, PACK_LANES), lambda i, c_ref: (0, i, 0)))
    return pl.pallas_call(body, grid_spec=grid_spec,
                          out_shape=jax.ShapeDtypeStruct((N_CHIPS, rh, PACK_LANES), BF16), name="grad_add_halves",
                          compiler_params=_cparams("parallel"))(core.reshape(1), *gs, got)


def _scatter_partials(part):
    rh = part.shape[1]

    def body(src, out, send_sems, recv_sems, local_sem):
        x, y, c = lax.axis_index("x"), lax.axis_index("y"), lax.axis_index("c")
        me = 2 * x + y
        own = pltpu.make_async_copy(src.at[me], out.at[me], local_sem)
        own.start()
        chips = _other_chips(x, y)
        sends = []
        for j, (px, py, pk) in enumerate(chips):
            cp = pltpu.make_async_remote_copy(src_ref=src.at[pk], dst_ref=out.at[me], send_sem=send_sems.at[j],
                                              recv_sem=recv_sems.at[j], device_id=(px, py, c), device_id_type=MESH_ID)
            cp.start()
            sends.append(cp)
        for j, (px, py, pk) in enumerate(chips):
            land = out.at[pk]
            pltpu.make_async_remote_copy(src_ref=land, dst_ref=land, send_sem=send_sems.at[j], recv_sem=recv_sems.at[j],
                                         device_id=(px, py, c), device_id_type=MESH_ID).wait_recv()
        for cp in sends:
            cp.wait_send()
        own.wait()

    return pl.pallas_call(body, out_shape=jax.ShapeDtypeStruct((N_CHIPS, rh, PACK_LANES), part.dtype),
                          in_specs=[ANY_SPEC], out_specs=ANY_SPEC,
                          scratch_shapes=[pltpu.SemaphoreType.DMA((3,)), pltpu.SemaphoreType.DMA((3,)),
                                          pltpu.SemaphoreType.DMA(())],
                          name="grad_scatter_partials")(part)


def _sum_chips(q, *, tr):
    rh = q.shape[1]

    def body(q_ref, o_ref):
        parts = [q_ref[k].astype(F32) for k in range(N_CHIPS)]
        o_ref[...] = ((parts[0] + parts[1]) + parts[2]) + parts[3]

    return pl.pallas_call(body, grid=(rh // tr,),
                          in_specs=[pl.BlockSpec((N_CHIPS, tr, PACK_LANES), lambda i: (0, i, 0))],
                          out_specs=pl.BlockSpec((tr, PACK_LANES), lambda i: (i, 0)),
                          out_shape=jax.ShapeDtypeStruct((rh, PACK_LANES), F32), name="grad_sum_chips",
                          compiler_params=_cparams("parallel"))(q)


def _send_half(half):
    def body(src, out, send_sem, recv_sem):
        x, y, c = lax.axis_index("x"), lax.axis_index("y"), lax.axis_index("c")
        cp = pltpu.make_async_remote_copy(src_ref=src, dst_ref=out, send_sem=send_sem, recv_sem=recv_sem,
                                          device_id=(x, y, 1 - c), device_id_type=MESH_ID)
        cp.start()
        cp.wait()

    return pl.pallas_call(body, out_shape=jax.ShapeDtypeStruct(half.shape, half.dtype),
                          in_specs=[ANY_SPEC], out_specs=ANY_SPEC,
                          scratch_shapes=[pltpu.SemaphoreType.DMA(()), pltpu.SemaphoreType.DMA(())],
                          name="grad_send_half")(half)


def _both_halves(own, got, core, *, tr):
    rh = own.shape[0]

    def body(c_ref, own_ref, got_ref, o_ref):
        mine = pl.program_id(0) == c_ref[0]
        o_ref[0] = jnp.where(mine, own_ref[...], got_ref[...])

    grid_spec = pltpu.PrefetchScalarGridSpec(
        num_scalar_prefetch=1, grid=(2, rh // tr),
        in_specs=[pl.BlockSpec((tr, PACK_LANES), lambda h, i, c_ref: (i, 0)),
                  pl.BlockSpec((tr, PACK_LANES), lambda h, i, c_ref: (i, 0))],
        out_specs=pl.BlockSpec((1, tr, PACK_LANES), lambda h, i, c_ref: (h, i, 0)))
    return pl.pallas_call(body, grid_spec=grid_spec, out_shape=jax.ShapeDtypeStruct((2, rh, PACK_LANES), own.dtype),
                          name="grad_both_halves", compiler_params=_cparams("parallel", "parallel"))(core.reshape(1), own, got)


def _allreduce_small(v):
    rows = v.shape[0]

    def body(v_ref, o_ref, buf, send_sems, recv_sems):
        x, y, c = lax.axis_index("x"), lax.axis_index("y"), lax.axis_index("c")
        me = 4 * x + 2 * y + c
        buf[me] = v_ref[...]
        sends = []
        for j in range(1, N_DEV):
            px, py, pc = x ^ ((j >> 2) & 1), y ^ ((j >> 1) & 1), c ^ (j & 1)
            cp = pltpu.make_async_remote_copy(src_ref=v_ref, dst_ref=buf.at[me], send_sem=send_sems.at[j - 1],
                                              recv_sem=recv_sems.at[j - 1], device_id=(px, py, pc), device_id_type=MESH_ID)
            cp.start()
            sends.append(cp)
        for j in range(1, N_DEV):
            px, py, pc = x ^ ((j >> 2) & 1), y ^ ((j >> 1) & 1), c ^ (j & 1)
            land = buf.at[4 * px + 2 * py + pc]
            pltpu.make_async_remote_copy(src_ref=land, dst_ref=land, send_sem=send_sems.at[j - 1],
                                         recv_sem=recv_sems.at[j - 1], device_id=(px, py, pc),
                                         device_id_type=MESH_ID).wait_recv()
        for cp in sends:
            cp.wait_send()
        acc = buf[0]
        for d in range(1, N_DEV):
            acc = acc + buf[d]
        o_ref[...] = acc

    vm = pl.BlockSpec(memory_space=pltpu.VMEM)
    return pl.pallas_call(body, out_shape=jax.ShapeDtypeStruct((rows, LANES), F32), in_specs=[vm], out_specs=vm,
                          scratch_shapes=[pltpu.VMEM((N_DEV, rows, LANES), F32), pltpu.SemaphoreType.DMA((N_DEV - 1,)),
                                          pltpu.SemaphoreType.DMA((N_DEV - 1,))],
                          name="allreduce_small")(v)


def _adamw(w, g, m, v, *, name):
    shape = w.shape
    cols = shape[-1] if w.ndim > 1 else shape[0]
    rows = w.size // cols
    w2, g2, m2, v2 = (t.reshape(rows, cols) for t in (w, g, m, v))
    tr = rows if rows <= 256 else _tile(rows, 256, 8)

    def fn(i, wv, gv, mv, vv):
        mn = ADAM_B1 * mv + (1.0 - ADAM_B1) * gv
        vn = ADAM_B2 * vv + (1.0 - ADAM_B2) * (gv * gv)
        m_hat = mn / (1.0 - ADAM_B1 ** ADAM_STEP)
        v_hat = vn / (1.0 - ADAM_B2 ** ADAM_STEP)
        delta = -ADAM_LR * (m_hat / (jnp.sqrt(v_hat) + ADAM_EPS) + ADAM_WD * wv)
        return delta, mn, vn

    ins = [_rspec(t, tr) for t in (w2, g2, m2, v2)]
    d, mn, vn = _row_call(fn, rows, tr, ins, [(cols, F32)] * 3, name=name)
    return d.reshape(shape), mn.reshape(shape), vn.reshape(shape)


def _rope_cols(w):
    z = jnp.zeros(w.shape[:-1] + (32,), w.dtype)
    return jnp.concatenate([w[..., :32], z, w[..., 32:], z], axis=-1)


def _rope_cols_inv(w):
    return jnp.concatenate([w[..., :32], w[..., 64:96]], axis=-1)


def _layer_layouts(W, i):
    w_in = W['w_in'][i]
    u, cq, ckv = w_in[:, :512], w_in[:, 512:1024], w_in[:, 1024:1280]
    kr, ga, gb = w_in[:, 1280:1344], w_in[:, 1344:2368], w_in[:, 2368:]
    L = {}
    L['w_in'] = jnp.concatenate([ga, gb, u, cq, ckv, _rope_cols(kr)], axis=1)
    wq = W['w_uq'][i]
    L['w_q'] = jnp.concatenate([wq[..., :QK_NOPE], _rope_cols(wq[..., QK_NOPE:])], axis=-1).reshape(Q_LORA, -1)
    wkv = W['w_ukv'][i]
    L['w_k'] = jnp.concatenate([wkv[..., :QK_NOPE], jnp.zeros_like(wkv[..., :LANES])], axis=-1).reshape(KV_LORA, -1)
    L['w_v'] = wkv[..., QK_NOPE:].reshape(KV_LORA, -1)
    L['w_gu'] = jnp.concatenate([W['w_gate'][i], W['w_up'][i]], axis=1)
    for n in ('w_a', 'w_b', 'w_o', 'w_down', 'w_ple_gate', 'w_ple'):
        L[n] = W[n][i]
    return L


def _layer_grads_to_reference_layout(G):
    d = G['w_in']
    ga, gb, u = d[:, ZC_GA:ZC_GB], d[:, ZC_GB:ZC_U], d[:, ZC_U:ZC_CQ]
    cq, ckv, kr = d[:, ZC_CQ:ZC_CKV], d[:, ZC_CKV:ZC_KR], _rope_cols_inv(d[:, ZC_KR:])
    out = {'w_in': jnp.concatenate([u, cq, ckv, kr, ga, gb], axis=1)}
    dq = G['w_q'].reshape(Q_LORA, N_HEADS, HEAD_PAD)
    out['w_uq'] = jnp.concatenate([dq[..., :QK_NOPE], _rope_cols_inv(dq[..., QK_NOPE:])], axis=-1)
    dk = G['w_k'].reshape(KV_LORA, N_HEADS, HEAD_PAD)[..., :QK_NOPE]
    dv = G['w_v'].reshape(KV_LORA, N_HEADS, V_HEAD)
    out['w_ukv'] = jnp.concatenate([dk, dv], axis=-1)
    out['w_gate'], out['w_up'] = G['w_gu'][:, :D_FF], G['w_gu'][:, D_FF:]
    for n in ('w_a', 'w_b', 'w_o', 'w_down', 'w_ple_gate', 'w_ple'):
        out[n] = G[n]
    return out


PACK_ROWS = 2048


def _pack_rows(parts, row_mult):
    flat = jnp.concatenate([p.reshape(-1) for p in parts])
    n = flat.shape[0]
    per = LANES * row_mult
    padded = -(-n // per) * per
    return jnp.pad(flat, (0, padded - n)).reshape(-1, LANES)


def _unpack_rows(flat2d, shapes):
    flat = flat2d.reshape(-1)
    out, off = [], 0
    for s in shapes:
        n = int(np.prod(s))
        out.append(flat[off:off + n].reshape(s))
        off += n
    return out


PACK_LANES = 256


def _lane_blocks(cols):
    return -(-cols // PACK_LANES)


def _pack_blocks(parts, row_mult):
    blocks = []
    for p in parts:
        p2 = p.reshape(-1, p.shape[-1])
        cols = p2.shape[1]
        nb = _lane_blocks(cols)
        if nb * PACK_LANES != cols:
            p2 = jnp.pad(p2, ((0, 0), (0, nb * PACK_LANES - cols)))
        blocks += [p2[:, j * PACK_LANES:(j + 1) * PACK_LANES] for j in range(nb)]
    buf = jnp.concatenate(blocks, axis=0)
    rows = buf.shape[0]
    padded = -(-rows // row_mult) * row_mult
    return buf if padded == rows else jnp.pad(buf, ((0, padded - rows), (0, 0)))


def _unpack_blocks(buf, shapes):
    out, off = [], 0
    for s in shapes:
        rows, cols = int(np.prod(s[:-1])), s[-1]
        nb = _lane_blocks(cols)
        piece = jnp.concatenate([buf[off + j * rows:off + (j + 1) * rows] for j in range(nb)], axis=1)
        out.append(piece[:, :cols].reshape(s))
        off += nb * rows
    return out


def _layer_fwd(i, x, p_i, L, norms, w_pool_bf, pool_scale, cos_t, sin_t, tr, blk):
    sv = {'x': x}
    z, sv['h'] = _mm_nn(x, L['w_in'], name=f"l{i}_in_proj", outs=[(Z_WIDTH, F32)], gain=norms['norm_mix'], emit_a=True,
                        tm=tr // 2)
    sv['z'] = z
    sv['pooled'], sv['mixed'], sv['pm'] = _pool_fwd(z, w_pool_bf, pool_scale, name=f"l{i}_pool", tr=tr)
    sv['ya'] = _mm_nn(sv['pm'], L['w_a'], name=f"l{i}_ya", outs=[(D_MODEL, BF16)], tm=tr)

    def heads(acc, rope_part):
        out = []
        for h in range(N_HEADS):
            out.append(acc[:, h * HEAD_PAD:h * HEAD_PAD + QK_NOPE])
            out.append(rope_part(acc[:, h * HEAD_PAD + QK_NOPE:(h + 1) * HEAD_PAD]))
        return jnp.concatenate(out, axis=1)

    def q_epi(acc, ct, st):
        return (heads(acc * EXP2_SCALE, lambda t: _rope(t, ct, st)),)

    def k_epi(acc, kr, ct, st):
        k_pe = _rope(kr, ct, st)
        return (heads(acc, lambda t: k_pe),)

    rope_rows = [(cos_t, LANES, 0), (sin_t, LANES, 0)]
    qk_width = N_HEADS * HEAD_PAD
    sv['q'], sv['cqn'] = _mm_nn(z, L['w_q'], name=f"l{i}_q_proj", outs=[(qk_width, BF16)], a_col=ZC_CQ,
                                gain=norms['q_norm'], emit_a=True, epi=q_epi, epi_rows=rope_rows, tm=tr)
    sv['k'], sv['ckvn'] = _mm_nn(z, L['w_k'], name=f"l{i}_k_proj", outs=[(qk_width, BF16)], a_col=ZC_CKV,
                                 gain=norms['kv_norm'], emit_a=True, epi=k_epi,
                                 epi_rows=[(z, LANES, ZC_KR // LANES)] + rope_rows, tm=tr)
    sv['v'] = _mm_nn(sv['ckvn'], L['w_v'], name=f"l{i}_v_proj", outs=[(N_HEADS * V_HEAD, BF16)], tm=tr)
    sv['o'], sv['lse'] = _flash_fwd(sv['q'], sv['k'], sv['v'], name=f"l{i}_attn",
                                    blk=min(ATTN_BLOCK_FWD, max(blk * ATTN_BLOCK_FWD // ATTN_BLOCK, 128)))

    def merge_epi(yb, ga, gb, ya):
        return yb, _sigmoid(ga) * ya.astype(F32) + _sigmoid(gb) * yb

    sv['yb'], sv['merged'] = _mm_nn(sv['o'], L['w_b'], name=f"l{i}_yb_merge", outs=[(D_MODEL, BF16), (D_MODEL, BF16)],
                                    epi=merge_epi, epi_rows=[(z, D_MODEL, 0), (z, D_MODEL, 1), (sv['ya'], D_MODEL, 0)],
                                    tm=tr)

    def add_epi(acc, res):
        return (acc + res,)

    x1 = _mm_nn(sv['merged'], L['w_o'], name=f"l{i}_wo", outs=[(D_MODEL, F32)], epi=add_epi, epi_rows=[(x, D_MODEL, 0)],
                tm=tr)
    sv['x1'] = x1

    def swiglu_epi(acc):
        g, u = acc[:, :D_FF], acc[:, D_FF:]
        return acc, g * _sigmoid(g) * u

    sv['gu'], sv['act'], sv['h2'] = _mm_nn(x1, L['w_gu'], name=f"l{i}_gate_up", outs=[(2 * D_FF, BF16), (D_FF, BF16)],
                                           gain=norms['norm_ffn'], emit_a=True, epi=swiglu_epi, tm=tr // 2)
    x2 = _mm_nn(sv['act'], L['w_down'], name=f"l{i}_down", outs=[(D_MODEL, F32)], epi=add_epi,
                epi_rows=[(x1, D_MODEL, 0)], tm=tr)
    sv['x2'] = x2
    sv['logit'], sv['h3'] = _mm_nn(x2, L['w_ple_gate'], name=f"l{i}_ple_gate", outs=[(D_MODEL, F32)],
                                   gain=norms['norm_ple'], emit_a=True, tm=tr)

    def ple_epi(pe, xv, lg):
        return pe, xv + _sigmoid(lg) * pe

    sv['pe'], x3 = _mm_nn(p_i, L['w_ple'], name=f"l{i}_ple", outs=[(D_MODEL, F32), (D_MODEL, F32)], epi=ple_epi,
                          epi_rows=[(x2, D_MODEL, 0), (sv['logit'], D_MODEL, 0)], tm=tr)
    return x3, sv


def _layer_bwd(i, dx3, sv, p_i, L, norms, w_pool_bf, pool_scale, cos_t, sin_t, tr, blk):
    T = dx3.shape[0]
    G = {}
    z = sv['z']

    def ple_bwd(_, d, lg, pe):
        g = _sigmoid(lg)
        return d * pe * g * (1.0 - g), d * g

    dlogit, dpe = _row_call(ple_bwd, T, tr, [_rspec(dx3, tr), _rspec(sv['logit'], tr), _rspec(sv['pe'], tr)],
                            [(D_MODEL, BF16), (D_MODEL, BF16)], name=f"l{i}_ple_bwd")
    G['w_ple_gate'] = _mm_tn(sv['h3'], dlogit, name=f"l{i}_dw_ple_gate", tn=1024)
    G['w_ple'] = _mm_tn(p_i, dpe, name=f"l{i}_dw_ple", tn=1024)
    def gain_row(n):
        return norms[n].reshape(1, -1).astype(F32)

    dx2, dx2_bf, G['norm_ple'] = _mm_nt([(dlogit, L['w_ple_gate'])], name=f"l{i}_dh3_norm_bwd",
                                        outs=[(D_MODEL, F32), (D_MODEL, BF16)], epi=_rms_bwd_epi(True, True),
                                        epi_rows=[(sv['x2'], D_MODEL, 0), (dx3, D_MODEL, 0)], consts=[gain_row('norm_ple')],
                                        accs=[(1, D_MODEL)], tm=tr)

    def swiglu_bwd_epi(da, gu):
        g, u = gu[:, :D_FF].astype(F32), gu[:, D_FF:].astype(F32)
        sg = _sigmoid(g)
        return (jnp.concatenate([da * u * sg * (1.0 + g * (1.0 - sg)), da * g * sg], axis=1),)

    dgu = _mm_nt([(dx2_bf, L['w_down'])], name=f"l{i}_dact_swiglu_bwd", outs=[(2 * D_FF, BF16)], epi=swiglu_bwd_epi,
                 epi_rows=[(sv['gu'], 2 * D_FF, 0)], tm=tr // 2)
    G['w_down'] = _mm_tn(sv['act'], dx2_bf, name=f"l{i}_dw_down", tk=1408, tn=1024)
    G['w_gu'] = _mm_tn(sv['h2'], dgu, name=f"l{i}_dw_gate_up", tn=1408)
    dx1, dx1_bf, G['norm_ffn'] = _mm_nt([(dgu, L['w_gu'])], name=f"l{i}_dh2_norm_bwd",
                                        outs=[(D_MODEL, F32), (D_MODEL, BF16)], epi=_rms_bwd_epi(True, True),
                                        epi_rows=[(sv['x1'], D_MODEL, 0), (dx2, D_MODEL, 0)], consts=[gain_row('norm_ffn')],
                                        accs=[(1, D_MODEL)], tm=tr // 2)

    def merge_bwd_epi(dm, ga, gb, ya, yb):
        sa, sb = _sigmoid(ga), _sigmoid(gb)
        ya, yb = ya.astype(F32), yb.astype(F32)
        d_gates = jnp.concatenate([dm * ya * sa * (1.0 - sa), dm * yb * sb * (1.0 - sb)], axis=1)
        return d_gates, dm * sa, dm * sb

    dz, dya, dyb = _mm_nt([(dx1_bf, L['w_o'])], name=f"l{i}_dmerged_bwd",
                          outs=[(2 * D_MODEL, BF16), (D_MODEL, BF16), (D_MODEL, BF16)], epi=merge_bwd_epi,
                          epi_rows=[(z, D_MODEL, 0), (z, D_MODEL, 1), (sv['ya'], D_MODEL, 0), (sv['yb'], D_MODEL, 0)],
                          tm=tr, wide0=Z_WIDTH)
    G['w_o'] = _mm_tn(sv['merged'], dx1_bf, name=f"l{i}_dw_o", tn=1024)

    G['w_b'] = _mm_tn(sv['o'], dyb, name=f"l{i}_dw_b", tn=1024)
    do = _mm_nt([(dyb, L['w_b'])], name=f"l{i}_do", outs=[(D_MODEL, BF16)], tm=tr)
    delta = _attn_delta(do, sv['o'], name=f"l{i}_attn_delta", tr=tr)
    dq, dk, dv = _flash_bwd(sv['q'], sv['k'], sv['v'], sv['lse'], delta, do, cos_t, sin_t, name=f"l{i}_attn_bwd", blk=blk)

    def dk_rope(_, d, ct, st):
        d = d.astype(F32)
        acc = d[:, QK_NOPE:HEAD_PAD]
        for h in range(1, N_HEADS):
            acc = acc + d[:, h * HEAD_PAD + QK_NOPE:(h + 1) * HEAD_PAD]
        return _rope_bwd(acc, ct, st)

    dz = _row_call(dk_rope, T, tr, [_rspec(dk, tr), _rspec(cos_t, tr), _rspec(sin_t, tr)], [(LANES, BF16)],
                   name=f"l{i}_dk_rope", into=(dz, LANES, ZC_KR // LANES))[0]
    G['w_q'] = _mm_tn(sv['cqn'], dq, name=f"l{i}_dw_q", tn=1024)
    G['w_k'] = _mm_tn(sv['ckvn'], dk, name=f"l{i}_dw_k", tn=1024)
    G['w_v'] = _mm_tn(sv['ckvn'], dv, name=f"l{i}_dw_v", tn=1024)
    dz, G['q_norm'] = _mm_nt([(dq, L['w_q'])], name=f"l{i}_dcq", outs=[(Q_LORA, BF16)], epi=_rms_bwd_epi(False, False),
                             epi_rows=[(z, Q_LORA, ZC_CQ // Q_LORA)], consts=[gain_row('q_norm')], accs=[(1, Q_LORA)], tm=tr,
                             into=(dz, Q_LORA, ZC_CQ // Q_LORA))
    dz, G['kv_norm'] = _mm_nt([(dk, L['w_k']), (dv, L['w_v'])], name=f"l{i}_dckv", outs=[(KV_LORA, BF16)],
                              epi=_rms_bwd_epi(False, False), epi_rows=[(z, KV_LORA, ZC_CKV // KV_LORA)],
                              consts=[gain_row('kv_norm')], accs=[(1, KV_LORA)], tm=tr,
                              into=(dz, KV_LORA, ZC_CKV // KV_LORA))

    G['w_a'] = _mm_tn(sv['pm'], dya, name=f"l{i}_dw_a", tn=1024)
    dpm = _mm_nt([(dya, L['w_a'])], name=f"l{i}_dpm", outs=[(POOL_WIDTH, F32)], tm=tr)
    dpool, dpool_cnt, G['pool_scale'], G['w_pool'] = _pool_bwd_mix(dpm, sv['mixed'], sv['pooled'], w_pool_bf, pool_scale,
                                                                   name=f"l{i}_pool_bwd_mix", tr=tr)
    dz = _pool_bwd_window(dpool, dpool_cnt, name=f"l{i}_pool_bwd_window", tr=tr, into=(dz, POOL_WIDTH, ZC_U // POOL_WIDTH))

    G['w_in'] = _mm_tn(sv['h'], dz, name=f"l{i}_dw_in", tn=1152)
    dx, G['norm_mix'] = _mm_nt([(dz, L['w_in'])], name=f"l{i}_dh_norm_bwd", outs=[(D_MODEL, F32)],
                               epi=_rms_bwd_epi(True, False), epi_rows=[(sv['x'], D_MODEL, 0), (dx1, D_MODEL, 0)],
                               consts=[gain_row('norm_mix')], accs=[(1, D_MODEL)], tm=tr)
    return dx, G


def kernel(x, p, positions, norm_mix, w_in, w_pool, pool_scale, q_norm, kv_norm, w_uq, w_ukv, w_a, w_b, w_o, norm_ffn, w_gate, w_up, w_down, norm_ple, w_ple_gate, w_ple, final_norm, loss_target, m_norm_mix, m_w_in, m_w_pool, m_pool_scale, m_q_norm, m_kv_norm, m_w_uq, m_w_ukv, m_w_a, m_w_b, m_w_o, m_norm_ffn, m_w_gate, m_w_up, m_w_down, m_norm_ple, m_w_ple_gate, m_w_ple, m_final_norm, v_norm_mix, v_w_in, v_w_pool, v_pool_scale, v_q_norm, v_kv_norm, v_w_uq, v_w_ukv, v_w_a, v_w_b, v_w_o, v_norm_ffn, v_w_gate, v_w_up, v_w_down, v_norm_ple, v_w_ple_gate, v_w_ple, v_final_norm):
    given = dict(locals())
    weights = {n: given[n] for n in WEIGHTS}
    T = x.shape[1]
    tr = min(512, max(T // 2, 8))
    blk = min(ATTN_BLOCK, max(T // 4, 128))
    x0 = x.reshape(T, D_MODEL)
    target = loss_target.reshape(T, D_MODEL)

    names = list(SHARDED)
    shard_shapes = [weights[n].shape for n in names]
    flat = _pack_blocks([weights[n].astype(BF16) for n in names], row_mult=PACK_ROWS)
    R = flat.shape[0]
    chip = (2 * lax.axis_index("x") + lax.axis_index("y")).astype(jnp.int32)
    core = lax.axis_index("c").astype(jnp.int32)
    gathered = _gather_weights(flat).reshape(N_CHIPS, R, PACK_LANES)
    gathered = lax.dynamic_update_slice(gathered, flat.reshape(1, R, PACK_LANES), (chip, 0, 0))
    per_chip = [_unpack_blocks(gathered[k], shard_shapes) for k in range(N_CHIPS)]
    W = {n: jnp.concatenate([per_chip[k][j] for k in range(N_CHIPS)], axis=SHARDED[n]) for j, n in enumerate(names)}
    layouts = [_layer_layouts(W, i) for i in range(DEPTH)]
    w_pool_bf = w_pool.astype(BF16)

    inv_freq = 1.0 / (ROPE_THETA ** (jnp.arange(0, QK_ROPE, 2, dtype=F32) / QK_ROPE))
    zero32 = jnp.zeros((32,), F32)
    freq_row = jnp.concatenate([inv_freq, zero32, inv_freq, zero32]).reshape(1, LANES)
    cos_mask = jnp.concatenate([jnp.ones((32,), F32), zero32, jnp.ones((32,), F32), zero32]).reshape(1, LANES)
    sin_sign = jnp.concatenate([-jnp.ones((32,), F32), zero32, jnp.ones((32,), F32), zero32]).reshape(1, LANES)

    def rope_tables(_, pos, fr, cm, ss):
        ang = pos.astype(F32) * fr
        return jnp.cos(ang) * cm, jnp.sin(ang) * ss

    pos_col = positions.reshape(T, 1)
    cos_t, sin_t = _row_call(rope_tables, T, tr, [_rspec(pos_col, tr), _bspec(freq_row), _bspec(cos_mask), _bspec(sin_sign)],
                             [(LANES, F32), (LANES, F32)], name="rope_tables")

    xs = x0
    saved = []
    for i in range(DEPTH):
        norms = {n: weights[n][i] for n in ('norm_mix', 'q_norm', 'kv_norm', 'norm_ffn', 'norm_ple')}
        xs, sv = _layer_fwd(i, xs, p[i, 0], layouts[i], norms, w_pool_bf[i], pool_scale[i], cos_t, sin_t, tr, blk)
        saved.append((sv, norms))

    def head(_, xv, tv, gv):
        rstd = lax.rsqrt(jnp.mean(xv * xv, axis=-1, keepdims=True) + EPS)
        xhat = xv * rstd
        err = xhat * gv - tv
        loss = 0.5 * jnp.sum(jnp.mean(err * err, axis=-1, keepdims=True), axis=0, keepdims=True)
        dy = err * (1.0 / D_MODEL)
        dg = jnp.sum(dy * xhat, axis=0, keepdims=True)
        dxh = dy * gv
        dx = rstd * (dxh - xhat * jnp.mean(dxh * xhat, axis=-1, keepdims=True))
        return dx, jnp.broadcast_to(loss, (1, LANES)), dg

    dx, loss_part, g_final = _row_call(head, T, tr, [_rspec(xs, tr), _rspec(target, tr), _bspec(final_norm.reshape(1, D_MODEL))],
                                       [(D_MODEL, F32)], [(1, LANES), (1, D_MODEL)], name="loss_head")
    loss = lax.psum(loss_part[0, 0], ("x", "y", "c"))

    layer_grads = [None] * DEPTH
    for i in reversed(range(DEPTH)):
        sv, norms = saved[i]
        dx, layer_grads[i] = _layer_bwd(i, dx, sv, p[i, 0], layouts[i], norms, w_pool_bf[i], pool_scale[i], cos_t, sin_t, tr,
                                        blk)
    grad_x = dx.reshape(x.shape)

    ref_layout = [_layer_grads_to_reference_layout(g) for g in layer_grads]
    local = {n: jnp.stack([ref_layout[i][n] for i in range(DEPTH)]) for n in names}
    for n in ('norm_mix', 'q_norm', 'kv_norm', 'norm_ffn', 'norm_ple', 'pool_scale'):
        local[n] = jnp.stack([layer_grads[i][n].reshape(-1) for i in range(DEPTH)])
    local['w_pool'] = jnp.stack([layer_grads[i]['w_pool'] for i in range(DEPTH)])
    local['final_norm'] = g_final.reshape(-1)

    send = []
    for k in range(N_CHIPS):
        parts = []
        for n in names:
            ax = SHARDED[n]
            size = local[n].shape[ax] // N_CHIPS
            parts.append(lax.slice_in_dim(local[n], k * size, (k + 1) * size, axis=ax))
        send.append(_pack_blocks(parts, row_mult=PACK_ROWS))
    rh = R // 2
    trr = PACK_ROWS // 2
    send = [g.reshape(2, rh, PACK_LANES) for g in send]
    part = _add_halves(send, _swap_halves(send), core, tr=trr)
    reduced_half = _sum_chips(_scatter_partials(part), tr=trr)
    reduced = _both_halves(reduced_half, _send_half(reduced_half), core, tr=trr).reshape(R, PACK_LANES)
    grads = dict(zip(names, _unpack_blocks(reduced, shard_shapes)))

    rep_shapes = [weights[n].shape for n in REPLICATED]
    rep = _allreduce_small(_pack_rows([local[n] for n in REPLICATED], row_mult=8))
    grads.update(zip(REPLICATED, _unpack_rows(rep, rep_shapes)))

    deltas, new_m, new_v = {}, {}, {}
    for n in WEIGHTS:
        deltas[n], new_m[n], new_v[n] = _adamw(weights[n], grads[n], given['m_' + n], given['v_' + n], name=f"adamw_{n}")
    return (loss, grad_x, *[grads[n] for n in WEIGHTS], *[deltas[n] for n in WEIGHTS], *[new_m[n] for n in WEIGHTS],
            *[new_v[n] for n in WEIGHTS])
```

```python
import numpy as np
import jax
import jax.numpy as jnp
from jax import lax
from jax.experimental import pallas as pl
from jax.experimental.pallas import tpu as pltpu

F32 = jnp.float32
BF16 = jnp.bfloat16

D_MODEL = 1024
DEPTH = 2
PLE_DIM = 256
POOL_WINDOWS = (2, 4, 8, 16)
POOL_GROUP = 128
POOL_WIDTH = 512
N_HEADS = 8
Q_LORA = 512
KV_LORA = 256
QK_NOPE = 128
QK_ROPE = 64
QK_HEAD = 192
V_HEAD = 128
D_FF = 2816
ROPE_THETA = 10000.0
EPS = 1e-6
ATTN_SCALE = QK_HEAD ** -0.5

ADAM_LR = 0.001
ADAM_B1 = 0.9
ADAM_B2 = 0.999
ADAM_EPS = 1e-08
ADAM_WD = 0.01
ADAM_STEP = 10

LANES = 128
HALO = 16
HEAD_PAD = 256
V7X_VMEM_BYTES = 64 * 1024 * 1024
VMEM_LIMIT = (V7X_VMEM_BYTES * 3) // 4
N_CHIPS = 4
N_DEV = 8
NEG_INF = -1e30

ZC_GA, ZC_GB, ZC_U, ZC_CQ, ZC_CKV, ZC_KR = 0, 1024, 2048, 2560, 3072, 3328
Z_WIDTH = 3456

WEIGHTS = ['norm_mix', 'w_in', 'w_pool', 'pool_scale', 'q_norm', 'kv_norm', 'w_uq', 'w_ukv', 'w_a', 'w_b', 'w_o',
           'norm_ffn', 'w_gate', 'w_up', 'w_down', 'norm_ple', 'w_ple_gate', 'w_ple', 'final_norm']
SHARDED = {'w_in': 2, 'w_uq': 1, 'w_ukv': 1, 'w_a': 2, 'w_b': 1, 'w_o': 1, 'w_gate': 2, 'w_up': 2, 'w_down': 1,
           'w_ple_gate': 1, 'w_ple': 2}
REPLICATED = [n for n in WEIGHTS if n not in SHARDED]


def _tile(n, target, mult=LANES):
    if n <= target:
        return n
    best = None
    for t in range(mult, target + 1, mult):
        if n % t == 0:
            best = t
    assert best is not None, (n, target)
    return best


def _cparams(*sem):
    return pltpu.CompilerParams(dimension_semantics=sem, vmem_limit_bytes=VMEM_LIMIT)


def _rope(t, cos_t, sin_t):
    return t * cos_t + pltpu.roll(t, 64, 1) * sin_t


def _rope_bwd(d, cos_t, sin_t):
    return d * cos_t + pltpu.roll(d * sin_t, 64, 1)


def _sigmoid(v):
    return 1.0 / (1.0 + jnp.exp(-v))


def _mm_nn(a, b, *, name, outs, a_col=0, gain=None, emit_a=False, epi=None, epi_rows=(), tm=512):
    M = a.shape[0]
    K, N = b.shape
    tm = min(tm, M)
    assert a_col % K == 0 and M % tm == 0
    a_blk = a_col // K
    n_rows, n_out = len(epi_rows), len(outs)

    def body(*refs):
        a_ref, b_ref = refs[0], refs[1]
        pos = 2
        g_ref = None
        if gain is not None:
            g_ref = refs[pos]
            pos += 1
        row_refs = refs[pos:pos + n_rows]
        out_refs = refs[pos + n_rows:pos + n_rows + n_out]
        lhs = a_ref[...]
        if gain is not None:
            av = lhs.astype(F32)
            lhs = av * lax.rsqrt(jnp.mean(av * av, axis=-1, keepdims=True) + EPS) * g_ref[...]
        lhs = lhs.astype(BF16)
        if emit_a:
            refs[pos + n_rows + n_out][...] = lhs
        acc = jnp.dot(lhs, b_ref[...], preferred_element_type=F32)
        vals = (acc,) if epi is None else epi(acc, *[r[...] for r in row_refs])
        for r, v in zip(out_refs, vals):
            r[...] = v.astype(r.dtype)

    in_specs = [pl.BlockSpec((tm, K), lambda i: (i, a_blk)), pl.BlockSpec((K, N), lambda i: (0, 0))]
    args = [a, b]
    if gain is not None:
        in_specs.append(pl.BlockSpec((1, K), lambda i: (0, 0)))
        args.append(gain.reshape(1, K).astype(F32))
    for arr, w, blk in epi_rows:
        in_specs.append(pl.BlockSpec((tm, w), lambda i, blk=blk: (i, blk)))
        args.append(arr)
    out_shape = [jax.ShapeDtypeStruct((M, w), dt) for w, dt in outs]
    out_specs = [pl.BlockSpec((tm, w), lambda i: (i, 0)) for w, dt in outs]
    if emit_a:
        out_shape.append(jax.ShapeDtypeStruct((M, K), BF16))
        out_specs.append(pl.BlockSpec((tm, K), lambda i: (i, 0)))
    res = pl.pallas_call(body, grid=(M // tm,), in_specs=in_specs, out_specs=out_specs, out_shape=out_shape,
                         name=name, compiler_params=_cparams("parallel"))(*args)
    return res[0] if len(res) == 1 else res


def _mm_nt(pairs, *, name, outs, epi=None, epi_rows=(), consts=(), accs=(), tm=512, into=None, wide0=None):
    M = pairs[0][0].shape[0]
    N = pairs[0][1].shape[0]
    tm = min(tm, M)
    n_p, n_in, n_out, n_acc = len(pairs), len(epi_rows) + len(consts), len(outs), len(accs)

    def body(*refs):
        acc = None
        for k in range(n_p):
            av = refs[2 * k][...].astype(BF16)
            part = lax.dot_general(av, refs[2 * k + 1][...], NT_DIMS, preferred_element_type=F32)
            acc = part if acc is None else acc + part
        pos = 2 * n_p
        extra = [r[...] for r in refs[pos:pos + n_in]]
        pos += n_in + (1 if into is not None else 0)
        vals = (acc,) if epi is None else epi(acc, *extra)
        for r, v in zip(refs[pos:pos + n_out], vals[:n_out]):
            r[...] = v.astype(r.dtype)
        if n_acc:
            acc_refs = refs[pos + n_out:pos + n_out + n_acc]

            @pl.when(pl.program_id(0) == 0)
            def _():
                for r in acc_refs:
                    r[...] = jnp.zeros_like(r)
            for r, v in zip(acc_refs, vals[n_out:]):
                r[...] += v

    in_specs, args = [], []
    for a, b in pairs:
        assert a.shape[1] == b.shape[1] and b.shape[0] == N and a.shape[0] == M
        in_specs.append(pl.BlockSpec((tm, a.shape[1]), lambda i: (i, 0)))
        in_specs.append(pl.BlockSpec(b.shape, lambda i: (0, 0)))
        args += [a, b]
    for arr, w, blk in epi_rows:
        in_specs.append(pl.BlockSpec((tm, w), lambda i, blk=blk: (i, blk)))
        args.append(arr)
    for arr in consts:
        in_specs.append(pl.BlockSpec(arr.shape, lambda i, n=arr.ndim: (0,) * n))
        args.append(arr)
    out_shape = [jax.ShapeDtypeStruct((M, w), dt) for w, dt in outs]
    out_specs = [pl.BlockSpec((tm, w), lambda i: (i, 0)) for w, dt in outs]
    for s in accs:
        out_shape.append(jax.ShapeDtypeStruct(s, F32))
        out_specs.append(pl.BlockSpec(s, lambda i, n=len(s): (0,) * n))
    aliases = _into_column_block(into, tm, out_shape, out_specs, in_specs, args) if into is not None else {}
    if wide0 is not None:
        out_shape[0] = jax.ShapeDtypeStruct((M, wide0), outs[0][1])
    res = pl.pallas_call(body, grid=(M // tm,), in_specs=in_specs, out_specs=out_specs, out_shape=out_shape, name=name,
                         input_output_aliases=aliases,
                         compiler_params=_cparams("arbitrary" if n_acc else "parallel"))(*args)
    return res[0] if len(res) == 1 else res


def _rms_bwd_epi(with_res, emit_bf16):
    def epi(dh, xv, *rest):
        gv = rest[-1]
        xv = xv.astype(F32)
        rstd = lax.rsqrt(jnp.mean(xv * xv, axis=-1, keepdims=True) + EPS)
        xhat = xv * rstd
        dg = jnp.sum(dh * xhat, axis=0, keepdims=True)
        dxh = dh * gv
        dx = rstd * (dxh - xhat * jnp.mean(dxh * xhat, axis=-1, keepdims=True))
        if with_res:
            dx = dx + rest[0].astype(F32)
        return (dx, dx, dg) if emit_bf16 else (dx, dg)
    return epi


def _mm_tn(a, b, *, name, a_col=0, a_w=None, tk=1024, tn=1152, tm=512):
    M = a.shape[0]
    a_w = a.shape[1] if a_w is None else a_w
    N = b.shape[1]
    tm = min(tm, M)
    tk = _tile(a_w, tk)
    tn = _tile(N, tn)
    assert a_col % tk == 0 and M % tm == 0
    a_blk0 = a_col // tk

    def body(a_ref, b_ref, o_ref):
        @pl.when(pl.program_id(2) == 0)
        def _():
            o_ref[...] = jnp.zeros_like(o_ref)
        o_ref[...] += lax.dot_general(a_ref[...].astype(BF16), b_ref[...].astype(BF16), (((0,), (0,)), ((), ())),
                                      preferred_element_type=F32)

    return pl.pallas_call(body, grid=(a_w // tk, N // tn, M // tm),
                          in_specs=[pl.BlockSpec((tm, tk), lambda k, j, m: (m, k + a_blk0)),
                                    pl.BlockSpec((tm, tn), lambda k, j, m: (m, j))],
                          out_specs=pl.BlockSpec((tk, tn), lambda k, j, m: (k, j)),
                          out_shape=jax.ShapeDtypeStruct((a_w, N), F32), name=name,
                          compiler_params=_cparams("parallel", "parallel", "arbitrary"))(a, b)


def _into_column_block(into, tile, out_shape, out_specs, in_specs, args):
    buf, width, blk = into
    out_shape[0] = jax.ShapeDtypeStruct(buf.shape, buf.dtype)
    out_specs[0] = pl.BlockSpec((tile, width), lambda i: (i, blk))
    in_specs.append(pl.BlockSpec(memory_space=pl.ANY))
    args.append(buf)
    return {len(args) - 1: 0}


def _row_call(fn, rows, tr, ins, outs, accs=(), *, name, into=None):
    n_in, n_out, n_acc = len(ins), len(outs), len(accs)
    first_out = n_in + (1 if into is not None else 0)

    def body(*refs):
        i = pl.program_id(0)
        vals = fn(i, *[r[...] for r in refs[:n_in]])
        if not isinstance(vals, (tuple, list)):
            vals = (vals,)
        for r, v in zip(refs[first_out:first_out + n_out], vals[:n_out]):
            r[...] = v.astype(r.dtype)
        if n_acc:
            acc_refs = refs[first_out + n_out:]

            @pl.when(i == 0)
            def _():
                for r in acc_refs:
                    r[...] = jnp.zeros_like(r)
            for r, v in zip(acc_refs, vals[n_out:]):
                r[...] += v

    out_shape = [jax.ShapeDtypeStruct((rows, w), dt) for w, dt in outs]
    out_specs = [pl.BlockSpec((tr, w), lambda i: (i, 0)) for w, dt in outs]
    for s in accs:
        out_shape.append(jax.ShapeDtypeStruct(s, F32))
        out_specs.append(pl.BlockSpec(s, lambda i, n=len(s): (0,) * n))
    in_specs = [pl.BlockSpec(bs, im) for _, bs, im in ins]
    args = [a for a, _, _ in ins]
    aliases = _into_column_block(into, tr, out_shape, out_specs, in_specs, args) if into is not None else {}
    res = pl.pallas_call(body, grid=(rows // tr,), in_specs=in_specs, out_specs=out_specs, out_shape=out_shape, name=name,
                         input_output_aliases=aliases, compiler_params=_cparams("arbitrary"))(*args)
    return res


def _rspec(arr, tr, w=None, blk=0):
    w = arr.shape[1] if w is None else w
    return (arr, (tr, w), lambda i, blk=blk: (i, blk))


def _bspec(arr):
    return (arr, arr.shape, lambda i, n=arr.ndim: (0,) * n)


def _pool_counts(i, tr):
    t = (i * tr + lax.broadcasted_iota(jnp.int32, (tr, 1), 0) + 1).astype(F32)
    return [jnp.minimum(t, float(w)) for w in POOL_WINDOWS]


def _pool_fwd(z, w_pool_bf, pool_scale, *, name, tr):
    rows = z.shape[0]
    ublk = ZC_U // POOL_WIDTH
    hpt = tr // HALO

    def fn(i, u, uprev, wp, ps):
        uprev = jnp.where(i > 0, uprev, 0.0)
        ext = jnp.concatenate([uprev, u], axis=0)
        s2 = ext + pltpu.roll(ext, 1, 0)
        s4 = s2 + pltpu.roll(s2, 2, 0)
        s8 = s4 + pltpu.roll(s4, 4, 0)
        s16 = s8 + pltpu.roll(s8, 8, 0)
        cnts = _pool_counts(i, tr)
        pooled, mixed = [], []
        for g, sw in enumerate((s2, s4, s8, s16)):
            lanes = slice(g * POOL_GROUP, (g + 1) * POOL_GROUP)
            pg = sw[HALO:, lanes] / cnts[g] - u[:, lanes]
            pooled.append(pg)
            mixed.append(jnp.dot(pg.astype(BF16), wp[g], preferred_element_type=F32))
        pooled = jnp.concatenate(pooled, axis=1)
        mixed = jnp.concatenate(mixed, axis=1)
        return pooled, mixed, mixed * ps

    ins = [_rspec(z, tr, POOL_WIDTH, ublk),
           (z, (HALO, POOL_WIDTH), lambda i: (jnp.maximum(i * hpt - 1, 0), ublk)),
           _bspec(w_pool_bf), _bspec(pool_scale.reshape(1, POOL_WIDTH))]
    return _row_call(fn, rows, tr, ins, [(POOL_WIDTH, BF16), (POOL_WIDTH, F32), (POOL_WIDTH, BF16)], name=name)


def _pool_bwd_mix(dpm, mixed, pooled, w_pool_bf, pool_scale, *, name, tr):
    rows = dpm.shape[0]

    def fn(i, dv, mv, pv, wp, ps):
        dv = dv.astype(F32)
        dscale = jnp.sum(dv * mv, axis=0, keepdims=True)
        dmix = (dv * ps).astype(BF16)
        cnts = _pool_counts(i, tr)
        dpool, dwp = [], []
        for g in range(len(POOL_WINDOWS)):
            lanes = slice(g * POOL_GROUP, (g + 1) * POOL_GROUP)
            dg = lax.dot_general(dmix[:, lanes], wp[g], (((1,), (1,)), ((), ())), preferred_element_type=F32)
            dpool.append(dg)
            dwp.append(lax.dot_general(pv[:, lanes], dmix[:, lanes], (((0,), (0,)), ((), ())),
                                       preferred_element_type=F32)[None])
        dpool = jnp.concatenate(dpool, axis=1)
        dpool_cnt = jnp.concatenate([dpool[:, g * POOL_GROUP:(g + 1) * POOL_GROUP] / cnts[g]
                                     for g in range(len(POOL_WINDOWS))], axis=1)
        return dpool, dpool_cnt, dscale, jnp.concatenate(dwp, axis=0)

    ins = [_rspec(dpm, tr), _rspec(mixed, tr), _rspec(pooled, tr), _bspec(w_pool_bf),
           _bspec(pool_scale.reshape(1, POOL_WIDTH))]
    return _row_call(fn, rows, tr, ins, [(POOL_WIDTH, F32), (POOL_WIDTH, F32)],
                     [(1, POOL_WIDTH), (len(POOL_WINDOWS), POOL_GROUP, POOL_GROUP)], name=name)


def _pool_bwd_window(dpool, dpool_cnt, *, name, tr, into):
    rows = dpool.shape[0]
    hpt = tr // HALO
    n_halo = rows // HALO
    n_tiles = rows // tr

    def fn(i, dp, dc, dnext):
        dnext = jnp.where(i < n_tiles - 1, dnext, 0.0)
        ext = jnp.concatenate([dc, dnext], axis=0)
        n = tr + HALO
        s2 = ext + pltpu.roll(ext, n - 1, 0)
        s4 = s2 + pltpu.roll(s2, n - 2, 0)
        s8 = s4 + pltpu.roll(s4, n - 4, 0)
        s16 = s8 + pltpu.roll(s8, n - 8, 0)
        out = []
        for g, sw in enumerate((s2, s4, s8, s16)):
            lanes = slice(g * POOL_GROUP, (g + 1) * POOL_GROUP)
            out.append(sw[:tr, lanes] - dp[:, lanes])
        return jnp.concatenate(out, axis=1)

    ins = [_rspec(dpool, tr), _rspec(dpool_cnt, tr),
           (dpool_cnt, (HALO, POOL_WIDTH), lambda i: (jnp.minimum((i + 1) * hpt, n_halo - 1), 0))]
    return _row_call(fn, rows, tr, ins, [(POOL_WIDTH, BF16)], name=name, into=into)[0]


def _causal_pairs(n, k_major):
    if k_major:
        pairs = [(qi, ki) for ki in range(n) for qi in range(ki, n)]
    else:
        pairs = [(qi, ki) for qi in range(n) for ki in range(qi + 1)]
    return (jnp.asarray(np.array([p[0] for p in pairs], np.int32)),
            jnp.asarray(np.array([p[1] for p in pairs], np.int32)), len(pairs))


SUBLANES = 8
NT_DIMS = (((1,), (1,)), ((), ()))
TN_DIMS = (((0,), (0,)), ((), ()))


ATTN_BLOCK = 1024
ATTN_BLOCK_FWD = 2048
QUERY_CHUNK = 256
LOG2_E = 1.4426950408889634
EXP2_SCALE = ATTN_SCALE * LOG2_E


def _scores_t(q_c, k, diag):
    s = lax.dot_general(k, q_c, NT_DIMS, preferred_element_type=F32)
    if diag:
        qc = q_c.shape[0]
        visible = lax.broadcasted_iota(jnp.int32, (qc, qc), 0) <= lax.broadcasted_iota(jnp.int32, (qc, qc), 1)
        last = jnp.where(visible, s[-qc:], NEG_INF)
        s = last if s.shape[0] == qc else jnp.concatenate([s[:-qc], last], axis=0)
    return s


def _flash_fwd(q, k, v, *, name, blk):
    T = q.shape[0]
    n = T // blk
    qc = min(QUERY_CHUNK, blk)
    qtab, ktab, n_pairs = _causal_pairs(n, k_major=False)

    def body(qt, kt, q_ref, k_ref, v_ref, o_ref, lse_ref, m_s, l_s, acc_s):
        p = pl.program_id(1)
        qi, ki = qt[p], kt[p]

        @pl.when(ki == 0)
        def _():
            m_s[...] = jnp.full_like(m_s, NEG_INF)
            l_s[...] = jnp.zeros_like(l_s)
            acc_s[...] = jnp.zeros_like(acc_s)

        def step(diag):
            kv, vv = k_ref[...], v_ref[...]
            chunks = [slice(c * qc, (c + 1) * qc) for c in range(blk // qc)]
            keys = [(c + 1) * qc if diag else blk for c in range(blk // qc)]
            scores = [_scores_t(q_ref[rows, :], kv[:n], diag) for rows, n in zip(chunks, keys)]
            probs, alphas = [], []
            for rows, s_t in zip(chunks, scores):
                m_prev = m_s[:, rows]
                m_new = jnp.maximum(m_prev, jnp.max(s_t, axis=0, keepdims=True))
                p_t = jnp.exp2(s_t - m_new)
                alpha = jnp.exp2(m_prev - m_new)
                l_s[:, rows] = alpha * l_s[:, rows] + jnp.sum(p_t, axis=0, keepdims=True)
                m_s[:, rows] = m_new
                probs.append(p_t.astype(BF16))
                alphas.append(alpha)
            for rows, n, p_t, alpha in zip(chunks, keys, probs, alphas):
                acc_s[:, rows] = alpha * acc_s[:, rows] + lax.dot_general(vv[:n], p_t, TN_DIMS,
                                                                          preferred_element_type=F32)

        @pl.when(ki != qi)
        def _():
            step(False)

        @pl.when(ki == qi)
        def _():
            step(True)
            o_ref[...] = (acc_s[...] / l_s[...]).T.astype(o_ref.dtype)
            lse2 = m_s[...] + jnp.log2(l_s[...])
            lse_ref[...] = jnp.broadcast_to(lse2, lse_ref.shape)

    grid_spec = pltpu.PrefetchScalarGridSpec(
        num_scalar_prefetch=2, grid=(N_HEADS, n_pairs),
        in_specs=[pl.BlockSpec((blk, HEAD_PAD), lambda h, p, qt, kt: (qt[p], h)),
                  pl.BlockSpec((blk, HEAD_PAD), lambda h, p, qt, kt: (kt[p], h)),
                  pl.BlockSpec((blk, V_HEAD), lambda h, p, qt, kt: (kt[p], h))],
        out_specs=[pl.BlockSpec((blk, V_HEAD), lambda h, p, qt, kt: (qt[p], h)),
                   pl.BlockSpec((SUBLANES, blk), lambda h, p, qt, kt: (h, qt[p]))],
        scratch_shapes=[pltpu.VMEM((1, blk), F32), pltpu.VMEM((1, blk), F32), pltpu.VMEM((V_HEAD, blk), F32)])
    return pl.pallas_call(body, grid_spec=grid_spec,
                          out_shape=[jax.ShapeDtypeStruct((T, N_HEADS * V_HEAD), BF16),
                                     jax.ShapeDtypeStruct((N_HEADS * SUBLANES, T), F32)],
                          name=name, compiler_params=_cparams("parallel", "arbitrary"))(qtab, ktab, q, k, v)


def _attn_delta(do, o, *, name, tr):
    T = do.shape[0]

    def body(do_ref, o_ref, d_ref):
        prod = do_ref[...].astype(F32) * o_ref[...].astype(F32)
        lane_head = lax.broadcasted_iota(jnp.int32, (tr, LANES), 1) // SUBLANES
        mat = jnp.zeros((tr, LANES), F32)
        for h in range(N_HEADS):
            d_h = jnp.sum(prod[:, h * V_HEAD:(h + 1) * V_HEAD], axis=1, keepdims=True)
            mat = jnp.where(lane_head == h, d_h, mat)
        d_ref[...] = mat.T[:N_HEADS * SUBLANES, :]

    return pl.pallas_call(body, grid=(T // tr,),
                          in_specs=[pl.BlockSpec((tr, N_HEADS * V_HEAD), lambda i: (i, 0)),
                                    pl.BlockSpec((tr, N_HEADS * V_HEAD), lambda i: (i, 0))],
                          out_specs=pl.BlockSpec((N_HEADS * SUBLANES, tr), lambda i: (0, i)),
                          out_shape=jax.ShapeDtypeStruct((N_HEADS * SUBLANES, T), F32), name=name,
                          compiler_params=_cparams("parallel"))(do, o)


def _flash_bwd(q, k, v, lse, delta, do, cos_t, sin_t, *, name, blk):
    T = q.shape[0]
    n = T // blk
    qc = min(QUERY_CHUNK, blk)
    qtab, ktab, n_pairs = _causal_pairs(n, k_major=True)

    def body(qt, kt, q_ref, k_ref, v_ref, lse_ref, delta_ref, do_ref, cos_ref, sin_ref, dq_ref, dk_ref, dv_ref,
             dq_s, dk_s, dv_s):
        p = pl.program_id(1)
        qi, ki = qt[p], kt[p]
        first = qi == ki

        @pl.when(p == 0)
        def _():
            dq_s[...] = jnp.zeros_like(dq_s)

        @pl.when(first)
        def _():
            dk_s[...] = jnp.zeros_like(dk_s)
            dv_s[...] = jnp.zeros_like(dv_s)

        def step(diag):
            kv, vv = k_ref[...], v_ref[...]
            chunks = [slice(c * qc, (c + 1) * qc) for c in range(blk // qc)]
            qs = [q_ref[rows, :] for rows in chunks]
            dos = [do_ref[rows, :] for rows in chunks]
            keys = [(c + 1) * qc if diag else blk for c in range(blk // qc)]
            scores = [_scores_t(q_c, kv[:n], diag) for q_c, n in zip(qs, keys)]
            dps = [lax.dot_general(vv[:n], do_c, NT_DIMS, preferred_element_type=F32) for do_c, n in zip(dos, keys)]
            probs, dss = [], []
            for rows, s_t, dp_t in zip(chunks, scores, dps):
                p_t = jnp.exp2(s_t - lse_ref[0:1, rows])
                dss.append((p_t * (dp_t - delta_ref[0:1, rows])).astype(BF16))
                probs.append(p_t.astype(BF16))
            dv_acc = dk_acc = None
            for n, p_t, ds_t, q_c, do_c in zip(keys, probs, dss, qs, dos):
                dv_c = jnp.dot(p_t, do_c, preferred_element_type=F32)
                dk_c = jnp.dot(ds_t, q_c, preferred_element_type=F32)
                if diag:
                    dv_s[:n, :] += dv_c
                    dk_s[:n, :] += dk_c
                else:
                    dv_acc = dv_c if dv_acc is None else dv_acc + dv_c
                    dk_acc = dk_c if dk_acc is None else dk_acc + dk_c
            for rows, n, ds_t in zip(chunks, keys, dss):
                dq_s[qi, :, rows] += lax.dot_general(kv[:n], ds_t, TN_DIMS, preferred_element_type=F32)
            if not diag:
                dv_s[...] += dv_acc
                dk_s[...] += dk_acc

        @pl.when(jnp.logical_not(first))
        def _():
            step(False)

        @pl.when(first)
        def _():
            step(True)
            dq_t = (dq_s[qi] * ATTN_SCALE).T
            dq_ref[:, :QK_NOPE] = dq_t[:, :QK_NOPE].astype(dq_ref.dtype)
            dq_ref[:, QK_NOPE:] = _rope_bwd(dq_t[:, QK_NOPE:], cos_ref[...], sin_ref[...]).astype(dq_ref.dtype)

        @pl.when(qi == n - 1)
        def _():
            dk_ref[...] = (dk_s[...] * (1.0 / LOG2_E)).astype(dk_ref.dtype)
            dv_ref[...] = dv_s[...].astype(dv_ref.dtype)

    qmap = lambda h, p, qt, kt: (qt[p], h)
    kmap = lambda h, p, qt, kt: (kt[p], h)
    smap = lambda h, p, qt, kt: (h, qt[p])
    tmap = lambda h, p, qt, kt: (kt[p], 0)
    grid_spec = pltpu.PrefetchScalarGridSpec(
        num_scalar_prefetch=2, grid=(N_HEADS, n_pairs),
        in_specs=[pl.BlockSpec((blk, HEAD_PAD), qmap), pl.BlockSpec((blk, HEAD_PAD), kmap),
                  pl.BlockSpec((blk, V_HEAD), kmap), pl.BlockSpec((SUBLANES, blk), smap),
                  pl.BlockSpec((SUBLANES, blk), smap), pl.BlockSpec((blk, V_HEAD), qmap),
                  pl.BlockSpec((blk, LANES), tmap), pl.BlockSpec((blk, LANES), tmap)],
        out_specs=[pl.BlockSpec((blk, HEAD_PAD), kmap), pl.BlockSpec((blk, HEAD_PAD), kmap),
                   pl.BlockSpec((blk, V_HEAD), kmap)],
        scratch_shapes=[pltpu.VMEM((n, HEAD_PAD, blk), F32), pltpu.VMEM((blk, HEAD_PAD), F32),
                        pltpu.VMEM((blk, V_HEAD), F32)])
    return pl.pallas_call(body, grid_spec=grid_spec,
                          out_shape=[jax.ShapeDtypeStruct((T, N_HEADS * HEAD_PAD), BF16),
                                     jax.ShapeDtypeStruct((T, N_HEADS * HEAD_PAD), BF16),
                                     jax.ShapeDtypeStruct((T, N_HEADS * V_HEAD), BF16)],
                          name=name, compiler_params=_cparams("arbitrary", "arbitrary"))(
                              qtab, ktab, q, k, v, lse, delta, do, cos_t, sin_t)


MESH_ID = pl.DeviceIdType.MESH
ANY_SPEC = pl.BlockSpec(memory_space=pl.ANY)


def _other_chips(x, y):
    out = []
    for dx, dy in ((1, 0), (0, 1), (1, 1)):
        px = x ^ dx if dx else x
        py = y ^ dy if dy else y
        out.append((px, py, 2 * px + py))
    return out


def _gather_weights(flat):
    rh = flat.shape[0] // 2
    rq = rh // 2

    def body(src2, out, send_sems, recv_sems):
        x, y, c = lax.axis_index("x"), lax.axis_index("y"), lax.axis_index("c")
        me = 2 * x + y
        sib = (x, y, 1 - c)
        (xx, xy, kx), (yx, yy, ky), (_, _, kd) = _other_chips(x, y)
        x_nbr, y_nbr = (xx, xy, c), (yx, yy, c)
        first, last = pl.ds(0, rq), pl.ds(rq, rq)

        def copy(j, src, dst, to):
            return pltpu.make_async_remote_copy(src_ref=src, dst_ref=dst, send_sem=send_sems.at[j], recv_sem=recv_sems.at[j],
                                                device_id=to, device_id_type=MESH_ID)

        def arrived(j, land):
            copy(j, land, land, sib).wait_recv()

        sends = [copy(0, src2.at[c], out.at[me, c], x_nbr), copy(1, src2.at[c], out.at[me, c], y_nbr)]
        for cp in sends:
            cp.start()
        landings = [(0, out.at[kx, c]), (1, out.at[ky, c]), (2, out.at[kd, c, first]), (3, out.at[kd, c, last])]
        relays = {0: (2, out.at[kx, c, first], y_nbr), 1: (3, out.at[ky, c, last], x_nbr)}
        for j, land in landings:
            arrived(j, land)
            if j in relays:
                rj, piece, to = relays[j]
                sends.append(copy(rj, piece, piece, to))
                sends[-1].start()
            sends.append(copy(4 + j, land, land, sib))
            sends[-1].start()
        for j, land in [(0, out.at[kx, 1 - c]), (1, out.at[ky, 1 - c]), (2, out.at[kd, 1 - c, first]),
                        (3, out.at[kd, 1 - c, last])]:
            arrived(4 + j, land)
        for cp in sends:
            cp.wait_send()

    return pl.pallas_call(body, out_shape=jax.ShapeDtypeStruct((N_CHIPS, 2, rh, PACK_LANES), flat.dtype),
                          in_specs=[ANY_SPEC], out_specs=ANY_SPEC,
                          scratch_shapes=[pltpu.SemaphoreType.DMA((8,)), pltpu.SemaphoreType.DMA((8,))],
                          name="gather_weights")(flat.reshape(2, rh, PACK_LANES))


def _swap_halves(gs):
    rh = gs[0].shape[1]

    def body(*refs):
        srcs, out, send_sems, recv_sems = refs[:N_CHIPS], refs[N_CHIPS], refs[N_CHIPS + 1], refs[N_CHIPS + 2]
        x, y, c = lax.axis_index("x"), lax.axis_index("y"), lax.axis_index("c")
        copies = [pltpu.make_async_remote_copy(src_ref=srcs[k].at[1 - c], dst_ref=out.at[k], send_sem=send_sems.at[k],
                                               recv_sem=recv_sems.at[k], device_id=(x, y, 1 - c), device_id_type=MESH_ID)
                  for k in range(N_CHIPS)]
        for cp in copies:
            cp.start()
        for cp in copies:
            cp.wait()

    return pl.pallas_call(body, out_shape=jax.ShapeDtypeStruct((N_CHIPS, rh, PACK_LANES), gs[0].dtype),
                          in_specs=[ANY_SPEC] * N_CHIPS, out_specs=ANY_SPEC,
                          scratch_shapes=[pltpu.SemaphoreType.DMA((N_CHIPS,)), pltpu.SemaphoreType.DMA((N_CHIPS,))],
                          name="grad_swap_halves")(*gs)


def _add_halves(gs, got, core, *, tr):
    rh = gs[0].shape[1]

    def body(*refs):
        g_refs, got_ref, o_ref = refs[1:1 + N_CHIPS], refs[1 + N_CHIPS], refs[2 + N_CHIPS]
        for k in range(N_CHIPS):
            o_ref[k] = (g_refs[k][0] + got_ref[k]).astype(o_ref.dtype)

    grid_spec = pltpu.PrefetchScalarGridSpec(
        num_scalar_prefetch=1, grid=(rh // tr,),
        in_specs=[pl.BlockSpec((1, tr, PACK_LANES), lambda i, c_ref: (c_ref[0], i, 0))] * N_CHIPS
        + [pl.BlockSpec((N_CHIPS, tr, PACK_LANES), lambda i, c_ref: (0, i, 0))],
        out_specs=pl.BlockSpec((N_CHIPS, tr, PACK_LANES), lambda i, c_ref: (0, i, 0)))
    return pl.pallas_call(body, grid_spec=grid_spec,
                          out_shape=jax.ShapeDtypeStruct((N_CHIPS, rh, PACK_LANES), BF16), name="grad_add_halves",
                          compiler_params=_cparams("parallel"))(core.reshape(1), *gs, got)


def _scatter_partials(part):
    rh = part.shape[1]

    def body(src, out, send_sems, recv_sems, local_sem):
        x, y, c = lax.axis_index("x"), lax.axis_index("y"), lax.axis_index("c")
        me = 2 * x + y
        own = pltpu.make_async_copy(src.at[me], out.at[me], local_sem)
        own.start()
        chips = _other_chips(x, y)
        sends = []
        for j, (px, py, pk) in enumerate(chips):
            cp = pltpu.make_async_remote_copy(src_ref=src.at[pk], dst_ref=out.at[me], send_sem=send_sems.at[j],
                                              recv_sem=recv_sems.at[j], device_id=(px, py, c), device_id_type=MESH_ID)
            cp.start()
            sends.append(cp)
        for j, (px, py, pk) in enumerate(chips):
            land = out.at[pk]
            pltpu.make_async_remote_copy(src_ref=land, dst_ref=land, send_sem=send_sems.at[j], recv_sem=recv_sems.at[j],
                                         device_id=(px, py, c), device_id_type=MESH_ID).wait_recv()
        for cp in sends:
            cp.wait_send()
        own.wait()

    return pl.pallas_call(body, out_shape=jax.ShapeDtypeStruct((N_CHIPS, rh, PACK_LANES), part.dtype),
                          in_specs=[ANY_SPEC], out_specs=ANY_SPEC,
                          scratch_shapes=[pltpu.SemaphoreType.DMA((3,)), pltpu.SemaphoreType.DMA((3,)),
                                          pltpu.SemaphoreType.DMA(())],
                          name="grad_scatter_partials")(part)


def _sum_chips(q, *, tr):
    rh = q.shape[1]

    def body(q_ref, o_ref):
        parts = [q_ref[k].astype(F32) for k in range(N_CHIPS)]
        o_ref[...] = ((parts[0] + parts[1]) + parts[2]) + parts[3]

    return pl.pallas_call(body, grid=(rh // tr,),
                          in_specs=[pl.BlockSpec((N_CHIPS, tr, PACK_LANES), lambda i: (0, i, 0))],
                          out_specs=pl.BlockSpec((tr, PACK_LANES), lambda i: (i, 0)),
                          out_shape=jax.ShapeDtypeStruct((rh, PACK_LANES), F32), name="grad_sum_chips",
                          compiler_params=_cparams("parallel"))(q)


def _send_half(half):
    def body(src, out, send_sem, recv_sem):
        x, y, c = lax.axis_index("x"), lax.axis_index("y"), lax.axis_index("c")
        cp = pltpu.make_async_remote_copy(src_ref=src, dst_ref=out, send_sem=send_sem, recv_sem=recv_sem,
                                          device_id=(x, y, 1 - c), device_id_type=MESH_ID)
        cp.start()
        cp.wait()

    return pl.pallas_call(body, out_shape=jax.ShapeDtypeStruct(half.shape, half.dtype),
                          in_specs=[ANY_SPEC], out_specs=ANY_SPEC,
                          scratch_shapes=[pltpu.SemaphoreType.DMA(()), pltpu.SemaphoreType.DMA(())],
                          name="grad_send_half")(half)


def _both_halves(own, got, core, *, tr):
    rh = own.shape[0]

    def body(c_ref, own_ref, got_ref, o_ref):
        mine = pl.program_id(0) == c_ref[0]
        o_ref[0] = jnp.where(mine, own_ref[...], got_ref[...])

    grid_spec = pltpu.PrefetchScalarGridSpec(
        num_scalar_prefetch=1, grid=(2, rh // tr),
        in_specs=[pl.BlockSpec((tr, PACK_LANES), lambda h, i, c_ref: (i, 0)),
                  pl.BlockSpec((tr, PACK_LANES), lambda h, i, c_ref: (i, 0))],
        out_specs=pl.BlockSpec((1, tr, PACK_LANES), lambda h, i, c_ref: (h, i, 0)))
    return pl.pallas_call(body, grid_spec=grid_spec, out_shape=jax.ShapeDtypeStruct((2, rh, PACK_LANES), own.dtype),
                          name="grad_both_halves", compiler_params=_cparams("parallel", "parallel"))(core.reshape(1), own, got)


def _allreduce_small(v):
    rows = v.shape[0]

    def body(v_ref, o_ref, buf, send_sems, recv_sems):
        x, y, c = lax.axis_index("x"), lax.axis_index("y"), lax.axis_index("c")
        me = 4 * x + 2 * y + c
        buf[me] = v_ref[...]
        sends = []
        for j in range(1, N_DEV):
            px, py, pc = x ^ ((j >> 2) & 1), y ^ ((j >> 1) & 1), c ^ (j & 1)
            cp = pltpu.make_async_remote_copy(src_ref=v_ref, dst_ref=buf.at[me], send_sem=send_sems.at[j - 1],
                                              recv_sem=recv_sems.at[j - 1], device_id=(px, py, pc), device_id_type=MESH_ID)
            cp.start()
            sends.append(cp)
        for j in range(1, N_DEV):
            px, py, pc = x ^ ((j >> 2) & 1), y ^ ((j >> 1) & 1), c ^ (j & 1)
            land = buf.at[4 * px + 2 * py + pc]
            pltpu.make_async_remote_copy(src_ref=land, dst_ref=land, send_sem=send_sems.at[j - 1],
                                         recv_sem=recv_sems.at[j - 1], device_id=(px, py, pc),
                                         device_id_type=MESH_ID).wait_recv()
        for cp in sends:
            cp.wait_send()
        acc = buf[0]
        for d in range(1, N_DEV):
            acc = acc + buf[d]
        o_ref[...] = acc

    vm = pl.BlockSpec(memory_space=pltpu.VMEM)
    return pl.pallas_call(body, out_shape=jax.ShapeDtypeStruct((rows, LANES), F32), in_specs=[vm], out_specs=vm,
                          scratch_shapes=[pltpu.VMEM((N_DEV, rows, LANES), F32), pltpu.SemaphoreType.DMA((N_DEV - 1,)),
                                          pltpu.SemaphoreType.DMA((N_DEV - 1,))],
                          name="allreduce_small")(v)


def _adamw(w, g, m, v, *, name):
    shape = w.shape
    cols = shape[-1] if w.ndim > 1 else shape[0]
    rows = w.size // cols
    w2, g2, m2, v2 = (t.reshape(rows, cols) for t in (w, g, m, v))
    tr = rows if rows <= 256 else _tile(rows, 256, 8)

    def fn(i, wv, gv, mv, vv):
        mn = ADAM_B1 * mv + (1.0 - ADAM_B1) * gv
        vn = ADAM_B2 * vv + (1.0 - ADAM_B2) * (gv * gv)
        m_hat = mn / (1.0 - ADAM_B1 ** ADAM_STEP)
        v_hat = vn / (1.0 - ADAM_B2 ** ADAM_STEP)
        delta = -ADAM_LR * (m_hat / (jnp.sqrt(v_hat) + ADAM_EPS) + ADAM_WD * wv)
        return delta, mn, vn

    ins = [_rspec(t, tr) for t in (w2, g2, m2, v2)]
    d, mn, vn = _row_call(fn, rows, tr, ins, [(cols, F32)] * 3, name=name)
    return d.reshape(shape), mn.reshape(shape), vn.reshape(shape)


def _rope_cols(w):
    z = jnp.zeros(w.shape[:-1] + (32,), w.dtype)
    return jnp.concatenate([w[..., :32], z, w[..., 32:], z], axis=-1)


def _rope_cols_inv(w):
    return jnp.concatenate([w[..., :32], w[..., 64:96]], axis=-1)


def _layer_layouts(W, i):
    w_in = W['w_in'][i]
    u, cq, ckv = w_in[:, :512], w_in[:, 512:1024], w_in[:, 1024:1280]
    kr, ga, gb = w_in[:, 1280:1344], w_in[:, 1344:2368], w_in[:, 2368:]
    L = {}
    L['w_in'] = jnp.concatenate([ga, gb, u, cq, ckv, _rope_cols(kr)], axis=1)
    wq = W['w_uq'][i]
    L['w_q'] = jnp.concatenate([wq[..., :QK_NOPE], _rope_cols(wq[..., QK_NOPE:])], axis=-1).reshape(Q_LORA, -1)
    wkv = W['w_ukv'][i]
    L['w_k'] = jnp.concatenate([wkv[..., :QK_NOPE], jnp.zeros_like(wkv[..., :LANES])], axis=-1).reshape(KV_LORA, -1)
    L['w_v'] = wkv[..., QK_NOPE:].reshape(KV_LORA, -1)
    L['w_gu'] = jnp.concatenate([W['w_gate'][i], W['w_up'][i]], axis=1)
    for n in ('w_a', 'w_b', 'w_o', 'w_down', 'w_ple_gate', 'w_ple'):
        L[n] = W[n][i]
    return L


def _layer_grads_to_reference_layout(G):
    d = G['w_in']
    ga, gb, u = d[:, ZC_GA:ZC_GB], d[:, ZC_GB:ZC_U], d[:, ZC_U:ZC_CQ]
    cq, ckv, kr = d[:, ZC_CQ:ZC_CKV], d[:, ZC_CKV:ZC_KR], _rope_cols_inv(d[:, ZC_KR:])
    out = {'w_in': jnp.concatenate([u, cq, ckv, kr, ga, gb], axis=1)}
    dq = G['w_q'].reshape(Q_LORA, N_HEADS, HEAD_PAD)
    out['w_uq'] = jnp.concatenate([dq[..., :QK_NOPE], _rope_cols_inv(dq[..., QK_NOPE:])], axis=-1)
    dk = G['w_k'].reshape(KV_LORA, N_HEADS, HEAD_PAD)[..., :QK_NOPE]
    dv = G['w_v'].reshape(KV_LORA, N_HEADS, V_HEAD)
    out['w_ukv'] = jnp.concatenate([dk, dv], axis=-1)
    out['w_gate'], out['w_up'] = G['w_gu'][:, :D_FF], G['w_gu'][:, D_FF:]
    for n in ('w_a', 'w_b', 'w_o', 'w_down', 'w_ple_gate', 'w_ple'):
        out[n] = G[n]
    return out


PACK_ROWS = 2048


def _pack_rows(parts, row_mult):
    flat = jnp.concatenate([p.reshape(-1) for p in parts])
    n = flat.shape[0]
    per = LANES * row_mult
    padded = -(-n // per) * per
    return jnp.pad(flat, (0, padded - n)).reshape(-1, LANES)


def _unpack_rows(flat2d, shapes):
    flat = flat2d.reshape(-1)
    out, off = [], 0
    for s in shapes:
        n = int(np.prod(s))
        out.append(flat[off:off + n].reshape(s))
        off += n
    return out


PACK_LANES = 256


def _lane_blocks(cols):
    return -(-cols // PACK_LANES)


def _pack_blocks(parts, row_mult):
    blocks = []
    for p in parts:
        p2 = p.reshape(-1, p.shape[-1])
        cols = p2.shape[1]
        nb = _lane_blocks(cols)
        if nb * PACK_LANES != cols:
            p2 = jnp.pad(p2, ((0, 0), (0, nb * PACK_LANES - cols)))
        blocks += [p2[:, j * PACK_LANES:(j + 1) * PACK_LANES] for j in range(nb)]
    buf = jnp.concatenate(blocks, axis=0)
    rows = buf.shape[0]
    padded = -(-rows // row_mult) * row_mult
    return buf if padded == rows else jnp.pad(buf, ((0, padded - rows), (0, 0)))


def _unpack_blocks(buf, shapes):
    out, off = [], 0
    for s in shapes:
        rows, cols = int(np.prod(s[:-1])), s[-1]
        nb = _lane_blocks(cols)
        piece = jnp.concatenate([buf[off + j * rows:off + (j + 1) * rows] for j in range(nb)], axis=1)
        out.append(piece[:, :cols].reshape(s))
        off += nb * rows
    return out


def _layer_fwd(i, x, p_i, L, norms, w_pool_bf, pool_scale, cos_t, sin_t, tr, blk):
    sv = {'x': x}
    z, sv['h'] = _mm_nn(x, L['w_in'], name=f"l{i}_in_proj", outs=[(Z_WIDTH, F32)], gain=norms['norm_mix'], emit_a=True,
                        tm=tr // 2)
    sv['z'] = z
    sv['pooled'], sv['mixed'], sv['pm'] = _pool_fwd(z, w_pool_bf, pool_scale, name=f"l{i}_pool", tr=tr)
    sv['ya'] = _mm_nn(sv['pm'], L['w_a'], name=f"l{i}_ya", outs=[(D_MODEL, BF16)], tm=tr)

    def heads(acc, rope_part):
        out = []
        for h in range(N_HEADS):
            out.append(acc[:, h * HEAD_PAD:h * HEAD_PAD + QK_NOPE])
            out.append(rope_part(acc[:, h * HEAD_PAD + QK_NOPE:(h + 1) * HEAD_PAD]))
        return jnp.concatenate(out, axis=1)

    def q_epi(acc, ct, st):
        return (heads(acc * EXP2_SCALE, lambda t: _rope(t, ct, st)),)

    def k_epi(acc, kr, ct, st):
        k_pe = _rope(kr, ct, st)
        return (heads(acc, lambda t: k_pe),)

    rope_rows = [(cos_t, LANES, 0), (sin_t, LANES, 0)]
    qk_width = N_HEADS * HEAD_PAD
    sv['q'], sv['cqn'] = _mm_nn(z, L['w_q'], name=f"l{i}_q_proj", outs=[(qk_width, BF16)], a_col=ZC_CQ,
                                gain=norms['q_norm'], emit_a=True, epi=q_epi, epi_rows=rope_rows, tm=tr)
    sv['k'], sv['ckvn'] = _mm_nn(z, L['w_k'], name=f"l{i}_k_proj", outs=[(qk_width, BF16)], a_col=ZC_CKV,
                                 gain=norms['kv_norm'], emit_a=True, epi=k_epi,
                                 epi_rows=[(z, LANES, ZC_KR // LANES)] + rope_rows, tm=tr)
    sv['v'] = _mm_nn(sv['ckvn'], L['w_v'], name=f"l{i}_v_proj", outs=[(N_HEADS * V_HEAD, BF16)], tm=tr)
    sv['o'], sv['lse'] = _flash_fwd(sv['q'], sv['k'], sv['v'], name=f"l{i}_attn",
                                    blk=min(ATTN_BLOCK_FWD, max(blk * ATTN_BLOCK_FWD // ATTN_BLOCK, 128)))

    def merge_epi(yb, ga, gb, ya):
        return yb, _sigmoid(ga) * ya.astype(F32) + _sigmoid(gb) * yb

    sv['yb'], sv['merged'] = _mm_nn(sv['o'], L['w_b'], name=f"l{i}_yb_merge", outs=[(D_MODEL, BF16), (D_MODEL, BF16)],
                                    epi=merge_epi, epi_rows=[(z, D_MODEL, 0), (z, D_MODEL, 1), (sv['ya'], D_MODEL, 0)],
                                    tm=tr)

    def add_epi(acc, res):
        return (acc + res,)

    x1 = _mm_nn(sv['merged'], L['w_o'], name=f"l{i}_wo", outs=[(D_MODEL, F32)], epi=add_epi, epi_rows=[(x, D_MODEL, 0)],
                tm=tr)
    sv['x1'] = x1

    def swiglu_epi(acc):
        g, u = acc[:, :D_FF], acc[:, D_FF:]
        return acc, g * _sigmoid(g) * u

    sv['gu'], sv['act'], sv['h2'] = _mm_nn(x1, L['w_gu'], name=f"l{i}_gate_up", outs=[(2 * D_FF, BF16), (D_FF, BF16)],
                                           gain=norms['norm_ffn'], emit_a=True, epi=swiglu_epi, tm=tr // 2)
    x2 = _mm_nn(sv['act'], L['w_down'], name=f"l{i}_down", outs=[(D_MODEL, F32)], epi=add_epi,
                epi_rows=[(x1, D_MODEL, 0)], tm=tr)
    sv['x2'] = x2
    sv['logit'], sv['h3'] = _mm_nn(x2, L['w_ple_gate'], name=f"l{i}_ple_gate", outs=[(D_MODEL, F32)],
                                   gain=norms['norm_ple'], emit_a=True, tm=tr)

    def ple_epi(pe, xv, lg):
        return pe, xv + _sigmoid(lg) * pe

    sv['pe'], x3 = _mm_nn(p_i, L['w_ple'], name=f"l{i}_ple", outs=[(D_MODEL, F32), (D_MODEL, F32)], epi=ple_epi,
                          epi_rows=[(x2, D_MODEL, 0), (sv['logit'], D_MODEL, 0)], tm=tr)
    return x3, sv


def _layer_bwd(i, dx3, sv, p_i, L, norms, w_pool_bf, pool_scale, cos_t, sin_t, tr, blk):
    T = dx3.shape[0]
    G = {}
    z = sv['z']

    def ple_bwd(_, d, lg, pe):
        g = _sigmoid(lg)
        return d * pe * g * (1.0 - g), d * g

    dlogit, dpe = _row_call(ple_bwd, T, tr, [_rspec(dx3, tr), _rspec(sv['logit'], tr), _rspec(sv['pe'], tr)],
                            [(D_MODEL, BF16), (D_MODEL, BF16)], name=f"l{i}_ple_bwd")
    G['w_ple_gate'] = _mm_tn(sv['h3'], dlogit, name=f"l{i}_dw_ple_gate", tn=1024)
    G['w_ple'] = _mm_tn(p_i, dpe, name=f"l{i}_dw_ple", tn=1024)
    def gain_row(n):
        return norms[n].reshape(1, -1).astype(F32)

    dx2, dx2_bf, G['norm_ple'] = _mm_nt([(dlogit, L['w_ple_gate'])], name=f"l{i}_dh3_norm_bwd",
                                        outs=[(D_MODEL, F32), (D_MODEL, BF16)], epi=_rms_bwd_epi(True, True),
                                        epi_rows=[(sv['x2'], D_MODEL, 0), (dx3, D_MODEL, 0)], consts=[gain_row('norm_ple')],
                                        accs=[(1, D_MODEL)], tm=tr)

    def swiglu_bwd_epi(da, gu):
        g, u = gu[:, :D_FF].astype(F32), gu[:, D_FF:].astype(F32)
        sg = _sigmoid(g)
        return (jnp.concatenate([da * u * sg * (1.0 + g * (1.0 - sg)), da * g * sg], axis=1),)

    dgu = _mm_nt([(dx2_bf, L['w_down'])], name=f"l{i}_dact_swiglu_bwd", outs=[(2 * D_FF, BF16)], epi=swiglu_bwd_epi,
                 epi_rows=[(sv['gu'], 2 * D_FF, 0)], tm=tr // 2)
    G['w_down'] = _mm_tn(sv['act'], dx2_bf, name=f"l{i}_dw_down", tk=1408, tn=1024)
    G['w_gu'] = _mm_tn(sv['h2'], dgu, name=f"l{i}_dw_gate_up", tn=1408)
    dx1, dx1_bf, G['norm_ffn'] = _mm_nt([(dgu, L['w_gu'])], name=f"l{i}_dh2_norm_bwd",
                                        outs=[(D_MODEL, F32), (D_MODEL, BF16)], epi=_rms_bwd_epi(True, True),
                                        epi_rows=[(sv['x1'], D_MODEL, 0), (dx2, D_MODEL, 0)], consts=[gain_row('norm_ffn')],
                                        accs=[(1, D_MODEL)], tm=tr // 2)

    def merge_bwd_epi(dm, ga, gb, ya, yb):
        sa, sb = _sigmoid(ga), _sigmoid(gb)
        ya, yb = ya.astype(F32), yb.astype(F32)
        d_gates = jnp.concatenate([dm * ya * sa * (1.0 - sa), dm * yb * sb * (1.0 - sb)], axis=1)
        return d_gates, dm * sa, dm * sb

    dz, dya, dyb = _mm_nt([(dx1_bf, L['w_o'])], name=f"l{i}_dmerged_bwd",
                          outs=[(2 * D_MODEL, BF16), (D_MODEL, BF16), (D_MODEL, BF16)], epi=merge_bwd_epi,
                          epi_rows=[(z, D_MODEL, 0), (z, D_MODEL, 1), (sv['ya'], D_MODEL, 0), (sv['yb'], D_MODEL, 0)],
                          tm=tr, wide0=Z_WIDTH)
    G['w_o'] = _mm_tn(sv['merged'], dx1_bf, name=f"l{i}_dw_o", tn=1024)

    G['w_b'] = _mm_tn(sv['o'], dyb, name=f"l{i}_dw_b", tn=1024)
    do = _mm_nt([(dyb, L['w_b'])], name=f"l{i}_do", outs=[(D_MODEL, BF16)], tm=tr)
    delta = _attn_delta(do, sv['o'], name=f"l{i}_attn_delta", tr=tr)
    dq, dk, dv = _flash_bwd(sv['q'], sv['k'], sv['v'], sv['lse'], delta, do, cos_t, sin_t, name=f"l{i}_attn_bwd", blk=blk)

    def dk_rope(_, d, ct, st):
        d = d.astype(F32)
        acc = d[:, QK_NOPE:HEAD_PAD]
        for h in range(1, N_HEADS):
            acc = acc + d[:, h * HEAD_PAD + QK_NOPE:(h + 1) * HEAD_PAD]
        return _rope_bwd(acc, ct, st)

    dz = _row_call(dk_rope, T, tr, [_rspec(dk, tr), _rspec(cos_t, tr), _rspec(sin_t, tr)], [(LANES, BF16)],
                   name=f"l{i}_dk_rope", into=(dz, LANES, ZC_KR // LANES))[0]
    G['w_q'] = _mm_tn(sv['cqn'], dq, name=f"l{i}_dw_q", tn=1024)
    G['w_k'] = _mm_tn(sv['ckvn'], dk, name=f"l{i}_dw_k", tn=1024)
    G['w_v'] = _mm_tn(sv['ckvn'], dv, name=f"l{i}_dw_v", tn=1024)
    dz, G['q_norm'] = _mm_nt([(dq, L['w_q'])], name=f"l{i}_dcq", outs=[(Q_LORA, BF16)], epi=_rms_bwd_epi(False, False),
                             epi_rows=[(z, Q_LORA, ZC_CQ // Q_LORA)], consts=[gain_row('q_norm')], accs=[(1, Q_LORA)], tm=tr,
                             into=(dz, Q_LORA, ZC_CQ // Q_LORA))
    dz, G['kv_norm'] = _mm_nt([(dk, L['w_k']), (dv, L['w_v'])], name=f"l{i}_dckv", outs=[(KV_LORA, BF16)],
                              epi=_rms_bwd_epi(False, False), epi_rows=[(z, KV_LORA, ZC_CKV // KV_LORA)],
                              consts=[gain_row('kv_norm')], accs=[(1, KV_LORA)], tm=tr,
                              into=(dz, KV_LORA, ZC_CKV // KV_LORA))

    G['w_a'] = _mm_tn(sv['pm'], dya, name=f"l{i}_dw_a", tn=1024)
    dpm = _mm_nt([(dya, L['w_a'])], name=f"l{i}_dpm", outs=[(POOL_WIDTH, F32)], tm=tr)
    dpool, dpool_cnt, G['pool_scale'], G['w_pool'] = _pool_bwd_mix(dpm, sv['mixed'], sv['pooled'], w_pool_bf, pool_scale,
                                                                   name=f"l{i}_pool_bwd_mix", tr=tr)
    dz = _pool_bwd_window(dpool, dpool_cnt, name=f"l{i}_pool_bwd_window", tr=tr, into=(dz, POOL_WIDTH, ZC_U // POOL_WIDTH))

    G['w_in'] = _mm_tn(sv['h'], dz, name=f"l{i}_dw_in", tn=1152)
    dx, G['norm_mix'] = _mm_nt([(dz, L['w_in'])], name=f"l{i}_dh_norm_bwd", outs=[(D_MODEL, F32)],
                               epi=_rms_bwd_epi(True, False), epi_rows=[(sv['x'], D_MODEL, 0), (dx1, D_MODEL, 0)],
                               consts=[gain_row('norm_mix')], accs=[(1, D_MODEL)], tm=tr)
    return dx, G


def kernel(x, p, positions, norm_mix, w_in, w_pool, pool_scale, q_norm, kv_norm, w_uq, w_ukv, w_a, w_b, w_o, norm_ffn, w_gate, w_up, w_down, norm_ple, w_ple_gate, w_ple, final_norm, loss_target, m_norm_mix, m_w_in, m_w_pool, m_pool_scale, m_q_norm, m_kv_norm, m_w_uq, m_w_ukv, m_w_a, m_w_b, m_w_o, m_norm_ffn, m_w_gate, m_w_up, m_w_down, m_norm_ple, m_w_ple_gate, m_w_ple, m_final_norm, v_norm_mix, v_w_in, v_w_pool, v_pool_scale, v_q_norm, v_kv_norm, v_w_uq, v_w_ukv, v_w_a, v_w_b, v_w_o, v_norm_ffn, v_w_gate, v_w_up, v_w_down, v_norm_ple, v_w_ple_gate, v_w_ple, v_final_norm):
    given = dict(locals())
    weights = {n: given[n] for n in WEIGHTS}
    T = x.shape[1]
    tr = min(512, max(T // 2, 8))
    blk = min(ATTN_BLOCK, max(T // 4, 128))
    x0 = x.reshape(T, D_MODEL)
    target = loss_target.reshape(T, D_MODEL)

    names = list(SHARDED)
    shard_shapes = [weights[n].shape for n in names]
    flat = _pack_blocks([weights[n].astype(BF16) for n in names], row_mult=PACK_ROWS)
    R = flat.shape[0]
    chip = (2 * lax.axis_index("x") + lax.axis_index("y")).astype(jnp.int32)
    core = lax.axis_index("c").astype(jnp.int32)
    gathered = _gather_weights(flat).reshape(N_CHIPS, R, PACK_LANES)
    gathered = lax.dynamic_update_slice(gathered, flat.reshape(1, R, PACK_LANES), (chip, 0, 0))
    per_chip = [_unpack_blocks(gathered[k], shard_shapes) for k in range(N_CHIPS)]
    W = {n: jnp.concatenate([per_chip[k][j] for k in range(N_CHIPS)], axis=SHARDED[n]) for j, n in enumerate(names)}
    layouts = [_layer_layouts(W, i) for i in range(DEPTH)]
    w_pool_bf = w_pool.astype(BF16)

    inv_freq = 1.0 / (ROPE_THETA ** (jnp.arange(0, QK_ROPE, 2, dtype=F32) / QK_ROPE))
    zero32 = jnp.zeros((32,), F32)
    freq_row = jnp.concatenate([inv_freq, zero32, inv_freq, zero32]).reshape(1, LANES)
    cos_mask = jnp.concatenate([jnp.ones((32,), F32), zero32, jnp.ones((32,), F32), zero32]).reshape(1, LANES)
    sin_sign = jnp.concatenate([-jnp.ones((32,), F32), zero32, jnp.ones((32,), F32), zero32]).reshape(1, LANES)

    def rope_tables(_, pos, fr, cm, ss):
        ang = pos.astype(F32) * fr
        return jnp.cos(ang) * cm, jnp.sin(ang) * ss

    pos_col = positions.reshape(T, 1)
    cos_t, sin_t = _row_call(rope_tables, T, tr, [_rspec(pos_col, tr), _bspec(freq_row), _bspec(cos_mask), _bspec(sin_sign)],
                             [(LANES, F32), (LANES, F32)], name="rope_tables")

    xs = x0
    saved = []
    for i in range(DEPTH):
        norms = {n: weights[n][i] for n in ('norm_mix', 'q_norm', 'kv_norm', 'norm_ffn', 'norm_ple')}
        xs, sv = _layer_fwd(i, xs, p[i, 0], layouts[i], norms, w_pool_bf[i], pool_scale[i], cos_t, sin_t, tr, blk)
        saved.append((sv, norms))

    def head(_, xv, tv, gv):
        rstd = lax.rsqrt(jnp.mean(xv * xv, axis=-1, keepdims=True) + EPS)
        xhat = xv * rstd
        err = xhat * gv - tv
        loss = 0.5 * jnp.sum(jnp.mean(err * err, axis=-1, keepdims=True), axis=0, keepdims=True)
        dy = err * (1.0 / D_MODEL)
        dg = jnp.sum(dy * xhat, axis=0, keepdims=True)
        dxh = dy * gv
        dx = rstd * (dxh - xhat * jnp.mean(dxh * xhat, axis=-1, keepdims=True))
        return dx, jnp.broadcast_to(loss, (1, LANES)), dg

    dx, loss_part, g_final = _row_call(head, T, tr, [_rspec(xs, tr), _rspec(target, tr), _bspec(final_norm.reshape(1, D_MODEL))],
                                       [(D_MODEL, F32)], [(1, LANES), (1, D_MODEL)], name="loss_head")
    loss = lax.psum(loss_part[0, 0], ("x", "y", "c"))

    layer_grads = [None] * DEPTH
    for i in reversed(range(DEPTH)):
        sv, norms = saved[i]
        dx, layer_grads[i] = _layer_bwd(i, dx, sv, p[i, 0], layouts[i], norms, w_pool_bf[i], pool_scale[i], cos_t, sin_t, tr,
                                        blk)
    grad_x = dx.reshape(x.shape)

    ref_layout = [_layer_grads_to_reference_layout(g) for g in layer_grads]
    local = {n: jnp.stack([ref_layout[i][n] for i in range(DEPTH)]) for n in names}
    for n in ('norm_mix', 'q_norm', 'kv_norm', 'norm_ffn', 'norm_ple', 'pool_scale'):
        local[n] = jnp.stack([layer_grads[i][n].reshape(-1) for i in range(DEPTH)])
    local['w_pool'] = jnp.stack([layer_grads[i]['w_pool'] for i in range(DEPTH)])
    local['final_norm'] = g_final.reshape(-1)

    send = []
    for k in range(N_CHIPS):
        parts = []
        for n in names:
            ax = SHARDED[n]
            size = local[n].shape[ax] // N_CHIPS
            parts.append(lax.slice_in_dim(local[n], k * size, (k + 1) * size, axis=ax))
        send.append(_pack_blocks(parts, row_mult=PACK_ROWS))
    rh = R // 2
    trr = PACK_ROWS // 2
    send = [g.reshape(2, rh, PACK_LANES) for g in send]
    part = _add_halves(send, _swap_halves(send), core, tr=trr)
    reduced_half = _sum_chips(_scatter_partials(part), tr=trr)
    reduced = _both_halves(reduced_half, _send_half(reduced_half), core, tr=trr).reshape(R, PACK_LANES)
    grads = dict(zip(names, _unpack_blocks(reduced, shard_shapes)))

    rep_shapes = [weights[n].shape for n in REPLICATED]
    rep = _allreduce_small(_pack_rows([local[n] for n in REPLICATED], row_mult=8))
    grads.update(zip(REPLICATED, _unpack_rows(rep, rep_shapes)))

    deltas, new_m, new_v = {}, {}, {}
    for n in WEIGHTS:
        deltas[n], new_m[n], new_v[n] = _adamw(weights[n], grads[n], given['m_' + n], given['v_' + n], name=f"adamw_{n}")
    return (loss, grad_x, *[grads[n] for n in WEIGHTS], *[deltas[n] for n in WEIGHTS], *[new_m[n] for n in WEIGHTS],
            *[new_v[n] for n in WEIGHTS])
```

```python
import numpy as np
import jax
import jax.numpy as jnp
from jax import lax
from jax.experimental import pallas as pl
from jax.experimental.pallas import tpu as pltpu

F32 = jnp.float32
BF16 = jnp.bfloat16

D_MODEL = 1024
DEPTH = 2
PLE_DIM = 256
POOL_WINDOWS = (2, 4, 8, 16)
POOL_GROUP = 128
POOL_WIDTH = 512
N_HEADS = 8
Q_LORA = 512
KV_LORA = 256
QK_NOPE = 128
QK_ROPE = 64
QK_HEAD = 192
V_HEAD = 128
D_FF = 2816
ROPE_THETA = 10000.0
EPS = 1e-6
ATTN_SCALE = QK_HEAD ** -0.5

ADAM_LR = 0.001
ADAM_B1 = 0.9
ADAM_B2 = 0.999
ADAM_EPS = 1e-08
ADAM_WD = 0.01
ADAM_STEP = 10

LANES = 128
HALO = 16
HEAD_PAD = 256
V7X_VMEM_BYTES = 64 * 1024 * 1024
VMEM_LIMIT = (V7X_VMEM_BYTES * 3) // 4
N_CHIPS = 4
N_DEV = 8
NEG_INF = -1e30

ZC_GA, ZC_GB, ZC_U, ZC_CQ, ZC_CKV, ZC_KR = 0, 1024, 2048, 2560, 3072, 3328
Z_WIDTH = 3456

WEIGHTS = ['norm_mix', 'w_in', 'w_pool', 'pool_scale', 'q_norm', 'kv_norm', 'w_uq', 'w_ukv', 'w_a', 'w_b', 'w_o',
           'norm_ffn', 'w_gate', 'w_up', 'w_down', 'norm_ple', 'w_ple_gate', 'w_ple', 'final_norm']
SHARDED = {'w_in': 2, 'w_uq': 1, 'w_ukv': 1, 'w_a': 2, 'w_b': 1, 'w_o': 1, 'w_gate': 2, 'w_up': 2, 'w_down': 1,
           'w_ple_gate': 1, 'w_ple': 2}
REPLICATED = [n for n in WEIGHTS if n not in SHARDED]


def _tile(n, target, mult=LANES):
    if n <= target:
        return n
    best = None
    for t in range(mult, target + 1, mult):
        if n % t == 0:
            best = t
    assert best is not None, (n, target)
    return best


def _cparams(*sem):
    return pltpu.CompilerParams(dimension_semantics=sem, vmem_limit_bytes=VMEM_LIMIT)


def _rope(t, cos_t, sin_t):
    return t * cos_t + pltpu.roll(t, 64, 1) * sin_t


def _rope_bwd(d, cos_t, sin_t):
    return d * cos_t + pltpu.roll(d * sin_t, 64, 1)


def _sigmoid(v):
    return 1.0 / (1.0 + jnp.exp(-v))


def _mm_nn(a, b, *, name, outs, a_col=0, gain=None, emit_a=False, epi=None, epi_rows=(), tm=512):
    M = a.shape[0]
    K, N = b.shape
    tm = min(tm, M)
    assert a_col % K == 0 and M % tm == 0
    a_blk = a_col // K
    n_rows, n_out = len(epi_rows), len(outs)

    def body(*refs):
        a_ref, b_ref = refs[0], refs[1]
        pos = 2
        g_ref = None
        if gain is not None:
            g_ref = refs[pos]
            pos += 1
        row_refs = refs[pos:pos + n_rows]
        out_refs = refs[pos + n_rows:pos + n_rows + n_out]
        lhs = a_ref[...]
        if gain is not None:
            av = lhs.astype(F32)
            lhs = av * lax.rsqrt(jnp.mean(av * av, axis=-1, keepdims=True) + EPS) * g_ref[...]
        lhs = lhs.astype(BF16)
        if emit_a:
            refs[pos + n_rows + n_out][...] = lhs
        acc = jnp.dot(lhs, b_ref[...], preferred_element_type=F32)
        vals = (acc,) if epi is None else epi(acc, *[r[...] for r in row_refs])
        for r, v in zip(out_refs, vals):
            r[...] = v.astype(r.dtype)

    in_specs = [pl.BlockSpec((tm, K), lambda i: (i, a_blk)), pl.BlockSpec((K, N), lambda i: (0, 0))]
    args = [a, b]
    if gain is not None:
        in_specs.append(pl.BlockSpec((1, K), lambda i: (0, 0)))
        args.append(gain.reshape(1, K).astype(F32))
    for arr, w, blk in epi_rows:
        in_specs.append(pl.BlockSpec((tm, w), lambda i, blk=blk: (i, blk)))
        args.append(arr)
    out_shape = [jax.ShapeDtypeStruct((M, w), dt) for w, dt in outs]
    out_specs = [pl.BlockSpec((tm, w), lambda i: (i, 0)) for w, dt in outs]
    if emit_a:
        out_shape.append(jax.ShapeDtypeStruct((M, K), BF16))
        out_specs.append(pl.BlockSpec((tm, K), lambda i: (i, 0)))
    res = pl.pallas_call(body, grid=(M // tm,), in_specs=in_specs, out_specs=out_specs, out_shape=out_shape,
                         name=name, compiler_params=_cparams("parallel"))(*args)
    return res[0] if len(res) == 1 else res


def _mm_nt(pairs, *, name, outs, epi=None, epi_rows=(), consts=(), accs=(), tm=512, into=None, wide0=None):
    M = pairs[0][0].shape[0]
    N = pairs[0][1].shape[0]
    tm = min(tm, M)
    n_p, n_in, n_out, n_acc = len(pairs), len(epi_rows) + len(consts), len(outs), len(accs)

    def body(*refs):
        acc = None
        for k in range(n_p):
            av = refs[2 * k][...].astype(BF16)
            part = lax.dot_general(av, refs[2 * k + 1][...], NT_DIMS, preferred_element_type=F32)
            acc = part if acc is None else acc + part
        pos = 2 * n_p
        extra = [r[...] for r in refs[pos:pos + n_in]]
        pos += n_in + (1 if into is not None else 0)
        vals = (acc,) if epi is None else epi(acc, *extra)
        for r, v in zip(refs[pos:pos + n_out], vals[:n_out]):
            r[...] = v.astype(r.dtype)
        if n_acc:
            acc_refs = refs[pos + n_out:pos + n_out + n_acc]

            @pl.when(pl.program_id(0) == 0)
            def _():
                for r in acc_refs:
                    r[...] = jnp.zeros_like(r)
            for r, v in zip(acc_refs, vals[n_out:]):
                r[...] += v

    in_specs, args = [], []
    for a, b in pairs:
        assert a.shape[1] == b.shape[1] and b.shape[0] == N and a.shape[0] == M
        in_specs.append(pl.BlockSpec((tm, a.shape[1]), lambda i: (i, 0)))
        in_specs.append(pl.BlockSpec(b.shape, lambda i: (0, 0)))
        args += [a, b]
    for arr, w, blk in epi_rows:
        in_specs.append(pl.BlockSpec((tm, w), lambda i, blk=blk: (i, blk)))
        args.append(arr)
    for arr in consts:
        in_specs.append(pl.BlockSpec(arr.shape, lambda i, n=arr.ndim: (0,) * n))
        args.append(arr)
    out_shape = [jax.ShapeDtypeStruct((M, w), dt) for w, dt in outs]
    out_specs = [pl.BlockSpec((tm, w), lambda i: (i, 0)) for w, dt in outs]
    for s in accs:
        out_shape.append(jax.ShapeDtypeStruct(s, F32))
        out_specs.append(pl.BlockSpec(s, lambda i, n=len(s): (0,) * n))
    aliases = _into_column_block(into, tm, out_shape, out_specs, in_specs, args) if into is not None else {}
    if wide0 is not None:
        out_shape[0] = jax.ShapeDtypeStruct((M, wide0), outs[0][1])
    res = pl.pallas_call(body, grid=(M // tm,), in_specs=in_specs, out_specs=out_specs, out_shape=out_shape, name=name,
                         input_output_aliases=aliases,
                         compiler_params=_cparams("arbitrary" if n_acc else "parallel"))(*args)
    return res[0] if len(res) == 1 else res


def _rms_bwd_epi(with_res, emit_bf16):
    def epi(dh, xv, *rest):
        gv = rest[-1]
        xv = xv.astype(F32)
        rstd = lax.rsqrt(jnp.mean(xv * xv, axis=-1, keepdims=True) + EPS)
        xhat = xv * rstd
        dg = jnp.sum(dh * xhat, axis=0, keepdims=True)
        dxh = dh * gv
        dx = rstd * (dxh - xhat * jnp.mean(dxh * xhat, axis=-1, keepdims=True))
        if with_res:
            dx = dx + rest[0].astype(F32)
        return (dx, dx, dg) if emit_bf16 else (dx, dg)
    return epi


def _mm_tn(a, b, *, name, a_col=0, a_w=None, tk=1024, tn=1152, tm=1024):
    M = a.shape[0]
    a_w = a.shape[1] if a_w is None else a_w
    N = b.shape[1]
    tm = min(tm, M)
    tk = _tile(a_w, tk)
    tn = _tile(N, tn)
    assert a_col % tk == 0 and M % tm == 0
    a_blk0 = a_col // tk

    def body(a_ref, b_ref, o_ref):
        @pl.when(pl.program_id(2) == 0)
        def _():
            o_ref[...] = jnp.zeros_like(o_ref)
        o_ref[...] += lax.dot_general(a_ref[...].astype(BF16), b_ref[...].astype(BF16), (((0,), (0,)), ((), ())),
                                      preferred_element_type=F32)

    return pl.pallas_call(body, grid=(a_w // tk, N // tn, M // tm),
                          in_specs=[pl.BlockSpec((tm, tk), lambda k, j, m: (m, k + a_blk0)),
                                    pl.BlockSpec((tm, tn), lambda k, j, m: (m, j))],
                          out_specs=pl.BlockSpec((tk, tn), lambda k, j, m: (k, j)),
                          out_shape=jax.ShapeDtypeStruct((a_w, N), F32), name=name,
                          compiler_params=_cparams("parallel", "parallel", "arbitrary"))(a, b)


def _into_column_block(into, tile, out_shape, out_specs, in_specs, args):
    buf, width, blk = into
    out_shape[0] = jax.ShapeDtypeStruct(buf.shape, buf.dtype)
    out_specs[0] = pl.BlockSpec((tile, width), lambda i: (i, blk))
    in_specs.append(pl.BlockSpec(memory_space=pl.ANY))
    args.append(buf)
    return {len(args) - 1: 0}


def _row_call(fn, rows, tr, ins, outs, accs=(), *, name, into=None):
    n_in, n_out, n_acc = len(ins), len(outs), len(accs)
    first_out = n_in + (1 if into is not None else 0)

    def body(*refs):
        i = pl.program_id(0)
        vals = fn(i, *[r[...] for r in refs[:n_in]])
        if not isinstance(vals, (tuple, list)):
            vals = (vals,)
        for r, v in zip(refs[first_out:first_out + n_out], vals[:n_out]):
            r[...] = v.astype(r.dtype)
        if n_acc:
            acc_refs = refs[first_out + n_out:]

            @pl.when(i == 0)
            def _():
                for r in acc_refs:
                    r[...] = jnp.zeros_like(r)
            for r, v in zip(acc_refs, vals[n_out:]):
                r[...] += v

    out_shape = [jax.ShapeDtypeStruct((rows, w), dt) for w, dt in outs]
    out_specs = [pl.BlockSpec((tr, w), lambda i: (i, 0)) for w, dt in outs]
    for s in accs:
        out_shape.append(jax.ShapeDtypeStruct(s, F32))
        out_specs.append(pl.BlockSpec(s, lambda i, n=len(s): (0,) * n))
    in_specs = [pl.BlockSpec(bs, im) for _, bs, im in ins]
    args = [a for a, _, _ in ins]
    aliases = _into_column_block(into, tr, out_shape, out_specs, in_specs, args) if into is not None else {}
    res = pl.pallas_call(body, grid=(rows // tr,), in_specs=in_specs, out_specs=out_specs, out_shape=out_shape, name=name,
                         input_output_aliases=aliases, compiler_params=_cparams("arbitrary"))(*args)
    return res


def _rspec(arr, tr, w=None, blk=0):
    w = arr.shape[1] if w is None else w
    return (arr, (tr, w), lambda i, blk=blk: (i, blk))


def _bspec(arr):
    return (arr, arr.shape, lambda i, n=arr.ndim: (0,) * n)


def _pool_counts(i, tr):
    t = (i * tr + lax.broadcasted_iota(jnp.int32, (tr, 1), 0) + 1).astype(F32)
    return [jnp.minimum(t, float(w)) for w in POOL_WINDOWS]


def _pool_fwd(z, w_pool_bf, pool_scale, *, name, tr):
    rows = z.shape[0]
    ublk = ZC_U // POOL_WIDTH
    hpt = tr // HALO

    def fn(i, u, uprev, wp, ps):
        uprev = jnp.where(i > 0, uprev, 0.0)
        ext = jnp.concatenate([uprev, u], axis=0)
        s2 = ext + pltpu.roll(ext, 1, 0)
        s4 = s2 + pltpu.roll(s2, 2, 0)
        s8 = s4 + pltpu.roll(s4, 4, 0)
        s16 = s8 + pltpu.roll(s8, 8, 0)
        cnts = _pool_counts(i, tr)
        pooled, mixed = [], []
        for g, sw in enumerate((s2, s4, s8, s16)):
            lanes = slice(g * POOL_GROUP, (g + 1) * POOL_GROUP)
            pg = sw[HALO:, lanes] / cnts[g] - u[:, lanes]
            pooled.append(pg)
            mixed.append(jnp.dot(pg.astype(BF16), wp[g], preferred_element_type=F32))
        pooled = jnp.concatenate(pooled, axis=1)
        mixed = jnp.concatenate(mixed, axis=1)
        return pooled, mixed, mixed * ps

    ins = [_rspec(z, tr, POOL_WIDTH, ublk),
           (z, (HALO, POOL_WIDTH), lambda i: (jnp.maximum(i * hpt - 1, 0), ublk)),
           _bspec(w_pool_bf), _bspec(pool_scale.reshape(1, POOL_WIDTH))]
    return _row_call(fn, rows, tr, ins, [(POOL_WIDTH, BF16), (POOL_WIDTH, F32), (POOL_WIDTH, BF16)], name=name)


def _pool_bwd_mix(dpm, mixed, pooled, w_pool_bf, pool_scale, *, name, tr):
    rows = dpm.shape[0]

    def fn(i, dv, mv, pv, wp, ps):
        dv = dv.astype(F32)
        dscale = jnp.sum(dv * mv, axis=0, keepdims=True)
        dmix = (dv * ps).astype(BF16)
        cnts = _pool_counts(i, tr)
        dpool, dwp = [], []
        for g in range(len(POOL_WINDOWS)):
            lanes = slice(g * POOL_GROUP, (g + 1) * POOL_GROUP)
            dg = lax.dot_general(dmix[:, lanes], wp[g], (((1,), (1,)), ((), ())), preferred_element_type=F32)
            dpool.append(dg)
            dwp.append(lax.dot_general(pv[:, lanes], dmix[:, lanes], (((0,), (0,)), ((), ())),
                                       preferred_element_type=F32)[None])
        dpool = jnp.concatenate(dpool, axis=1)
        dpool_cnt = jnp.concatenate([dpool[:, g * POOL_GROUP:(g + 1) * POOL_GROUP] / cnts[g]
                                     for g in range(len(POOL_WINDOWS))], axis=1)
        return dpool, dpool_cnt, dscale, jnp.concatenate(dwp, axis=0)

    ins = [_rspec(dpm, tr), _rspec(mixed, tr), _rspec(pooled, tr), _bspec(w_pool_bf),
           _bspec(pool_scale.reshape(1, POOL_WIDTH))]
    return _row_call(fn, rows, tr, ins, [(POOL_WIDTH, F32), (POOL_WIDTH, F32)],
                     [(1, POOL_WIDTH), (len(POOL_WINDOWS), POOL_GROUP, POOL_GROUP)], name=name)


def _pool_bwd_window(dpool, dpool_cnt, *, name, tr, into):
    rows = dpool.shape[0]
    hpt = tr // HALO
    n_halo = rows // HALO
    n_tiles = rows // tr

    def fn(i, dp, dc, dnext):
        dnext = jnp.where(i < n_tiles - 1, dnext, 0.0)
        ext = jnp.concatenate([dc, dnext], axis=0)
        n = tr + HALO
        s2 = ext + pltpu.roll(ext, n - 1, 0)
        s4 = s2 + pltpu.roll(s2, n - 2, 0)
        s8 = s4 + pltpu.roll(s4, n - 4, 0)
        s16 = s8 + pltpu.roll(s8, n - 8, 0)
        out = []
        for g, sw in enumerate((s2, s4, s8, s16)):
            lanes = slice(g * POOL_GROUP, (g + 1) * POOL_GROUP)
            out.append(sw[:tr, lanes] - dp[:, lanes])
        return jnp.concatenate(out, axis=1)

    ins = [_rspec(dpool, tr), _rspec(dpool_cnt, tr),
           (dpool_cnt, (HALO, POOL_WIDTH), lambda i: (jnp.minimum((i + 1) * hpt, n_halo - 1), 0))]
    return _row_call(fn, rows, tr, ins, [(POOL_WIDTH, BF16)], name=name, into=into)[0]


def _causal_pairs(n, k_major):
    if k_major:
        pairs = [(qi, ki) for ki in range(n) for qi in range(ki, n)]
    else:
        pairs = [(qi, ki) for qi in range(n) for ki in range(qi + 1)]
    return (jnp.asarray(np.array([p[0] for p in pairs], np.int32)),
            jnp.asarray(np.array([p[1] for p in pairs], np.int32)), len(pairs))


SUBLANES = 8
NT_DIMS = (((1,), (1,)), ((), ()))
TN_DIMS = (((0,), (0,)), ((), ()))


ATTN_BLOCK = 1024
ATTN_BLOCK_FWD = 2048
QUERY_CHUNK = 256
LOG2_E = 1.4426950408889634
EXP2_SCALE = ATTN_SCALE * LOG2_E


def _scores_t(q_c, k, diag):
    s = lax.dot_general(k, q_c, NT_DIMS, preferred_element_type=F32)
    if diag:
        qc = q_c.shape[0]
        visible = lax.broadcasted_iota(jnp.int32, (qc, qc), 0) <= lax.broadcasted_iota(jnp.int32, (qc, qc), 1)
        last = jnp.where(visible, s[-qc:], NEG_INF)
        s = last if s.shape[0] == qc else jnp.concatenate([s[:-qc], last], axis=0)
    return s


def _flash_fwd(q, k, v, *, name, blk):
    T = q.shape[0]
    n = T // blk
    qc = min(QUERY_CHUNK, blk)
    qtab, ktab, n_pairs = _causal_pairs(n, k_major=False)

    def body(qt, kt, q_ref, k_ref, v_ref, o_ref, lse_ref, m_s, l_s, acc_s):
        p = pl.program_id(1)
        qi, ki = qt[p], kt[p]

        @pl.when(ki == 0)
        def _():
            m_s[...] = jnp.full_like(m_s, NEG_INF)
            l_s[...] = jnp.zeros_like(l_s)
            acc_s[...] = jnp.zeros_like(acc_s)

        def step(diag):
            kv, vv = k_ref[...], v_ref[...]
            chunks = [slice(c * qc, (c + 1) * qc) for c in range(blk // qc)]
            keys = [(c + 1) * qc if diag else blk for c in range(blk // qc)]
            scores = [_scores_t(q_ref[rows, :], kv[:n], diag) for rows, n in zip(chunks, keys)]
            probs, alphas = [], []
            for rows, s_t in zip(chunks, scores):
                m_prev = m_s[:, rows]
                m_new = jnp.maximum(m_prev, jnp.max(s_t, axis=0, keepdims=True))
                p_t = jnp.exp2(s_t - m_new)
                alpha = jnp.exp2(m_prev - m_new)
                l_s[:, rows] = alpha * l_s[:, rows] + jnp.sum(p_t, axis=0, keepdims=True)
                m_s[:, rows] = m_new
                probs.append(p_t.astype(BF16))
                alphas.append(alpha)
            for rows, n, p_t, alpha in zip(chunks, keys, probs, alphas):
                acc_s[:, rows] = alpha * acc_s[:, rows] + lax.dot_general(vv[:n], p_t, TN_DIMS,
                                                                          preferred_element_type=F32)

        @pl.when(ki != qi)
        def _():
            step(False)

        @pl.when(ki == qi)
        def _():
            step(True)
            o_ref[...] = (acc_s[...] / l_s[...]).T.astype(o_ref.dtype)
            lse2 = m_s[...] + jnp.log2(l_s[...])
            lse_ref[...] = jnp.broadcast_to(lse2, lse_ref.shape)

    grid_spec = pltpu.PrefetchScalarGridSpec(
        num_scalar_prefetch=2, grid=(N_HEADS, n_pairs),
        in_specs=[pl.BlockSpec((blk, HEAD_PAD), lambda h, p, qt, kt: (qt[p], h)),
                  pl.BlockSpec((blk, HEAD_PAD), lambda h, p, qt, kt: (kt[p], h)),
                  pl.BlockSpec((blk, V_HEAD), lambda h, p, qt, kt: (kt[p], h))],
        out_specs=[pl.BlockSpec((blk, V_HEAD), lambda h, p, qt, kt: (qt[p], h)),
                   pl.BlockSpec((SUBLANES, blk), lambda h, p, qt, kt: (h, qt[p]))],
        scratch_shapes=[pltpu.VMEM((1, blk), F32), pltpu.VMEM((1, blk), F32), pltpu.VMEM((V_HEAD, blk), F32)])
    return pl.pallas_call(body, grid_spec=grid_spec,
                          out_shape=[jax.ShapeDtypeStruct((T, N_HEADS * V_HEAD), BF16),
                                     jax.ShapeDtypeStruct((N_HEADS * SUBLANES, T), F32)],
                          name=name, compiler_params=_cparams("parallel", "arbitrary"))(qtab, ktab, q, k, v)


def _attn_delta(do, o, *, name, tr):
    T = do.shape[0]

    def body(do_ref, o_ref, d_ref):
        prod = do_ref[...].astype(F32) * o_ref[...].astype(F32)
        lane_head = lax.broadcasted_iota(jnp.int32, (tr, LANES), 1) // SUBLANES
        mat = jnp.zeros((tr, LANES), F32)
        for h in range(N_HEADS):
            d_h = jnp.sum(prod[:, h * V_HEAD:(h + 1) * V_HEAD], axis=1, keepdims=True)
            mat = jnp.where(lane_head == h, d_h, mat)
        d_ref[...] = mat.T[:N_HEADS * SUBLANES, :]

    return pl.pallas_call(body, grid=(T // tr,),
                          in_specs=[pl.BlockSpec((tr, N_HEADS * V_HEAD), lambda i: (i, 0)),
                                    pl.BlockSpec((tr, N_HEADS * V_HEAD), lambda i: (i, 0))],
                          out_specs=pl.BlockSpec((N_HEADS * SUBLANES, tr), lambda i: (0, i)),
                          out_shape=jax.ShapeDtypeStruct((N_HEADS * SUBLANES, T), F32), name=name,
                          compiler_params=_cparams("parallel"))(do, o)


def _flash_bwd(q, k, v, lse, delta, do, cos_t, sin_t, *, name, blk):
    T = q.shape[0]
    n = T // blk
    qc = min(QUERY_CHUNK, blk)
    qtab, ktab, n_pairs = _causal_pairs(n, k_major=True)

    def body(qt, kt, q_ref, k_ref, v_ref, lse_ref, delta_ref, do_ref, cos_ref, sin_ref, dq_ref, dk_ref, dv_ref,
             dq_s, dk_s, dv_s):
        p = pl.program_id(1)
        qi, ki = qt[p], kt[p]
        first = qi == ki

        @pl.when(p == 0)
        def _():
            dq_s[...] = jnp.zeros_like(dq_s)

        @pl.when(first)
        def _():
            dk_s[...] = jnp.zeros_like(dk_s)
            dv_s[...] = jnp.zeros_like(dv_s)

        def step(diag):
            kv, vv = k_ref[...], v_ref[...]
            chunks = [slice(c * qc, (c + 1) * qc) for c in range(blk // qc)]
            qs = [q_ref[rows, :] for rows in chunks]
            dos = [do_ref[rows, :] for rows in chunks]
            keys = [(c + 1) * qc if diag else blk for c in range(blk // qc)]
            scores = [_scores_t(q_c, kv[:n], diag) for q_c, n in zip(qs, keys)]
            dps = [lax.dot_general(vv[:n], do_c, NT_DIMS, preferred_element_type=F32) for do_c, n in zip(dos, keys)]
            probs, dss = [], []
            for rows, s_t, dp_t in zip(chunks, scores, dps):
                p_t = jnp.exp2(s_t - lse_ref[0:1, rows])
                dss.append((p_t * (dp_t - delta_ref[0:1, rows])).astype(BF16))
                probs.append(p_t.astype(BF16))
            dv_acc = dk_acc = None
            for n, p_t, ds_t, q_c, do_c in zip(keys, probs, dss, qs, dos):
                dv_c = jnp.dot(p_t, do_c, preferred_element_type=F32)
                dk_c = jnp.dot(ds_t, q_c, preferred_element_type=F32)
                if diag:
                    dv_s[:n, :] += dv_c
                    dk_s[:n, :] += dk_c
                else:
                    dv_acc = dv_c if dv_acc is None else dv_acc + dv_c
                    dk_acc = dk_c if dk_acc is None else dk_acc + dk_c
            for rows, n, ds_t in zip(chunks, keys, dss):
                dq_s[qi, :, rows] += lax.dot_general(kv[:n], ds_t, TN_DIMS, preferred_element_type=F32)
            if not diag:
                dv_s[...] += dv_acc
                dk_s[...] += dk_acc

        @pl.when(jnp.logical_not(first))
        def _():
            step(False)

        @pl.when(first)
        def _():
            step(True)
            dq_t = (dq_s[qi] * ATTN_SCALE).T
            dq_ref[:, :QK_NOPE] = dq_t[:, :QK_NOPE].astype(dq_ref.dtype)
            dq_ref[:, QK_NOPE:] = _rope_bwd(dq_t[:, QK_NOPE:], cos_ref[...], sin_ref[...]).astype(dq_ref.dtype)

        @pl.when(qi == n - 1)
        def _():
            dk_ref[...] = (dk_s[...] * (1.0 / LOG2_E)).astype(dk_ref.dtype)
            dv_ref[...] = dv_s[...].astype(dv_ref.dtype)

    qmap = lambda h, p, qt, kt: (qt[p], h)
    kmap = lambda h, p, qt, kt: (kt[p], h)
    smap = lambda h, p, qt, kt: (h, qt[p])
    tmap = lambda h, p, qt, kt: (kt[p], 0)
    grid_spec = pltpu.PrefetchScalarGridSpec(
        num_scalar_prefetch=2, grid=(N_HEADS, n_pairs),
        in_specs=[pl.BlockSpec((blk, HEAD_PAD), qmap), pl.BlockSpec((blk, HEAD_PAD), kmap),
                  pl.BlockSpec((blk, V_HEAD), kmap), pl.BlockSpec((SUBLANES, blk), smap),
                  pl.BlockSpec((SUBLANES, blk), smap), pl.BlockSpec((blk, V_HEAD), qmap),
                  pl.BlockSpec((blk, LANES), tmap), pl.BlockSpec((blk, LANES), tmap)],
        out_specs=[pl.BlockSpec((blk, HEAD_PAD), kmap), pl.BlockSpec((blk, HEAD_PAD), kmap),
                   pl.BlockSpec((blk, V_HEAD), kmap)],
        scratch_shapes=[pltpu.VMEM((n, HEAD_PAD, blk), F32), pltpu.VMEM((blk, HEAD_PAD), F32),
                        pltpu.VMEM((blk, V_HEAD), F32)])
    return pl.pallas_call(body, grid_spec=grid_spec,
                          out_shape=[jax.ShapeDtypeStruct((T, N_HEADS * HEAD_PAD), BF16),
                                     jax.ShapeDtypeStruct((T, N_HEADS * HEAD_PAD), BF16),
                                     jax.ShapeDtypeStruct((T, N_HEADS * V_HEAD), BF16)],
                          name=name, compiler_params=_cparams("arbitrary", "arbitrary"))(
                              qtab, ktab, q, k, v, lse, delta, do, cos_t, sin_t)


MESH_ID = pl.DeviceIdType.MESH
ANY_SPEC = pl.BlockSpec(memory_space=pl.ANY)


def _other_chips(x, y):
    out = []
    for dx, dy in ((1, 0), (0, 1), (1, 1)):
        px = x ^ dx if dx else x
        py = y ^ dy if dy else y
        out.append((px, py, 2 * px + py))
    return out


def _gather_weights(flat):
    rh = flat.shape[0] // 2
    rq = rh // 2

    def body(src2, out, send_sems, recv_sems):
        x, y, c = lax.axis_index("x"), lax.axis_index("y"), lax.axis_index("c")
        me = 2 * x + y
        sib = (x, y, 1 - c)
        (xx, xy, kx), (yx, yy, ky), (_, _, kd) = _other_chips(x, y)
        x_nbr, y_nbr = (xx, xy, c), (yx, yy, c)
        first, last = pl.ds(0, rq), pl.ds(rq, rq)

        def copy(j, src, dst, to):
            return pltpu.make_async_remote_copy(src_ref=src, dst_ref=dst, send_sem=send_sems.at[j], recv_sem=recv_sems.at[j],
                                                device_id=to, device_id_type=MESH_ID)

        def arrived(j, land):
            copy(j, land, land, sib).wait_recv()

        sends = [copy(0, src2.at[c], out.at[me, c], x_nbr), copy(1, src2.at[c], out.at[me, c], y_nbr)]
        for cp in sends:
            cp.start()
        landings = [(0, out.at[kx, c]), (1, out.at[ky, c]), (2, out.at[kd, c, first]), (3, out.at[kd, c, last])]
        relays = {0: (2, out.at[kx, c, first], y_nbr), 1: (3, out.at[ky, c, last], x_nbr)}
        for j, land in landings:
            arrived(j, land)
            if j in relays:
                rj, piece, to = relays[j]
                sends.append(copy(rj, piece, piece, to))
                sends[-1].start()
            sends.append(copy(4 + j, land, land, sib))
            sends[-1].start()
        for j, land in [(0, out.at[kx, 1 - c]), (1, out.at[ky, 1 - c]), (2, out.at[kd, 1 - c, first]),
                        (3, out.at[kd, 1 - c, last])]:
            arrived(4 + j, land)
        for cp in sends:
            cp.wait_send()

    return pl.pallas_call(body, out_shape=jax.ShapeDtypeStruct((N_CHIPS, 2, rh, PACK_LANES), flat.dtype),
                          in_specs=[ANY_SPEC], out_specs=ANY_SPEC,
                          scratch_shapes=[pltpu.SemaphoreType.DMA((8,)), pltpu.SemaphoreType.DMA((8,))],
                          name="gather_weights")(flat.reshape(2, rh, PACK_LANES))


def _swap_halves(gs):
    rh = gs[0].shape[1]

    def body(*refs):
        srcs, out, send_sems, recv_sems = refs[:N_CHIPS], refs[N_CHIPS], refs[N_CHIPS + 1], refs[N_CHIPS + 2]
        x, y, c = lax.axis_index("x"), lax.axis_index("y"), lax.axis_index("c")
        copies = [pltpu.make_async_remote_copy(src_ref=srcs[k].at[1 - c], dst_ref=out.at[k], send_sem=send_sems.at[k],
                                               recv_sem=recv_sems.at[k], device_id=(x, y, 1 - c), device_id_type=MESH_ID)
                  for k in range(N_CHIPS)]
        for cp in copies:
            cp.start()
        for cp in copies:
            cp.wait()

    return pl.pallas_call(body, out_shape=jax.ShapeDtypeStruct((N_CHIPS, rh, PACK_LANES), gs[0].dtype),
                          in_specs=[ANY_SPEC] * N_CHIPS, out_specs=ANY_SPEC,
                          scratch_shapes=[pltpu.SemaphoreType.DMA((N_CHIPS,)), pltpu.SemaphoreType.DMA((N_CHIPS,))],
                          name="grad_swap_halves")(*gs)


def _add_halves(gs, got, core, *, tr):
    rh = gs[0].shape[1]

    def body(*refs):
        g_refs, got_ref, o_ref = refs[1:1 + N_CHIPS], refs[1 + N_CHIPS], refs[2 + N_CHIPS]
        for k in range(N_CHIPS):
            o_ref[k] = (g_refs[k][0] + got_ref[k]).astype(o_ref.dtype)

    grid_spec = pltpu.PrefetchScalarGridSpec(
        num_scalar_prefetch=1, grid=(rh // tr,),
        in_specs=[pl.BlockSpec((1, tr, PACK_LANES), lambda i, c_ref: (c_ref[0], i, 0))] * N_CHIPS
        + [pl.BlockSpec((N_CHIPS, tr, PACK_LANES), lambda i, c_ref: (0, i, 0))],
        out_specs=pl.BlockSpec((N_CHIPS, tr, PACK_LANES), lambda i, c_ref: (0, i, 0)))
    return pl.pallas_call(body, grid_spec=grid_spec,
                          out_shape=jax.ShapeDtypeStruct((N_CHIPS, rh, PACK_LANES), BF16), name="grad_add_halves",
                          compiler_params=_cparams("parallel"))(core.reshape(1), *gs, got)


def _scatter_partials(part):
    rh = part.shape[1]

    def body(src, out, send_sems, recv_sems, local_sem):
        x, y, c = lax.axis_index("x"), lax.axis_index("y"), lax.axis_index("c")
        me = 2 * x + y
        own = pltpu.make_async_copy(src.at[me], out.at[me], local_sem)
        own.start()
        chips = _other_chips(x, y)
        sends = []
        for j, (px, py, pk) in enumerate(chips):
            cp = pltpu.make_async_remote_copy(src_ref=src.at[pk], dst_ref=out.at[me], send_sem=send_sems.at[j],
                                              recv_sem=recv_sems.at[j], device_id=(px, py, c), device_id_type=MESH_ID)
            cp.start()
            sends.append(cp)
        for j, (px, py, pk) in enumerate(chips):
            land = out.at[pk]
            pltpu.make_async_remote_copy(src_ref=land, dst_ref=land, send_sem=send_sems.at[j], recv_sem=recv_sems.at[j],
                                         device_id=(px, py, c), device_id_type=MESH_ID).wait_recv()
        for cp in sends:
            cp.wait_send()
        own.wait()

    return pl.pallas_call(body, out_shape=jax.ShapeDtypeStruct((N_CHIPS, rh, PACK_LANES), part.dtype),
                          in_specs=[ANY_SPEC], out_specs=ANY_SPEC,
                          scratch_shapes=[pltpu.SemaphoreType.DMA((3,)), pltpu.SemaphoreType.DMA((3,)),
                                          pltpu.SemaphoreType.DMA(())],
                          name="grad_scatter_partials")(part)


def _sum_chips(q, *, tr):
    rh = q.shape[1]

    def body(q_ref, o_ref):
        parts = [q_ref[k].astype(F32) for k in range(N_CHIPS)]
        o_ref[...] = ((parts[0] + parts[1]) + parts[2]) + parts[3]

    return pl.pallas_call(body, grid=(rh // tr,),
                          in_specs=[pl.BlockSpec((N_CHIPS, tr, PACK_LANES), lambda i: (0, i, 0))],
                          out_specs=pl.BlockSpec((tr, PACK_LANES), lambda i: (i, 0)),
                          out_shape=jax.ShapeDtypeStruct((rh, PACK_LANES), F32), name="grad_sum_chips",
                          compiler_params=_cparams("parallel"))(q)


def _send_half(half):
    def body(src, out, send_sem, recv_sem):
        x, y, c = lax.axis_index("x"), lax.axis_index("y"), lax.axis_index("c")
        cp = pltpu.make_async_remote_copy(src_ref=src, dst_ref=out, send_sem=send_sem, recv_sem=recv_sem,
                                          device_id=(x, y, 1 - c), device_id_type=MESH_ID)
        cp.start()
        cp.wait()

    return pl.pallas_call(body, out_shape=jax.ShapeDtypeStruct(half.shape, half.dtype),
                          in_specs=[ANY_SPEC], out_specs=ANY_SPEC,
                          scratch_shapes=[pltpu.SemaphoreType.DMA(()), pltpu.SemaphoreType.DMA(())],
                          name="grad_send_half")(half)


def _both_halves(own, got, core, *, tr):
    rh = own.shape[0]

    def body(c_ref, own_ref, got_ref, o_ref):
        mine = pl.program_id(0) == c_ref[0]
        o_ref[0] = jnp.where(mine, own_ref[...], got_ref[...])

    grid_spec = pltpu.PrefetchScalarGridSpec(
        num_scalar_prefetch=1, grid=(2, rh // tr),
        in_specs=[pl.BlockSpec((tr, PACK_LANES), lambda h, i, c_ref: (i, 0)),
                  pl.BlockSpec((tr, PACK_LANES), lambda h, i, c_ref: (i, 0))],
        out_specs=pl.BlockSpec((1, tr, PACK_LANES), lambda h, i, c_ref: (h, i, 0)))
    return pl.pallas_call(body, grid_spec=grid_spec, out_shape=jax.ShapeDtypeStruct((2, rh, PACK_LANES), own.dtype),
                          name="grad_both_halves", compiler_params=_cparams("parallel", "parallel"))(core.reshape(1), own, got)


def _allreduce_small(v):
    rows = v.shape[0]

    def body(v_ref, o_ref, buf, send_sems, recv_sems):
        x, y, c = lax.axis_index("x"), lax.axis_index("y"), lax.axis_index("c")
        me = 4 * x + 2 * y + c
        buf[me] = v_ref[...]
        sends = []
        for j in range(1, N_DEV):
            px, py, pc = x ^ ((j >> 2) & 1), y ^ ((j >> 1) & 1), c ^ (j & 1)
            cp = pltpu.make_async_remote_copy(src_ref=v_ref, dst_ref=buf.at[me], send_sem=send_sems.at[j - 1],
                                              recv_sem=recv_sems.at[j - 1], device_id=(px, py, pc), device_id_type=MESH_ID)
            cp.start()
            sends.append(cp)
        for j in range(1, N_DEV):
            px, py, pc = x ^ ((j >> 2) & 1), y ^ ((j >> 1) & 1), c ^ (j & 1)
            land = buf.at[4 * px + 2 * py + pc]
            pltpu.make_async_remote_copy(src_ref=land, dst_ref=land, send_sem=send_sems.at[j - 1],
                                         recv_sem=recv_sems.at[j - 1], device_id=(px, py, pc),
                                         device_id_type=MESH_ID).wait_recv()
        for cp in sends:
            cp.wait_send()
        acc = buf[0]
        for d in range(1, N_DEV):
            acc = acc + buf[d]
        o_ref[...] = acc

    vm = pl.BlockSpec(memory_space=pltpu.VMEM)
    return pl.pallas_call(body, out_shape=jax.ShapeDtypeStruct((rows, LANES), F32), in_specs=[vm], out_specs=vm,
                          scratch_shapes=[pltpu.VMEM((N_DEV, rows, LANES), F32), pltpu.SemaphoreType.DMA((N_DEV - 1,)),
                                          pltpu.SemaphoreType.DMA((N_DEV - 1,))],
                          name="allreduce_small")(v)


def _adamw(w, g, m, v, *, name):
    shape = w.shape
    cols = shape[-1] if w.ndim > 1 else shape[0]
    rows = w.size // cols
    w2, g2, m2, v2 = (t.reshape(rows, cols) for t in (w, g, m, v))
    tr = rows if rows <= 256 else _tile(rows, 256, 8)

    def fn(i, wv, gv, mv, vv):
        mn = ADAM_B1 * mv + (1.0 - ADAM_B1) * gv
        vn = ADAM_B2 * vv + (1.0 - ADAM_B2) * (gv * gv)
        m_hat = mn / (1.0 - ADAM_B1 ** ADAM_STEP)
        v_hat = vn / (1.0 - ADAM_B2 ** ADAM_STEP)
        delta = -ADAM_LR * (m_hat / (jnp.sqrt(v_hat) + ADAM_EPS) + ADAM_WD * wv)
        return delta, mn, vn

    ins = [_rspec(t, tr) for t in (w2, g2, m2, v2)]
    d, mn, vn = _row_call(fn, rows, tr, ins, [(cols, F32)] * 3, name=name)
    return d.reshape(shape), mn.reshape(shape), vn.reshape(shape)


def _rope_cols(w):
    z = jnp.zeros(w.shape[:-1] + (32,), w.dtype)
    return jnp.concatenate([w[..., :32], z, w[..., 32:], z], axis=-1)


def _rope_cols_inv(w):
    return jnp.concatenate([w[..., :32], w[..., 64:96]], axis=-1)


def _layer_layouts(W, i):
    w_in = W['w_in'][i]
    u, cq, ckv = w_in[:, :512], w_in[:, 512:1024], w_in[:, 1024:1280]
    kr, ga, gb = w_in[:, 1280:1344], w_in[:, 1344:2368], w_in[:, 2368:]
    L = {}
    L['w_in'] = jnp.concatenate([ga, gb, u, cq, ckv, _rope_cols(kr)], axis=1)
    wq = W['w_uq'][i]
    L['w_q'] = jnp.concatenate([wq[..., :QK_NOPE], _rope_cols(wq[..., QK_NOPE:])], axis=-1).reshape(Q_LORA, -1)
    wkv = W['w_ukv'][i]
    L['w_k'] = jnp.concatenate([wkv[..., :QK_NOPE], jnp.zeros_like(wkv[..., :LANES])], axis=-1).reshape(KV_LORA, -1)
    L['w_v'] = wkv[..., QK_NOPE:].reshape(KV_LORA, -1)
    L['w_gu'] = jnp.concatenate([W['w_gate'][i], W['w_up'][i]], axis=1)
    for n in ('w_a', 'w_b', 'w_o', 'w_down', 'w_ple_gate', 'w_ple'):
        L[n] = W[n][i]
    return L


def _layer_grads_to_reference_layout(G):
    d = G['w_in']
    ga, gb, u = d[:, ZC_GA:ZC_GB], d[:, ZC_GB:ZC_U], d[:, ZC_U:ZC_CQ]
    cq, ckv, kr = d[:, ZC_CQ:ZC_CKV], d[:, ZC_CKV:ZC_KR], _rope_cols_inv(d[:, ZC_KR:])
    out = {'w_in': jnp.concatenate([u, cq, ckv, kr, ga, gb], axis=1)}
    dq = G['w_q'].reshape(Q_LORA, N_HEADS, HEAD_PAD)
    out['w_uq'] = jnp.concatenate([dq[..., :QK_NOPE], _rope_cols_inv(dq[..., QK_NOPE:])], axis=-1)
    dk = G['w_k'].reshape(KV_LORA, N_HEADS, HEAD_PAD)[..., :QK_NOPE]
    dv = G['w_v'].reshape(KV_LORA, N_HEADS, V_HEAD)
    out['w_ukv'] = jnp.concatenate([dk, dv], axis=-1)
    out['w_gate'], out['w_up'] = G['w_gu'][:, :D_FF], G['w_gu'][:, D_FF:]
    for n in ('w_a', 'w_b', 'w_o', 'w_down', 'w_ple_gate', 'w_ple'):
        out[n] = G[n]
    return out


PACK_ROWS = 2048


def _pack_rows(parts, row_mult):
    flat = jnp.concatenate([p.reshape(-1) for p in parts])
    n = flat.shape[0]
    per = LANES * row_mult
    padded = -(-n // per) * per
    return jnp.pad(flat, (0, padded - n)).reshape(-1, LANES)


def _unpack_rows(flat2d, shapes):
    flat = flat2d.reshape(-1)
    out, off = [], 0
    for s in shapes:
        n = int(np.prod(s))
        out.append(flat[off:off + n].reshape(s))
        off += n
    return out


PACK_LANES = 256


def _lane_blocks(cols):
    return -(-cols // PACK_LANES)


def _pack_blocks(parts, row_mult):
    blocks = []
    for p in parts:
        p2 = p.reshape(-1, p.shape[-1])
        cols = p2.shape[1]
        nb = _lane_blocks(cols)
        if nb * PACK_LANES != cols:
            p2 = jnp.pad(p2, ((0, 0), (0, nb * PACK_LANES - cols)))
        blocks += [p2[:, j * PACK_LANES:(j + 1) * PACK_LANES] for j in range(nb)]
    buf = jnp.concatenate(blocks, axis=0)
    rows = buf.shape[0]
    padded = -(-rows // row_mult) * row_mult
    return buf if padded == rows else jnp.pad(buf, ((0, padded - rows), (0, 0)))


def _unpack_blocks(buf, shapes):
    out, off = [], 0
    for s in shapes:
        rows, cols = int(np.prod(s[:-1])), s[-1]
        nb = _lane_blocks(cols)
        piece = jnp.concatenate([buf[off + j * rows:off + (j + 1) * rows] for j in range(nb)], axis=1)
        out.append(piece[:, :cols].reshape(s))
        off += nb * rows
    return out


def _layer_fwd(i, x, p_i, L, norms, w_pool_bf, pool_scale, cos_t, sin_t, tr, blk):
    sv = {'x': x}
    z, sv['h'] = _mm_nn(x, L['w_in'], name=f"l{i}_in_proj", outs=[(Z_WIDTH, F32)], gain=norms['norm_mix'], emit_a=True,
                        tm=tr // 2)
    sv['z'] = z
    sv['pooled'], sv['mixed'], sv['pm'] = _pool_fwd(z, w_pool_bf, pool_scale, name=f"l{i}_pool", tr=tr)
    sv['ya'] = _mm_nn(sv['pm'], L['w_a'], name=f"l{i}_ya", outs=[(D_MODEL, BF16)], tm=tr)

    def heads(acc, rope_part):
        out = []
        for h in range(N_HEADS):
            out.append(acc[:, h * HEAD_PAD:h * HEAD_PAD + QK_NOPE])
            out.append(rope_part(acc[:, h * HEAD_PAD + QK_NOPE:(h + 1) * HEAD_PAD]))
        return jnp.concatenate(out, axis=1)

    def q_epi(acc, ct, st):
        return (heads(acc * EXP2_SCALE, lambda t: _rope(t, ct, st)),)

    def k_epi(acc, kr, ct, st):
        k_pe = _rope(kr, ct, st)
        return (heads(acc, lambda t: k_pe),)

    rope_rows = [(cos_t, LANES, 0), (sin_t, LANES, 0)]
    qk_width = N_HEADS * HEAD_PAD
    sv['q'], sv['cqn'] = _mm_nn(z, L['w_q'], name=f"l{i}_q_proj", outs=[(qk_width, BF16)], a_col=ZC_CQ,
                                gain=norms['q_norm'], emit_a=True, epi=q_epi, epi_rows=rope_rows, tm=tr)
    sv['k'], sv['ckvn'] = _mm_nn(z, L['w_k'], name=f"l{i}_k_proj", outs=[(qk_width, BF16)], a_col=ZC_CKV,
                                 gain=norms['kv_norm'], emit_a=True, epi=k_epi,
                                 epi_rows=[(z, LANES, ZC_KR // LANES)] + rope_rows, tm=tr)
    sv['v'] = _mm_nn(sv['ckvn'], L['w_v'], name=f"l{i}_v_proj", outs=[(N_HEADS * V_HEAD, BF16)], tm=tr)
    sv['o'], sv['lse'] = _flash_fwd(sv['q'], sv['k'], sv['v'], name=f"l{i}_attn",
                                    blk=min(ATTN_BLOCK_FWD, max(blk * ATTN_BLOCK_FWD // ATTN_BLOCK, 128)))

    def merge_epi(yb, ga, gb, ya):
        return yb, _sigmoid(ga) * ya.astype(F32) + _sigmoid(gb) * yb

    sv['yb'], sv['merged'] = _mm_nn(sv['o'], L['w_b'], name=f"l{i}_yb_merge", outs=[(D_MODEL, BF16), (D_MODEL, BF16)],
                                    epi=merge_epi, epi_rows=[(z, D_MODEL, 0), (z, D_MODEL, 1), (sv['ya'], D_MODEL, 0)],
                                    tm=tr)

    def add_epi(acc, res):
        return (acc + res,)

    x1 = _mm_nn(sv['merged'], L['w_o'], name=f"l{i}_wo", outs=[(D_MODEL, F32)], epi=add_epi, epi_rows=[(x, D_MODEL, 0)],
                tm=tr)
    sv['x1'] = x1

    def swiglu_epi(acc):
        g, u = acc[:, :D_FF], acc[:, D_FF:]
        return acc, g * _sigmoid(g) * u

    sv['gu'], sv['act'], sv['h2'] = _mm_nn(x1, L['w_gu'], name=f"l{i}_gate_up", outs=[(2 * D_FF, BF16), (D_FF, BF16)],
                                           gain=norms['norm_ffn'], emit_a=True, epi=swiglu_epi, tm=tr // 2)
    x2 = _mm_nn(sv['act'], L['w_down'], name=f"l{i}_down", outs=[(D_MODEL, F32)], epi=add_epi,
                epi_rows=[(x1, D_MODEL, 0)], tm=tr)
    sv['x2'] = x2
    sv['logit'], sv['h3'] = _mm_nn(x2, L['w_ple_gate'], name=f"l{i}_ple_gate", outs=[(D_MODEL, F32)],
                                   gain=norms['norm_ple'], emit_a=True, tm=tr)

    def ple_epi(pe, xv, lg):
        return pe, xv + _sigmoid(lg) * pe

    sv['pe'], x3 = _mm_nn(p_i, L['w_ple'], name=f"l{i}_ple", outs=[(D_MODEL, F32), (D_MODEL, F32)], epi=ple_epi,
                          epi_rows=[(x2, D_MODEL, 0), (sv['logit'], D_MODEL, 0)], tm=tr)
    return x3, sv


def _layer_bwd(i, dx3, sv, p_i, L, norms, w_pool_bf, pool_scale, cos_t, sin_t, tr, blk):
    T = dx3.shape[0]
    G = {}
    z = sv['z']

    def ple_bwd(_, d, lg, pe):
        g = _sigmoid(lg)
        return d * pe * g * (1.0 - g), d * g

    dlogit, dpe = _row_call(ple_bwd, T, tr, [_rspec(dx3, tr), _rspec(sv['logit'], tr), _rspec(sv['pe'], tr)],
                            [(D_MODEL, BF16), (D_MODEL, BF16)], name=f"l{i}_ple_bwd")
    G['w_ple_gate'] = _mm_tn(sv['h3'], dlogit, name=f"l{i}_dw_ple_gate", tn=1024)
    G['w_ple'] = _mm_tn(p_i, dpe, name=f"l{i}_dw_ple", tn=1024)
    def gain_row(n):
        return norms[n].reshape(1, -1).astype(F32)

    dx2, dx2_bf, G['norm_ple'] = _mm_nt([(dlogit, L['w_ple_gate'])], name=f"l{i}_dh3_norm_bwd",
                                        outs=[(D_MODEL, F32), (D_MODEL, BF16)], epi=_rms_bwd_epi(True, True),
                                        epi_rows=[(sv['x2'], D_MODEL, 0), (dx3, D_MODEL, 0)], consts=[gain_row('norm_ple')],
                                        accs=[(1, D_MODEL)], tm=tr)

    def swiglu_bwd_epi(da, gu):
        g, u = gu[:, :D_FF].astype(F32), gu[:, D_FF:].astype(F32)
        sg = _sigmoid(g)
        return (jnp.concatenate([da * u * sg * (1.0 + g * (1.0 - sg)), da * g * sg], axis=1),)

    dgu = _mm_nt([(dx2_bf, L['w_down'])], name=f"l{i}_dact_swiglu_bwd", outs=[(2 * D_FF, BF16)], epi=swiglu_bwd_epi,
                 epi_rows=[(sv['gu'], 2 * D_FF, 0)], tm=tr // 2)
    G['w_down'] = _mm_tn(sv['act'], dx2_bf, name=f"l{i}_dw_down", tk=1408, tn=1024)
    G['w_gu'] = _mm_tn(sv['h2'], dgu, name=f"l{i}_dw_gate_up", tn=1408)
    dx1, dx1_bf, G['norm_ffn'] = _mm_nt([(dgu, L['w_gu'])], name=f"l{i}_dh2_norm_bwd",
                                        outs=[(D_MODEL, F32), (D_MODEL, BF16)], epi=_rms_bwd_epi(True, True),
                                        epi_rows=[(sv['x1'], D_MODEL, 0), (dx2, D_MODEL, 0)], consts=[gain_row('norm_ffn')],
                                        accs=[(1, D_MODEL)], tm=tr // 2)

    def merge_bwd_epi(dm, ga, gb, ya, yb):
        sa, sb = _sigmoid(ga), _sigmoid(gb)
        ya, yb = ya.astype(F32), yb.astype(F32)
        d_gates = jnp.concatenate([dm * ya * sa * (1.0 - sa), dm * yb * sb * (1.0 - sb)], axis=1)
        return d_gates, dm * sa, dm * sb

    dz, dya, dyb = _mm_nt([(dx1_bf, L['w_o'])], name=f"l{i}_dmerged_bwd",
                          outs=[(2 * D_MODEL, BF16), (D_MODEL, BF16), (D_MODEL, BF16)], epi=merge_bwd_epi,
                          epi_rows=[(z, D_MODEL, 0), (z, D_MODEL, 1), (sv['ya'], D_MODEL, 0), (sv['yb'], D_MODEL, 0)],
                          tm=tr, wide0=Z_WIDTH)
    G['w_o'] = _mm_tn(sv['merged'], dx1_bf, name=f"l{i}_dw_o", tn=1024)

    G['w_b'] = _mm_tn(sv['o'], dyb, name=f"l{i}_dw_b", tn=1024)
    do = _mm_nt([(dyb, L['w_b'])], name=f"l{i}_do", outs=[(D_MODEL, BF16)], tm=tr)
    delta = _attn_delta(do, sv['o'], name=f"l{i}_attn_delta", tr=tr)
    dq, dk, dv = _flash_bwd(sv['q'], sv['k'], sv['v'], sv['lse'], delta, do, cos_t, sin_t, name=f"l{i}_attn_bwd", blk=blk)

    def dk_rope(_, d, ct, st):
        d = d.astype(F32)
        acc = d[:, QK_NOPE:HEAD_PAD]
        for h in range(1, N_HEADS):
            acc = acc + d[:, h * HEAD_PAD + QK_NOPE:(h + 1) * HEAD_PAD]
        return _rope_bwd(acc, ct, st)

    dz = _row_call(dk_rope, T, tr, [_rspec(dk, tr), _rspec(cos_t, tr), _rspec(sin_t, tr)], [(LANES, BF16)],
                   name=f"l{i}_dk_rope", into=(dz, LANES, ZC_KR // LANES))[0]
    G['w_q'] = _mm_tn(sv['cqn'], dq, name=f"l{i}_dw_q", tn=1024)
    G['w_k'] = _mm_tn(sv['ckvn'], dk, name=f"l{i}_dw_k", tn=1024)
    G['w_v'] = _mm_tn(sv['ckvn'], dv, name=f"l{i}_dw_v", tn=1024)
    dz, G['q_norm'] = _mm_nt([(dq, L['w_q'])], name=f"l{i}_dcq", outs=[(Q_LORA, BF16)], epi=_rms_bwd_epi(False, False),
                             epi_rows=[(z, Q_LORA, ZC_CQ // Q_LORA)], consts=[gain_row('q_norm')], accs=[(1, Q_LORA)], tm=tr,
                             into=(dz, Q_LORA, ZC_CQ // Q_LORA))
    dz, G['kv_norm'] = _mm_nt([(dk, L['w_k']), (dv, L['w_v'])], name=f"l{i}_dckv", outs=[(KV_LORA, BF16)],
                              epi=_rms_bwd_epi(False, False), epi_rows=[(z, KV_LORA, ZC_CKV // KV_LORA)],
                              consts=[gain_row('kv_norm')], accs=[(1, KV_LORA)], tm=tr,
                              into=(dz, KV_LORA, ZC_CKV // KV_LORA))

    G['w_a'] = _mm_tn(sv['pm'], dya, name=f"l{i}_dw_a", tn=1024)
    dpm = _mm_nt([(dya, L['w_a'])], name=f"l{i}_dpm", outs=[(POOL_WIDTH, F32)], tm=tr)
    dpool, dpool_cnt, G['pool_scale'], G['w_pool'] = _pool_bwd_mix(dpm, sv['mixed'], sv['pooled'], w_pool_bf, pool_scale,
                                                                   name=f"l{i}_pool_bwd_mix", tr=tr)
    dz = _pool_bwd_window(dpool, dpool_cnt, name=f"l{i}_pool_bwd_window", tr=tr, into=(dz, POOL_WIDTH, ZC_U // POOL_WIDTH))

    G['w_in'] = _mm_tn(sv['h'], dz, name=f"l{i}_dw_in", tn=1152)
    dx, G['norm_mix'] = _mm_nt([(dz, L['w_in'])], name=f"l{i}_dh_norm_bwd", outs=[(D_MODEL, F32)],
                               epi=_rms_bwd_epi(True, False), epi_rows=[(sv['x'], D_MODEL, 0), (dx1, D_MODEL, 0)],
                               consts=[gain_row('norm_mix')], accs=[(1, D_MODEL)], tm=tr)
    return dx, G


def kernel(x, p, positions, norm_mix, w_in, w_pool, pool_scale, q_norm, kv_norm, w_uq, w_ukv, w_a, w_b, w_o, norm_ffn, w_gate, w_up, w_down, norm_ple, w_ple_gate, w_ple, final_norm, loss_target, m_norm_mix, m_w_in, m_w_pool, m_pool_scale, m_q_norm, m_kv_norm, m_w_uq, m_w_ukv, m_w_a, m_w_b, m_w_o, m_norm_ffn, m_w_gate, m_w_up, m_w_down, m_norm_ple, m_w_ple_gate, m_w_ple, m_final_norm, v_norm_mix, v_w_in, v_w_pool, v_pool_scale, v_q_norm, v_kv_norm, v_w_uq, v_w_ukv, v_w_a, v_w_b, v_w_o, v_norm_ffn, v_w_gate, v_w_up, v_w_down, v_norm_ple, v_w_ple_gate, v_w_ple, v_final_norm):
    given = dict(locals())
    weights = {n: given[n] for n in WEIGHTS}
    T = x.shape[1]
    tr = min(512, max(T // 2, 8))
    blk = min(ATTN_BLOCK, max(T // 4, 128))
    x0 = x.reshape(T, D_MODEL)
    target = loss_target.reshape(T, D_MODEL)

    names = list(SHARDED)
    shard_shapes = [weights[n].shape for n in names]
    flat = _pack_blocks([weights[n].astype(BF16) for n in names], row_mult=PACK_ROWS)
    R = flat.shape[0]
    chip = (2 * lax.axis_index("x") + lax.axis_index("y")).astype(jnp.int32)
    core = lax.axis_index("c").astype(jnp.int32)
    gathered = _gather_weights(flat).reshape(N_CHIPS, R, PACK_LANES)
    gathered = lax.dynamic_update_slice(gathered, flat.reshape(1, R, PACK_LANES), (chip, 0, 0))
    per_chip = [_unpack_blocks(gathered[k], shard_shapes) for k in range(N_CHIPS)]
    W = {n: jnp.concatenate([per_chip[k][j] for k in range(N_CHIPS)], axis=SHARDED[n]) for j, n in enumerate(names)}
    layouts = [_layer_layouts(W, i) for i in range(DEPTH)]
    w_pool_bf = w_pool.astype(BF16)

    inv_freq = 1.0 / (ROPE_THETA ** (jnp.arange(0, QK_ROPE, 2, dtype=F32) / QK_ROPE))
    zero32 = jnp.zeros((32,), F32)
    freq_row = jnp.concatenate([inv_freq, zero32, inv_freq, zero32]).reshape(1, LANES)
    cos_mask = jnp.concatenate([jnp.ones((32,), F32), zero32, jnp.ones((32,), F32), zero32]).reshape(1, LANES)
    sin_sign = jnp.concatenate([-jnp.ones((32,), F32), zero32, jnp.ones((32,), F32), zero32]).reshape(1, LANES)

    def rope_tables(_, pos, fr, cm, ss):
        ang = pos.astype(F32) * fr
        return jnp.cos(ang) * cm, jnp.sin(ang) * ss

    pos_col = positions.reshape(T, 1)
    cos_t, sin_t = _row_call(rope_tables, T, tr, [_rspec(pos_col, tr), _bspec(freq_row), _bspec(cos_mask), _bspec(sin_sign)],
                             [(LANES, F32), (LANES, F32)], name="rope_tables")

    xs = x0
    saved = []
    for i in range(DEPTH):
        norms = {n: weights[n][i] for n in ('norm_mix', 'q_norm', 'kv_norm', 'norm_ffn', 'norm_ple')}
        xs, sv = _layer_fwd(i, xs, p[i, 0], layouts[i], norms, w_pool_bf[i], pool_scale[i], cos_t, sin_t, tr, blk)
        saved.append((sv, norms))

    def head(_, xv, tv, gv):
        rstd = lax.rsqrt(jnp.mean(xv * xv, axis=-1, keepdims=True) + EPS)
        xhat = xv * rstd
        err = xhat * gv - tv
        loss = 0.5 * jnp.sum(jnp.mean(err * err, axis=-1, keepdims=True), axis=0, keepdims=True)
        dy = err * (1.0 / D_MODEL)
        dg = jnp.sum(dy * xhat, axis=0, keepdims=True)
        dxh = dy * gv
        dx = rstd * (dxh - xhat * jnp.mean(dxh * xhat, axis=-1, keepdims=True))
        return dx, jnp.broadcast_to(loss, (1, LANES)), dg

    dx, loss_part, g_final = _row_call(head, T, tr, [_rspec(xs, tr), _rspec(target, tr), _bspec(final_norm.reshape(1, D_MODEL))],
                                       [(D_MODEL, F32)], [(1, LANES), (1, D_MODEL)], name="loss_head")
    loss = lax.psum(loss_part[0, 0], ("x", "y", "c"))

    layer_grads = [None] * DEPTH
    for i in reversed(range(DEPTH)):
        sv, norms = saved[i]
        dx, layer_grads[i] = _layer_bwd(i, dx, sv, p[i, 0], layouts[i], norms, w_pool_bf[i], pool_scale[i], cos_t, sin_t, tr,
                                        blk)
    grad_x = dx.reshape(x.shape)

    ref_layout = [_layer_grads_to_reference_layout(g) for g in layer_grads]
    local = {n: jnp.stack([ref_layout[i][n] for i in range(DEPTH)]) for n in names}
    for n in ('norm_mix', 'q_norm', 'kv_norm', 'norm_ffn', 'norm_ple', 'pool_scale'):
        local[n] = jnp.stack([layer_grads[i][n].reshape(-1) for i in range(DEPTH)])
    local['w_pool'] = jnp.stack([layer_grads[i]['w_pool'] for i in range(DEPTH)])
    local['final_norm'] = g_final.reshape(-1)

    send = []
    for k in range(N_CHIPS):
        parts = []
        for n in names:
            ax = SHARDED[n]
            size = local[n].shape[ax] // N_CHIPS
            parts.append(lax.slice_in_dim(local[n], k * size, (k + 1) * size, axis=ax))
        send.append(_pack_blocks(parts, row_mult=PACK_ROWS))
    rh = R // 2
    trr = PACK_ROWS // 2
    send = [g.reshape(2, rh, PACK_LANES) for g in send]
    part = _add_halves(send, _swap_halves(send), core, tr=trr)
    reduced_half = _sum_chips(_scatter_partials(part), tr=trr)
    reduced = _both_halves(reduced_half, _send_half(reduced_half), core, tr=trr).reshape(R, PACK_LANES)
    grads = dict(zip(names, _unpack_blocks(reduced, shard_shapes)))

    rep_shapes = [weights[n].shape for n in REPLICATED]
    rep = _allreduce_small(_pack_rows([local[n] for n in REPLICATED], row_mult=8))
    grads.update(zip(REPLICATED, _unpack_rows(rep, rep_shapes)))

    deltas, new_m, new_v = {}, {}, {}
    for n in WEIGHTS:
        deltas[n], new_m[n], new_v[n] = _adamw(weights[n], grads[n], given['m_' + n], given['v_' + n], name=f"adamw_{n}")
    return (loss, grad_x, *[grads[n] for n in WEIGHTS], *[deltas[n] for n in WEIGHTS], *[new_m[n] for n in WEIGHTS],
            *[new_v[n] for n in WEIGHTS])
```

```python
import numpy as np
import jax
import jax.numpy as jnp
from jax import lax
from jax.experimental import pallas as pl
from jax.experimental.pallas import tpu as pltpu

F32 = jnp.float32
BF16 = jnp.bfloat16

D_MODEL = 1024
DEPTH = 2
PLE_DIM = 256
POOL_WINDOWS = (2, 4, 8, 16)
POOL_GROUP = 128
POOL_WIDTH = 512
N_HEADS = 8
Q_LORA = 512
KV_LORA = 256
QK_NOPE = 128
QK_ROPE = 64
QK_HEAD = 192
V_HEAD = 128
D_FF = 2816
ROPE_THETA = 10000.0
EPS = 1e-6
ATTN_SCALE = QK_HEAD ** -0.5

ADAM_LR = 0.001
ADAM_B1 = 0.9
ADAM_B2 = 0.999
ADAM_EPS = 1e-08
ADAM_WD = 0.01
ADAM_STEP = 10

LANES = 128
HALO = 16
HEAD_PAD = 256
V7X_VMEM_BYTES = 64 * 1024 * 1024
VMEM_LIMIT = (V7X_VMEM_BYTES * 3) // 4
N_CHIPS = 4
N_DEV = 8
NEG_INF = -1e30

ZC_GA, ZC_GB, ZC_U, ZC_CQ, ZC_CKV, ZC_KR = 0, 1024, 2048, 2560, 3072, 3328
Z_WIDTH = 3456

WEIGHTS = ['norm_mix', 'w_in', 'w_pool', 'pool_scale', 'q_norm', 'kv_norm', 'w_uq', 'w_ukv', 'w_a', 'w_b', 'w_o',
           'norm_ffn', 'w_gate', 'w_up', 'w_down', 'norm_ple', 'w_ple_gate', 'w_ple', 'final_norm']
SHARDED = {'w_in': 2, 'w_uq': 1, 'w_ukv': 1, 'w_a': 2, 'w_b': 1, 'w_o': 1, 'w_gate': 2, 'w_up': 2, 'w_down': 1,
           'w_ple_gate': 1, 'w_ple': 2}
REPLICATED = [n for n in WEIGHTS if n not in SHARDED]


def _tile(n, target, mult=LANES):
    if n <= target:
        return n
    best = None
    for t in range(mult, target + 1, mult):
        if n % t == 0:
            best = t
    assert best is not None, (n, target)
    return best


def _cparams(*sem):
    return pltpu.CompilerParams(dimension_semantics=sem, vmem_limit_bytes=VMEM_LIMIT)


def _rope(t, cos_t, sin_t):
    return t * cos_t + pltpu.roll(t, 64, 1) * sin_t


def _rope_bwd(d, cos_t, sin_t):
    return d * cos_t + pltpu.roll(d * sin_t, 64, 1)


def _sigmoid(v):
    return 1.0 / (1.0 + jnp.exp(-v))


def _mm_nn(a, b, *, name, outs, a_col=0, gain=None, emit_a=False, epi=None, epi_rows=(), tm=512):
    M = a.shape[0]
    K, N = b.shape
    tm = min(tm, M)
    assert a_col % K == 0 and M % tm == 0
    a_blk = a_col // K
    n_rows, n_out = len(epi_rows), len(outs)

    def body(*refs):
        a_ref, b_ref = refs[0], refs[1]
        pos = 2
        g_ref = None
        if gain is not None:
            g_ref = refs[pos]
            pos += 1
        row_refs = refs[pos:pos + n_rows]
        out_refs = refs[pos + n_rows:pos + n_rows + n_out]
        lhs = a_ref[...]
        if gain is not None:
            av = lhs.astype(F32)
            lhs = av * lax.rsqrt(jnp.mean(av * av, axis=-1, keepdims=True) + EPS) * g_ref[...]
        lhs = lhs.astype(BF16)
        if emit_a:
            refs[pos + n_rows + n_out][...] = lhs
        acc = jnp.dot(lhs, b_ref[...], preferred_element_type=F32)
        vals = (acc,) if epi is None else epi(acc, *[r[...] for r in row_refs])
        for r, v in zip(out_refs, vals):
            r[...] = v.astype(r.dtype)

    in_specs = [pl.BlockSpec((tm, K), lambda i: (i, a_blk)), pl.BlockSpec((K, N), lambda i: (0, 0))]
    args = [a, b]
    if gain is not None:
        in_specs.append(pl.BlockSpec((1, K), lambda i: (0, 0)))
        args.append(gain.reshape(1, K).astype(F32))
    for arr, w, blk in epi_rows:
        in_specs.append(pl.BlockSpec((tm, w), lambda i, blk=blk: (i, blk)))
        args.append(arr)
    out_shape = [jax.ShapeDtypeStruct((M, w), dt) for w, dt in outs]
    out_specs = [pl.BlockSpec((tm, w), lambda i: (i, 0)) for w, dt in outs]
    if emit_a:
        out_shape.append(jax.ShapeDtypeStruct((M, K), BF16))
        out_specs.append(pl.BlockSpec((tm, K), lambda i: (i, 0)))
    res = pl.pallas_call(body, grid=(M // tm,), in_specs=in_specs, out_specs=out_specs, out_shape=out_shape,
                         name=name, compiler_params=_cparams("parallel"))(*args)
    return res[0] if len(res) == 1 else res


def _mm_nt(pairs, *, name, outs, epi=None, epi_rows=(), consts=(), accs=(), tm=512, into=None, wide0=None):
    M = pairs[0][0].shape[0]
    N = pairs[0][1].shape[0]
    tm = min(tm, M)
    n_p, n_in, n_out, n_acc = len(pairs), len(epi_rows) + len(consts), len(outs), len(accs)

    def body(*refs):
        acc = None
        for k in range(n_p):
            av = refs[2 * k][...].astype(BF16)
            part = lax.dot_general(av, refs[2 * k + 1][...], NT_DIMS, preferred_element_type=F32)
            acc = part if acc is None else acc + part
        pos = 2 * n_p
        extra = [r[...] for r in refs[pos:pos + n_in]]
        pos += n_in + (1 if into is not None else 0)
        vals = (acc,) if epi is None else epi(acc, *extra)
        for r, v in zip(refs[pos:pos + n_out], vals[:n_out]):
            r[...] = v.astype(r.dtype)
        if n_acc:
            acc_refs = refs[pos + n_out:pos + n_out + n_acc]

            @pl.when(pl.program_id(0) == 0)
            def _():
                for r in acc_refs:
                    r[...] = jnp.zeros_like(r)
            for r, v in zip(acc_refs, vals[n_out:]):
                r[...] += v

    in_specs, args = [], []
    for a, b in pairs:
        assert a.shape[1] == b.shape[1] and b.shape[0] == N and a.shape[0] == M
        in_specs.append(pl.BlockSpec((tm, a.shape[1]), lambda i: (i, 0)))
        in_specs.append(pl.BlockSpec(b.shape, lambda i: (0, 0)))
        args += [a, b]
    for arr, w, blk in epi_rows:
        in_specs.append(pl.BlockSpec((tm, w), lambda i, blk=blk: (i, blk)))
        args.append(arr)
    for arr in consts:
        in_specs.append(pl.BlockSpec(arr.shape, lambda i, n=arr.ndim: (0,) * n))
        args.append(arr)
    out_shape = [jax.ShapeDtypeStruct((M, w), dt) for w, dt in outs]
    out_specs = [pl.BlockSpec((tm, w), lambda i: (i, 0)) for w, dt in outs]
    for s in accs:
        out_shape.append(jax.ShapeDtypeStruct(s, F32))
        out_specs.append(pl.BlockSpec(s, lambda i, n=len(s): (0,) * n))
    aliases = _into_column_block(into, tm, out_shape, out_specs, in_specs, args) if into is not None else {}
    if wide0 is not None:
        out_shape[0] = jax.ShapeDtypeStruct((M, wide0), outs[0][1])
    res = pl.pallas_call(body, grid=(M // tm,), in_specs=in_specs, out_specs=out_specs, out_shape=out_shape, name=name,
                         input_output_aliases=aliases,
                         compiler_params=_cparams("arbitrary" if n_acc else "parallel"))(*args)
    return res[0] if len(res) == 1 else res


def _rms_bwd_epi(with_res, emit_bf16):
    def epi(dh, xv, *rest):
        gv = rest[-1]
        xv = xv.astype(F32)
        rstd = lax.rsqrt(jnp.mean(xv * xv, axis=-1, keepdims=True) + EPS)
        xhat = xv * rstd
        dg = jnp.sum(dh * xhat, axis=0, keepdims=True)
        dxh = dh * gv
        dx = rstd * (dxh - xhat * jnp.mean(dxh * xhat, axis=-1, keepdims=True))
        if with_res:
            dx = dx + rest[0].astype(F32)
        return (dx, dx, dg) if emit_bf16 else (dx, dg)
    return epi


def _mm_tn(a, b, *, name, a_col=0, a_w=None, tk=1024, tn=1152, tm=1024):
    M = a.shape[0]
    a_w = a.shape[1] if a_w is None else a_w
    N = b.shape[1]
    tm = min(tm, M)
    tk = _tile(a_w, tk)
    tn = _tile(N, tn)
    assert a_col % tk == 0 and M % tm == 0
    a_blk0 = a_col // tk

    def body(a_ref, b_ref, o_ref):
        @pl.when(pl.program_id(2) == 0)
        def _():
            o_ref[...] = jnp.zeros_like(o_ref)
        o_ref[...] += lax.dot_general(a_ref[...].astype(BF16), b_ref[...].astype(BF16), (((0,), (0,)), ((), ())),
                                      preferred_element_type=F32)

    return pl.pallas_call(body, grid=(a_w // tk, N // tn, M // tm),
                          in_specs=[pl.BlockSpec((tm, tk), lambda k, j, m: (m, k + a_blk0)),
                                    pl.BlockSpec((tm, tn), lambda k, j, m: (m, j))],
                          out_specs=pl.BlockSpec((tk, tn), lambda k, j, m: (k, j)),
                          out_shape=jax.ShapeDtypeStruct((a_w, N), F32), name=name,
                          compiler_params=_cparams("parallel", "parallel", "arbitrary"))(a, b)


def _into_column_block(into, tile, out_shape, out_specs, in_specs, args):
    buf, width, blk = into
    out_shape[0] = jax.ShapeDtypeStruct(buf.shape, buf.dtype)
    out_specs[0] = pl.BlockSpec((tile, width), lambda i: (i, blk))
    in_specs.append(pl.BlockSpec(memory_space=pl.ANY))
    args.append(buf)
    return {len(args) - 1: 0}


def _row_call(fn, rows, tr, ins, outs, accs=(), *, name, into=None):
    n_in, n_out, n_acc = len(ins), len(outs), len(accs)
    first_out = n_in + (1 if into is not None else 0)

    def body(*refs):
        i = pl.program_id(0)
        vals = fn(i, *[r[...] for r in refs[:n_in]])
        if not isinstance(vals, (tuple, list)):
            vals = (vals,)
        for r, v in zip(refs[first_out:first_out + n_out], vals[:n_out]):
            r[...] = v.astype(r.dtype)
        if n_acc:
            acc_refs = refs[first_out + n_out:]

            @pl.when(i == 0)
            def _():
                for r in acc_refs:
                    r[...] = jnp.zeros_like(r)
            for r, v in zip(acc_refs, vals[n_out:]):
                r[...] += v

    out_shape = [jax.ShapeDtypeStruct((rows, w), dt) for w, dt in outs]
    out_specs = [pl.BlockSpec((tr, w), lambda i: (i, 0)) for w, dt in outs]
    for s in accs:
        out_shape.append(jax.ShapeDtypeStruct(s, F32))
        out_specs.append(pl.BlockSpec(s, lambda i, n=len(s): (0,) * n))
    in_specs = [pl.BlockSpec(bs, im) for _, bs, im in ins]
    args = [a for a, _, _ in ins]
    aliases = _into_column_block(into, tr, out_shape, out_specs, in_specs, args) if into is not None else {}
    res = pl.pallas_call(body, grid=(rows // tr,), in_specs=in_specs, out_specs=out_specs, out_shape=out_shape, name=name,
                         input_output_aliases=aliases, compiler_params=_cparams("arbitrary"))(*args)
    return res


def _rspec(arr, tr, w=None, blk=0):
    w = arr.shape[1] if w is None else w
    return (arr, (tr, w), lambda i, blk=blk: (i, blk))


def _bspec(arr):
    return (arr, arr.shape, lambda i, n=arr.ndim: (0,) * n)


def _pool_counts(i, tr):
    t = (i * tr + lax.broadcasted_iota(jnp.int32, (tr, 1), 0) + 1).astype(F32)
    return [jnp.minimum(t, float(w)) for w in POOL_WINDOWS]


def _pool_fwd(z, w_pool_bf, pool_scale, *, name, tr):
    rows = z.shape[0]
    ublk = ZC_U // POOL_WIDTH
    hpt = tr // HALO

    def fn(i, u, uprev, wp, ps):
        uprev = jnp.where(i > 0, uprev, 0.0)
        ext = jnp.concatenate([uprev, u], axis=0)
        s2 = ext + pltpu.roll(ext, 1, 0)
        s4 = s2 + pltpu.roll(s2, 2, 0)
        s8 = s4 + pltpu.roll(s4, 4, 0)
        s16 = s8 + pltpu.roll(s8, 8, 0)
        cnts = _pool_counts(i, tr)
        pooled, mixed = [], []
        for g, sw in enumerate((s2, s4, s8, s16)):
            lanes = slice(g * POOL_GROUP, (g + 1) * POOL_GROUP)
            pg = sw[HALO:, lanes] / cnts[g] - u[:, lanes]
            pooled.append(pg)
            mixed.append(jnp.dot(pg.astype(BF16), wp[g], preferred_element_type=F32))
        pooled = jnp.concatenate(pooled, axis=1)
        mixed = jnp.concatenate(mixed, axis=1)
        return pooled, mixed, mixed * ps

    ins = [_rspec(z, tr, POOL_WIDTH, ublk),
           (z, (HALO, POOL_WIDTH), lambda i: (jnp.maximum(i * hpt - 1, 0), ublk)),
           _bspec(w_pool_bf), _bspec(pool_scale.reshape(1, POOL_WIDTH))]
    return _row_call(fn, rows, tr, ins, [(POOL_WIDTH, BF16), (POOL_WIDTH, F32), (POOL_WIDTH, BF16)], name=name)


def _pool_bwd_mix(dpm, mixed, pooled, w_pool_bf, pool_scale, *, name, tr):
    rows = dpm.shape[0]

    def fn(i, dv, mv, pv, wp, ps):
        dv = dv.astype(F32)
        dscale = jnp.sum(dv * mv, axis=0, keepdims=True)
        dmix = (dv * ps).astype(BF16)
        cnts = _pool_counts(i, tr)
        dpool, dwp = [], []
        for g in range(len(POOL_WINDOWS)):
            lanes = slice(g * POOL_GROUP, (g + 1) * POOL_GROUP)
            dg = lax.dot_general(dmix[:, lanes], wp[g], (((1,), (1,)), ((), ())), preferred_element_type=F32)
            dpool.append(dg)
            dwp.append(lax.dot_general(pv[:, lanes], dmix[:, lanes], (((0,), (0,)), ((), ())),
                                       preferred_element_type=F32)[None])
        dpool = jnp.concatenate(dpool, axis=1)
        dpool_cnt = jnp.concatenate([dpool[:, g * POOL_GROUP:(g + 1) * POOL_GROUP] / cnts[g]
                                     for g in range(len(POOL_WINDOWS))], axis=1)
        return dpool, dpool_cnt, dscale, jnp.concatenate(dwp, axis=0)

    ins = [_rspec(dpm, tr), _rspec(mixed, tr), _rspec(pooled, tr), _bspec(w_pool_bf),
           _bspec(pool_scale.reshape(1, POOL_WIDTH))]
    return _row_call(fn, rows, tr, ins, [(POOL_WIDTH, F32), (POOL_WIDTH, F32)],
                     [(1, POOL_WIDTH), (len(POOL_WINDOWS), POOL_GROUP, POOL_GROUP)], name=name)


def _pool_bwd_window(dpool, dpool_cnt, *, name, tr, into):
    rows = dpool.shape[0]
    hpt = tr // HALO
    n_halo = rows // HALO
    n_tiles = rows // tr

    def fn(i, dp, dc, dnext):
        dnext = jnp.where(i < n_tiles - 1, dnext, 0.0)
        ext = jnp.concatenate([dc, dnext], axis=0)
        n = tr + HALO
        s2 = ext + pltpu.roll(ext, n - 1, 0)
        s4 = s2 + pltpu.roll(s2, n - 2, 0)
        s8 = s4 + pltpu.roll(s4, n - 4, 0)
        s16 = s8 + pltpu.roll(s8, n - 8, 0)
        out = []
        for g, sw in enumerate((s2, s4, s8, s16)):
            lanes = slice(g * POOL_GROUP, (g + 1) * POOL_GROUP)
            out.append(sw[:tr, lanes] - dp[:, lanes])
        return jnp.concatenate(out, axis=1)

    ins = [_rspec(dpool, tr), _rspec(dpool_cnt, tr),
           (dpool_cnt, (HALO, POOL_WIDTH), lambda i: (jnp.minimum((i + 1) * hpt, n_halo - 1), 0))]
    return _row_call(fn, rows, tr, ins, [(POOL_WIDTH, BF16)], name=name, into=into)[0]


def _causal_pairs(n, k_major):
    if k_major:
        pairs = [(qi, ki) for ki in range(n) for qi in range(ki, n)]
    else:
        pairs = [(qi, ki) for qi in range(n) for ki in range(qi + 1)]
    return (jnp.asarray(np.array([p[0] for p in pairs], np.int32)),
            jnp.asarray(np.array([p[1] for p in pairs], np.int32)), len(pairs))


SUBLANES = 8
NT_DIMS = (((1,), (1,)), ((), ()))
TN_DIMS = (((0,), (0,)), ((), ()))


ATTN_BLOCK = 1024
ATTN_BLOCK_FWD = 2048
QUERY_CHUNK = 256
LOG2_E = 1.4426950408889634
EXP2_SCALE = ATTN_SCALE * LOG2_E


def _scores_t(q_c, k, diag):
    s = lax.dot_general(k, q_c, NT_DIMS, preferred_element_type=F32)
    if diag:
        qc = q_c.shape[0]
        visible = lax.broadcasted_iota(jnp.int32, (qc, qc), 0) <= lax.broadcasted_iota(jnp.int32, (qc, qc), 1)
        last = jnp.where(visible, s[-qc:], NEG_INF)
        s = last if s.shape[0] == qc else jnp.concatenate([s[:-qc], last], axis=0)
    return s


def _flash_fwd(q, k, v, *, name, blk):
    T = q.shape[0]
    n = T // blk
    qc = min(QUERY_CHUNK, blk)
    qtab, ktab, n_pairs = _causal_pairs(n, k_major=False)

    def body(qt, kt, q_ref, k_ref, v_ref, o_ref, lse_ref, m_s, l_s, acc_s):
        p = pl.program_id(1)
        qi, ki = qt[p], kt[p]

        @pl.when(ki == 0)
        def _():
            m_s[...] = jnp.full_like(m_s, NEG_INF)
            l_s[...] = jnp.zeros_like(l_s)
            acc_s[...] = jnp.zeros_like(acc_s)

        def step(diag):
            chunks = [slice(c * qc, (c + 1) * qc) for c in range(blk // qc)]
            keys = [(c + 1) * qc if diag else blk for c in range(blk // qc)]
            scores = [_scores_t(q_ref[rows, :], k_ref[:n, :], diag) for rows, n in zip(chunks, keys)]
            probs, alphas = [], []
            for rows, s_t in zip(chunks, scores):
                m_prev = m_s[:, rows]
                m_new = jnp.maximum(m_prev, jnp.max(s_t, axis=0, keepdims=True))
                p_t = jnp.exp2(s_t - m_new)
                alpha = jnp.exp2(m_prev - m_new)
                l_s[:, rows] = alpha * l_s[:, rows] + jnp.sum(p_t, axis=0, keepdims=True)
                m_s[:, rows] = m_new
                probs.append(p_t.astype(BF16))
                alphas.append(alpha)
            for rows, n, p_t, alpha in zip(chunks, keys, probs, alphas):
                acc_s[:, rows] = alpha * acc_s[:, rows] + lax.dot_general(v_ref[:n, :], p_t, TN_DIMS,
                                                                          preferred_element_type=F32)

        @pl.when(ki != qi)
        def _():
            step(False)

        @pl.when(ki == qi)
        def _():
            step(True)
            o_ref[...] = (acc_s[...] / l_s[...]).T.astype(o_ref.dtype)
            lse2 = m_s[...] + jnp.log2(l_s[...])
            lse_ref[...] = jnp.broadcast_to(lse2, lse_ref.shape)

    grid_spec = pltpu.PrefetchScalarGridSpec(
        num_scalar_prefetch=2, grid=(N_HEADS, n_pairs),
        in_specs=[pl.BlockSpec((blk, HEAD_PAD), lambda h, p, qt, kt: (qt[p], h)),
                  pl.BlockSpec((blk, HEAD_PAD), lambda h, p, qt, kt: (kt[p], h)),
                  pl.BlockSpec((blk, V_HEAD), lambda h, p, qt, kt: (kt[p], h))],
        out_specs=[pl.BlockSpec((blk, V_HEAD), lambda h, p, qt, kt: (qt[p], h)),
                   pl.BlockSpec((SUBLANES, blk), lambda h, p, qt, kt: (h, qt[p]))],
        scratch_shapes=[pltpu.VMEM((1, blk), F32), pltpu.VMEM((1, blk), F32), pltpu.VMEM((V_HEAD, blk), F32)])
    return pl.pallas_call(body, grid_spec=grid_spec,
                          out_shape=[jax.ShapeDtypeStruct((T, N_HEADS * V_HEAD), BF16),
                                     jax.ShapeDtypeStruct((N_HEADS * SUBLANES, T), F32)],
                          name=name, compiler_params=_cparams("parallel", "arbitrary"))(qtab, ktab, q, k, v)


def _attn_delta(do, o, *, name, tr):
    T = do.shape[0]

    def body(do_ref, o_ref, d_ref):
        prod = do_ref[...].astype(F32) * o_ref[...].astype(F32)
        lane_head = lax.broadcasted_iota(jnp.int32, (tr, LANES), 1) // SUBLANES
        mat = jnp.zeros((tr, LANES), F32)
        for h in range(N_HEADS):
            d_h = jnp.sum(prod[:, h * V_HEAD:(h + 1) * V_HEAD], axis=1, keepdims=True)
            mat = jnp.where(lane_head == h, d_h, mat)
        d_ref[...] = mat.T[:N_HEADS * SUBLANES, :]

    return pl.pallas_call(body, grid=(T // tr,),
                          in_specs=[pl.BlockSpec((tr, N_HEADS * V_HEAD), lambda i: (i, 0)),
                                    pl.BlockSpec((tr, N_HEADS * V_HEAD), lambda i: (i, 0))],
                          out_specs=pl.BlockSpec((N_HEADS * SUBLANES, tr), lambda i: (0, i)),
                          out_shape=jax.ShapeDtypeStruct((N_HEADS * SUBLANES, T), F32), name=name,
                          compiler_params=_cparams("parallel"))(do, o)


def _flash_bwd(q, k, v, lse, delta, do, cos_t, sin_t, *, name, blk):
    T = q.shape[0]
    n = T // blk
    qc = min(QUERY_CHUNK, blk)
    qtab, ktab, n_pairs = _causal_pairs(n, k_major=True)

    def body(qt, kt, q_ref, k_ref, v_ref, lse_ref, delta_ref, do_ref, cos_ref, sin_ref, dq_ref, dk_ref, dv_ref,
             dq_s, dk_s, dv_s):
        p = pl.program_id(1)
        qi, ki = qt[p], kt[p]
        first = qi == ki

        @pl.when(p == 0)
        def _():
            dq_s[...] = jnp.zeros_like(dq_s)

        @pl.when(first)
        def _():
            dk_s[...] = jnp.zeros_like(dk_s)
            dv_s[...] = jnp.zeros_like(dv_s)

        def step(diag):
            chunks = [slice(c * qc, (c + 1) * qc) for c in range(blk // qc)]
            qs = [q_ref[rows, :] for rows in chunks]
            dos = [do_ref[rows, :] for rows in chunks]
            keys = [(c + 1) * qc if diag else blk for c in range(blk // qc)]
            scores = [_scores_t(q_c, k_ref[:n, :], diag) for q_c, n in zip(qs, keys)]
            dps = [lax.dot_general(v_ref[:n, :], do_c, NT_DIMS, preferred_element_type=F32) for do_c, n in zip(dos, keys)]
            probs, dss = [], []
            for rows, s_t, dp_t in zip(chunks, scores, dps):
                p_t = jnp.exp2(s_t - lse_ref[0:1, rows])
                dss.append((p_t * (dp_t - delta_ref[0:1, rows])).astype(BF16))
                probs.append(p_t.astype(BF16))
            dv_acc = dk_acc = None
            for n, p_t, ds_t, q_c, do_c in zip(keys, probs, dss, qs, dos):
                dv_c = jnp.dot(p_t, do_c, preferred_element_type=F32)
                dk_c = jnp.dot(ds_t, q_c, preferred_element_type=F32)
                if diag:
                    dv_s[:n, :] += dv_c
                    dk_s[:n, :] += dk_c
                else:
                    dv_acc = dv_c if dv_acc is None else dv_acc + dv_c
                    dk_acc = dk_c if dk_acc is None else dk_acc + dk_c
            for rows, n, ds_t in zip(chunks, keys, dss):
                dq_s[qi, :, rows] += lax.dot_general(k_ref[:n, :], ds_t, TN_DIMS, preferred_element_type=F32)
            if not diag:
                dv_s[...] += dv_acc
                dk_s[...] += dk_acc

        @pl.when(jnp.logical_not(first))
        def _():
            step(False)

        @pl.when(first)
        def _():
            step(True)
            dq_t = (dq_s[qi] * ATTN_SCALE).T
            dq_ref[:, :QK_NOPE] = dq_t[:, :QK_NOPE].astype(dq_ref.dtype)
            dq_ref[:, QK_NOPE:] = _rope_bwd(dq_t[:, QK_NOPE:], cos_ref[...], sin_ref[...]).astype(dq_ref.dtype)

        @pl.when(qi == n - 1)
        def _():
            dk_ref[...] = (dk_s[...] * (1.0 / LOG2_E)).astype(dk_ref.dtype)
            dv_ref[...] = dv_s[...].astype(dv_ref.dtype)

    qmap = lambda h, p, qt, kt: (qt[p], h)
    kmap = lambda h, p, qt, kt: (kt[p], h)
    smap = lambda h, p, qt, kt: (h, qt[p])
    tmap = lambda h, p, qt, kt: (kt[p], 0)
    grid_spec = pltpu.PrefetchScalarGridSpec(
        num_scalar_prefetch=2, grid=(N_HEADS, n_pairs),
        in_specs=[pl.BlockSpec((blk, HEAD_PAD), qmap), pl.BlockSpec((blk, HEAD_PAD), kmap),
                  pl.BlockSpec((blk, V_HEAD), kmap), pl.BlockSpec((SUBLANES, blk), smap),
                  pl.BlockSpec((SUBLANES, blk), smap), pl.BlockSpec((blk, V_HEAD), qmap),
                  pl.BlockSpec((blk, LANES), tmap), pl.BlockSpec((blk, LANES), tmap)],
        out_specs=[pl.BlockSpec((blk, HEAD_PAD), kmap), pl.BlockSpec((blk, HEAD_PAD), kmap),
                   pl.BlockSpec((blk, V_HEAD), kmap)],
        scratch_shapes=[pltpu.VMEM((n, HEAD_PAD, blk), F32), pltpu.VMEM((blk, HEAD_PAD), F32),
                        pltpu.VMEM((blk, V_HEAD), F32)])
    return pl.pallas_call(body, grid_spec=grid_spec,
                          out_shape=[jax.ShapeDtypeStruct((T, N_HEADS * HEAD_PAD), BF16),
                                     jax.ShapeDtypeStruct((T, N_HEADS * HEAD_PAD), BF16),
                                     jax.ShapeDtypeStruct((T, N_HEADS * V_HEAD), BF16)],
                          name=name, compiler_params=_cparams("arbitrary", "arbitrary"))(
                              qtab, ktab, q, k, v, lse, delta, do, cos_t, sin_t)


MESH_ID = pl.DeviceIdType.MESH
ANY_SPEC = pl.BlockSpec(memory_space=pl.ANY)


def _other_chips(x, y):
    out = []
    for dx, dy in ((1, 0), (0, 1), (1, 1)):
        px = x ^ dx if dx else x
        py = y ^ dy if dy else y
        out.append((px, py, 2 * px + py))
    return out


def _gather_weights(flat):
    rh = flat.shape[0] // 2
    rq = rh // 2

    def body(src2, out, send_sems, recv_sems):
        x, y, c = lax.axis_index("x"), lax.axis_index("y"), lax.axis_index("c")
        me = 2 * x + y
        sib = (x, y, 1 - c)
        (xx, xy, kx), (yx, yy, ky), (_, _, kd) = _other_chips(x, y)
        x_nbr, y_nbr = (xx, xy, c), (yx, yy, c)
        first, last = pl.ds(0, rq), pl.ds(rq, rq)

        def copy(j, src, dst, to):
            return pltpu.make_async_remote_copy(src_ref=src, dst_ref=dst, send_sem=send_sems.at[j], recv_sem=recv_sems.at[j],
                                                device_id=to, device_id_type=MESH_ID)

        def arrived(j, land):
            copy(j, land, land, sib).wait_recv()

        sends = [copy(0, src2.at[c], out.at[me, c], x_nbr), copy(1, src2.at[c], out.at[me, c], y_nbr)]
        for cp in sends:
            cp.start()
        landings = [(0, out.at[kx, c]), (1, out.at[ky, c]), (2, out.at[kd, c, first]), (3, out.at[kd, c, last])]
        relays = {0: (2, out.at[kx, c, first], y_nbr), 1: (3, out.at[ky, c, last], x_nbr)}
        for j, land in landings:
            arrived(j, land)
            if j in relays:
                rj, piece, to = relays[j]
                sends.append(copy(rj, piece, piece, to))
                sends[-1].start()
            sends.append(copy(4 + j, land, land, sib))
            sends[-1].start()
        for j, land in [(0, out.at[kx, 1 - c]), (1, out.at[ky, 1 - c]), (2, out.at[kd, 1 - c, first]),
                        (3, out.at[kd, 1 - c, last])]:
            arrived(4 + j, land)
        for cp in sends:
            cp.wait_send()

    return pl.pallas_call(body, out_shape=jax.ShapeDtypeStruct((N_CHIPS, 2, rh, PACK_LANES), flat.dtype),
                          in_specs=[ANY_SPEC], out_specs=ANY_SPEC,
                          scratch_shapes=[pltpu.SemaphoreType.DMA((8,)), pltpu.SemaphoreType.DMA((8,))],
                          name="gather_weights")(flat.reshape(2, rh, PACK_LANES))


def _swap_halves(gs):
    rh = gs[0].shape[1]

    def body(*refs):
        srcs, out, send_sems, recv_sems = refs[:N_CHIPS], refs[N_CHIPS], refs[N_CHIPS + 1], refs[N_CHIPS + 2]
        x, y, c = lax.axis_index("x"), lax.axis_index("y"), lax.axis_index("c")
        copies = [pltpu.make_async_remote_copy(src_ref=srcs[k].at[1 - c], dst_ref=out.at[k], send_sem=send_sems.at[k],
                                               recv_sem=recv_sems.at[k], device_id=(x, y, 1 - c), device_id_type=MESH_ID)
                  for k in range(N_CHIPS)]
        for cp in copies:
            cp.start()
        for cp in copies:
            cp.wait()

    return pl.pallas_call(body, out_shape=jax.ShapeDtypeStruct((N_CHIPS, rh, PACK_LANES), gs[0].dtype),
                          in_specs=[ANY_SPEC] * N_CHIPS, out_specs=ANY_SPEC,
                          scratch_shapes=[pltpu.SemaphoreType.DMA((N_CHIPS,)), pltpu.SemaphoreType.DMA((N_CHIPS,))],
                          name="grad_swap_halves")(*gs)


def _add_halves(gs, got, core, *, tr):
    rh = gs[0].shape[1]

    def body(*refs):
        g_refs, got_ref, o_ref = refs[1:1 + N_CHIPS], refs[1 + N_CHIPS], refs[2 + N_CHIPS]
        for k in range(N_CHIPS):
            o_ref[k] = (g_refs[k][0] + got_ref[k]).astype(o_ref.dtype)

    grid_spec = pltpu.PrefetchScalarGridSpec(
        num_scalar_prefetch=1, grid=(rh // tr,),
        in_specs=[pl.BlockSpec((1, tr, PACK_LANES), lambda i, c_ref: (c_ref[0], i, 0))] * N_CHIPS
        + [pl.BlockSpec((N_CHIPS, tr, PACK_LANES), lambda i, c_ref: (0, i, 0))],
        out_specs=pl.BlockSpec((N_CHIPS, tr, PACK_LANES), lambda i, c_ref: (0, i, 0)))
    return pl.pallas_call(body, grid_spec=grid_spec,
                          out_shape=jax.ShapeDtypeStruct((N_CHIPS, rh, PACK_LANES), BF16), name="grad_add_halves",
                          compiler_params=_cparams("parallel"))(core.reshape(1), *gs, got)


def _scatter_partials(part):
    rh = part.shape[1]

    def body(src, out, send_sems, recv_sems, local_sem):
        x, y, c = lax.axis_index("x"), lax.axis_index("y"), lax.axis_index("c")
        me = 2 * x + y
        own = pltpu.make_async_copy(src.at[me], out.at[me], local_sem)
        own.start()
        chips = _other_chips(x, y)
        sends = []
        for j, (px, py, pk) in enumerate(chips):
            cp = pltpu.make_async_remote_copy(src_ref=src.at[pk], dst_ref=out.at[me], send_sem=send_sems.at[j],
                                              recv_sem=recv_sems.at[j], device_id=(px, py, c), device_id_type=MESH_ID)
            cp.start()
            sends.append(cp)
        for j, (px, py, pk) in enumerate(chips):
            land = out.at[pk]
            pltpu.make_async_remote_copy(src_ref=land, dst_ref=land, send_sem=send_sems.at[j], recv_sem=recv_sems.at[j],
                                         device_id=(px, py, c), device_id_type=MESH_ID).wait_recv()
        for cp in sends:
            cp.wait_send()
        own.wait()

    return pl.pallas_call(body, out_shape=jax.ShapeDtypeStruct((N_CHIPS, rh, PACK_LANES), part.dtype),
                          in_specs=[ANY_SPEC], out_specs=ANY_SPEC,
                          scratch_shapes=[pltpu.SemaphoreType.DMA((3,)), pltpu.SemaphoreType.DMA((3,)),
                                          pltpu.SemaphoreType.DMA(())],
                          name="grad_scatter_partials")(part)


def _sum_chips(q, *, tr):
    rh = q.shape[1]

    def body(q_ref, o_ref):
        parts = [q_ref[k].astype(F32) for k in range(N_CHIPS)]
        o_ref[...] = ((parts[0] + parts[1]) + parts[2]) + parts[3]

    return pl.pallas_call(body, grid=(rh // tr,),
                          in_specs=[pl.BlockSpec((N_CHIPS, tr, PACK_LANES), lambda i: (0, i, 0))],
                          out_specs=pl.BlockSpec((tr, PACK_LANES), lambda i: (i, 0)),
                          out_shape=jax.ShapeDtypeStruct((rh, PACK_LANES), F32), name="grad_sum_chips",
                          compiler_params=_cparams("parallel"))(q)


def _send_half(half):
    def body(src, out, send_sem, recv_sem):
        x, y, c = lax.axis_index("x"), lax.axis_index("y"), lax.axis_index("c")
        cp = pltpu.make_async_remote_copy(src_ref=src, dst_ref=out, send_sem=send_sem, recv_sem=recv_sem,
                                          device_id=(x, y, 1 - c), device_id_type=MESH_ID)
        cp.start()
        cp.wait()

    return pl.pallas_call(body, out_shape=jax.ShapeDtypeStruct(half.shape, half.dtype),
                          in_specs=[ANY_SPEC], out_specs=ANY_SPEC,
                          scratch_shapes=[pltpu.SemaphoreType.DMA(()), pltpu.SemaphoreType.DMA(())],
                          name="grad_send_half")(half)


def _both_halves(own, got, core, *, tr):
    rh = own.shape[0]

    def body(c_ref, own_ref, got_ref, o_ref):
        mine = pl.program_id(0) == c_ref[0]
        o_ref[0] = jnp.where(mine, own_ref[...], got_ref[...])

    grid_spec = pltpu.PrefetchScalarGridSpec(
        num_scalar_prefetch=1, grid=(2, rh // tr),
        in_specs=[pl.BlockSpec((tr, PACK_LANES), lambda h, i, c_ref: (i, 0)),
                  pl.BlockSpec((tr, PACK_LANES), lambda h, i, c_ref: (i, 0))],
        out_specs=pl.BlockSpec((1, tr, PACK_LANES), lambda h, i, c_ref: (h, i, 0)))
    return pl.pallas_call(body, grid_spec=grid_spec, out_shape=jax.ShapeDtypeStruct((2, rh, PACK_LANES), own.dtype),
                          name="grad_both_halves", compiler_params=_cparams("parallel", "parallel"))(core.reshape(1), own, got)


def _allreduce_small(v):
    rows = v.shape[0]

    def body(v_ref, o_ref, buf, send_sems, recv_sems):
        x, y, c = lax.axis_index("x"), lax.axis_index("y"), lax.axis_index("c")
        me = 4 * x + 2 * y + c
        buf[me] = v_ref[...]
        sends = []
        for j in range(1, N_DEV):
            px, py, pc = x ^ ((j >> 2) & 1), y ^ ((j >> 1) & 1), c ^ (j & 1)
            cp = pltpu.make_async_remote_copy(src_ref=v_ref, dst_ref=buf.at[me], send_sem=send_sems.at[j - 1],
                                              recv_sem=recv_sems.at[j - 1], device_id=(px, py, pc), device_id_type=MESH_ID)
            cp.start()
            sends.append(cp)
        for j in range(1, N_DEV):
            px, py, pc = x ^ ((j >> 2) & 1), y ^ ((j >> 1) & 1), c ^ (j & 1)
            land = buf.at[4 * px + 2 * py + pc]
            pltpu.make_async_remote_copy(src_ref=land, dst_ref=land, send_sem=send_sems.at[j - 1],
                                         recv_sem=recv_sems.at[j - 1], device_id=(px, py, pc),
                                         device_id_type=MESH_ID).wait_recv()
        for cp in sends:
            cp.wait_send()
        acc = buf[0]
        for d in range(1, N_DEV):
            acc = acc + buf[d]
        o_ref[...] = acc

    vm = pl.BlockSpec(memory_space=pltpu.VMEM)
    return pl.pallas_call(body, out_shape=jax.ShapeDtypeStruct((rows, LANES), F32), in_specs=[vm], out_specs=vm,
                          scratch_shapes=[pltpu.VMEM((N_DEV, rows, LANES), F32), pltpu.SemaphoreType.DMA((N_DEV - 1,)),
                                          pltpu.SemaphoreType.DMA((N_DEV - 1,))],
                          name="allreduce_small")(v)


def _adamw(w, g, m, v, *, name):
    shape = w.shape
    cols = shape[-1] if w.ndim > 1 else shape[0]
    rows = w.size // cols
    w2, g2, m2, v2 = (t.reshape(rows, cols) for t in (w, g, m, v))
    tr = rows if rows <= 256 else _tile(rows, 256, 8)

    def fn(i, wv, gv, mv, vv):
        mn = ADAM_B1 * mv + (1.0 - ADAM_B1) * gv
        vn = ADAM_B2 * vv + (1.0 - ADAM_B2) * (gv * gv)
        m_hat = mn / (1.0 - ADAM_B1 ** ADAM_STEP)
        v_hat = vn / (1.0 - ADAM_B2 ** ADAM_STEP)
        delta = -ADAM_LR * (m_hat / (jnp.sqrt(v_hat) + ADAM_EPS) + ADAM_WD * wv)
        return delta, mn, vn

    ins = [_rspec(t, tr) for t in (w2, g2, m2, v2)]
    d, mn, vn = _row_call(fn, rows, tr, ins, [(cols, F32)] * 3, name=name)
    return d.reshape(shape), mn.reshape(shape), vn.reshape(shape)


def _rope_cols(w):
    z = jnp.zeros(w.shape[:-1] + (32,), w.dtype)
    return jnp.concatenate([w[..., :32], z, w[..., 32:], z], axis=-1)


def _rope_cols_inv(w):
    return jnp.concatenate([w[..., :32], w[..., 64:96]], axis=-1)


def _layer_layouts(W, i):
    w_in = W['w_in'][i]
    u, cq, ckv = w_in[:, :512], w_in[:, 512:1024], w_in[:, 1024:1280]
    kr, ga, gb = w_in[:, 1280:1344], w_in[:, 1344:2368], w_in[:, 2368:]
    L = {}
    L['w_in'] = jnp.concatenate([ga, gb, u, cq, ckv, _rope_cols(kr)], axis=1)
    wq = W['w_uq'][i]
    L['w_q'] = jnp.concatenate([wq[..., :QK_NOPE], _rope_cols(wq[..., QK_NOPE:])], axis=-1).reshape(Q_LORA, -1)
    wkv = W['w_ukv'][i]
    L['w_k'] = jnp.concatenate([wkv[..., :QK_NOPE], jnp.zeros_like(wkv[..., :LANES])], axis=-1).reshape(KV_LORA, -1)
    L['w_v'] = wkv[..., QK_NOPE:].reshape(KV_LORA, -1)
    L['w_gu'] = jnp.concatenate([W['w_gate'][i], W['w_up'][i]], axis=1)
    for n in ('w_a', 'w_b', 'w_o', 'w_down', 'w_ple_gate', 'w_ple'):
        L[n] = W[n][i]
    return L


def _layer_grads_to_reference_layout(G):
    d = G['w_in']
    ga, gb, u = d[:, ZC_GA:ZC_GB], d[:, ZC_GB:ZC_U], d[:, ZC_U:ZC_CQ]
    cq, ckv, kr = d[:, ZC_CQ:ZC_CKV], d[:, ZC_CKV:ZC_KR], _rope_cols_inv(d[:, ZC_KR:])
    out = {'w_in': jnp.concatenate([u, cq, ckv, kr, ga, gb], axis=1)}
    dq = G['w_q'].reshape(Q_LORA, N_HEADS, HEAD_PAD)
    out['w_uq'] = jnp.concatenate([dq[..., :QK_NOPE], _rope_cols_inv(dq[..., QK_NOPE:])], axis=-1)
    dk = G['w_k'].reshape(KV_LORA, N_HEADS, HEAD_PAD)[..., :QK_NOPE]
    dv = G['w_v'].reshape(KV_LORA, N_HEADS, V_HEAD)
    out['w_ukv'] = jnp.concatenate([dk, dv], axis=-1)
    out['w_gate'], out['w_up'] = G['w_gu'][:, :D_FF], G['w_gu'][:, D_FF:]
    for n in ('w_a', 'w_b', 'w_o', 'w_down', 'w_ple_gate', 'w_ple'):
        out[n] = G[n]
    return out


PACK_ROWS = 2048


def _pack_rows(parts, row_mult):
    flat = jnp.concatenate([p.reshape(-1) for p in parts])
    n = flat.shape[0]
    per = LANES * row_mult
    padded = -(-n // per) * per
    return jnp.pad(flat, (0, padded - n)).reshape(-1, LANES)


def _unpack_rows(flat2d, shapes):
    flat = flat2d.reshape(-1)
    out, off = [], 0
    for s in shapes:
        n = int(np.prod(s))
        out.append(flat[off:off + n].reshape(s))
        off += n
    return out


PACK_LANES = 256


def _lane_blocks(cols):
    return -(-cols // PACK_LANES)


def _pack_blocks(parts, row_mult):
    blocks = []
    for p in parts:
        p2 = p.reshape(-1, p.shape[-1])
        cols = p2.shape[1]
        nb = _lane_blocks(cols)
        if nb * PACK_LANES != cols:
            p2 = jnp.pad(p2, ((0, 0), (0, nb * PACK_LANES - cols)))
        blocks += [p2[:, j * PACK_LANES:(j + 1) * PACK_LANES] for j in range(nb)]
    buf = jnp.concatenate(blocks, axis=0)
    rows = buf.shape[0]
    padded = -(-rows // row_mult) * row_mult
    return buf if padded == rows else jnp.pad(buf, ((0, padded - rows), (0, 0)))


def _unpack_blocks(buf, shapes):
    out, off = [], 0
    for s in shapes:
        rows, cols = int(np.prod(s[:-1])), s[-1]
        nb = _lane_blocks(cols)
        piece = jnp.concatenate([buf[off + j * rows:off + (j + 1) * rows] for j in range(nb)], axis=1)
        out.append(piece[:, :cols].reshape(s))
        off += nb * rows
    return out


def _layer_fwd(i, x, p_i, L, norms, w_pool_bf, pool_scale, cos_t, sin_t, tr, blk):
    sv = {'x': x}
    z, sv['h'] = _mm_nn(x, L['w_in'], name=f"l{i}_in_proj", outs=[(Z_WIDTH, F32)], gain=norms['norm_mix'], emit_a=True,
                        tm=tr // 2)
    sv['z'] = z
    sv['pooled'], sv['mixed'], sv['pm'] = _pool_fwd(z, w_pool_bf, pool_scale, name=f"l{i}_pool", tr=tr)
    sv['ya'] = _mm_nn(sv['pm'], L['w_a'], name=f"l{i}_ya", outs=[(D_MODEL, BF16)], tm=tr)

    def heads(acc, rope_part):
        out = []
        for h in range(N_HEADS):
            out.append(acc[:, h * HEAD_PAD:h * HEAD_PAD + QK_NOPE])
            out.append(rope_part(acc[:, h * HEAD_PAD + QK_NOPE:(h + 1) * HEAD_PAD]))
        return jnp.concatenate(out, axis=1)

    def q_epi(acc, ct, st):
        return (heads(acc * EXP2_SCALE, lambda t: _rope(t, ct, st)),)

    def k_epi(acc, kr, ct, st):
        k_pe = _rope(kr, ct, st)
        return (heads(acc, lambda t: k_pe),)

    rope_rows = [(cos_t, LANES, 0), (sin_t, LANES, 0)]
    qk_width = N_HEADS * HEAD_PAD
    sv['q'], sv['cqn'] = _mm_nn(z, L['w_q'], name=f"l{i}_q_proj", outs=[(qk_width, BF16)], a_col=ZC_CQ,
                                gain=norms['q_norm'], emit_a=True, epi=q_epi, epi_rows=rope_rows, tm=tr)
    sv['k'], sv['ckvn'] = _mm_nn(z, L['w_k'], name=f"l{i}_k_proj", outs=[(qk_width, BF16)], a_col=ZC_CKV,
                                 gain=norms['kv_norm'], emit_a=True, epi=k_epi,
                                 epi_rows=[(z, LANES, ZC_KR // LANES)] + rope_rows, tm=tr)
    sv['v'] = _mm_nn(sv['ckvn'], L['w_v'], name=f"l{i}_v_proj", outs=[(N_HEADS * V_HEAD, BF16)], tm=tr)
    sv['o'], sv['lse'] = _flash_fwd(sv['q'], sv['k'], sv['v'], name=f"l{i}_attn",
                                    blk=min(ATTN_BLOCK_FWD, max(blk * ATTN_BLOCK_FWD // ATTN_BLOCK, 128)))

    def merge_epi(yb, ga, gb, ya):
        return yb, _sigmoid(ga) * ya.astype(F32) + _sigmoid(gb) * yb

    sv['yb'], sv['merged'] = _mm_nn(sv['o'], L['w_b'], name=f"l{i}_yb_merge", outs=[(D_MODEL, BF16), (D_MODEL, BF16)],
                                    epi=merge_epi, epi_rows=[(z, D_MODEL, 0), (z, D_MODEL, 1), (sv['ya'], D_MODEL, 0)],
                                    tm=tr)

    def add_epi(acc, res):
        return (acc + res,)

    x1 = _mm_nn(sv['merged'], L['w_o'], name=f"l{i}_wo", outs=[(D_MODEL, F32)], epi=add_epi, epi_rows=[(x, D_MODEL, 0)],
                tm=tr)
    sv['x1'] = x1

    def swiglu_epi(acc):
        g, u = acc[:, :D_FF], acc[:, D_FF:]
        return acc, g * _sigmoid(g) * u

    sv['gu'], sv['act'], sv['h2'] = _mm_nn(x1, L['w_gu'], name=f"l{i}_gate_up", outs=[(2 * D_FF, BF16), (D_FF, BF16)],
                                           gain=norms['norm_ffn'], emit_a=True, epi=swiglu_epi, tm=tr // 2)
    x2 = _mm_nn(sv['act'], L['w_down'], name=f"l{i}_down", outs=[(D_MODEL, F32)], epi=add_epi,
                epi_rows=[(x1, D_MODEL, 0)], tm=tr)
    sv['x2'] = x2
    sv['logit'], sv['h3'] = _mm_nn(x2, L['w_ple_gate'], name=f"l{i}_ple_gate", outs=[(D_MODEL, F32)],
                                   gain=norms['norm_ple'], emit_a=True, tm=tr)

    def ple_epi(pe, xv, lg):
        return pe, xv + _sigmoid(lg) * pe

    sv['pe'], x3 = _mm_nn(p_i, L['w_ple'], name=f"l{i}_ple", outs=[(D_MODEL, F32), (D_MODEL, F32)], epi=ple_epi,
                          epi_rows=[(x2, D_MODEL, 0), (sv['logit'], D_MODEL, 0)], tm=tr)
    return x3, sv


def _layer_bwd(i, dx3, sv, p_i, L, norms, w_pool_bf, pool_scale, cos_t, sin_t, tr, blk):
    T = dx3.shape[0]
    G = {}
    z = sv['z']

    def ple_bwd(_, d, lg, pe):
        g = _sigmoid(lg)
        return d * pe * g * (1.0 - g), d * g

    dlogit, dpe = _row_call(ple_bwd, T, tr, [_rspec(dx3, tr), _rspec(sv['logit'], tr), _rspec(sv['pe'], tr)],
                            [(D_MODEL, BF16), (D_MODEL, BF16)], name=f"l{i}_ple_bwd")
    G['w_ple_gate'] = _mm_tn(sv['h3'], dlogit, name=f"l{i}_dw_ple_gate", tn=1024)
    G['w_ple'] = _mm_tn(p_i, dpe, name=f"l{i}_dw_ple", tn=1024)
    def gain_row(n):
        return norms[n].reshape(1, -1).astype(F32)

    dx2, dx2_bf, G['norm_ple'] = _mm_nt([(dlogit, L['w_ple_gate'])], name=f"l{i}_dh3_norm_bwd",
                                        outs=[(D_MODEL, F32), (D_MODEL, BF16)], epi=_rms_bwd_epi(True, True),
                                        epi_rows=[(sv['x2'], D_MODEL, 0), (dx3, D_MODEL, 0)], consts=[gain_row('norm_ple')],
                                        accs=[(1, D_MODEL)], tm=tr)

    def swiglu_bwd_epi(da, gu):
        g, u = gu[:, :D_FF].astype(F32), gu[:, D_FF:].astype(F32)
        sg = _sigmoid(g)
        return (jnp.concatenate([da * u * sg * (1.0 + g * (1.0 - sg)), da * g * sg], axis=1),)

    dgu = _mm_nt([(dx2_bf, L['w_down'])], name=f"l{i}_dact_swiglu_bwd", outs=[(2 * D_FF, BF16)], epi=swiglu_bwd_epi,
                 epi_rows=[(sv['gu'], 2 * D_FF, 0)], tm=tr // 2)
    G['w_down'] = _mm_tn(sv['act'], dx2_bf, name=f"l{i}_dw_down", tk=1408, tn=1024)
    G['w_gu'] = _mm_tn(sv['h2'], dgu, name=f"l{i}_dw_gate_up", tn=1408)
    dx1, dx1_bf, G['norm_ffn'] = _mm_nt([(dgu, L['w_gu'])], name=f"l{i}_dh2_norm_bwd",
                                        outs=[(D_MODEL, F32), (D_MODEL, BF16)], epi=_rms_bwd_epi(True, True),
                                        epi_rows=[(sv['x1'], D_MODEL, 0), (dx2, D_MODEL, 0)], consts=[gain_row('norm_ffn')],
                                        accs=[(1, D_MODEL)], tm=tr // 2)

    def merge_bwd_epi(dm, ga, gb, ya, yb):
        sa, sb = _sigmoid(ga), _sigmoid(gb)
        ya, yb = ya.astype(F32), yb.astype(F32)
        d_gates = jnp.concatenate([dm * ya * sa * (1.0 - sa), dm * yb * sb * (1.0 - sb)], axis=1)
        return d_gates, dm * sa, dm * sb

    dz, dya, dyb = _mm_nt([(dx1_bf, L['w_o'])], name=f"l{i}_dmerged_bwd",
                          outs=[(2 * D_MODEL, BF16), (D_MODEL, BF16), (D_MODEL, BF16)], epi=merge_bwd_epi,
                          epi_rows=[(z, D_MODEL, 0), (z, D_MODEL, 1), (sv['ya'], D_MODEL, 0), (sv['yb'], D_MODEL, 0)],
                          tm=tr, wide0=Z_WIDTH)
    G['w_o'] = _mm_tn(sv['merged'], dx1_bf, name=f"l{i}_dw_o", tn=1024)

    G['w_b'] = _mm_tn(sv['o'], dyb, name=f"l{i}_dw_b", tn=1024)
    do = _mm_nt([(dyb, L['w_b'])], name=f"l{i}_do", outs=[(D_MODEL, BF16)], tm=tr)
    delta = _attn_delta(do, sv['o'], name=f"l{i}_attn_delta", tr=tr)
    dq, dk, dv = _flash_bwd(sv['q'], sv['k'], sv['v'], sv['lse'], delta, do, cos_t, sin_t, name=f"l{i}_attn_bwd", blk=blk)

    def dk_rope(_, d, ct, st):
        d = d.astype(F32)
        acc = d[:, QK_NOPE:HEAD_PAD]
        for h in range(1, N_HEADS):
            acc = acc + d[:, h * HEAD_PAD + QK_NOPE:(h + 1) * HEAD_PAD]
        return _rope_bwd(acc, ct, st)

    dz = _row_call(dk_rope, T, tr, [_rspec(dk, tr), _rspec(cos_t, tr), _rspec(sin_t, tr)], [(LANES, BF16)],
                   name=f"l{i}_dk_rope", into=(dz, LANES, ZC_KR // LANES))[0]
    G['w_q'] = _mm_tn(sv['cqn'], dq, name=f"l{i}_dw_q", tn=1024)
    G['w_k'] = _mm_tn(sv['ckvn'], dk, name=f"l{i}_dw_k", tn=1024)
    G['w_v'] = _mm_tn(sv['ckvn'], dv, name=f"l{i}_dw_v", tn=1024)
    dz, G['q_norm'] = _mm_nt([(dq, L['w_q'])], name=f"l{i}_dcq", outs=[(Q_LORA, BF16)], epi=_rms_bwd_epi(False, False),
                             epi_rows=[(z, Q_LORA, ZC_CQ // Q_LORA)], consts=[gain_row('q_norm')], accs=[(1, Q_LORA)], tm=tr,
                             into=(dz, Q_LORA, ZC_CQ // Q_LORA))
    dz, G['kv_norm'] = _mm_nt([(dk, L['w_k']), (dv, L['w_v'])], name=f"l{i}_dckv", outs=[(KV_LORA, BF16)],
                              epi=_rms_bwd_epi(False, False), epi_rows=[(z, KV_LORA, ZC_CKV // KV_LORA)],
                              consts=[gain_row('kv_norm')], accs=[(1, KV_LORA)], tm=tr,
                              into=(dz, KV_LORA, ZC_CKV // KV_LORA))

    G['w_a'] = _mm_tn(sv['pm'], dya, name=f"l{i}_dw_a", tn=1024)
    dpm = _mm_nt([(dya, L['w_a'])], name=f"l{i}_dpm", outs=[(POOL_WIDTH, F32)], tm=tr)
    dpool, dpool_cnt, G['pool_scale'], G['w_pool'] = _pool_bwd_mix(dpm, sv['mixed'], sv['pooled'], w_pool_bf, pool_scale,
                                                                   name=f"l{i}_pool_bwd_mix", tr=tr)
    dz = _pool_bwd_window(dpool, dpool_cnt, name=f"l{i}_pool_bwd_window", tr=tr, into=(dz, POOL_WIDTH, ZC_U // POOL_WIDTH))

    G['w_in'] = _mm_tn(sv['h'], dz, name=f"l{i}_dw_in", tn=1152)
    dx, G['norm_mix'] = _mm_nt([(dz, L['w_in'])], name=f"l{i}_dh_norm_bwd", outs=[(D_MODEL, F32)],
                               epi=_rms_bwd_epi(True, False), epi_rows=[(sv['x'], D_MODEL, 0), (dx1, D_MODEL, 0)],
                               consts=[gain_row('norm_mix')], accs=[(1, D_MODEL)], tm=tr)
    return dx, G


def kernel(x, p, positions, norm_mix, w_in, w_pool, pool_scale, q_norm, kv_norm, w_uq, w_ukv, w_a, w_b, w_o, norm_ffn, w_gate, w_up, w_down, norm_ple, w_ple_gate, w_ple, final_norm, loss_target, m_norm_mix, m_w_in, m_w_pool, m_pool_scale, m_q_norm, m_kv_norm, m_w_uq, m_w_ukv, m_w_a, m_w_b, m_w_o, m_norm_ffn, m_w_gate, m_w_up, m_w_down, m_norm_ple, m_w_ple_gate, m_w_ple, m_final_norm, v_norm_mix, v_w_in, v_w_pool, v_pool_scale, v_q_norm, v_kv_norm, v_w_uq, v_w_ukv, v_w_a, v_w_b, v_w_o, v_norm_ffn, v_w_gate, v_w_up, v_w_down, v_norm_ple, v_w_ple_gate, v_w_ple, v_final_norm):
    given = dict(locals())
    weights = {n: given[n] for n in WEIGHTS}
    T = x.shape[1]
    tr = min(512, max(T // 2, 8))
    blk = min(ATTN_BLOCK, max(T // 4, 128))
    x0 = x.reshape(T, D_MODEL)
    target = loss_target.reshape(T, D_MODEL)

    names = list(SHARDED)
    shard_shapes = [weights[n].shape for n in names]
    flat = _pack_blocks([weights[n].astype(BF16) for n in names], row_mult=PACK_ROWS)
    R = flat.shape[0]
    chip = (2 * lax.axis_index("x") + lax.axis_index("y")).astype(jnp.int32)
    core = lax.axis_index("c").astype(jnp.int32)
    gathered = _gather_weights(flat).reshape(N_CHIPS, R, PACK_LANES)
    gathered = lax.dynamic_update_slice(gathered, flat.reshape(1, R, PACK_LANES), (chip, 0, 0))
    per_chip = [_unpack_blocks(gathered[k], shard_shapes) for k in range(N_CHIPS)]
    W = {n: jnp.concatenate([per_chip[k][j] for k in range(N_CHIPS)], axis=SHARDED[n]) for j, n in enumerate(names)}
    layouts = [_layer_layouts(W, i) for i in range(DEPTH)]
    w_pool_bf = w_pool.astype(BF16)

    inv_freq = 1.0 / (ROPE_THETA ** (jnp.arange(0, QK_ROPE, 2, dtype=F32) / QK_ROPE))
    zero32 = jnp.zeros((32,), F32)
    freq_row = jnp.concatenate([inv_freq, zero32, inv_freq, zero32]).reshape(1, LANES)
    cos_mask = jnp.concatenate([jnp.ones((32,), F32), zero32, jnp.ones((32,), F32), zero32]).reshape(1, LANES)
    sin_sign = jnp.concatenate([-jnp.ones((32,), F32), zero32, jnp.ones((32,), F32), zero32]).reshape(1, LANES)

    def rope_tables(_, pos, fr, cm, ss):
        ang = pos.astype(F32) * fr
        return jnp.cos(ang) * cm, jnp.sin(ang) * ss

    pos_col = positions.reshape(T, 1)
    cos_t, sin_t = _row_call(rope_tables, T, tr, [_rspec(pos_col, tr), _bspec(freq_row), _bspec(cos_mask), _bspec(sin_sign)],
                             [(LANES, F32), (LANES, F32)], name="rope_tables")

    xs = x0
    saved = []
    for i in range(DEPTH):
        norms = {n: weights[n][i] for n in ('norm_mix', 'q_norm', 'kv_norm', 'norm_ffn', 'norm_ple')}
        xs, sv = _layer_fwd(i, xs, p[i, 0], layouts[i], norms, w_pool_bf[i], pool_scale[i], cos_t, sin_t, tr, blk)
        saved.append((sv, norms))

    def head(_, xv, tv, gv):
        rstd = lax.rsqrt(jnp.mean(xv * xv, axis=-1, keepdims=True) + EPS)
        xhat = xv * rstd
        err = xhat * gv - tv
        loss = 0.5 * jnp.sum(jnp.mean(err * err, axis=-1, keepdims=True), axis=0, keepdims=True)
        dy = err * (1.0 / D_MODEL)
        dg = jnp.sum(dy * xhat, axis=0, keepdims=True)
        dxh = dy * gv
        dx = rstd * (dxh - xhat * jnp.mean(dxh * xhat, axis=-1, keepdims=True))
        return dx, jnp.broadcast_to(loss, (1, LANES)), dg

    dx, loss_part, g_final = _row_call(head, T, tr, [_rspec(xs, tr), _rspec(target, tr), _bspec(final_norm.reshape(1, D_MODEL))],
                                       [(D_MODEL, F32)], [(1, LANES), (1, D_MODEL)], name="loss_head")
    loss = lax.psum(loss_part[0, 0], ("x", "y", "c"))

    layer_grads = [None] * DEPTH
    for i in reversed(range(DEPTH)):
        sv, norms = saved[i]
        dx, layer_grads[i] = _layer_bwd(i, dx, sv, p[i, 0], layouts[i], norms, w_pool_bf[i], pool_scale[i], cos_t, sin_t, tr,
                                        blk)
    grad_x = dx.reshape(x.shape)

    ref_layout = [_layer_grads_to_reference_layout(g) for g in layer_grads]
    local = {n: jnp.stack([ref_layout[i][n] for i in range(DEPTH)]) for n in names}
    for n in ('norm_mix', 'q_norm', 'kv_norm', 'norm_ffn', 'norm_ple', 'pool_scale'):
        local[n] = jnp.stack([layer_grads[i][n].reshape(-1) for i in range(DEPTH)])
    local['w_pool'] = jnp.stack([layer_grads[i]['w_pool'] for i in range(DEPTH)])
    local['final_norm'] = g_final.reshape(-1)

    send = []
    for k in range(N_CHIPS):
        parts = []
        for n in names:
            ax = SHARDED[n]
            size = local[n].shape[ax] // N_CHIPS
            parts.append(lax.slice_in_dim(local[n], k * size, (k + 1) * size, axis=ax))
        send.append(_pack_blocks(parts, row_mult=PACK_ROWS))
    rh = R // 2
    trr = PACK_ROWS // 2
    send = [g.reshape(2, rh, PACK_LANES) for g in send]
    part = _add_halves(send, _swap_halves(send), core, tr=trr)
    reduced_half = _sum_chips(_scatter_partials(part), tr=trr)
    reduced = _both_halves(reduced_half, _send_half(reduced_half), core, tr=trr).reshape(R, PACK_LANES)
    grads = dict(zip(names, _unpack_blocks(reduced, shard_shapes)))

    rep_shapes = [weights[n].shape for n in REPLICATED]
    rep = _allreduce_small(_pack_rows([local[n] for n in REPLICATED], row_mult=8))
    grads.update(zip(REPLICATED, _unpack_rows(rep, rep_shapes)))

    deltas, new_m, new_v = {}, {}, {}
    for n in WEIGHTS:
        deltas[n], new_m[n], new_v[n] = _adamw(weights[n], grads[n], given['m_' + n], given['v_' + n], name=f"adamw_{n}")
    return (loss, grad_x, *[grads[n] for n in WEIGHTS], *[deltas[n] for n in WEIGHTS], *[new_m[n] for n in WEIGHTS],
            *[new_v[n] for n in WEIGHTS])
```
